```python
import math
import jax, jax.numpy as jnp
from jax import lax
import numpy as np

D_MODEL = 1024
BATCH = 16
SEQ = 256
DEPTH = 2
DEC_BATCH = 2
DEC_SEQ = 2048
PAST_LEN = 256

GRID_W = 64
N_MIXERS = 2
N_ATTN_LAYERS = (DEPTH + N_MIXERS - 1) // N_MIXERS
N_GLA_LAYERS = DEPTH // N_MIXERS
DA_HEADS = 8
DA_DIM = D_MODEL // (2 * DA_HEADS)
DA_VDIM = 2 * DA_DIM
ROPE_BASE = 10000.0
Q_BLOCK = 128
GLA_HEADS = 4
GLA_DK = D_MODEL // 2 // GLA_HEADS
GLA_DV = D_MODEL // GLA_HEADS
GLA_GATE_RANK = 16
GLA_GATE_NORM = 16.0
GLA_CHUNK = 64
N_EXPERTS = 16
EC_CAPACITY_FACTOR = 2
D_EXPERT = 2048
EPS = 1e-6

kernel_name = "hybrid_diffattn_gla_ecmoe_diffusion_step"


def rmsnorm(x, g):
    xf = x.astype(jnp.float32)
    y = xf * lax.rsqrt(jnp.mean(xf * xf, axis=-1, keepdims=True) + EPS)
    return (y * g.astype(jnp.float32)).astype(x.dtype)


def ada_params(cvec, w_mod, b_mod):
    m = jnp.einsum('...d,de->...e', jax.nn.silu(cvec), w_mod) + b_mod
    return jnp.split(m[..., None, :], 6, axis=-1)


def modulate(x, g, shift, scale):
    return rmsnorm(x, g) * (1 + scale) + shift


def rope_1d(x, pos):
    half = x.shape[-1] // 2
    freqs = ROPE_BASE ** (-jnp.arange(half, dtype=jnp.float32) / half)
    ang = pos.astype(jnp.float32)[:, None] * freqs
    cos = jnp.cos(ang)[None, :, None, :].astype(x.dtype)
    sin = jnp.sin(ang)[None, :, None, :].astype(x.dtype)
    x1, x2 = x[..., :half], x[..., half:]
    return jnp.concatenate([x1 * cos - x2 * sin, x1 * sin + x2 * cos], axis=-1)


def rope_2d(x, rows, cols):
    r = x.shape[-1] // 2
    return jnp.concatenate([rope_1d(x[..., :r], rows), rope_1d(x[..., r:], cols)], axis=-1)


def da_project(h, w_in):
    B, T, _ = h.shape
    z = h @ w_in
    nqk = 2 * DA_HEADS * DA_DIM
    q, k, v = jnp.split(z, [nqk, 2 * nqk], axis=-1)
    return (q.reshape(B, T, 2 * DA_HEADS, DA_DIM),
            k.reshape(B, T, 2 * DA_HEADS, DA_DIM),
            v.reshape(B, T, DA_HEADS, DA_VDIM))


def da_lambda(lq1, lk1, lq2, lk2, lam_init):
    return (jnp.exp(jnp.sum((lq1 * lk1).astype(jnp.float32)))
            - jnp.exp(jnp.sum((lq2 * lk2).astype(jnp.float32))) + lam_init)


def diff_attention(q, k, v, lam):
    B, Tq = q.shape[0], q.shape[1]
    nb = Tq // Q_BLOCK
    qb = q.reshape(B, nb, Q_BLOCK, 2 * DA_HEADS, DA_DIM).transpose(1, 0, 2, 3, 4)
    scale = DA_DIM ** -0.5

    def block(qi):
        s = jnp.einsum('bqhd,bkhd->bhqk', qi, k).astype(jnp.float32) * scale
        p = jax.nn.softmax(s, axis=-1).reshape(B, DA_HEADS, 2, Q_BLOCK, -1)
        a = p[:, :, 0] - lam * p[:, :, 1]
        return jnp.einsum('bhqk,bkhe->bqhe', a.astype(v.dtype), v)

    out = lax.map(block, qb)
    return out.transpose(1, 0, 2, 3, 4).reshape(B, Tq, DA_HEADS, DA_VDIM)


def da_output(o, g_sub, lam_init, w_out):
    B, T = o.shape[0], o.shape[1]
    o = rmsnorm(o, g_sub) * (1 - lam_init)
    return o.reshape(B, T, DA_HEADS * DA_VDIM) @ w_out


def gla_project(h, w_in, w_gf2, b_gf, w_gb2, b_gb):
    B, T, _ = h.shape
    nq = GLA_HEADS * GLA_DK
    nv = GLA_HEADS * GLA_DV
    z = h @ w_in
    q, k, v, r, gf, gb = jnp.split(
        z, [nq, 2 * nq, 2 * nq + nv, 2 * nq + 2 * nv, 2 * nq + 2 * nv + GLA_GATE_RANK], axis=-1)
    q = q.reshape(B, T, GLA_HEADS, GLA_DK) * (GLA_DK ** -0.5)
    k = k.reshape(B, T, GLA_HEADS, GLA_DK)
    v = v.reshape(B, T, GLA_HEADS, GLA_DV)
    log_gf = jax.nn.log_sigmoid((gf @ w_gf2 + b_gf).astype(jnp.float32)) / GLA_GATE_NORM
    log_gb = jax.nn.log_sigmoid((gb @ w_gb2 + b_gb).astype(jnp.float32)) / GLA_GATE_NORM
    return (q, k, v, r,
            log_gf.reshape(B, T, GLA_HEADS, GLA_DK),
            log_gb.reshape(B, T, GLA_HEADS, GLA_DK))


def gla_chunk_scan(q, k, v, g, s0):
    B, T, H, _ = q.shape
    n = T // GLA_CHUNK

    def to_chunks(x):
        return x.reshape(B, n, GLA_CHUNK, H, x.shape[-1]).transpose(1, 0, 3, 2, 4)

    qc, kc, vc, gc = to_chunks(q), to_chunks(k), to_chunks(v), to_chunks(g)
    mask = jnp.tril(jnp.ones((GLA_CHUNK, GLA_CHUNK), dtype=bool))[:, :, None]

    def step(S, inp):
        qi, ki, vi, gi = inp
        qf = qi.astype(jnp.float32)
        kf = ki.astype(jnp.float32)
        vf = vi.astype(jnp.float32)
        b = jnp.cumsum(gi.astype(jnp.float32), axis=2)
        o_inter = jnp.einsum('bhcd,bhde->bhce', qf * jnp.exp(b), S)
        diff = b[:, :, :, None, :] - b[:, :, None, :, :]
        decay = jnp.where(mask, jnp.exp(jnp.where(mask, diff, 0.0)), 0.0)
        A = jnp.einsum('bhid,bhjd,bhijd->bhij', qf, kf, decay)
        o_intra = jnp.einsum('bhij,bhje->bhie', A, vf)
        b_last = b[:, :, -1:, :]
        S_new = (S * jnp.exp(b_last[:, :, 0, :])[..., None]
                 + jnp.einsum('bhjd,bhje->bhde', kf * jnp.exp(b_last - b), vf))
        return S_new, o_inter + o_intra

    S_fin, o = lax.scan(step, s0.astype(jnp.float32), (qc, kc, vc, gc))
    o = o.transpose(1, 0, 3, 2, 4).reshape(B, T, H, v.shape[-1])
    return o.astype(v.dtype), S_fin.astype(v.dtype)


def gla_bidir(q, k, v, log_gf, log_gb, s0f, s0b):
    of, sf = gla_chunk_scan(q, k, v, log_gf, s0f)
    ob, sb = gla_chunk_scan(jnp.flip(q, 1), jnp.flip(k, 1), jnp.flip(v, 1), jnp.flip(log_gb, 1), s0b)
    return of + jnp.flip(ob, 1), sf, sb


def gla_output(o, r, g_out, w_out):
    B, T = o.shape[0], o.shape[1]
    o = rmsnorm(o, g_out).reshape(B, T, GLA_HEADS * GLA_DV) * jax.nn.silu(r)
    return o @ w_out


def expert_choice_ffn(h, w_router, w1, w3, w2):
    B, T, D = h.shape
    N = B * T
    x = h.reshape(N, D)
    cap = EC_CAPACITY_FACTOR * N // N_EXPERTS
    aff = jax.nn.softmax((x @ w_router).astype(jnp.float32), axis=-1)
    gate, idx = lax.top_k(aff.T, cap)
    xe = x[idx]
    hid = jax.nn.silu(jnp.einsum('ecd,edf->ecf', xe, w1)) * jnp.einsum('ecd,edf->ecf', xe, w3)
    ye = jnp.einsum('ecf,efd->ecd', hid, w2) * gate[..., None].astype(x.dtype)
    y = jnp.zeros_like(x).at[idx.reshape(-1)].add(ye.reshape(-1, D))
    return y.reshape(B, T, D)


def setup_inputs(seed: int = 0) -> dict:
    key = jax.random.key(seed)
    ks = jax.random.split(key, 32)
    D = D_MODEL

    def nrm(k, shape, s):
        return jax.random.normal(k, shape, jnp.float32) * s

    da_in = 4 * DA_HEADS * DA_DIM + DA_HEADS * DA_VDIM
    gla_in = 2 * GLA_HEADS * GLA_DK + 2 * GLA_HEADS * GLA_DV + 2 * GLA_GATE_RANK
    return {
        "x_prompt": nrm(ks[0], (BATCH, SEQ, D), 1.0),
        "x_sample": nrm(ks[1], (DEC_BATCH, DEC_SEQ, D), 1.0),
        "cache_k": nrm(ks[2], (DEC_BATCH, N_ATTN_LAYERS, PAST_LEN, 2 * DA_HEADS, DA_DIM), 1.0),
        "cache_v": nrm(ks[3], (DEC_BATCH, N_ATTN_LAYERS, PAST_LEN, DA_HEADS, DA_VDIM), 1.0),
        "state_fwd": nrm(ks[4], (DEC_BATCH, N_GLA_LAYERS, GLA_HEADS, GLA_DK, GLA_DV), 0.5),
        "state_bwd": nrm(ks[5], (DEC_BATCH, N_GLA_LAYERS, GLA_HEADS, GLA_DK, GLA_DV), 0.5),
        "c": nrm(ks[6], (DEC_BATCH, D), 1.0),
        "c_ctx": nrm(ks[7], (D,), 1.0),
        "w_mod": nrm(ks[8], (DEPTH, D, 6 * D), 0.5 * D ** -0.5),
        "b_mod": nrm(ks[9], (DEPTH, 6 * D), 0.02),
        "norm_g": 1.0 + nrm(ks[10], (DEPTH, 2, D), 0.02),
        "da_w_in": nrm(ks[11], (N_ATTN_LAYERS, D, da_in), D ** -0.5),
        "da_w_out": nrm(ks[12], (N_ATTN_LAYERS, DA_HEADS * DA_VDIM, D), (DA_HEADS * DA_VDIM) ** -0.5),
        "da_lam_q1": nrm(ks[13], (N_ATTN_LAYERS, DA_DIM), 0.1),
        "da_lam_k1": nrm(ks[14], (N_ATTN_LAYERS, DA_DIM), 0.1),
        "da_lam_q2": nrm(ks[15], (N_ATTN_LAYERS, DA_DIM), 0.1),
        "da_lam_k2": nrm(ks[16], (N_ATTN_LAYERS, DA_DIM), 0.1),
        "da_g_sub": 1.0 + nrm(ks[17], (N_ATTN_LAYERS, DA_VDIM), 0.02),
        "gla_w_in": nrm(ks[18], (N_GLA_LAYERS, D, gla_in), D ** -0.5),
        "gla_w_gf2": nrm(ks[19], (N_GLA_LAYERS, GLA_GATE_RANK, GLA_HEADS * GLA_DK), GLA_GATE_RANK ** -0.5),
        "gla_b_gf": nrm(ks[20], (N_GLA_LAYERS, GLA_HEADS * GLA_DK), 0.1),
        "gla_w_gb2": nrm(ks[21], (N_GLA_LAYERS, GLA_GATE_RANK, GLA_HEADS * GLA_DK), GLA_GATE_RANK ** -0.5),
        "gla_b_gb": nrm(ks[22], (N_GLA_LAYERS, GLA_HEADS * GLA_DK), 0.1),
        "gla_g_out": 1.0 + nrm(ks[23], (N_GLA_LAYERS, GLA_DV), 0.02),
        "gla_w_out": nrm(ks[24], (N_GLA_LAYERS, GLA_HEADS * GLA_DV, D), (GLA_HEADS * GLA_DV) ** -0.5),
        "moe_w_router": nrm(ks[25], (DEPTH, D, N_EXPERTS), D ** -0.5),
        "moe_w1": nrm(ks[26], (DEPTH, N_EXPERTS, D, D_EXPERT), D ** -0.5),
        "moe_w3": nrm(ks[27], (DEPTH, N_EXPERTS, D, D_EXPERT), D ** -0.5),
        "moe_w2": nrm(ks[28], (DEPTH, N_EXPERTS, D_EXPERT, D), D_EXPERT ** -0.5),
        "final_g": 1.0 + nrm(ks[29], (D,), 0.02),
    }


def reference(x_prompt, x_sample, cache_k, cache_v, state_fwd, state_bwd, c, c_ctx,
              w_mod, b_mod, norm_g, da_w_in, da_w_out, da_lam_q1, da_lam_k1, da_lam_q2,
              da_lam_k2, da_g_sub, gla_w_in, gla_w_gf2, gla_b_gf, gla_w_gb2, gla_b_gb,
              gla_g_out, gla_w_out, moe_w_router, moe_w1, moe_w3, moe_w2, final_g):
    ctx = x_prompt
    lat = x_sample
    T_lat = x_sample.shape[1]
    ROWS = T_lat // GRID_W
    rr, cc = jnp.meshgrid(jnp.arange(ROWS), jnp.arange(GRID_W), indexing='ij')
    rows = rr.reshape(-1)
    cols = cc.reshape(-1)
    ks_out, vs_out, sf_out, sb_out = [], [], [], []

    for i in range(DEPTH):
        m_ctx = ada_params(c_ctx, w_mod[i], b_mod[i])
        m_lat = ada_params(c, w_mod[i], b_mod[i])
        h_ctx = modulate(ctx, norm_g[i, 0], m_ctx[0], m_ctx[1])
        h_lat = modulate(lat, norm_g[i, 0], m_lat[0], m_lat[1])
        j = i // N_MIXERS
        if i % N_MIXERS == 0:
            lam_init = 0.8 - 0.6 * math.exp(-0.3 * i)
            lam = da_lambda(da_lam_q1[j], da_lam_k1[j], da_lam_q2[j], da_lam_k2[j], lam_init)
            qc_, kc_, vc_ = da_project(h_ctx, da_w_in[j])
            o_ctx = diff_attention(qc_, kc_, vc_, lam)
            ks_out.append(kc_)
            vs_out.append(vc_)
            ql, kl, vl = da_project(h_lat, da_w_in[j])
            ql = rope_2d(ql, rows, cols)
            kl = rope_2d(kl, rows, cols)
            k_all = jnp.concatenate([kl, cache_k[:, j]], axis=1)
            v_all = jnp.concatenate([vl, cache_v[:, j]], axis=1)
            o_lat = diff_attention(ql, k_all, v_all, lam)
            mix_ctx = da_output(o_ctx, da_g_sub[j], lam_init, da_w_out[j])
            mix_lat = da_output(o_lat, da_g_sub[j], lam_init, da_w_out[j])
        else:
            qc_, kc_, vc_, rc_, gfc, gbc = gla_project(
                h_ctx, gla_w_in[j], gla_w_gf2[j], gla_b_gf[j], gla_w_gb2[j], gla_b_gb[j])
            zeros = jnp.zeros((ctx.shape[0], GLA_HEADS, GLA_DK, GLA_DV), ctx.dtype)
            o_ctx, sf, sb = gla_bidir(qc_, kc_, vc_, gfc, gbc, zeros, zeros)
            sf_out.append(sf)
            sb_out.append(sb)
            ql, kl, vl, rl, gfl, gbl = gla_project(
                h_lat, gla_w_in[j], gla_w_gf2[j], gla_b_gf[j], gla_w_gb2[j], gla_b_gb[j])
            o_lat, _, _ = gla_bidir(ql, kl, vl, gfl, gbl, state_fwd[:, j], state_bwd[:, j])
            mix_ctx = gla_output(o_ctx, rc_, gla_g_out[j], gla_w_out[j])
            mix_lat = gla_output(o_lat, rl, gla_g_out[j], gla_w_out[j])
        ctx = ctx + m_ctx[2] * mix_ctx
        lat = lat + m_lat[2] * mix_lat
        h_ctx = modulate(ctx, norm_g[i, 1], m_ctx[3], m_ctx[4])
        h_lat = modulate(lat, norm_g[i, 1], m_lat[3], m_lat[4])
        ctx = ctx + m_ctx[5] * expert_choice_ffn(h_ctx, moe_w_router[i], moe_w1[i], moe_w3[i], moe_w2[i])
        lat = lat + m_lat[5] * expert_choice_ffn(h_lat, moe_w_router[i], moe_w1[i], moe_w3[i], moe_w2[i])

    y_prompt = rmsnorm(ctx, final_g)
    y_sample = rmsnorm(lat, final_g)
    new_k = jnp.stack(ks_out, axis=1)
    new_v = jnp.stack(vs_out, axis=1)
    new_sf = jnp.stack(sf_out, axis=1)
    new_sb = jnp.stack(sb_out, axis=1)
    return (y_prompt, y_sample, new_k, new_v, new_sf, new_sb)
```

```python
import functools
import math

import numpy as np
import jax
import jax.numpy as jnp
from jax import lax
from jax.experimental import pallas as pl
from jax.experimental.pallas import tpu as pltpu

F32 = jnp.float32
BF16 = jnp.bfloat16

D_MODEL = 1024
DEPTH = 2
SEQ = 256
DEC_SEQ = 2048
GRID_W = 64
N_TOK = 4096
DA_HEADS = 8
DA_DIM = 64
DA_VDIM = 128
ROPE_BASE = 10000.0
GLA_HEADS = 4
GLA_DK = 128
GLA_DV = 256
GLA_GATE_RANK = 16
GLA_GATE_NORM = 16.0
GLA_CHUNK = 64
N_EXPERTS = 16
CAPACITY = 512
D_EXPERT = 2048
EPS = 1e-6

LANES = 128
SUBLANES = 8
ROW_TILE = 512
FF_TILE = 512
TOK_SLABS = D_MODEL // LANES
XE_STRIDE = CAPACITY + SUBLANES
MIB = 1024 * 1024


def _cp(vmem_mib, sem=None):
    return pltpu.CompilerParams(vmem_limit_bytes=vmem_mib * MIB, dimension_semantics=sem)


def _dot(a, b):
    return jnp.dot(a, b, preferred_element_type=F32)


def _dot_nt(a, b):
    return lax.dot_general(a, b, (((1,), (1,)), ((), ())), preferred_element_type=F32)


def _dot_tn(a, b):
    return lax.dot_general(a, b, (((0,), (0,)), ((), ())), preferred_element_type=F32)


def _rmsnorm(x, g):
    return x * lax.rsqrt(jnp.mean(x * x, axis=-1, keepdims=True) + EPS) * g


def _modulate(x, g, shift, scale):
    return _rmsnorm(x, g) * (1.0 + scale) + shift


def _split3(x):
    hi = x.astype(BF16)
    r = x - hi.astype(F32)
    mid = r.astype(BF16)
    lo = (r - mid.astype(F32)).astype(BF16)
    return hi, mid, lo


def _from_slabs(ref, rows):
    return jnp.concatenate([ref[pl.ds(c, rows, stride=TOK_SLABS), :] for c in range(TOK_SLABS)], axis=1)


def _to_slabs(ref, val, rows):
    for c in range(TOK_SLABS):
        ref[pl.ds(c, rows, stride=TOK_SLABS), :] = val[:, LANES * c:LANES * (c + 1)]


def _mod_kernel(c_ref, w_ref, b_ref, o_ref):
    c = c_ref[...]
    s = c * jax.nn.sigmoid(c)
    w = w_ref[...]
    s_hi = s.astype(BF16)
    s_lo = (s - s_hi.astype(F32)).astype(BF16)
    w_hi = w.astype(BF16)
    w_lo = (w - w_hi.astype(F32)).astype(BF16)
    o_ref[...] = _dot(s_hi, w_hi) + _dot(s_hi, w_lo) + _dot(s_lo, w_hi) + b_ref[...]


def _mod_params(cvec, w_mod, b_mod):
    n6 = 6 * D_MODEL
    out = pl.pallas_call(
        _mod_kernel,
        grid=(DEPTH, 6),
        in_specs=[
            pl.BlockSpec((SUBLANES, D_MODEL), lambda i, j: (0, 0)),
            pl.BlockSpec((None, D_MODEL, D_MODEL), lambda i, j: (i, 0, j)),
            pl.BlockSpec((None, 1, D_MODEL), lambda i, j: (i, 0, j)),
        ],
        out_specs=pl.BlockSpec((None, SUBLANES, D_MODEL), lambda i, j: (i, 0, j)),
        out_shape=jax.ShapeDtypeStruct((DEPTH, SUBLANES, n6), F32),
        compiler_params=_cp(32),
        name="mod_params",
    )(cvec, w_mod, b_mod.reshape(DEPTH, 1, n6))
    return out.reshape(DEPTH, SUBLANES * 6, 1, D_MODEL)


def _mod_spec(k, lat):
    tiles_per_batch = DEC_SEQ // ROW_TILE
    if lat:
        return pl.BlockSpec((None, 1, D_MODEL), lambda i: ((1 + i // tiles_per_batch) * 6 + k, 0, 0))
    return pl.BlockSpec((None, 1, D_MODEL), lambda i: (k, 0, 0))


def _row_spec(width, dtype_rows=ROW_TILE):
    return pl.BlockSpec((dtype_rows, width), lambda i: (i, 0))


def _full_spec(shape):
    nd = len(shape)
    return pl.BlockSpec(shape, lambda i: (0,) * nd)


N_ROW_TILES = N_TOK // ROW_TILE


def _da_proj_kernel(*refs, rope, emit_f32):
    x_ref, g_ref, sh_ref, sc_ref, w_ref = refs[:5]
    pos = 5
    if rope:
        cos_ref, sin_ref = refs[pos:pos + 2]
        pos += 2
    q_ref, k_ref, v_ref = refs[pos:pos + 3]
    pos += 3
    h = _modulate(x_ref[...], g_ref[...], sh_ref[...], sc_ref[...])
    z = _dot(h.astype(BF16), w_ref[...])
    q = z[:, :D_MODEL]
    k = z[:, D_MODEL:2 * D_MODEL]
    v = z[:, 2 * D_MODEL:]
    if emit_f32:
        kf_ref, vf_ref = refs[pos:pos + 2]
        kf_ref[...] = k
        vf_ref[...] = v
    if rope:
        reps = D_MODEL // LANES
        cos = jnp.concatenate([cos_ref[...]] * reps, axis=1)
        sin = jnp.concatenate([sin_ref[...]] * reps, axis=1)
        lane = lax.broadcasted_iota(jnp.int32, (1, D_MODEL), 1)
        first = (lane & 16) == 0

        def rot(t):
            partner = jnp.where(first, pltpu.roll(t, D_MODEL - 16, 1), pltpu.roll(t, 16, 1))
            return t * cos + partner * sin

        q = rot(q)
        k = rot(k)
    q_ref[...] = (q * (DA_DIM ** -0.5)).astype(BF16)
    k_ref[...] = k.astype(BF16)
    v_ref[...] = v.astype(BF16)


def _da_proj(x, mods, norm_g, w_bf16, lat, rope_tabs=None):
    rope = rope_tabs is not None
    emit_f32 = not lat
    in_specs = [
        _row_spec(D_MODEL),
        _full_spec((1, D_MODEL)),
        _mod_spec(0, lat),
        _mod_spec(1, lat),
        _full_spec((D_MODEL, 3 * D_MODEL)),
    ]
    args = [x, norm_g, mods, mods, w_bf16]
    if rope:
        tiles_per_batch = DEC_SEQ // ROW_TILE
        tab_spec = pl.BlockSpec((ROW_TILE, LANES), lambda i: (i % tiles_per_batch, 0))
        in_specs += [tab_spec, tab_spec]
        args += list(rope_tabs)
    out_specs = [_row_spec(D_MODEL)] * 3
    out_shape = [jax.ShapeDtypeStruct((N_TOK, D_MODEL), BF16)] * 3
    if emit_f32:
        out_specs += [_row_spec(D_MODEL)] * 2
        out_shape += [jax.ShapeDtypeStruct((N_TOK, D_MODEL), F32)] * 2
    return pl.pallas_call(
        functools.partial(_da_proj_kernel, rope=rope, emit_f32=emit_f32),
        grid=(N_ROW_TILES,),
        in_specs=in_specs,
        out_specs=out_specs,
        out_shape=out_shape,
        compiler_params=_cp(48),
        name="da_proj_lat" if lat else "da_proj_ctx",
    )(*args)


def _rope_tables():
    t = jnp.arange(DEC_SEQ)
    rows = (t // GRID_W).astype(F32)
    cols = (t % GRID_W).astype(F32)
    half = DA_DIM // 4
    freqs = ROPE_BASE ** (-jnp.arange(half, dtype=F32) / half)
    ang_r = rows[:, None] * freqs
    ang_c = cols[:, None] * freqs
    cos64 = jnp.concatenate([jnp.cos(ang_r)] * 2 + [jnp.cos(ang_c)] * 2, axis=1)
    sin64 = jnp.concatenate([-jnp.sin(ang_r), jnp.sin(ang_r), -jnp.sin(ang_c), jnp.sin(ang_c)], axis=1)
    reps = LANES // DA_DIM
    return jnp.concatenate([cos64] * reps, axis=1), jnp.concatenate([sin64] * reps, axis=1)


def _attn_kernel(lq1_ref, lk1_ref, lq2_ref, lk2_ref, gs_ref, q_ref, k_ref, v_ref, o_ref, *, lam_init):
    lam = (jnp.exp(jnp.sum(lq1_ref[...] * lk1_ref[...], axis=-1, keepdims=True))
           - jnp.exp(jnp.sum(lq2_ref[...] * lk2_ref[...], axis=-1, keepdims=True)) + lam_init)
    lane = lax.broadcasted_iota(jnp.int32, (1, DA_VDIM), 1)
    first = lane < DA_DIM
    gs = gs_ref[...]
    for h in range(DA_HEADS):
        sl = slice(DA_VDIM * h, DA_VDIM * (h + 1))
        qh = q_ref[:, sl]
        kh = k_ref[:, sl]
        vh = v_ref[:, sl]
        zero = jnp.zeros_like(qh)
        s1 = _dot_nt(jnp.where(first, qh, zero), kh)
        s2 = _dot_nt(jnp.where(first, zero, qh), kh)
        p1 = jnp.exp(s1 - jnp.max(s1, axis=-1, keepdims=True))
        p2 = jnp.exp(s2 - jnp.max(s2, axis=-1, keepdims=True))
        a = p1 * (1.0 / jnp.sum(p1, axis=-1, keepdims=True)) - p2 * (lam / jnp.sum(p2, axis=-1, keepdims=True))
        o = _dot(a.astype(BF16), vh)
        o_ref[:, sl] = (_rmsnorm(o, gs) * (1.0 - lam_init)).astype(BF16)


def _attention(lam_vecs, g_sub, q, k, v, n_batch, t_q, t_k, q_tile, lam_init, name):
    nq = t_q // q_tile
    vec_spec = pl.BlockSpec((1, DA_DIM), lambda b, i: (0, 0))
    return pl.pallas_call(
        functools.partial(_attn_kernel, lam_init=lam_init),
        grid=(n_batch, nq),
        in_specs=[vec_spec] * 4 + [
            pl.BlockSpec((1, DA_VDIM), lambda b, i: (0, 0)),
            pl.BlockSpec((q_tile, D_MODEL), lambda b, i: (b * nq + i, 0)),
            pl.BlockSpec((t_k, D_MODEL), lambda b, i: (b, 0)),
            pl.BlockSpec((t_k, D_MODEL), lambda b, i: (b, 0)),
        ],
        out_specs=pl.BlockSpec((q_tile, D_MODEL), lambda b, i: (b * nq + i, 0)),
        out_shape=jax.ShapeDtypeStruct((n_batch * t_q, D_MODEL), BF16),
        compiler_params=_cp(56),
        name=name,
    )(*lam_vecs, g_sub, q, k, v)


def _outproj_kernel(u_ref, w_ref, x_ref, gate_ref, g_ref, sh_ref, sc_ref, wr_ref, x1_ref, hs_ref, lt_ref):
    x1 = x_ref[...] + gate_ref[...] * _dot(u_ref[...], w_ref[...])
    x1_ref[...] = x1
    h2 = _modulate(x1, g_ref[...], sh_ref[...], sc_ref[...])
    _to_slabs(hs_ref, h2, ROW_TILE)
    lt_ref[...] = _dot_nt(wr_ref[...], h2.astype(BF16))


def _outproj(u, w_bf16, x, mods, norm_g, wr_t, lat, name):
    return pl.pallas_call(
        _outproj_kernel,
        grid=(N_ROW_TILES,),
        in_specs=[
            _row_spec(D_MODEL),
            _full_spec((D_MODEL, D_MODEL)),
            _row_spec(D_MODEL),
            _mod_spec(2, lat),
            _full_spec((1, D_MODEL)),
            _mod_spec(3, lat),
            _mod_spec(4, lat),
            _full_spec((LANES, D_MODEL)),
        ],
        out_specs=[
            _row_spec(D_MODEL),
            pl.BlockSpec((ROW_TILE * TOK_SLABS, LANES), lambda i: (i, 0)),
            pl.BlockSpec((LANES, ROW_TILE), lambda i: (0, i)),
        ],
        out_shape=[
            jax.ShapeDtypeStruct((N_TOK, D_MODEL), F32),
            jax.ShapeDtypeStruct((N_TOK * TOK_SLABS, LANES), F32),
            jax.ShapeDtypeStruct((LANES, N_TOK), F32),
        ],
        compiler_params=_cp(48),
        name=name,
    )(u, w_bf16, x, mods, norm_g, mods, mods, wr_t)


N_TOK_BLOCKS = N_TOK // LANES
STACK_ROWS = N_TOK_BLOCKS * N_EXPERTS


def _router_kernel(lt_ref, tri_ref, blk_ref, idx_ref, gate_ref):
    lt = lt_ref[...]
    e = jnp.exp(lt - jnp.max(lt, axis=0, keepdims=True))
    aff = e / jnp.sum(e, axis=0, keepdims=True)
    keys = pltpu.bitcast(aff, jnp.int32)

    def search(_, c):
        lo, hi = c
        mid = lo + ((hi - lo + 1) >> 1)
        cnt = jnp.sum((keys >= mid).astype(F32), axis=1, keepdims=True)
        ok = cnt >= float(CAPACITY)
        return jnp.where(ok, mid, lo), jnp.where(ok, hi, mid - 1)

    lo0 = jnp.zeros((N_EXPERTS, 1), jnp.int32)
    hi0 = jnp.full((N_EXPERTS, 1), 0x7F800000, jnp.int32)
    thr, _ = lax.fori_loop(0, 31, search, (lo0, hi0))

    tri = tri_ref[...]
    blk = blk_ref[...]

    def stack(x):
        return jnp.concatenate([x[:, LANES * b:LANES * (b + 1)] for b in range(N_TOK_BLOCKS)], axis=0)

    def unstack(x):
        return jnp.concatenate([x[N_EXPERTS * b:N_EXPERTS * (b + 1), :] for b in range(N_TOK_BLOCKS)], axis=1)

    def cumsum_stacked(mask_st):
        win = _dot(mask_st.astype(BF16), tri)
        tot = win[:, LANES - 1:LANES]
        off = _dot(blk, jnp.broadcast_to(tot, (STACK_ROWS, LANES)).astype(BF16))[:, :1]
        return win, off, tot

    gt = keys > thr
    eq = keys == thr
    need = float(CAPACITY) - jnp.sum(gt.astype(F32), axis=1, keepdims=True)
    eq_f = eq.astype(F32)
    ewin, eoff, _ = cumsum_stacked(stack(eq_f))
    eq_before = unstack(ewin + eoff) - eq_f
    sel = jnp.logical_or(gt, jnp.logical_and(eq, eq_before < need))
    cwin, coff, ctot = cumsum_stacked(stack(sel.astype(F32)))
    cend = coff + ctot
    cwin_b = cwin.astype(BF16)
    a_hi, a_mid, a_lo = _split3(stack(aff))

    row = lax.broadcasted_iota(jnp.int32, (STACK_ROWS, 1), 0)
    row_f = row.astype(F32)
    slot = lax.broadcasted_iota(jnp.int32, (1, CAPACITY), 1).astype(F32)
    lane_f = lax.broadcasted_iota(jnp.int32, (LANES, 1), 0).astype(F32)

    def per_expert(ex, carry):
        mine = (row & (N_EXPERTS - 1)) == ex
        before = jnp.logical_and(mine, cend <= slot)
        nblk = jnp.sum(before.astype(F32), axis=0, keepdims=True)
        base = jnp.sum(jnp.where(before, ctot, 0.0), axis=0, keepdims=True)
        target = nblk * float(N_EXPERTS) + ex.astype(F32)
        pick = jnp.where(row_f == target, 1.0, 0.0).astype(BF16)
        cnt_in = _dot_tn(cwin_b, pick)
        lane_idx = jnp.sum((cnt_in <= slot - base).astype(F32), axis=0, keepdims=True)
        idx_ref[pl.ds(ex, 1), :] = (nblk * float(LANES) + lane_idx).astype(jnp.int32)
        aff_blk = _dot_tn(a_hi, pick) + _dot_tn(a_mid, pick) + _dot_tn(a_lo, pick)
        gate_ref[pl.ds(ex, 1), :] = jnp.sum(jnp.where(lane_f == lane_idx, aff_blk, 0.0), axis=0, keepdims=True)
        return carry

    lax.fori_loop(0, N_EXPERTS, per_expert, 0)


def _cumsum_consts():
    l = np.arange(LANES)
    tri = (l[:, None] <= l[None, :]).astype(np.float32)
    r = np.arange(STACK_ROWS)
    same = (r[:, None] % N_EXPERTS) == (r[None, :] % N_EXPERTS)
    earlier = (r[None, :] // N_EXPERTS) < (r[:, None] // N_EXPERTS)
    blk = (same & earlier).astype(np.float32)
    return jnp.asarray(tri, BF16), jnp.asarray(blk, BF16)


def _router(logits_t, tri, blk, name):
    return pl.pallas_call(
        _router_kernel,
        grid=(1,),
        in_specs=[
            pl.BlockSpec((N_EXPERTS, N_TOK), lambda i: (0, 0)),
            _full_spec((LANES, LANES)),
            _full_spec((STACK_ROWS, STACK_ROWS)),
        ],
        out_specs=[_full_spec((N_EXPERTS, CAPACITY))] * 2,
        out_shape=[jax.ShapeDtypeStruct((N_EXPERTS, CAPACITY), jnp.int32),
                   jax.ShapeDtypeStruct((N_EXPERTS, CAPACITY), F32)],
        compiler_params=_cp(48),
        name=name,
    )(logits_t, tri, blk)


GATHER_UNROLL = 8


def _ffn_kernel(idx_ref, hs_ref, gate_ref, w1_ref, w3_ref, w2_ref, ye_ref, tile_ref, xe_ref, acc_ref):
    ex = pl.program_id(0)
    f = pl.program_id(1)

    @pl.when(f == 0)
    def _():
        def body(i, carry):
            for u in range(GATHER_UNROLL):
                s = i * GATHER_UNROLL + u
                t = idx_ref[ex * CAPACITY + s]
                slab = hs_ref[pl.ds(pl.multiple_of(t * TOK_SLABS, TOK_SLABS), TOK_SLABS), :]
                tile_ref[pl.ds(s, TOK_SLABS, stride=XE_STRIDE), :] = slab
            return carry

        lax.fori_loop(0, CAPACITY // GATHER_UNROLL, body, 0)
        xe_ref[...] = jnp.concatenate(
            [tile_ref[pl.ds(c * XE_STRIDE, CAPACITY), :] for c in range(TOK_SLABS)], axis=1).astype(BF16)
        acc_ref[...] = jnp.zeros_like(acc_ref)

    x = xe_ref[...]
    h1 = _dot(x, w1_ref[...].astype(BF16))
    h3 = _dot(x, w3_ref[...].astype(BF16))
    hid = (h1 * jax.nn.sigmoid(h1) * h3).astype(BF16)
    acc_ref[...] += _dot(hid, w2_ref[...].astype(BF16))

    @pl.when(f == pl.num_programs(1) - 1)
    def _():
        _to_slabs(ye_ref, acc_ref[...] * gate_ref[...], CAPACITY)


def _expert_ffn(idx_flat, hs, gate_col, w1, w3, w2, name):
    nf = D_EXPERT // FF_TILE
    return pl.pallas_call(
        _ffn_kernel,
        grid_spec=pltpu.PrefetchScalarGridSpec(
            num_scalar_prefetch=1,
            grid=(N_EXPERTS, nf),
            in_specs=[
                pl.BlockSpec((N_TOK * TOK_SLABS, LANES), lambda e, f, idx: (0, 0), pipeline_mode=pl.Buffered(1)),
                pl.BlockSpec((None, CAPACITY, 1), lambda e, f, idx: (e, 0, 0)),
                pl.BlockSpec((None, D_MODEL, FF_TILE), lambda e, f, idx: (e, 0, f)),
                pl.BlockSpec((None, D_MODEL, FF_TILE), lambda e, f, idx: (e, 0, f)),
                pl.BlockSpec((None, FF_TILE, D_MODEL), lambda e, f, idx: (e, f, 0)),
            ],
            out_specs=pl.BlockSpec((None, CAPACITY * TOK_SLABS, LANES), lambda e, f, idx: (e, 0, 0)),
            scratch_shapes=[
                pltpu.VMEM((TOK_SLABS * XE_STRIDE, LANES), F32),
                pltpu.VMEM((CAPACITY, D_MODEL), BF16),
                pltpu.VMEM((CAPACITY, D_MODEL), F32),
            ],
        ),
        out_shape=jax.ShapeDtypeStruct((N_EXPERTS, CAPACITY * TOK_SLABS, LANES), F32),
        compiler_params=_cp(56),
        name=name,
    )(idx_flat, hs, gate_col, w1, w3, w2)


SCATTER_UNROLL = 8


def _combine_kernel(idx_ref, ye_ref, y_ref):
    ex = pl.program_id(0)

    @pl.when(ex == 0)
    def _():
        y_ref[...] = jnp.zeros_like(y_ref)

    def body(i, carry):
        upd = []
        for u in range(SCATTER_UNROLL):
            s = i * SCATTER_UNROLL + u
            t = idx_ref[ex * CAPACITY + s]
            rows = pl.ds(pl.multiple_of(t * TOK_SLABS, TOK_SLABS), TOK_SLABS)
            src = ye_ref[pl.ds(pl.multiple_of(s * TOK_SLABS, TOK_SLABS), TOK_SLABS), :]
            upd.append((rows, y_ref[rows, :] + src))
        for rows, val in upd:
            y_ref[rows, :] = val
        return carry

    lax.fori_loop(0, CAPACITY // SCATTER_UNROLL, body, 0)


def _combine(idx_flat, ye, name):
    return pl.pallas_call(
        _combine_kernel,
        grid_spec=pltpu.PrefetchScalarGridSpec(
            num_scalar_prefetch=1,
            grid=(N_EXPERTS,),
            in_specs=[pl.BlockSpec((None, CAPACITY * TOK_SLABS, LANES), lambda e, idx: (e, 0, 0))],
            out_specs=pl.BlockSpec((N_TOK * TOK_SLABS, LANES), lambda e, idx: (0, 0)),
        ),
        out_shape=jax.ShapeDtypeStruct((N_TOK * TOK_SLABS, LANES), F32),
        compiler_params=_cp(56),
        name=name,
    )(idx_flat, ye)


def _moe(hs, logits_t, tri, blk, w1, w3, w2, tag):
    idx, gate = _router(logits_t, tri, blk, "router_" + tag)
    idx_flat = idx.reshape(N_EXPERTS * CAPACITY)
    ye = _expert_ffn(idx_flat, hs, gate.reshape(N_EXPERTS, CAPACITY, 1), w1, w3, w2, "ffn_" + tag)
    return _combine(idx_flat, ye, "combine_" + tag)


def _gla_proj_kernel(xp_ref, yt_ref, gp_ref, g_ref, sh_ref, sc_ref, w_ref, wg_ref, bg_ref,
                     x_ref, q_ref, k_ref, v_ref, r_ref, lgf_ref, lgb_ref):
    x = xp_ref[...] + gp_ref[...] * _from_slabs(yt_ref, ROW_TILE)
    x_ref[...] = x
    h = _modulate(x, g_ref[...], sh_ref[...], sc_ref[...])
    z = _dot(h.astype(BF16), w_ref[...])
    nq = GLA_HEADS * GLA_DK
    nv = GLA_HEADS * GLA_DV
    q_ref[...] = z[:, :nq] * (GLA_DK ** -0.5)
    k_ref[...] = z[:, nq:2 * nq]
    v_ref[...] = z[:, 2 * nq:2 * nq + nv].astype(BF16)
    r_ref[...] = z[:, 2 * nq + nv:2 * nq + 2 * nv].astype(BF16)
    zg = z[:, 2 * nq + 2 * nv:].astype(BF16)
    a = _dot(zg, wg_ref[...]) + bg_ref[...]
    ls = (jnp.minimum(a, 0.0) - jnp.log1p(jnp.exp(-jnp.abs(a)))) * (1.0 / GLA_GATE_NORM)
    lgf_ref[...] = ls[:, :nq]
    lgb_ref[...] = ls[:, nq:]


def _gla_proj(x_prev, y_slabs, mods_prev, mods, norm_g, w_bf16, wg, bg, lat, name):
    nq = GLA_HEADS * GLA_DK
    nv = GLA_HEADS * GLA_DV
    n_in = w_bf16.shape[1]
    return pl.pallas_call(
        _gla_proj_kernel,
        grid=(N_ROW_TILES,),
        in_specs=[
            _row_spec(D_MODEL),
            pl.BlockSpec((ROW_TILE * TOK_SLABS, LANES), lambda i: (i, 0)),
            _mod_spec(5, lat),
            _full_spec((1, D_MODEL)),
            _mod_spec(0, lat),
            _mod_spec(1, lat),
            _full_spec((D_MODEL, n_in)),
            _full_spec((LANES, 2 * nq)),
            _full_spec((1, 2 * nq)),
        ],
        out_specs=[_row_spec(D_MODEL), _row_spec(nq), _row_spec(nq), _row_spec(nv), _row_spec(nv),
                   _row_spec(nq), _row_spec(nq)],
        out_shape=[
            jax.ShapeDtypeStruct((N_TOK, D_MODEL), F32),
            jax.ShapeDtypeStruct((N_TOK, nq), F32),
            jax.ShapeDtypeStruct((N_TOK, nq), F32),
            jax.ShapeDtypeStruct((N_TOK, nv), BF16),
            jax.ShapeDtypeStruct((N_TOK, nv), BF16),
            jax.ShapeDtypeStruct((N_TOK, nq), F32),
            jax.ShapeDtypeStruct((N_TOK, nq), F32),
        ],
        compiler_params=_cp(48),
        name=name,
    )(x_prev, y_slabs, mods_prev, norm_g, mods, mods, w_bf16, wg, bg)


GLA_LEVELS = (32, 16, 8)


def _gla_chunk(qc, kc, vc, gc, st, tri, fwd):
    c = GLA_CHUNK
    g_hi, g_mid, g_lo = _split3(gc)
    b = _dot(tri, g_hi) + _dot(tri, g_mid) + _dot(tri, g_lo)
    btot = b[c - 1:c] if fwd else b[0:1]
    o = _dot_nt((qc * jnp.exp(b)).astype(BF16), st.astype(BF16))

    row = lax.broadcasted_iota(jnp.int32, (c, 1), 0)
    col = lax.broadcasted_iota(jnp.int32, (1, c), 1)
    a = jnp.zeros((c, c), F32)
    for g in GLA_LEVELS:
        odd = ((row >> int(math.log2(g))) & 1) == 1
        later = odd if fwd else jnp.logical_not(odd)
        refs = []
        for p in range(c // (2 * g)):
            r0 = 2 * g * p + (g - 1 if fwd else g)
            refs.append(jnp.broadcast_to(b[r0:r0 + 1], (2 * g, GLA_DK)))
        ref = jnp.concatenate(refs, axis=0) if len(refs) > 1 else refs[0]
        qt = jnp.where(later, qc * jnp.exp(jnp.where(later, b - ref, 0.0)), 0.0)
        kt = jnp.where(later, 0.0, kc * jnp.exp(jnp.where(later, 0.0, ref - b)))
        same_parent = (row >> int(math.log2(2 * g))) == (col >> int(math.log2(2 * g)))
        a = a + jnp.where(same_parent, _dot_nt(qt.astype(BF16), kt.astype(BF16)), 0.0)

    sub = lax.broadcasted_iota(jnp.int32, (SUBLANES, 1), 0)
    strips = []
    for blk in range(c // SUBLANES):
        r0 = SUBLANES * blk
        qb = qc[r0:r0 + SUBLANES]
        bb = b[r0:r0 + SUBLANES]
        strip = jnp.zeros((SUBLANES, c), F32)
        for jj in range(SUBLANES):
            j = r0 + jj
            cond = (sub >= jj) if fwd else (sub <= jj)
            e = jnp.exp(jnp.where(cond, bb - b[j:j + 1], 0.0))
            t = jnp.where(cond, qb * kc[j:j + 1] * e, 0.0)
            strip = jnp.where(col == j, jnp.sum(t, axis=1, keepdims=True), strip)
        strips.append(strip)
    a = a + jnp.concatenate(strips, axis=0)

    o = o + _dot(a.astype(BF16), vc)
    kd = (kc * jnp.exp(btot - b)).astype(BF16)
    st_new = st * jnp.exp(btot) + _dot_tn(vc, kd)
    return o, st_new


def _gla_kernel(q_ref, k_ref, v_ref, r_ref, gf_ref, gb_ref, s0f_ref, s0b_ref, go_ref, trif_ref, trib_ref,
                u_ref, sf_ref, sb_ref, o_acc, st_ref, *, t_len):
    n = t_len // GLA_CHUNK

    def run(g_ref, s0_ref, s_out_ref, tri_ref, fwd):
        st_ref[...] = s0_ref[...].T
        tri = tri_ref[...]

        def body(i, carry):
            ci = i if fwd else n - 1 - i
            rows = pl.ds(pl.multiple_of(ci * GLA_CHUNK, GLA_CHUNK), GLA_CHUNK)
            o, st_new = _gla_chunk(q_ref[rows, :], k_ref[rows, :], v_ref[rows, :], g_ref[rows, :],
                                   st_ref[...], tri, fwd)
            st_ref[...] = st_new
            if fwd:
                o_acc[rows, :] = o
            else:
                o_acc[rows, :] = o_acc[rows, :] + o
            return carry

        lax.fori_loop(0, n, body, 0)
        s_out_ref[...] = st_ref[...].T

    run(gf_ref, s0f_ref, sf_ref, trif_ref, True)
    run(gb_ref, s0b_ref, sb_ref, trib_ref, False)
    r = r_ref[...].astype(F32)
    u_ref[...] = (_rmsnorm(o_acc[...], go_ref[...]) * (r * jax.nn.sigmoid(r))).astype(BF16)


def _gla_tri():
    i = np.arange(GLA_CHUNK)
    fwd = (i[None, :] <= i[:, None]).astype(np.float32)
    bwd = (i[None, :] >= i[:, None]).astype(np.float32)
    return jnp.asarray(fwd, BF16), jnp.asarray(bwd, BF16)


def _gla(q, k, v, r, lgf, lgb, s0f, s0b, g_out, n_batch, t_len, name):
    trif, trib = _gla_tri()
    qk_spec = pl.BlockSpec((None, t_len, GLA_DK), lambda b, h: (b, 0, h))
    v_spec = pl.BlockSpec((None, t_len, GLA_DV), lambda b, h: (b, 0, h))
    s_spec = pl.BlockSpec((None, None, GLA_DK, GLA_DV), lambda b, h: (b, h, 0, 0))
    const = lambda shape: pl.BlockSpec(shape, lambda b, h: (0,) * len(shape))
    s_shape = jax.ShapeDtypeStruct((n_batch, GLA_HEADS, GLA_DK, GLA_DV), F32)
    return pl.pallas_call(
        functools.partial(_gla_kernel, t_len=t_len),
        grid=(n_batch, GLA_HEADS),
        in_specs=[qk_spec, qk_spec, v_spec, v_spec, qk_spec, qk_spec, s_spec, s_spec,
                  const((1, GLA_DV)), const((GLA_CHUNK, GLA_CHUNK)), const((GLA_CHUNK, GLA_CHUNK))],
        out_specs=[v_spec, s_spec, s_spec],
        out_shape=[jax.ShapeDtypeStruct((n_batch, t_len, GLA_HEADS * GLA_DV), BF16), s_shape, s_shape],
        scratch_shapes=[pltpu.VMEM((t_len, GLA_DV), F32), pltpu.VMEM((GLA_DV, GLA_DK), F32)],
        compiler_params=_cp(48),
        name=name,
    )(q, k, v, r, lgf, lgb, s0f, s0b, g_out, trif, trib)


def _final_kernel(xp_ref, yt_ref, gp_ref, fg_ref, o_ref):
    x = xp_ref[...] + gp_ref[...] * _from_slabs(yt_ref, ROW_TILE)
    o_ref[...] = _rmsnorm(x, fg_ref[...])


def _final(x_prev, y_slabs, mods_prev, final_g, lat, name):
    return pl.pallas_call(
        _final_kernel,
        grid=(N_ROW_TILES,),
        in_specs=[
            _row_spec(D_MODEL),
            pl.BlockSpec((ROW_TILE * TOK_SLABS, LANES), lambda i: (i, 0)),
            _mod_spec(5, lat),
            _full_spec((1, D_MODEL)),
        ],
        out_specs=_row_spec(D_MODEL),
        out_shape=jax.ShapeDtypeStruct((N_TOK, D_MODEL), F32),
        compiler_params=_cp(32),
        name=name,
    )(x_prev, y_slabs, mods_prev, final_g)


def kernel(x_prompt, x_sample, cache_k, cache_v, state_fwd, state_bwd, c, c_ctx, w_mod, b_mod, norm_g,
           da_w_in, da_w_out, da_lam_q1, da_lam_k1, da_lam_q2, da_lam_k2, da_g_sub, gla_w_in, gla_w_gf2,
           gla_b_gf, gla_w_gb2, gla_b_gb, gla_g_out, gla_w_out, moe_w_router, moe_w1, moe_w3, moe_w2, final_g):
    n_ctx_b = x_prompt.shape[0]
    n_lat_b = x_sample.shape[0]
    x_ctx = x_prompt.reshape(N_TOK, D_MODEL)
    x_lat = x_sample.reshape(N_TOK, D_MODEL)

    cvec = jnp.zeros((SUBLANES, D_MODEL), F32).at[0].set(c_ctx).at[1:1 + n_lat_b].set(c)
    mods = _mod_params(cvec, w_mod, b_mod)
    tri, blk = _cumsum_consts()
    wr_t = [jnp.zeros((LANES, D_MODEL), BF16).at[:N_EXPERTS].set(moe_w_router[i].T.astype(BF16))
            for i in range(DEPTH)]
    ng = norm_g.reshape(DEPTH, 2, 1, D_MODEL)

    lam_init = 0.8 - 0.6 * math.exp(-0.3 * 0)
    w_in = da_w_in[0].astype(BF16)
    w_out = da_w_out[0].astype(BF16)
    lam_vecs = [v[0].reshape(1, DA_DIM) for v in (da_lam_q1, da_lam_k1, da_lam_q2, da_lam_k2)]
    g_sub = da_g_sub[0].reshape(1, DA_VDIM)

    qc, kc, vc, kf, vf = _da_proj(x_ctx, mods[0], ng[0, 0], w_in, lat=False)
    ql, kl, vl = _da_proj(x_lat, mods[0], ng[0, 0], w_in, lat=True, rope_tabs=_rope_tables())
    past = cache_k.shape[2]
    k_all = jnp.concatenate([kl.reshape(n_lat_b, DEC_SEQ, D_MODEL),
                             cache_k[:, 0].reshape(n_lat_b, past, D_MODEL).astype(BF16)], axis=1)
    v_all = jnp.concatenate([vl.reshape(n_lat_b, DEC_SEQ, D_MODEL),
                             cache_v[:, 0].reshape(n_lat_b, past, D_MODEL).astype(BF16)], axis=1)
    t_all = DEC_SEQ + past
    u_ctx = _attention(lam_vecs, g_sub, qc, kc, vc, n_ctx_b, SEQ, SEQ, SEQ, lam_init, "attn_ctx")
    u_lat = _attention(lam_vecs, g_sub, ql, k_all.reshape(n_lat_b * t_all, D_MODEL),
                       v_all.reshape(n_lat_b * t_all, D_MODEL), n_lat_b, DEC_SEQ, t_all, 256, lam_init, "attn_lat")

    x1_ctx, hs_ctx, lt_ctx = _outproj(u_ctx, w_out, x_ctx, mods[0], ng[0, 1], wr_t[0], False, "outproj0_ctx")
    x1_lat, hs_lat, lt_lat = _outproj(u_lat, w_out, x_lat, mods[0], ng[0, 1], wr_t[0], True, "outproj0_lat")
    y_ctx = _moe(hs_ctx, lt_ctx, tri, blk, moe_w1[0], moe_w3[0], moe_w2[0], "l0_ctx")
    y_lat = _moe(hs_lat, lt_lat, tri, blk, moe_w1[0], moe_w3[0], moe_w2[0], "l0_lat")

    nq = GLA_HEADS * GLA_DK
    gla_in = gla_w_in.shape[2]
    n_pad = (-gla_in) % LANES
    w_in1 = jnp.pad(gla_w_in[0], ((0, 0), (0, n_pad))).astype(BF16)
    wg = jnp.zeros((LANES, 2 * nq), F32)
    wg = wg.at[:GLA_GATE_RANK, :nq].set(gla_w_gf2[0]).at[GLA_GATE_RANK:2 * GLA_GATE_RANK, nq:].set(gla_w_gb2[0])
    wg = wg.astype(BF16)
    bg = jnp.concatenate([gla_b_gf[0], gla_b_gb[0]]).reshape(1, 2 * nq)
    w_out1 = gla_w_out[0].astype(BF16)
    g_out = gla_g_out[0].reshape(1, GLA_DV)

    def gla_side(x1, y, lat, n_b, t_len, s0f, s0b, tag):
        x2, q, k, v, r, lgf, lgb = _gla_proj(x1, y, mods[0], mods[1], ng[1, 0], w_in1, wg, bg, lat, "gla_proj_" + tag)
        sh3 = lambda a: a.reshape(n_b, t_len, a.shape[-1])
        u, sf, sb = _gla(sh3(q), sh3(k), sh3(v), sh3(r), sh3(lgf), sh3(lgb), s0f, s0b, g_out, n_b, t_len, "gla_" + tag)
        return x2, u.reshape(N_TOK, D_MODEL), sf, sb

    zeros_s = jnp.zeros((n_ctx_b, GLA_HEADS, GLA_DK, GLA_DV), F32)
    x2_ctx, ug_ctx, sf, sb = gla_side(x1_ctx, y_ctx, False, n_ctx_b, SEQ, zeros_s, zeros_s, "ctx")
    x2_lat, ug_lat, _, _ = gla_side(x1_lat, y_lat, True, n_lat_b, DEC_SEQ, state_fwd[:, 0], state_bwd[:, 0], "lat")

    x3_ctx, hs_ctx, lt_ctx = _outproj(ug_ctx, w_out1, x2_ctx, mods[1], ng[1, 1], wr_t[1], False, "outproj1_ctx")
    x3_lat, hs_lat, lt_lat = _outproj(ug_lat, w_out1, x2_lat, mods[1], ng[1, 1], wr_t[1], True, "outproj1_lat")
    y_ctx = _moe(hs_ctx, lt_ctx, tri, blk, moe_w1[1], moe_w3[1], moe_w2[1], "l1_ctx")
    y_lat = _moe(hs_lat, lt_lat, tri, blk, moe_w1[1], moe_w3[1], moe_w2[1], "l1_lat")

    fg = final_g.reshape(1, D_MODEL)
    y_prompt = _final(x3_ctx, y_ctx, mods[1], fg, False, "final_ctx").reshape(x_prompt.shape)
    y_sample = _final(x3_lat, y_lat, mods[1], fg, True, "final_lat").reshape(x_sample.shape)
    new_k = kf.reshape(n_ctx_b, 1, SEQ, 2 * DA_HEADS, DA_DIM)
    new_v = vf.reshape(n_ctx_b, 1, SEQ, DA_HEADS, DA_VDIM)
    return (y_prompt, y_sample, new_k, new_v, sf[:, None], sb[:, None])
```

```python
import functools
import math

import numpy as np
import jax
import jax.numpy as jnp
from jax import lax
from jax.experimental import pallas as pl
from jax.experimental.pallas import tpu as pltpu

F32 = jnp.float32
BF16 = jnp.bfloat16

D_MODEL = 1024
DEPTH = 2
SEQ = 256
DEC_SEQ = 2048
GRID_W = 64
N_TOK = 4096
DA_HEADS = 8
DA_DIM = 64
DA_VDIM = 128
ROPE_BASE = 10000.0
GLA_HEADS = 4
GLA_DK = 128
GLA_DV = 256
GLA_GATE_RANK = 16
GLA_GATE_NORM = 16.0
GLA_CHUNK = 64
N_EXPERTS = 16
CAPACITY = 512
D_EXPERT = 2048
EPS = 1e-6
F32_MIN_NORMAL = 2.0 ** -126

LANES = 128
SUBLANES = 8
ROW_TILE = 512
FF_TILE = 512
TOK_SLABS = D_MODEL // LANES
XE_STRIDE = CAPACITY + SUBLANES
MIB = 1024 * 1024


def _cp(vmem_mib, sem=None):
    return pltpu.CompilerParams(vmem_limit_bytes=vmem_mib * MIB, dimension_semantics=sem)


def _dot(a, b):
    return jnp.dot(a, b, preferred_element_type=F32)


def _dot_nt(a, b):
    return lax.dot_general(a, b, (((1,), (1,)), ((), ())), preferred_element_type=F32)


def _dot_tn(a, b):
    return lax.dot_general(a, b, (((0,), (0,)), ((), ())), preferred_element_type=F32)


def _rmsnorm(x, g):
    return x * lax.rsqrt(jnp.mean(x * x, axis=-1, keepdims=True) + EPS) * g


def _modulate(x, g, shift, scale):
    return _rmsnorm(x, g) * (1.0 + scale) + shift


def _split3(x):
    hi = x.astype(BF16)
    r = x - hi.astype(F32)
    mid = r.astype(BF16)
    lo = (r - mid.astype(F32)).astype(BF16)
    return hi, mid, lo


def _from_slabs(ref, rows):
    return jnp.concatenate([ref[pl.ds(c, rows, stride=TOK_SLABS), :] for c in range(TOK_SLABS)], axis=1)


def _to_slabs(ref, val, rows):
    for c in range(TOK_SLABS):
        ref[pl.ds(c, rows, stride=TOK_SLABS), :] = val[:, LANES * c:LANES * (c + 1)]


def _mod_kernel(c_ref, w_ref, b_ref, o_ref):
    c = c_ref[...]
    s = c * jax.nn.sigmoid(c)
    w = w_ref[...]
    s_hi = s.astype(BF16)
    s_lo = (s - s_hi.astype(F32)).astype(BF16)
    w_hi = w.astype(BF16)
    w_lo = (w - w_hi.astype(F32)).astype(BF16)
    o_ref[...] = _dot(s_hi, w_hi) + _dot(s_hi, w_lo) + _dot(s_lo, w_hi) + b_ref[...]


def _mod_params(cvec, w_mod, b_mod):
    n6 = 6 * D_MODEL
    out = pl.pallas_call(
        _mod_kernel,
        grid=(DEPTH, 6),
        in_specs=[
            pl.BlockSpec((SUBLANES, D_MODEL), lambda i, j: (0, 0)),
            pl.BlockSpec((None, D_MODEL, D_MODEL), lambda i, j: (i, 0, j)),
            pl.BlockSpec((None, 1, D_MODEL), lambda i, j: (i, 0, j)),
        ],
        out_specs=pl.BlockSpec((None, SUBLANES, D_MODEL), lambda i, j: (i, 0, j)),
        out_shape=jax.ShapeDtypeStruct((DEPTH, SUBLANES, n6), F32),
        compiler_params=_cp(32),
        name="mod_params",
    )(cvec, w_mod, b_mod.reshape(DEPTH, 1, n6))
    return out.reshape(DEPTH, SUBLANES * 6, 1, D_MODEL)


def _mod_spec(k, lat):
    tiles_per_batch = DEC_SEQ // ROW_TILE
    if lat:
        return pl.BlockSpec((None, 1, D_MODEL), lambda i: ((1 + i // tiles_per_batch) * 6 + k, 0, 0))
    return pl.BlockSpec((None, 1, D_MODEL), lambda i: (k, 0, 0))


def _row_spec(width, dtype_rows=ROW_TILE):
    return pl.BlockSpec((dtype_rows, width), lambda i: (i, 0))


def _full_spec(shape):
    nd = len(shape)
    return pl.BlockSpec(shape, lambda i: (0,) * nd)


N_ROW_TILES = N_TOK // ROW_TILE


def _da_proj_kernel(*refs, rope, emit_f32):
    x_ref, g_ref, sh_ref, sc_ref, w_ref = refs[:5]
    pos = 5
    if rope:
        cos_ref, sin_ref = refs[pos:pos + 2]
        pos += 2
    q_ref, k_ref, v_ref = refs[pos:pos + 3]
    pos += 3
    h = _modulate(x_ref[...], g_ref[...], sh_ref[...], sc_ref[...])
    z = _dot(h.astype(BF16), w_ref[...])
    q = z[:, :D_MODEL]
    k = z[:, D_MODEL:2 * D_MODEL]
    v = z[:, 2 * D_MODEL:]
    if emit_f32:
        kf_ref, vf_ref = refs[pos:pos + 2]
        kf_ref[...] = k
        vf_ref[...] = v
    if rope:
        reps = D_MODEL // LANES
        cos = jnp.concatenate([cos_ref[...]] * reps, axis=1)
        sin = jnp.concatenate([sin_ref[...]] * reps, axis=1)
        lane = lax.broadcasted_iota(jnp.int32, (1, D_MODEL), 1)
        first = (lane & 16) == 0

        def rot(t):
            partner = jnp.where(first, pltpu.roll(t, D_MODEL - 16, 1), pltpu.roll(t, 16, 1))
            return t * cos + partner * sin

        q = rot(q)
        k = rot(k)
    q_ref[...] = (q * (DA_DIM ** -0.5)).astype(BF16)
    k_ref[...] = k.astype(BF16)
    v_ref[...] = v.astype(BF16)


def _da_proj(x, mods, norm_g, w_bf16, lat, rope_tabs=None):
    rope = rope_tabs is not None
    emit_f32 = not lat
    in_specs = [
        _row_spec(D_MODEL),
        _full_spec((1, D_MODEL)),
        _mod_spec(0, lat),
        _mod_spec(1, lat),
        _full_spec((D_MODEL, 3 * D_MODEL)),
    ]
    args = [x, norm_g, mods, mods, w_bf16]
    if rope:
        tiles_per_batch = DEC_SEQ // ROW_TILE
        tab_spec = pl.BlockSpec((ROW_TILE, LANES), lambda i: (i % tiles_per_batch, 0))
        in_specs += [tab_spec, tab_spec]
        args += list(rope_tabs)
    out_specs = [_row_spec(D_MODEL)] * 3
    out_shape = [jax.ShapeDtypeStruct((N_TOK, D_MODEL), BF16)] * 3
    if emit_f32:
        out_specs += [_row_spec(D_MODEL)] * 2
        out_shape += [jax.ShapeDtypeStruct((N_TOK, D_MODEL), F32)] * 2
    return pl.pallas_call(
        functools.partial(_da_proj_kernel, rope=rope, emit_f32=emit_f32),
        grid=(N_ROW_TILES,),
        in_specs=in_specs,
        out_specs=out_specs,
        out_shape=out_shape,
        compiler_params=_cp(48),
        name="da_proj_lat" if lat else "da_proj_ctx",
    )(*args)


def _rope_tables():
    t = jnp.arange(DEC_SEQ)
    rows = (t // GRID_W).astype(F32)
    cols = (t % GRID_W).astype(F32)
    half = DA_DIM // 4
    freqs = ROPE_BASE ** (-jnp.arange(half, dtype=F32) / half)
    ang_r = rows[:, None] * freqs
    ang_c = cols[:, None] * freqs
    cos64 = jnp.concatenate([jnp.cos(ang_r)] * 2 + [jnp.cos(ang_c)] * 2, axis=1)
    sin64 = jnp.concatenate([-jnp.sin(ang_r), jnp.sin(ang_r), -jnp.sin(ang_c), jnp.sin(ang_c)], axis=1)
    reps = LANES // DA_DIM
    return jnp.concatenate([cos64] * reps, axis=1), jnp.concatenate([sin64] * reps, axis=1)


def _attn_kernel(lq1_ref, lk1_ref, lq2_ref, lk2_ref, gs_ref, q_ref, k_ref, v_ref, o_ref, *, lam_init):
    lam = (jnp.exp(jnp.sum(lq1_ref[...] * lk1_ref[...], axis=-1, keepdims=True))
           - jnp.exp(jnp.sum(lq2_ref[...] * lk2_ref[...], axis=-1, keepdims=True)) + lam_init)
    lane = lax.broadcasted_iota(jnp.int32, (1, DA_VDIM), 1)
    first = lane < DA_DIM
    gs = gs_ref[...]
    for h in range(DA_HEADS):
        sl = slice(DA_VDIM * h, DA_VDIM * (h + 1))
        qh = q_ref[:, sl]
        kh = k_ref[:, sl]
        vh = v_ref[:, sl]
        zero = jnp.zeros_like(qh)
        s1 = _dot_nt(jnp.where(first, qh, zero), kh)
        s2 = _dot_nt(jnp.where(first, zero, qh), kh)
        p1 = jnp.exp(s1 - jnp.max(s1, axis=-1, keepdims=True))
        p2 = jnp.exp(s2 - jnp.max(s2, axis=-1, keepdims=True))
        a = p1 * (1.0 / jnp.sum(p1, axis=-1, keepdims=True)) - p2 * (lam / jnp.sum(p2, axis=-1, keepdims=True))
        o = _dot(a.astype(BF16), vh)
        o_ref[:, sl] = (_rmsnorm(o, gs) * (1.0 - lam_init)).astype(BF16)


def _attention(lam_vecs, g_sub, q, k, v, n_batch, t_q, t_k, q_tile, lam_init, name):
    nq = t_q // q_tile
    vec_spec = pl.BlockSpec((1, DA_DIM), lambda b, i: (0, 0))
    return pl.pallas_call(
        functools.partial(_attn_kernel, lam_init=lam_init),
        grid=(n_batch, nq),
        in_specs=[vec_spec] * 4 + [
            pl.BlockSpec((1, DA_VDIM), lambda b, i: (0, 0)),
            pl.BlockSpec((q_tile, D_MODEL), lambda b, i: (b * nq + i, 0)),
            pl.BlockSpec((t_k, D_MODEL), lambda b, i: (b, 0)),
            pl.BlockSpec((t_k, D_MODEL), lambda b, i: (b, 0)),
        ],
        out_specs=pl.BlockSpec((q_tile, D_MODEL), lambda b, i: (b * nq + i, 0)),
        out_shape=jax.ShapeDtypeStruct((n_batch * t_q, D_MODEL), BF16),
        compiler_params=_cp(56),
        name=name,
    )(*lam_vecs, g_sub, q, k, v)


def _outproj_kernel(u_ref, w_ref, x_ref, gate_ref, g_ref, sh_ref, sc_ref, wr_ref, x1_ref, hs_ref, lt_ref):
    x1 = x_ref[...] + gate_ref[...] * _dot(u_ref[...], w_ref[...])
    x1_ref[...] = x1
    h2 = _modulate(x1, g_ref[...], sh_ref[...], sc_ref[...])
    _to_slabs(hs_ref, h2, ROW_TILE)
    lt_ref[...] = _dot_nt(wr_ref[...], h2.astype(BF16))


def _outproj(u, w_bf16, x, mods, norm_g, wr_t, lat, name):
    return pl.pallas_call(
        _outproj_kernel,
        grid=(N_ROW_TILES,),
        in_specs=[
            _row_spec(D_MODEL),
            _full_spec((D_MODEL, D_MODEL)),
            _row_spec(D_MODEL),
            _mod_spec(2, lat),
            _full_spec((1, D_MODEL)),
            _mod_spec(3, lat),
            _mod_spec(4, lat),
            _full_spec((LANES, D_MODEL)),
        ],
        out_specs=[
            _row_spec(D_MODEL),
            pl.BlockSpec((ROW_TILE * TOK_SLABS, LANES), lambda i: (i, 0)),
            pl.BlockSpec((LANES, ROW_TILE), lambda i: (0, i)),
        ],
        out_shape=[
            jax.ShapeDtypeStruct((N_TOK, D_MODEL), F32),
            jax.ShapeDtypeStruct((N_TOK * TOK_SLABS, LANES), F32),
            jax.ShapeDtypeStruct((LANES, N_TOK), F32),
        ],
        compiler_params=_cp(48),
        name=name,
    )(u, w_bf16, x, mods, norm_g, mods, mods, wr_t)


N_TOK_BLOCKS = N_TOK // LANES
STACK_ROWS = N_TOK_BLOCKS * N_EXPERTS


def _router_kernel(lt_ref, tri_ref, blk_ref, idx_ref, gate_ref):
    lt = lt_ref[...]
    e = jnp.exp(lt - jnp.max(lt, axis=0, keepdims=True))
    aff = e / jnp.sum(e, axis=0, keepdims=True)
    aff = jnp.where(aff >= F32_MIN_NORMAL, aff, 0.0)

    def count_ge(x):
        return jnp.sum((aff >= x).astype(F32), axis=1, keepdims=True)

    def step(mid_of):
        def body(_, c):
            lo, hi = c
            mid = jnp.minimum(jnp.maximum(mid_of(lo, hi), lo), hi)
            ok = count_ge(mid) >= float(CAPACITY)
            return jnp.where(ok, mid, lo), jnp.where(ok, hi, mid)
        return body

    lo = jnp.full((N_EXPERTS, 1), F32_MIN_NORMAL, F32)
    hi = jnp.full((N_EXPERTS, 1), 2.0, F32)
    lo, hi = lax.fori_loop(0, 8, step(lambda a, b: jnp.sqrt(a * b)), (lo, hi))
    lo, hi = lax.fori_loop(0, 40, step(lambda a, b: a + (b - a) * 0.5), (lo, hi))
    kth = jnp.max(jnp.where(aff < hi, aff, 0.0), axis=1, keepdims=True)
    thr = jnp.where(count_ge(lo) >= float(CAPACITY), kth, 0.0)

    tri = tri_ref[...]
    blk = blk_ref[...]

    def stack(x):
        return jnp.concatenate([x[:, LANES * b:LANES * (b + 1)] for b in range(N_TOK_BLOCKS)], axis=0)

    def unstack(x):
        return jnp.concatenate([x[N_EXPERTS * b:N_EXPERTS * (b + 1), :] for b in range(N_TOK_BLOCKS)], axis=1)

    def cumsum_stacked(mask_st):
        win = _dot(mask_st.astype(BF16), tri)
        tot = win[:, LANES - 1:LANES]
        off = _dot(blk, jnp.broadcast_to(tot, (STACK_ROWS, LANES)).astype(BF16))[:, :1]
        return win, off, tot

    gt = aff > thr
    eq = aff == thr
    need = float(CAPACITY) - jnp.sum(gt.astype(F32), axis=1, keepdims=True)
    eq_f = eq.astype(F32)
    ewin, eoff, _ = cumsum_stacked(stack(eq_f))
    eq_before = unstack(ewin + eoff) - eq_f
    sel = jnp.logical_or(gt, jnp.logical_and(eq, eq_before < need))
    cwin, coff, ctot = cumsum_stacked(stack(sel.astype(F32)))
    cend = coff + ctot
    cwin_b = cwin.astype(BF16)
    a_hi, a_mid, a_lo = _split3(stack(aff))

    row = lax.broadcasted_iota(jnp.int32, (STACK_ROWS, 1), 0)
    row_f = row.astype(F32)
    slot = lax.broadcasted_iota(jnp.int32, (1, CAPACITY), 1).astype(F32)
    lane_f = lax.broadcasted_iota(jnp.int32, (LANES, 1), 0).astype(F32)

    def per_expert(ex, carry):
        mine = (row & (N_EXPERTS - 1)) == ex
        before = jnp.logical_and(mine, cend <= slot)
        nblk = jnp.sum(before.astype(F32), axis=0, keepdims=True)
        base = jnp.sum(jnp.where(before, ctot, 0.0), axis=0, keepdims=True)
        target = nblk * float(N_EXPERTS) + ex.astype(F32)
        pick = jnp.where(row_f == target, 1.0, 0.0).astype(BF16)
        cnt_in = _dot_tn(cwin_b, pick)
        lane_idx = jnp.sum((cnt_in <= slot - base).astype(F32), axis=0, keepdims=True)
        idx_ref[pl.ds(ex, 1), :] = (nblk * float(LANES) + lane_idx).astype(jnp.int32)
        aff_blk = _dot_tn(a_hi, pick) + _dot_tn(a_mid, pick) + _dot_tn(a_lo, pick)
        gate_ref[pl.ds(ex, 1), :] = jnp.sum(jnp.where(lane_f == lane_idx, aff_blk, 0.0), axis=0, keepdims=True)
        return carry

    lax.fori_loop(0, N_EXPERTS, per_expert, 0)


def _cumsum_consts():
    l = np.arange(LANES)
    tri = (l[:, None] <= l[None, :]).astype(np.float32)
    r = np.arange(STACK_ROWS)
    same = (r[:, None] % N_EXPERTS) == (r[None, :] % N_EXPERTS)
    earlier = (r[None, :] // N_EXPERTS) < (r[:, None] // N_EXPERTS)
    blk = (same & earlier).astype(np.float32)
    return jnp.asarray(tri, BF16), jnp.asarray(blk, BF16)


def _router(logits_t, tri, blk, name):
    return pl.pallas_call(
        _router_kernel,
        grid=(1,),
        in_specs=[
            pl.BlockSpec((N_EXPERTS, N_TOK), lambda i: (0, 0)),
            _full_spec((LANES, LANES)),
            _full_spec((STACK_ROWS, STACK_ROWS)),
        ],
        out_specs=[_full_spec((N_EXPERTS, CAPACITY))] * 2,
        out_shape=[jax.ShapeDtypeStruct((N_EXPERTS, CAPACITY), jnp.int32),
                   jax.ShapeDtypeStruct((N_EXPERTS, CAPACITY), F32)],
        compiler_params=_cp(48),
        name=name,
    )(logits_t, tri, blk)


GATHER_UNROLL = 8


def _ffn_kernel(idx_ref, hs_ref, gate_ref, w1_ref, w3_ref, w2_ref, ye_ref, tile_ref, xe_ref, acc_ref):
    ex = pl.program_id(0)
    f = pl.program_id(1)

    @pl.when(f == 0)
    def _():
        def body(i, carry):
            for u in range(GATHER_UNROLL):
                s = i * GATHER_UNROLL + u
                t = idx_ref[ex * CAPACITY + s]
                slab = hs_ref[pl.ds(pl.multiple_of(t * TOK_SLABS, TOK_SLABS), TOK_SLABS), :]
                tile_ref[pl.ds(s, TOK_SLABS, stride=XE_STRIDE), :] = slab
            return carry

        lax.fori_loop(0, CAPACITY // GATHER_UNROLL, body, 0)
        xe_ref[...] = jnp.concatenate(
            [tile_ref[pl.ds(c * XE_STRIDE, CAPACITY), :] for c in range(TOK_SLABS)], axis=1).astype(BF16)
        acc_ref[...] = jnp.zeros_like(acc_ref)

    x = xe_ref[...]
    h1 = _dot(x, w1_ref[...].astype(BF16))
    h3 = _dot(x, w3_ref[...].astype(BF16))
    hid = (h1 * jax.nn.sigmoid(h1) * h3).astype(BF16)
    acc_ref[...] += _dot(hid, w2_ref[...].astype(BF16))

    @pl.when(f == pl.num_programs(1) - 1)
    def _():
        _to_slabs(ye_ref, acc_ref[...] * gate_ref[...], CAPACITY)


def _expert_ffn(idx_flat, hs, gate_col, w1, w3, w2, layer, name):
    nf = D_EXPERT // FF_TILE
    return pl.pallas_call(
        _ffn_kernel,
        grid_spec=pltpu.PrefetchScalarGridSpec(
            num_scalar_prefetch=1,
            grid=(N_EXPERTS, nf),
            in_specs=[
                pl.BlockSpec((N_TOK * TOK_SLABS, LANES), lambda e, f, idx: (0, 0), pipeline_mode=pl.Buffered(1)),
                pl.BlockSpec((None, CAPACITY, 1), lambda e, f, idx: (e, 0, 0)),
                pl.BlockSpec((None, None, D_MODEL, FF_TILE), lambda e, f, idx: (layer, e, 0, f)),
                pl.BlockSpec((None, None, D_MODEL, FF_TILE), lambda e, f, idx: (layer, e, 0, f)),
                pl.BlockSpec((None, None, FF_TILE, D_MODEL), lambda e, f, idx: (layer, e, f, 0)),
            ],
            out_specs=pl.BlockSpec((None, CAPACITY * TOK_SLABS, LANES), lambda e, f, idx: (e, 0, 0)),
            scratch_shapes=[
                pltpu.VMEM((TOK_SLABS * XE_STRIDE, LANES), F32),
                pltpu.VMEM((CAPACITY, D_MODEL), BF16),
                pltpu.VMEM((CAPACITY, D_MODEL), F32),
            ],
        ),
        out_shape=jax.ShapeDtypeStruct((N_EXPERTS, CAPACITY * TOK_SLABS, LANES), F32),
        compiler_params=_cp(56),
        name=name,
    )(idx_flat, hs, gate_col, w1, w3, w2)


SCATTER_UNROLL = 8


def _combine_kernel(idx_ref, ye_ref, y_ref):
    ex = pl.program_id(0)

    @pl.when(ex == 0)
    def _():
        y_ref[...] = jnp.zeros_like(y_ref)

    def body(i, carry):
        upd = []
        for u in range(SCATTER_UNROLL):
            s = i * SCATTER_UNROLL + u
            t = idx_ref[ex * CAPACITY + s]
            rows = pl.ds(pl.multiple_of(t * TOK_SLABS, TOK_SLABS), TOK_SLABS)
            src = ye_ref[pl.ds(pl.multiple_of(s * TOK_SLABS, TOK_SLABS), TOK_SLABS), :]
            upd.append((rows, y_ref[rows, :] + src))
        for rows, val in upd:
            y_ref[rows, :] = val
        return carry

    lax.fori_loop(0, CAPACITY // SCATTER_UNROLL, body, 0)


def _combine(idx_flat, ye, name):
    return pl.pallas_call(
        _combine_kernel,
        grid_spec=pltpu.PrefetchScalarGridSpec(
            num_scalar_prefetch=1,
            grid=(N_EXPERTS,),
            in_specs=[pl.BlockSpec((None, CAPACITY * TOK_SLABS, LANES), lambda e, idx: (e, 0, 0))],
            out_specs=pl.BlockSpec((N_TOK * TOK_SLABS, LANES), lambda e, idx: (0, 0)),
        ),
        out_shape=jax.ShapeDtypeStruct((N_TOK * TOK_SLABS, LANES), F32),
        compiler_params=_cp(56),
        name=name,
    )(idx_flat, ye)


def _moe(hs, logits_t, tri, blk, w1, w3, w2, layer, tag):
    idx, gate = _router(logits_t, tri, blk, "router_" + tag)
    idx_flat = idx.reshape(N_EXPERTS * CAPACITY)
    ye = _expert_ffn(idx_flat, hs, gate.reshape(N_EXPERTS, CAPACITY, 1), w1, w3, w2, layer, "ffn_" + tag)
    return _combine(idx_flat, ye, "combine_" + tag)


def _gla_proj_kernel(xp_ref, yt_ref, gp_ref, g_ref, sh_ref, sc_ref, w_ref, wg_ref, bg_ref,
                     x_ref, q_ref, k_ref, v_ref, r_ref, lgf_ref, lgb_ref):
    x = xp_ref[...] + gp_ref[...] * _from_slabs(yt_ref, ROW_TILE)
    x_ref[...] = x
    h = _modulate(x, g_ref[...], sh_ref[...], sc_ref[...])
    z = _dot(h.astype(BF16), w_ref[...])
    nq = GLA_HEADS * GLA_DK
    nv = GLA_HEADS * GLA_DV
    q_ref[...] = z[:, :nq] * (GLA_DK ** -0.5)
    k_ref[...] = z[:, nq:2 * nq]
    v_ref[...] = z[:, 2 * nq:2 * nq + nv].astype(BF16)
    r_ref[...] = z[:, 2 * nq + nv:2 * nq + 2 * nv].astype(BF16)
    zg = z[:, 2 * nq + 2 * nv:].astype(BF16)
    a = _dot(zg, wg_ref[...]) + bg_ref[...]
    ls = (jnp.minimum(a, 0.0) - jnp.log1p(jnp.exp(-jnp.abs(a)))) * (1.0 / GLA_GATE_NORM)
    lgf_ref[...] = ls[:, :nq]
    lgb_ref[...] = ls[:, nq:]


def _gla_proj(x_prev, y_slabs, mods_prev, mods, norm_g, w_bf16, wg, bg, lat, name):
    nq = GLA_HEADS * GLA_DK
    nv = GLA_HEADS * GLA_DV
    n_in = w_bf16.shape[1]
    return pl.pallas_call(
        _gla_proj_kernel,
        grid=(N_ROW_TILES,),
        in_specs=[
            _row_spec(D_MODEL),
            pl.BlockSpec((ROW_TILE * TOK_SLABS, LANES), lambda i: (i, 0)),
            _mod_spec(5, lat),
            _full_spec((1, D_MODEL)),
            _mod_spec(0, lat),
            _mod_spec(1, lat),
            _full_spec((D_MODEL, n_in)),
            _full_spec((LANES, 2 * nq)),
            _full_spec((1, 2 * nq)),
        ],
        out_specs=[_row_spec(D_MODEL), _row_spec(nq), _row_spec(nq), _row_spec(nv), _row_spec(nv),
                   _row_spec(nq), _row_spec(nq)],
        out_shape=[
            jax.ShapeDtypeStruct((N_TOK, D_MODEL), F32),
            jax.ShapeDtypeStruct((N_TOK, nq), F32),
            jax.ShapeDtypeStruct((N_TOK, nq), F32),
            jax.ShapeDtypeStruct((N_TOK, nv), BF16),
            jax.ShapeDtypeStruct((N_TOK, nv), BF16),
            jax.ShapeDtypeStruct((N_TOK, nq), F32),
            jax.ShapeDtypeStruct((N_TOK, nq), F32),
        ],
        compiler_params=_cp(48),
        name=name,
    )(x_prev, y_slabs, mods_prev, norm_g, mods, mods, w_bf16, wg, bg)


GLA_LEVELS = (32, 16, 8)


def _gla_chunk(qc, kc, vc, gc, st, tri, fwd):
    c = GLA_CHUNK
    g_hi, g_mid, g_lo = _split3(gc)
    b = _dot(tri, g_hi) + _dot(tri, g_mid) + _dot(tri, g_lo)
    btot = b[c - 1:c] if fwd else b[0:1]
    o = _dot_nt((qc * jnp.exp(b)).astype(BF16), st.astype(BF16))

    row = lax.broadcasted_iota(jnp.int32, (c, 1), 0)
    col = lax.broadcasted_iota(jnp.int32, (1, c), 1)
    a = jnp.zeros((c, c), F32)
    for g in GLA_LEVELS:
        odd = ((row >> int(math.log2(g))) & 1) == 1
        later = odd if fwd else jnp.logical_not(odd)
        refs = []
        for p in range(c // (2 * g)):
            r0 = 2 * g * p + (g - 1 if fwd else g)
            refs.append(jnp.broadcast_to(b[r0:r0 + 1], (2 * g, GLA_DK)))
        ref = jnp.concatenate(refs, axis=0) if len(refs) > 1 else refs[0]
        qt = jnp.where(later, qc * jnp.exp(jnp.where(later, b - ref, 0.0)), 0.0)
        kt = jnp.where(later, 0.0, kc * jnp.exp(jnp.where(later, 0.0, ref - b)))
        same_parent = (row >> int(math.log2(2 * g))) == (col >> int(math.log2(2 * g)))
        a = a + jnp.where(same_parent, _dot_nt(qt.astype(BF16), kt.astype(BF16)), 0.0)

    sub = lax.broadcasted_iota(jnp.int32, (SUBLANES, 1), 0)
    strips = []
    for blk in range(c // SUBLANES):
        r0 = SUBLANES * blk
        qb = qc[r0:r0 + SUBLANES]
        bb = b[r0:r0 + SUBLANES]
        strip = jnp.zeros((SUBLANES, c), F32)
        for jj in range(SUBLANES):
            j = r0 + jj
            cond = (sub >= jj) if fwd else (sub <= jj)
            e = jnp.exp(jnp.where(cond, bb - b[j:j + 1], 0.0))
            t = jnp.where(cond, qb * kc[j:j + 1] * e, 0.0)
            strip = jnp.where(col == j, jnp.sum(t, axis=1, keepdims=True), strip)
        strips.append(strip)
    a = a + jnp.concatenate(strips, axis=0)

    o = o + _dot(a.astype(BF16), vc)
    kd = (kc * jnp.exp(btot - b)).astype(BF16)
    st_new = st * jnp.exp(btot) + _dot_tn(vc, kd)
    return o, st_new


def _gla_kernel(q_ref, k_ref, v_ref, r_ref, gf_ref, gb_ref, s0f_ref, s0b_ref, go_ref, trif_ref, trib_ref,
                u_ref, sf_ref, sb_ref, of_acc, ob_acc, stf_ref, stb_ref, *, t_len):
    n = t_len // GLA_CHUNK
    stf_ref[...] = s0f_ref[...].T
    stb_ref[...] = s0b_ref[...].T
    trif = trif_ref[...]
    trib = trib_ref[...]

    def body(i, carry):
        rf = pl.ds(pl.multiple_of(i * GLA_CHUNK, GLA_CHUNK), GLA_CHUNK)
        rb = pl.ds(pl.multiple_of((n - 1 - i) * GLA_CHUNK, GLA_CHUNK), GLA_CHUNK)
        o_f, stf_new = _gla_chunk(q_ref[rf, :], k_ref[rf, :], v_ref[rf, :], gf_ref[rf, :], stf_ref[...], trif, True)
        o_b, stb_new = _gla_chunk(q_ref[rb, :], k_ref[rb, :], v_ref[rb, :], gb_ref[rb, :], stb_ref[...], trib, False)
        stf_ref[...] = stf_new
        stb_ref[...] = stb_new
        of_acc[rf, :] = o_f
        ob_acc[rb, :] = o_b
        return carry

    lax.fori_loop(0, n, body, 0)
    sf_ref[...] = stf_ref[...].T
    sb_ref[...] = stb_ref[...].T
    r = r_ref[...].astype(F32)
    u_ref[...] = (_rmsnorm(of_acc[...] + ob_acc[...], go_ref[...]) * (r * jax.nn.sigmoid(r))).astype(BF16)


def _gla_tri():
    i = np.arange(GLA_CHUNK)
    fwd = (i[None, :] <= i[:, None]).astype(np.float32)
    bwd = (i[None, :] >= i[:, None]).astype(np.float32)
    return jnp.asarray(fwd, BF16), jnp.asarray(bwd, BF16)


def _gla(q, k, v, r, lgf, lgb, s0f, s0b, g_out, n_batch, t_len, name):
    trif, trib = _gla_tri()
    qk_spec = pl.BlockSpec((None, t_len, GLA_DK), lambda b, h: (b, 0, h))
    v_spec = pl.BlockSpec((None, t_len, GLA_DV), lambda b, h: (b, 0, h))
    s_spec = pl.BlockSpec((None, None, GLA_DK, GLA_DV), lambda b, h: (b, h, 0, 0))
    const = lambda shape: pl.BlockSpec(shape, lambda b, h: (0,) * len(shape))
    s_shape = jax.ShapeDtypeStruct((n_batch, GLA_HEADS, GLA_DK, GLA_DV), F32)
    return pl.pallas_call(
        functools.partial(_gla_kernel, t_len=t_len),
        grid=(n_batch, GLA_HEADS),
        in_specs=[qk_spec, qk_spec, v_spec, v_spec, qk_spec, qk_spec, s_spec, s_spec,
                  const((1, GLA_DV)), const((GLA_CHUNK, GLA_CHUNK)), const((GLA_CHUNK, GLA_CHUNK))],
        out_specs=[v_spec, s_spec, s_spec],
        out_shape=[jax.ShapeDtypeStruct((n_batch, t_len, GLA_HEADS * GLA_DV), BF16), s_shape, s_shape],
        scratch_shapes=[pltpu.VMEM((t_len, GLA_DV), F32), pltpu.VMEM((t_len, GLA_DV), F32),
                        pltpu.VMEM((GLA_DV, GLA_DK), F32), pltpu.VMEM((GLA_DV, GLA_DK), F32)],
        compiler_params=_cp(48),
        name=name,
    )(q, k, v, r, lgf, lgb, s0f, s0b, g_out, trif, trib)


def _final_kernel(xp_ref, yt_ref, gp_ref, fg_ref, o_ref):
    x = xp_ref[...] + gp_ref[...] * _from_slabs(yt_ref, ROW_TILE)
    o_ref[...] = _rmsnorm(x, fg_ref[...])


def _final(x_prev, y_slabs, mods_prev, final_g, lat, name):
    return pl.pallas_call(
        _final_kernel,
        grid=(N_ROW_TILES,),
        in_specs=[
            _row_spec(D_MODEL),
            pl.BlockSpec((ROW_TILE * TOK_SLABS, LANES), lambda i: (i, 0)),
            _mod_spec(5, lat),
            _full_spec((1, D_MODEL)),
        ],
        out_specs=_row_spec(D_MODEL),
        out_shape=jax.ShapeDtypeStruct((N_TOK, D_MODEL), F32),
        compiler_params=_cp(32),
        name=name,
    )(x_prev, y_slabs, mods_prev, final_g)


def kernel(x_prompt, x_sample, cache_k, cache_v, state_fwd, state_bwd, c, c_ctx, w_mod, b_mod, norm_g,
           da_w_in, da_w_out, da_lam_q1, da_lam_k1, da_lam_q2, da_lam_k2, da_g_sub, gla_w_in, gla_w_gf2,
           gla_b_gf, gla_w_gb2, gla_b_gb, gla_g_out, gla_w_out, moe_w_router, moe_w1, moe_w3, moe_w2, final_g):
    n_ctx_b = x_prompt.shape[0]
    n_lat_b = x_sample.shape[0]
    x_ctx = x_prompt.reshape(N_TOK, D_MODEL)
    x_lat = x_sample.reshape(N_TOK, D_MODEL)

    cvec = jnp.zeros((SUBLANES, D_MODEL), F32).at[0].set(c_ctx).at[1:1 + n_lat_b].set(c)
    mods = _mod_params(cvec, w_mod, b_mod)
    tri, blk = _cumsum_consts()
    wr_t = [jnp.zeros((LANES, D_MODEL), BF16).at[:N_EXPERTS].set(moe_w_router[i].T.astype(BF16))
            for i in range(DEPTH)]
    ng = norm_g.reshape(DEPTH, 2, 1, D_MODEL)

    lam_init = 0.8 - 0.6 * math.exp(-0.3 * 0)
    w_in = da_w_in[0].astype(BF16)
    w_out = da_w_out[0].astype(BF16)
    lam_vecs = [v[0].reshape(1, DA_DIM) for v in (da_lam_q1, da_lam_k1, da_lam_q2, da_lam_k2)]
    g_sub = da_g_sub[0].reshape(1, DA_VDIM)

    qc, kc, vc, kf, vf = _da_proj(x_ctx, mods[0], ng[0, 0], w_in, lat=False)
    ql, kl, vl = _da_proj(x_lat, mods[0], ng[0, 0], w_in, lat=True, rope_tabs=_rope_tables())
    past = cache_k.shape[2]
    k_all = jnp.concatenate([kl.reshape(n_lat_b, DEC_SEQ, D_MODEL),
                             cache_k[:, 0].reshape(n_lat_b, past, D_MODEL).astype(BF16)], axis=1)
    v_all = jnp.concatenate([vl.reshape(n_lat_b, DEC_SEQ, D_MODEL),
                             cache_v[:, 0].reshape(n_lat_b, past, D_MODEL).astype(BF16)], axis=1)
    t_all = DEC_SEQ + past
    u_ctx = _attention(lam_vecs, g_sub, qc, kc, vc, n_ctx_b, SEQ, SEQ, SEQ, lam_init, "attn_ctx")
    u_lat = _attention(lam_vecs, g_sub, ql, k_all.reshape(n_lat_b * t_all, D_MODEL),
                       v_all.reshape(n_lat_b * t_all, D_MODEL), n_lat_b, DEC_SEQ, t_all, 256, lam_init, "attn_lat")

    x1_ctx, hs_ctx, lt_ctx = _outproj(u_ctx, w_out, x_ctx, mods[0], ng[0, 1], wr_t[0], False, "outproj0_ctx")
    x1_lat, hs_lat, lt_lat = _outproj(u_lat, w_out, x_lat, mods[0], ng[0, 1], wr_t[0], True, "outproj0_lat")
    y_ctx = _moe(hs_ctx, lt_ctx, tri, blk, moe_w1, moe_w3, moe_w2, 0, "l0_ctx")
    y_lat = _moe(hs_lat, lt_lat, tri, blk, moe_w1, moe_w3, moe_w2, 0, "l0_lat")

    nq = GLA_HEADS * GLA_DK
    gla_in = gla_w_in.shape[2]
    n_pad = (-gla_in) % LANES
    w_in1 = jnp.pad(gla_w_in[0], ((0, 0), (0, n_pad))).astype(BF16)
    wg = jnp.zeros((LANES, 2 * nq), F32)
    wg = wg.at[:GLA_GATE_RANK, :nq].set(gla_w_gf2[0]).at[GLA_GATE_RANK:2 * GLA_GATE_RANK, nq:].set(gla_w_gb2[0])
    wg = wg.astype(BF16)
    bg = jnp.concatenate([gla_b_gf[0], gla_b_gb[0]]).reshape(1, 2 * nq)
    w_out1 = gla_w_out[0].astype(BF16)
    g_out = gla_g_out[0].reshape(1, GLA_DV)

    def gla_side(x1, y, lat, n_b, t_len, s0f, s0b, tag):
        x2, q, k, v, r, lgf, lgb = _gla_proj(x1, y, mods[0], mods[1], ng[1, 0], w_in1, wg, bg, lat, "gla_proj_" + tag)
        sh3 = lambda a: a.reshape(n_b, t_len, a.shape[-1])
        u, sf, sb = _gla(sh3(q), sh3(k), sh3(v), sh3(r), sh3(lgf), sh3(lgb), s0f, s0b, g_out, n_b, t_len, "gla_" + tag)
        return x2, u.reshape(N_TOK, D_MODEL), sf, sb

    zeros_s = jnp.zeros((n_ctx_b, GLA_HEADS, GLA_DK, GLA_DV), F32)
    x2_ctx, ug_ctx, sf, sb = gla_side(x1_ctx, y_ctx, False, n_ctx_b, SEQ, zeros_s, zeros_s, "ctx")
    x2_lat, ug_lat, _, _ = gla_side(x1_lat, y_lat, True, n_lat_b, DEC_SEQ, state_fwd[:, 0], state_bwd[:, 0], "lat")

    x3_ctx, hs_ctx, lt_ctx = _outproj(ug_ctx, w_out1, x2_ctx, mods[1], ng[1, 1], wr_t[1], False, "outproj1_ctx")
    x3_lat, hs_lat, lt_lat = _outproj(ug_lat, w_out1, x2_lat, mods[1], ng[1, 1], wr_t[1], True, "outproj1_lat")
    y_ctx = _moe(hs_ctx, lt_ctx, tri, blk, moe_w1, moe_w3, moe_w2, 1, "l1_ctx")
    y_lat = _moe(hs_lat, lt_lat, tri, blk, moe_w1, moe_w3, moe_w2, 1, "l1_lat")

    fg = final_g.reshape(1, D_MODEL)
    y_prompt = _final(x3_ctx, y_ctx, mods[1], fg, False, "final_ctx").reshape(x_prompt.shape)
    y_sample = _final(x3_lat, y_lat, mods[1], fg, True, "final_lat").reshape(x_sample.shape)
    new_k = kf.reshape(n_ctx_b, 1, SEQ, 2 * DA_HEADS, DA_DIM)
    new_v = vf.reshape(n_ctx_b, 1, SEQ, DA_HEADS, DA_VDIM)
    return (y_prompt, y_sample, new_k, new_v, sf[:, None], sb[:, None])
```

```python
import functools
import math

import numpy as np
import jax
import jax.numpy as jnp
from jax import lax
from jax.experimental import pallas as pl
from jax.experimental.pallas import tpu as pltpu

F32 = jnp.float32
BF16 = jnp.bfloat16

D_MODEL = 1024
DEPTH = 2
SEQ = 256
DEC_SEQ = 2048
GRID_W = 64
N_TOK = 4096
DA_HEADS = 8
DA_DIM = 64
DA_VDIM = 128
ROPE_BASE = 10000.0
GLA_HEADS = 4
GLA_DK = 128
GLA_DV = 256
GLA_GATE_RANK = 16
GLA_GATE_NORM = 16.0
GLA_CHUNK = 64
N_EXPERTS = 16
CAPACITY = 512
D_EXPERT = 2048
EPS = 1e-6
F32_MIN_NORMAL = 2.0 ** -126
LOG2E = 1.4426950408889634

LANES = 128
SUBLANES = 8
ROW_TILE = 512
FF_TILE = 512
TOK_SLABS = D_MODEL // LANES
XE_STRIDE = CAPACITY + SUBLANES
MIB = 1024 * 1024


def _cp(vmem_mib, sem=None):
    return pltpu.CompilerParams(vmem_limit_bytes=vmem_mib * MIB, dimension_semantics=sem)


def _dot(a, b):
    return jnp.dot(a, b, preferred_element_type=F32)


def _dot_nt(a, b):
    return lax.dot_general(a, b, (((1,), (1,)), ((), ())), preferred_element_type=F32)


def _dot_tn(a, b):
    return lax.dot_general(a, b, (((0,), (0,)), ((), ())), preferred_element_type=F32)


def _rmsnorm(x, g):
    return x * lax.rsqrt(jnp.mean(x * x, axis=-1, keepdims=True) + EPS) * g


def _modulate(x, g, shift, scale):
    return _rmsnorm(x, g) * (1.0 + scale) + shift


def _split3(x):
    hi = x.astype(BF16)
    r = x - hi.astype(F32)
    mid = r.astype(BF16)
    lo = (r - mid.astype(F32)).astype(BF16)
    return hi, mid, lo


def _from_slabs(ref, rows):
    return jnp.concatenate([ref[pl.ds(c, rows, stride=TOK_SLABS), :] for c in range(TOK_SLABS)], axis=1)


def _to_slabs(ref, val, rows):
    for c in range(TOK_SLABS):
        ref[pl.ds(c, rows, stride=TOK_SLABS), :] = val[:, LANES * c:LANES * (c + 1)]


def _mod_kernel(c_ref, w_ref, b_ref, o_ref):
    c = c_ref[...]
    s = c * jax.nn.sigmoid(c)
    w = w_ref[...]
    s_hi = s.astype(BF16)
    s_lo = (s - s_hi.astype(F32)).astype(BF16)
    w_hi = w.astype(BF16)
    w_lo = (w - w_hi.astype(F32)).astype(BF16)
    o_ref[...] = _dot(s_hi, w_hi) + _dot(s_hi, w_lo) + _dot(s_lo, w_hi) + b_ref[...]


def _mod_params(cvec, w_mod, b_mod):
    n6 = 6 * D_MODEL
    out = pl.pallas_call(
        _mod_kernel,
        grid=(DEPTH, 6),
        in_specs=[
            pl.BlockSpec((SUBLANES, D_MODEL), lambda i, j: (0, 0)),
            pl.BlockSpec((None, D_MODEL, D_MODEL), lambda i, j: (i, 0, j)),
            pl.BlockSpec((None, 1, D_MODEL), lambda i, j: (i, 0, j)),
        ],
        out_specs=pl.BlockSpec((None, SUBLANES, D_MODEL), lambda i, j: (i, 0, j)),
        out_shape=jax.ShapeDtypeStruct((DEPTH, SUBLANES, n6), F32),
        compiler_params=_cp(32),
        name="mod_params",
    )(cvec, w_mod, b_mod.reshape(DEPTH, 1, n6))
    return out.reshape(DEPTH, SUBLANES * 6, 1, D_MODEL)


def _mod_spec(k, lat):
    tiles_per_batch = DEC_SEQ // ROW_TILE
    if lat:
        return pl.BlockSpec((None, 1, D_MODEL), lambda i: ((1 + i // tiles_per_batch) * 6 + k, 0, 0))
    return pl.BlockSpec((None, 1, D_MODEL), lambda i: (k, 0, 0))


def _row_spec(width, dtype_rows=ROW_TILE):
    return pl.BlockSpec((dtype_rows, width), lambda i: (i, 0))


def _full_spec(shape):
    nd = len(shape)
    return pl.BlockSpec(shape, lambda i: (0,) * nd)


N_ROW_TILES = N_TOK // ROW_TILE


def _da_proj_kernel(*refs, rope, emit_f32):
    x_ref, g_ref, sh_ref, sc_ref, w_ref = refs[:5]
    pos = 5
    if rope:
        cos_ref, sin_ref = refs[pos:pos + 2]
        pos += 2
    q_ref, k_ref, v_ref = refs[pos:pos + 3]
    pos += 3
    h = _modulate(x_ref[...], g_ref[...], sh_ref[...], sc_ref[...])
    z = _dot(h.astype(BF16), w_ref[...])
    q = z[:, :D_MODEL]
    k = z[:, D_MODEL:2 * D_MODEL]
    v = z[:, 2 * D_MODEL:]
    if emit_f32:
        kf_ref, vf_ref = refs[pos:pos + 2]
        kf_ref[...] = k
        vf_ref[...] = v
    if rope:
        reps = D_MODEL // LANES
        cos = jnp.concatenate([cos_ref[...]] * reps, axis=1)
        sin = jnp.concatenate([sin_ref[...]] * reps, axis=1)
        lane = lax.broadcasted_iota(jnp.int32, (1, D_MODEL), 1)
        first = (lane & 16) == 0

        def rot(t):
            partner = jnp.where(first, pltpu.roll(t, D_MODEL - 16, 1), pltpu.roll(t, 16, 1))
            return t * cos + partner * sin

        q = rot(q)
        k = rot(k)
    q_ref[...] = (q * (DA_DIM ** -0.5 * LOG2E)).astype(BF16)
    k_ref[...] = k.astype(BF16)
    v_ref[...] = v.astype(BF16)


def _da_proj(x, mods, norm_g, w_bf16, lat, rope_tabs=None):
    rope = rope_tabs is not None
    emit_f32 = not lat
    in_specs = [
        _row_spec(D_MODEL),
        _full_spec((1, D_MODEL)),
        _mod_spec(0, lat),
        _mod_spec(1, lat),
        _full_spec((D_MODEL, 3 * D_MODEL)),
    ]
    args = [x, norm_g, mods, mods, w_bf16]
    if rope:
        tiles_per_batch = DEC_SEQ // ROW_TILE
        tab_spec = pl.BlockSpec((ROW_TILE, LANES), lambda i: (i % tiles_per_batch, 0))
        in_specs += [tab_spec, tab_spec]
        args += list(rope_tabs)
    out_specs = [_row_spec(D_MODEL)] * 3
    out_shape = [jax.ShapeDtypeStruct((N_TOK, D_MODEL), BF16)] * 3
    if emit_f32:
        out_specs += [_row_spec(D_MODEL)] * 2
        out_shape += [jax.ShapeDtypeStruct((N_TOK, D_MODEL), F32)] * 2
    return pl.pallas_call(
        functools.partial(_da_proj_kernel, rope=rope, emit_f32=emit_f32),
        grid=(N_ROW_TILES,),
        in_specs=in_specs,
        out_specs=out_specs,
        out_shape=out_shape,
        compiler_params=_cp(48),
        name="da_proj_lat" if lat else "da_proj_ctx",
    )(*args)


def _rope_tables():
    t = jnp.arange(DEC_SEQ)
    rows = (t // GRID_W).astype(F32)
    cols = (t % GRID_W).astype(F32)
    half = DA_DIM // 4
    freqs = ROPE_BASE ** (-jnp.arange(half, dtype=F32) / half)
    ang_r = rows[:, None] * freqs
    ang_c = cols[:, None] * freqs
    cos64 = jnp.concatenate([jnp.cos(ang_r)] * 2 + [jnp.cos(ang_c)] * 2, axis=1)
    sin64 = jnp.concatenate([-jnp.sin(ang_r), jnp.sin(ang_r), -jnp.sin(ang_c), jnp.sin(ang_c)], axis=1)
    reps = LANES // DA_DIM
    return jnp.concatenate([cos64] * reps, axis=1), jnp.concatenate([sin64] * reps, axis=1)


def _attn_kernel(lq1_ref, lk1_ref, lq2_ref, lk2_ref, gs_ref, q_ref, *refs, lam_init, n_seg):
    kv_refs, o_ref = refs[:2 * n_seg], refs[2 * n_seg]
    lam = (jnp.exp(jnp.sum(lq1_ref[...] * lk1_ref[...], axis=-1, keepdims=True))
           - jnp.exp(jnp.sum(lq2_ref[...] * lk2_ref[...], axis=-1, keepdims=True)) + lam_init)
    lane = lax.broadcasted_iota(jnp.int32, (1, DA_VDIM), 1)
    first = lane < DA_DIM
    gs = gs_ref[...]

    def softmax_pv(qm, ks, vs):
        ss = [_dot_nt(qm, kk) for kk in ks]
        m = functools.reduce(jnp.maximum, [jnp.max(s, axis=-1, keepdims=True) for s in ss])
        ps = [jnp.exp2(s - m) for s in ss]
        l = functools.reduce(jnp.add, [jnp.sum(p, axis=-1, keepdims=True) for p in ps])
        o = functools.reduce(jnp.add, [_dot(p.astype(BF16), vv) for p, vv in zip(ps, vs)])
        return o, l

    for h in range(DA_HEADS):
        sl = slice(DA_VDIM * h, DA_VDIM * (h + 1))
        qh = q_ref[:, sl]
        ks = [kv_refs[2 * s][:, sl].astype(BF16) for s in range(n_seg)]
        vs = [kv_refs[2 * s + 1][:, sl].astype(BF16) for s in range(n_seg)]
        zero = jnp.zeros_like(qh)
        o1, l1 = softmax_pv(jnp.where(first, qh, zero), ks, vs)
        o2, l2 = softmax_pv(jnp.where(first, zero, qh), ks, vs)
        o = o1 * (1.0 / l1) - o2 * (lam / l2)
        o_ref[:, sl] = (_rmsnorm(o, gs) * (1.0 - lam_init)).astype(BF16)


def _attention(lam_vecs, g_sub, q, kv_segs, n_batch, t_q, q_tile, lam_init, name):
    nq = t_q // q_tile
    vec_spec = pl.BlockSpec((1, DA_DIM), lambda b, i: (0, 0))
    kv_specs, kv_args = [], []
    for k, v, t_k in kv_segs:
        kv_specs += [pl.BlockSpec((t_k, D_MODEL), lambda b, i: (b, 0))] * 2
        kv_args += [k, v]
    return pl.pallas_call(
        functools.partial(_attn_kernel, lam_init=lam_init, n_seg=len(kv_segs)),
        grid=(n_batch, nq),
        in_specs=[vec_spec] * 4 + [
            pl.BlockSpec((1, DA_VDIM), lambda b, i: (0, 0)),
            pl.BlockSpec((q_tile, D_MODEL), lambda b, i: (b * nq + i, 0)),
        ] + kv_specs,
        out_specs=pl.BlockSpec((q_tile, D_MODEL), lambda b, i: (b * nq + i, 0)),
        out_shape=jax.ShapeDtypeStruct((n_batch * t_q, D_MODEL), BF16),
        compiler_params=_cp(56),
        name=name,
    )(*lam_vecs, g_sub, q, *kv_args)


def _outproj_kernel(u_ref, w_ref, x_ref, gate_ref, g_ref, sh_ref, sc_ref, wr_ref, x1_ref, hs_ref, lt_ref):
    x1 = x_ref[...] + gate_ref[...] * _dot(u_ref[...], w_ref[...])
    x1_ref[...] = x1
    h2 = _modulate(x1, g_ref[...], sh_ref[...], sc_ref[...])
    _to_slabs(hs_ref, h2, ROW_TILE)
    lt_ref[...] = _dot_nt(wr_ref[...], h2.astype(BF16))


def _outproj(u, w_bf16, x, mods, norm_g, wr_t, lat, name):
    return pl.pallas_call(
        _outproj_kernel,
        grid=(N_ROW_TILES,),
        in_specs=[
            _row_spec(D_MODEL),
            _full_spec((D_MODEL, D_MODEL)),
            _row_spec(D_MODEL),
            _mod_spec(2, lat),
            _full_spec((1, D_MODEL)),
            _mod_spec(3, lat),
            _mod_spec(4, lat),
            _full_spec((LANES, D_MODEL)),
        ],
        out_specs=[
            _row_spec(D_MODEL),
            pl.BlockSpec((ROW_TILE * TOK_SLABS, LANES), lambda i: (i, 0)),
            pl.BlockSpec((LANES, ROW_TILE), lambda i: (0, i)),
        ],
        out_shape=[
            jax.ShapeDtypeStruct((N_TOK, D_MODEL), F32),
            jax.ShapeDtypeStruct((N_TOK * TOK_SLABS, LANES), F32),
            jax.ShapeDtypeStruct((LANES, N_TOK), F32),
        ],
        compiler_params=_cp(48),
        name=name,
    )(u, w_bf16, x, mods, norm_g, mods, mods, wr_t)


N_TOK_BLOCKS = N_TOK // LANES
STACK_ROWS = N_TOK_BLOCKS * N_EXPERTS


def _router_kernel(lt_ref, tri_ref, blk_ref, idx_ref, gate_ref):
    lt = lt_ref[...]
    e = jnp.exp(lt - jnp.max(lt, axis=0, keepdims=True))
    aff = e / jnp.sum(e, axis=0, keepdims=True)
    aff = jnp.where(aff >= F32_MIN_NORMAL, aff, 0.0)

    def count_ge(x):
        return jnp.sum((aff >= x).astype(F32), axis=1, keepdims=True)

    def step(mid_of):
        def body(_, c):
            lo, hi = c
            mid = jnp.minimum(jnp.maximum(mid_of(lo, hi), lo), hi)
            ok = count_ge(mid) >= float(CAPACITY)
            return jnp.where(ok, mid, lo), jnp.where(ok, hi, mid)
        return body

    lo = jnp.full((N_EXPERTS, 1), F32_MIN_NORMAL, F32)
    hi = jnp.full((N_EXPERTS, 1), 2.0, F32)
    lo, hi = lax.fori_loop(0, 8, step(lambda a, b: jnp.sqrt(a * b)), (lo, hi))
    lo, hi = lax.fori_loop(0, 40, step(lambda a, b: a + (b - a) * 0.5), (lo, hi))
    kth = jnp.max(jnp.where(aff < hi, aff, 0.0), axis=1, keepdims=True)
    thr = jnp.where(count_ge(lo) >= float(CAPACITY), kth, 0.0)

    tri = tri_ref[...]
    blk = blk_ref[...]

    def stack(x):
        return jnp.concatenate([x[:, LANES * b:LANES * (b + 1)] for b in range(N_TOK_BLOCKS)], axis=0)

    def unstack(x):
        return jnp.concatenate([x[N_EXPERTS * b:N_EXPERTS * (b + 1), :] for b in range(N_TOK_BLOCKS)], axis=1)

    def cumsum_stacked(mask_st):
        win = _dot(mask_st.astype(BF16), tri)
        tot = win[:, LANES - 1:LANES]
        off = _dot(blk, jnp.broadcast_to(tot, (STACK_ROWS, LANES)).astype(BF16))[:, :1]
        return win, off, tot

    gt = aff > thr
    eq = aff == thr
    need = float(CAPACITY) - jnp.sum(gt.astype(F32), axis=1, keepdims=True)
    eq_f = eq.astype(F32)
    ewin, eoff, _ = cumsum_stacked(stack(eq_f))
    eq_before = unstack(ewin + eoff) - eq_f
    sel = jnp.logical_or(gt, jnp.logical_and(eq, eq_before < need))
    cwin, coff, ctot = cumsum_stacked(stack(sel.astype(F32)))
    cend = coff + ctot
    cwin_b = cwin.astype(BF16)
    a_hi, a_mid, a_lo = _split3(stack(aff))

    row = lax.broadcasted_iota(jnp.int32, (STACK_ROWS, 1), 0)
    row_f = row.astype(F32)
    slot = lax.broadcasted_iota(jnp.int32, (1, CAPACITY), 1).astype(F32)
    lane_f = lax.broadcasted_iota(jnp.int32, (LANES, 1), 0).astype(F32)

    def per_expert(ex, carry):
        mine = (row & (N_EXPERTS - 1)) == ex
        before = jnp.logical_and(mine, cend <= slot)
        nblk = jnp.sum(before.astype(F32), axis=0, keepdims=True)
        base = jnp.sum(jnp.where(before, ctot, 0.0), axis=0, keepdims=True)
        target = nblk * float(N_EXPERTS) + jnp.asarray(ex, F32)
        pick = jnp.where(row_f == target, 1.0, 0.0).astype(BF16)
        cnt_in = _dot_tn(cwin_b, pick)
        lane_idx = jnp.sum((cnt_in <= slot - base).astype(F32), axis=0, keepdims=True)
        idx_ref[pl.ds(ex, 1), :] = (nblk * float(LANES) + lane_idx).astype(jnp.int32)
        aff_blk = _dot_tn(a_hi, pick) + _dot_tn(a_mid, pick) + _dot_tn(a_lo, pick)
        gate_ref[pl.ds(ex, 1), :] = jnp.sum(jnp.where(lane_f == lane_idx, aff_blk, 0.0), axis=0, keepdims=True)
        return carry

    lax.fori_loop(0, N_EXPERTS, per_expert, 0)


def _cumsum_consts():
    l = np.arange(LANES)
    tri = (l[:, None] <= l[None, :]).astype(np.float32)
    r = np.arange(STACK_ROWS)
    same = (r[:, None] % N_EXPERTS) == (r[None, :] % N_EXPERTS)
    earlier = (r[None, :] // N_EXPERTS) < (r[:, None] // N_EXPERTS)
    blk = (same & earlier).astype(np.float32)
    return jnp.asarray(tri, BF16), jnp.asarray(blk, BF16)


def _router(logits_t, tri, blk, name):
    return pl.pallas_call(
        _router_kernel,
        grid=(1,),
        in_specs=[
            pl.BlockSpec((N_EXPERTS, N_TOK), lambda i: (0, 0)),
            _full_spec((LANES, LANES)),
            _full_spec((STACK_ROWS, STACK_ROWS)),
        ],
        out_specs=[_full_spec((N_EXPERTS, CAPACITY))] * 2,
        out_shape=[jax.ShapeDtypeStruct((N_EXPERTS, CAPACITY), jnp.int32),
                   jax.ShapeDtypeStruct((N_EXPERTS, CAPACITY), F32)],
        compiler_params=_cp(48),
        name=name,
    )(logits_t, tri, blk)


GATHER_UNROLL = 8


def _gather_kernel(idx_ref, hs_ref, xe_ref, tile_ref):
    ex = pl.program_id(0)

    def body(i, carry):
        for u in range(GATHER_UNROLL):
            s = i * GATHER_UNROLL + u
            t = idx_ref[ex * CAPACITY + s]
            slab = hs_ref[pl.ds(pl.multiple_of(t * TOK_SLABS, TOK_SLABS), TOK_SLABS), :]
            tile_ref[pl.ds(s, TOK_SLABS, stride=XE_STRIDE), :] = slab
        return carry

    lax.fori_loop(0, CAPACITY // GATHER_UNROLL, body, 0)
    xe_ref[...] = jnp.concatenate(
        [tile_ref[pl.ds(c * XE_STRIDE, CAPACITY), :] for c in range(TOK_SLABS)], axis=1).astype(BF16)


def _gather(idx_flat, hs, name):
    return pl.pallas_call(
        _gather_kernel,
        grid_spec=pltpu.PrefetchScalarGridSpec(
            num_scalar_prefetch=1,
            grid=(N_EXPERTS,),
            in_specs=[pl.BlockSpec((N_TOK * TOK_SLABS, LANES), lambda e, idx: (0, 0), pipeline_mode=pl.Buffered(1))],
            out_specs=pl.BlockSpec((None, CAPACITY, D_MODEL), lambda e, idx: (e, 0, 0)),
            scratch_shapes=[pltpu.VMEM((TOK_SLABS * XE_STRIDE, LANES), F32)],
        ),
        out_shape=jax.ShapeDtypeStruct((N_EXPERTS, CAPACITY, D_MODEL), BF16),
        compiler_params=_cp(40),
        name=name,
    )(idx_flat, hs)


def _ffn_kernel(xc_ref, xl_ref, gc_ref, gl_ref, w1_ref, w3_ref, w2_ref, yc_ref, yl_ref, xe_ref, acc_ref):
    f = pl.program_id(1)

    @pl.when(f == 0)
    def _():
        xe_ref[:CAPACITY, :] = xc_ref[...]
        xe_ref[CAPACITY:, :] = xl_ref[...]
        acc_ref[...] = jnp.zeros_like(acc_ref)

    x = xe_ref[...]
    h1 = _dot(x, w1_ref[...].astype(BF16))
    h3 = _dot(x, w3_ref[...].astype(BF16))
    hid = (h1 * jax.nn.sigmoid(h1) * h3).astype(BF16)
    acc_ref[...] += _dot(hid, w2_ref[...].astype(BF16))

    @pl.when(f == pl.num_programs(1) - 1)
    def _():
        _to_slabs(yc_ref, acc_ref[:CAPACITY, :] * gc_ref[...], CAPACITY)
        _to_slabs(yl_ref, acc_ref[CAPACITY:, :] * gl_ref[...], CAPACITY)


def _expert_ffn(xe_ctx, xe_lat, gate_ctx, gate_lat, w1, w3, w2, layer, name):
    nf = D_EXPERT // FF_TILE
    x_spec = pl.BlockSpec((None, CAPACITY, D_MODEL), lambda e, f: (e, 0, 0))
    g_spec = pl.BlockSpec((None, CAPACITY, 1), lambda e, f: (e, 0, 0))
    y_spec = pl.BlockSpec((None, CAPACITY * TOK_SLABS, LANES), lambda e, f: (e, 0, 0))
    y_shape = jax.ShapeDtypeStruct((N_EXPERTS, CAPACITY * TOK_SLABS, LANES), F32)
    return pl.pallas_call(
        _ffn_kernel,
        grid=(N_EXPERTS, nf),
        in_specs=[
            x_spec, x_spec, g_spec, g_spec,
            pl.BlockSpec((None, None, D_MODEL, FF_TILE), lambda e, f: (layer, e, 0, f)),
            pl.BlockSpec((None, None, D_MODEL, FF_TILE), lambda e, f: (layer, e, 0, f)),
            pl.BlockSpec((None, None, FF_TILE, D_MODEL), lambda e, f: (layer, e, f, 0)),
        ],
        out_specs=[y_spec, y_spec],
        out_shape=[y_shape, y_shape],
        scratch_shapes=[
            pltpu.VMEM((2 * CAPACITY, D_MODEL), BF16),
            pltpu.VMEM((2 * CAPACITY, D_MODEL), F32),
        ],
        compiler_params=_cp(56),
        name=name,
    )(xe_ctx, xe_lat, gate_ctx, gate_lat, w1, w3, w2)


SCATTER_UNROLL = 8


def _combine_kernel(idx_ref, ye_ref, y_ref):
    ex = pl.program_id(0)

    @pl.when(ex == 0)
    def _():
        y_ref[...] = jnp.zeros_like(y_ref)

    def body(i, carry):
        upd = []
        for u in range(SCATTER_UNROLL):
            s = i * SCATTER_UNROLL + u
            t = idx_ref[ex * CAPACITY + s]
            rows = pl.ds(pl.multiple_of(t * TOK_SLABS, TOK_SLABS), TOK_SLABS)
            src = ye_ref[pl.ds(pl.multiple_of(s * TOK_SLABS, TOK_SLABS), TOK_SLABS), :]
            upd.append((rows, y_ref[rows, :] + src))
        for rows, val in upd:
            y_ref[rows, :] = val
        return carry

    lax.fori_loop(0, CAPACITY // SCATTER_UNROLL, body, 0)


def _combine(idx_flat, ye, name):
    return pl.pallas_call(
        _combine_kernel,
        grid_spec=pltpu.PrefetchScalarGridSpec(
            num_scalar_prefetch=1,
            grid=(N_EXPERTS,),
            in_specs=[pl.BlockSpec((None, CAPACITY * TOK_SLABS, LANES), lambda e, idx: (e, 0, 0))],
            out_specs=pl.BlockSpec((N_TOK * TOK_SLABS, LANES), lambda e, idx: (0, 0)),
        ),
        out_shape=jax.ShapeDtypeStruct((N_TOK * TOK_SLABS, LANES), F32),
        compiler_params=_cp(56),
        name=name,
    )(idx_flat, ye)


def _moe(hs_ctx, lt_ctx, hs_lat, lt_lat, tri, blk, w1, w3, w2, layer):
    tag = "l%d" % layer
    routed = []
    for hs, lt, name in ((hs_ctx, lt_ctx, "ctx"), (hs_lat, lt_lat, "lat")):
        idx, gate = _router(lt, tri, blk, "router_%s_%s" % (tag, name))
        idx_flat = idx.reshape(N_EXPERTS * CAPACITY)
        xe = _gather(idx_flat, hs, "gather_%s_%s" % (tag, name))
        routed.append((idx_flat, xe, gate.reshape(N_EXPERTS, CAPACITY, 1)))
    (idx_c, xe_c, g_c), (idx_l, xe_l, g_l) = routed
    ye_c, ye_l = _expert_ffn(xe_c, xe_l, g_c, g_l, w1, w3, w2, layer, "ffn_" + tag)
    return _combine(idx_c, ye_c, "combine_%s_ctx" % tag), _combine(idx_l, ye_l, "combine_%s_lat" % tag)


def _gla_proj_kernel(xp_ref, yt_ref, gp_ref, g_ref, sh_ref, sc_ref, w_ref, wg_ref, bg_ref,
                     x_ref, q_ref, k_ref, v_ref, r_ref, lgf_ref, lgb_ref):
    x = xp_ref[...] + gp_ref[...] * _from_slabs(yt_ref, ROW_TILE)
    x_ref[...] = x
    h = _modulate(x, g_ref[...], sh_ref[...], sc_ref[...])
    z = _dot(h.astype(BF16), w_ref[...])
    nq = GLA_HEADS * GLA_DK
    nv = GLA_HEADS * GLA_DV
    q_ref[...] = z[:, :nq] * (GLA_DK ** -0.5)
    k_ref[...] = z[:, nq:2 * nq]
    v_ref[...] = z[:, 2 * nq:2 * nq + nv].astype(BF16)
    r_ref[...] = z[:, 2 * nq + nv:2 * nq + 2 * nv].astype(BF16)
    zg = z[:, 2 * nq + 2 * nv:].astype(BF16)
    a = _dot(zg, wg_ref[...]) + bg_ref[...]
    ls = (jnp.minimum(a, 0.0) - jnp.log1p(jnp.exp(-jnp.abs(a)))) * (1.0 / GLA_GATE_NORM)
    lgf_ref[...] = ls[:, :nq]
    lgb_ref[...] = ls[:, nq:]


def _gla_proj(x_prev, y_slabs, mods_prev, mods, norm_g, w_bf16, wg, bg, lat, name):
    nq = GLA_HEADS * GLA_DK
    nv = GLA_HEADS * GLA_DV
    n_in = w_bf16.shape[1]
    return pl.pallas_call(
        _gla_proj_kernel,
        grid=(N_ROW_TILES,),
        in_specs=[
            _row_spec(D_MODEL),
            pl.BlockSpec((ROW_TILE * TOK_SLABS, LANES), lambda i: (i, 0)),
            _mod_spec(5, lat),
            _full_spec((1, D_MODEL)),
            _mod_spec(0, lat),
            _mod_spec(1, lat),
            _full_spec((D_MODEL, n_in)),
            _full_spec((LANES, 2 * nq)),
            _full_spec((1, 2 * nq)),
        ],
        out_specs=[_row_spec(D_MODEL), _row_spec(nq), _row_spec(nq), _row_spec(nv), _row_spec(nv),
                   _row_spec(nq), _row_spec(nq)],
        out_shape=[
            jax.ShapeDtypeStruct((N_TOK, D_MODEL), F32),
            jax.ShapeDtypeStruct((N_TOK, nq), F32),
            jax.ShapeDtypeStruct((N_TOK, nq), F32),
            jax.ShapeDtypeStruct((N_TOK, nv), BF16),
            jax.ShapeDtypeStruct((N_TOK, nv), BF16),
            jax.ShapeDtypeStruct((N_TOK, nq), F32),
            jax.ShapeDtypeStruct((N_TOK, nq), F32),
        ],
        compiler_params=_cp(48),
        name=name,
    )(x_prev, y_slabs, mods_prev, norm_g, mods, mods, w_bf16, wg, bg)


GLA_LEVELS = (32, 16, 8)
GLA_UNROLL = 2
GLA_SAFE_DECAY = 60.0
NEG_BIG = -1e30


def _gla_intra_fast(qe, kc, b2, fwd):
    c = GLA_CHUNK
    row = lax.broadcasted_iota(jnp.int32, (c, 1), 0)
    col = lax.broadcasted_iota(jnp.int32, (1, c), 1)
    keep = (col <= row) if fwd else (col >= row)
    kq = (kc * jnp.exp2(-b2)).astype(BF16)
    return jnp.where(keep, _dot_nt(qe, kq), 0.0)


def _gla_intra_robust(qc, kc, b2, fwd):
    c = GLA_CHUNK
    row = lax.broadcasted_iota(jnp.int32, (c, 1), 0)
    col = lax.broadcasted_iota(jnp.int32, (1, c), 1)
    a = jnp.zeros((c, c), F32)
    for g in GLA_LEVELS:
        odd = ((row >> int(math.log2(g))) & 1) == 1
        later = odd if fwd else jnp.logical_not(odd)
        refs = []
        for p in range(c // (2 * g)):
            r0 = 2 * g * p + (g - 1 if fwd else g)
            refs.append(jnp.broadcast_to(b2[r0:r0 + 1], (2 * g, GLA_DK)))
        ref = jnp.concatenate(refs, axis=0) if len(refs) > 1 else refs[0]
        qt = qc * jnp.exp2(jnp.where(later, b2 - ref, NEG_BIG))
        kt = kc * jnp.exp2(jnp.where(later, NEG_BIG, ref - b2))
        same_parent = (row >> int(math.log2(2 * g))) == (col >> int(math.log2(2 * g)))
        a = a + jnp.where(same_parent, _dot_nt(qt.astype(BF16), kt.astype(BF16)), 0.0)

    sub = lax.broadcasted_iota(jnp.int32, (SUBLANES, 1), 0)
    strips = []
    for blk in range(c // SUBLANES):
        r0 = SUBLANES * blk
        qb = qc[r0:r0 + SUBLANES]
        bb = b2[r0:r0 + SUBLANES]
        strip = jnp.zeros((SUBLANES, c), F32)
        for jj in range(SUBLANES):
            j = r0 + jj
            cond = (sub >= jj) if fwd else (sub <= jj)
            t = qb * kc[j:j + 1] * jnp.exp2(jnp.where(cond, bb - b2[j:j + 1], NEG_BIG))
            strip = jnp.where(col == j, jnp.sum(t, axis=1, keepdims=True), strip)
        strips.append(strip)
    return a + jnp.concatenate(strips, axis=0)


def _gla_chunk(qc, kc, vc, gc, st, tri, fwd, fast):
    c = GLA_CHUNK
    g_hi, g_mid, g_lo = _split3(gc)
    b2 = (_dot(tri, g_hi) + _dot(tri, g_mid) + _dot(tri, g_lo)) * LOG2E
    btot2 = b2[c - 1:c] if fwd else b2[0:1]
    qe = (qc * jnp.exp2(b2)).astype(BF16)
    a = _gla_intra_fast(qe, kc, b2, fwd) if fast else _gla_intra_robust(qc, kc, b2, fwd)
    o = _dot_nt(qe, st.astype(BF16)) + _dot(a.astype(BF16), vc)
    kd = (kc * jnp.exp2(btot2 - b2)).astype(BF16)
    st_new = st * jnp.exp2(btot2) + _dot_tn(vc, kd)
    return o, st_new


def _gla_kernel(q_ref, k_ref, v_ref, r_ref, gf_ref, gb_ref, s0f_ref, s0b_ref, go_ref, trif_ref, trib_ref,
                u_ref, sf_ref, sb_ref, of_acc, ob_acc, stf_ref, stb_ref, *, t_len):
    n = t_len // GLA_CHUNK
    stf_ref[...] = s0f_ref[...].T
    stb_ref[...] = s0b_ref[...].T
    trif = trif_ref[...]
    trib = trib_ref[...]

    def chunk_total(g_ref):
        return jnp.min(jnp.sum(g_ref[...].reshape(n, GLA_CHUNK, GLA_DK), axis=1))

    safe = jnp.minimum(chunk_total(gf_ref), chunk_total(gb_ref)) >= -GLA_SAFE_DECAY

    def make_body(fast):
        def body(i, carry):
            for u in range(GLA_UNROLL):
                cf = i * GLA_UNROLL + u
                rf = pl.ds(pl.multiple_of(cf * GLA_CHUNK, GLA_CHUNK), GLA_CHUNK)
                rb = pl.ds(pl.multiple_of((n - 1 - cf) * GLA_CHUNK, GLA_CHUNK), GLA_CHUNK)
                o_f, stf_new = _gla_chunk(q_ref[rf, :], k_ref[rf, :], v_ref[rf, :], gf_ref[rf, :],
                                          stf_ref[...], trif, True, fast)
                o_b, stb_new = _gla_chunk(q_ref[rb, :], k_ref[rb, :], v_ref[rb, :], gb_ref[rb, :],
                                          stb_ref[...], trib, False, fast)
                stf_ref[...] = stf_new
                stb_ref[...] = stb_new
                of_acc[rf, :] = o_f
                ob_acc[rb, :] = o_b
            return carry
        return body

    @pl.when(safe)
    def _():
        lax.fori_loop(0, n // GLA_UNROLL, make_body(True), 0)

    @pl.when(jnp.logical_not(safe))
    def _():
        lax.fori_loop(0, n // GLA_UNROLL, make_body(False), 0)

    sf_ref[...] = stf_ref[...].T
    sb_ref[...] = stb_ref[...].T
    r = r_ref[...].astype(F32)
    u_ref[...] = (_rmsnorm(of_acc[...] + ob_acc[...], go_ref[...]) * (r * jax.nn.sigmoid(r))).astype(BF16)


def _gla_tri():
    i = np.arange(GLA_CHUNK)
    fwd = (i[None, :] <= i[:, None]).astype(np.float32)
    bwd = (i[None, :] >= i[:, None]).astype(np.float32)
    return jnp.asarray(fwd, BF16), jnp.asarray(bwd, BF16)


def _gla(q, k, v, r, lgf, lgb, s0f, s0b, g_out, n_batch, t_len, name):
    trif, trib = _gla_tri()
    qk_spec = pl.BlockSpec((None, t_len, GLA_DK), lambda b, h: (b, 0, h))
    v_spec = pl.BlockSpec((None, t_len, GLA_DV), lambda b, h: (b, 0, h))
    s_spec = pl.BlockSpec((None, None, GLA_DK, GLA_DV), lambda b, h: (b, h, 0, 0))
    const = lambda shape: pl.BlockSpec(shape, lambda b, h: (0,) * len(shape))
    s_shape = jax.ShapeDtypeStruct((n_batch, GLA_HEADS, GLA_DK, GLA_DV), F32)
    return pl.pallas_call(
        functools.partial(_gla_kernel, t_len=t_len),
        grid=(n_batch, GLA_HEADS),
        in_specs=[qk_spec, qk_spec, v_spec, v_spec, qk_spec, qk_spec, s_spec, s_spec,
                  const((1, GLA_DV)), const((GLA_CHUNK, GLA_CHUNK)), const((GLA_CHUNK, GLA_CHUNK))],
        out_specs=[v_spec, s_spec, s_spec],
        out_shape=[jax.ShapeDtypeStruct((n_batch, t_len, GLA_HEADS * GLA_DV), BF16), s_shape, s_shape],
        scratch_shapes=[pltpu.VMEM((t_len, GLA_DV), F32), pltpu.VMEM((t_len, GLA_DV), F32),
                        pltpu.VMEM((GLA_DV, GLA_DK), F32), pltpu.VMEM((GLA_DV, GLA_DK), F32)],
        compiler_params=_cp(48),
        name=name,
    )(q, k, v, r, lgf, lgb, s0f, s0b, g_out, trif, trib)


def _final_kernel(xp_ref, yt_ref, gp_ref, fg_ref, o_ref):
    x = xp_ref[...] + gp_ref[...] * _from_slabs(yt_ref, ROW_TILE)
    o_ref[...] = _rmsnorm(x, fg_ref[...])


def _final(x_prev, y_slabs, mods_prev, final_g, lat, name):
    return pl.pallas_call(
        _final_kernel,
        grid=(N_ROW_TILES,),
        in_specs=[
            _row_spec(D_MODEL),
            pl.BlockSpec((ROW_TILE * TOK_SLABS, LANES), lambda i: (i, 0)),
            _mod_spec(5, lat),
            _full_spec((1, D_MODEL)),
        ],
        out_specs=_row_spec(D_MODEL),
        out_shape=jax.ShapeDtypeStruct((N_TOK, D_MODEL), F32),
        compiler_params=_cp(32),
        name=name,
    )(x_prev, y_slabs, mods_prev, final_g)


def kernel(x_prompt, x_sample, cache_k, cache_v, state_fwd, state_bwd, c, c_ctx, w_mod, b_mod, norm_g,
           da_w_in, da_w_out, da_lam_q1, da_lam_k1, da_lam_q2, da_lam_k2, da_g_sub, gla_w_in, gla_w_gf2,
           gla_b_gf, gla_w_gb2, gla_b_gb, gla_g_out, gla_w_out, moe_w_router, moe_w1, moe_w3, moe_w2, final_g):
    n_ctx_b = x_prompt.shape[0]
    n_lat_b = x_sample.shape[0]
    x_ctx = x_prompt.reshape(N_TOK, D_MODEL)
    x_lat = x_sample.reshape(N_TOK, D_MODEL)

    cvec = jnp.zeros((SUBLANES, D_MODEL), F32).at[0].set(c_ctx).at[1:1 + n_lat_b].set(c)
    mods = _mod_params(cvec, w_mod, b_mod)
    tri, blk = _cumsum_consts()
    wr_t = [jnp.zeros((LANES, D_MODEL), BF16).at[:N_EXPERTS].set(moe_w_router[i].T.astype(BF16))
            for i in range(DEPTH)]
    ng = norm_g.reshape(DEPTH, 2, 1, D_MODEL)

    lam_init = 0.8 - 0.6 * math.exp(-0.3 * 0)
    w_in = da_w_in[0].astype(BF16)
    w_out = da_w_out[0].astype(BF16)
    lam_vecs = [v[0].reshape(1, DA_DIM) for v in (da_lam_q1, da_lam_k1, da_lam_q2, da_lam_k2)]
    g_sub = da_g_sub[0].reshape(1, DA_VDIM)

    qc, kc, vc, kf, vf = _da_proj(x_ctx, mods[0], ng[0, 0], w_in, lat=False)
    ql, kl, vl = _da_proj(x_lat, mods[0], ng[0, 0], w_in, lat=True, rope_tabs=_rope_tables())
    past = cache_k.shape[2]
    ck = cache_k[:, 0].reshape(n_lat_b * past, D_MODEL)
    cv = cache_v[:, 0].reshape(n_lat_b * past, D_MODEL)
    u_ctx = _attention(lam_vecs, g_sub, qc, [(kc, vc, SEQ)], n_ctx_b, SEQ, SEQ, lam_init, "attn_ctx")
    u_lat = _attention(lam_vecs, g_sub, ql, [(kl, vl, DEC_SEQ), (ck, cv, past)], n_lat_b, DEC_SEQ, 256,
                       lam_init, "attn_lat")

    x1_ctx, hs_ctx, lt_ctx = _outproj(u_ctx, w_out, x_ctx, mods[0], ng[0, 1], wr_t[0], False, "outproj0_ctx")
    x1_lat, hs_lat, lt_lat = _outproj(u_lat, w_out, x_lat, mods[0], ng[0, 1], wr_t[0], True, "outproj0_lat")
    y_ctx, y_lat = _moe(hs_ctx, lt_ctx, hs_lat, lt_lat, tri, blk, moe_w1, moe_w3, moe_w2, 0)

    nq = GLA_HEADS * GLA_DK
    gla_in = gla_w_in.shape[2]
    n_pad = (-gla_in) % LANES
    w_in1 = jnp.pad(gla_w_in[0], ((0, 0), (0, n_pad))).astype(BF16)
    wg = jnp.zeros((LANES, 2 * nq), F32)
    wg = wg.at[:GLA_GATE_RANK, :nq].set(gla_w_gf2[0]).at[GLA_GATE_RANK:2 * GLA_GATE_RANK, nq:].set(gla_w_gb2[0])
    wg = wg.astype(BF16)
    bg = jnp.concatenate([gla_b_gf[0], gla_b_gb[0]]).reshape(1, 2 * nq)
    w_out1 = gla_w_out[0].astype(BF16)
    g_out = gla_g_out[0].reshape(1, GLA_DV)

    def gla_side(x1, y, lat, n_b, t_len, s0f, s0b, tag):
        x2, q, k, v, r, lgf, lgb = _gla_proj(x1, y, mods[0], mods[1], ng[1, 0], w_in1, wg, bg, lat, "gla_proj_" + tag)
        sh3 = lambda a: a.reshape(n_b, t_len, a.shape[-1])
        u, sf, sb = _gla(sh3(q), sh3(k), sh3(v), sh3(r), sh3(lgf), sh3(lgb), s0f, s0b, g_out, n_b, t_len, "gla_" + tag)
        return x2, u.reshape(N_TOK, D_MODEL), sf, sb

    zeros_s = jnp.zeros((n_ctx_b, GLA_HEADS, GLA_DK, GLA_DV), F32)
    x2_ctx, ug_ctx, sf, sb = gla_side(x1_ctx, y_ctx, False, n_ctx_b, SEQ, zeros_s, zeros_s, "ctx")
    x2_lat, ug_lat, _, _ = gla_side(x1_lat, y_lat, True, n_lat_b, DEC_SEQ, state_fwd[:, 0], state_bwd[:, 0], "lat")

    x3_ctx, hs_ctx, lt_ctx = _outproj(ug_ctx, w_out1, x2_ctx, mods[1], ng[1, 1], wr_t[1], False, "outproj1_ctx")
    x3_lat, hs_lat, lt_lat = _outproj(ug_lat, w_out1, x2_lat, mods[1], ng[1, 1], wr_t[1], True, "outproj1_lat")
    y_ctx, y_lat = _moe(hs_ctx, lt_ctx, hs_lat, lt_lat, tri, blk, moe_w1, moe_w3, moe_w2, 1)

    fg = final_g.reshape(1, D_MODEL)
    y_prompt = _final(x3_ctx, y_ctx, mods[1], fg, False, "final_ctx").reshape(x_prompt.shape)
    y_sample = _final(x3_lat, y_lat, mods[1], fg, True, "final_lat").reshape(x_sample.shape)
    new_k = kf.reshape(n_ctx_b, 1, SEQ, 2 * DA_HEADS, DA_DIM)
    new_v = vf.reshape(n_ctx_b, 1, SEQ, DA_HEADS, DA_VDIM)
    return (y_prompt, y_sample, new_k, new_v, sf[:, None], sb[:, None])
```

```python
import functools
import math

import numpy as np
import jax
import jax.numpy as jnp
from jax import lax
from jax.experimental import pallas as pl
from jax.experimental.pallas import tpu as pltpu

F32 = jnp.float32
BF16 = jnp.bfloat16

D_MODEL = 1024
DEPTH = 2
SEQ = 256
DEC_SEQ = 2048
GRID_W = 64
N_TOK = 4096
DA_HEADS = 8
DA_DIM = 64
DA_VDIM = 128
ROPE_BASE = 10000.0
GLA_HEADS = 4
GLA_DK = 128
GLA_DV = 256
GLA_GATE_RANK = 16
GLA_GATE_NORM = 16.0
GLA_CHUNK = 64
N_EXPERTS = 16
CAPACITY = 512
D_EXPERT = 2048
EPS = 1e-6
F32_MIN_NORMAL = 2.0 ** -126
LOG2E = 1.4426950408889634

LANES = 128
SUBLANES = 8
ROW_TILE = 512
FF_TILE = 512
TOK_SLABS = D_MODEL // LANES
XE_STRIDE = CAPACITY + SUBLANES
MIB = 1024 * 1024


def _cp(vmem_mib, sem=None):
    return pltpu.CompilerParams(vmem_limit_bytes=vmem_mib * MIB, dimension_semantics=sem)


def _dot(a, b):
    return jnp.dot(a, b, preferred_element_type=F32)


def _dot_nt(a, b):
    return lax.dot_general(a, b, (((1,), (1,)), ((), ())), preferred_element_type=F32)


def _dot_tn(a, b):
    return lax.dot_general(a, b, (((0,), (0,)), ((), ())), preferred_element_type=F32)


def _rmsnorm(x, g):
    return x * lax.rsqrt(jnp.mean(x * x, axis=-1, keepdims=True) + EPS) * g


def _modulate(x, g, shift, scale):
    return _rmsnorm(x, g) * (1.0 + scale) + shift


def _split3(x):
    hi = x.astype(BF16)
    r = x - hi.astype(F32)
    mid = r.astype(BF16)
    lo = (r - mid.astype(F32)).astype(BF16)
    return hi, mid, lo


def _from_slabs(ref, rows):
    return jnp.concatenate([ref[pl.ds(c, rows, stride=TOK_SLABS), :] for c in range(TOK_SLABS)], axis=1)


def _to_slabs(ref, val, rows):
    for c in range(TOK_SLABS):
        ref[pl.ds(c, rows, stride=TOK_SLABS), :] = val[:, LANES * c:LANES * (c + 1)]


def _mod_kernel(c_ref, w_ref, b_ref, o_ref):
    c = c_ref[...]
    s = c * jax.nn.sigmoid(c)
    w = w_ref[...]
    s_hi = s.astype(BF16)
    s_lo = (s - s_hi.astype(F32)).astype(BF16)
    w_hi = w.astype(BF16)
    w_lo = (w - w_hi.astype(F32)).astype(BF16)
    o_ref[...] = _dot(s_hi, w_hi) + _dot(s_hi, w_lo) + _dot(s_lo, w_hi) + b_ref[...]


def _mod_params(cvec, w_mod, b_mod):
    n6 = 6 * D_MODEL
    out = pl.pallas_call(
        _mod_kernel,
        grid=(DEPTH, 6),
        in_specs=[
            pl.BlockSpec((SUBLANES, D_MODEL), lambda i, j: (0, 0)),
            pl.BlockSpec((None, D_MODEL, D_MODEL), lambda i, j: (i, 0, j)),
            pl.BlockSpec((None, 1, D_MODEL), lambda i, j: (i, 0, j)),
        ],
        out_specs=pl.BlockSpec((None, SUBLANES, D_MODEL), lambda i, j: (i, 0, j)),
        out_shape=jax.ShapeDtypeStruct((DEPTH, SUBLANES, n6), F32),
        compiler_params=_cp(32),
        name="mod_params",
    )(cvec, w_mod, b_mod.reshape(DEPTH, 1, n6))
    return out.reshape(DEPTH, SUBLANES * 6, 1, D_MODEL)


def _mod_spec(k, lat):
    tiles_per_batch = DEC_SEQ // ROW_TILE
    if lat:
        return pl.BlockSpec((None, 1, D_MODEL), lambda i: ((1 + i // tiles_per_batch) * 6 + k, 0, 0))
    return pl.BlockSpec((None, 1, D_MODEL), lambda i: (k, 0, 0))


def _row_spec(width, dtype_rows=ROW_TILE):
    return pl.BlockSpec((dtype_rows, width), lambda i: (i, 0))


def _full_spec(shape):
    nd = len(shape)
    return pl.BlockSpec(shape, lambda i: (0,) * nd)


N_ROW_TILES = N_TOK // ROW_TILE


def _da_proj_kernel(*refs, rope, emit_f32):
    x_ref, g_ref, sh_ref, sc_ref, w_ref = refs[:5]
    pos = 5
    if rope:
        cos_ref, sin_ref = refs[pos:pos + 2]
        pos += 2
    q_ref, k_ref, v_ref = refs[pos:pos + 3]
    pos += 3
    h = _modulate(x_ref[...], g_ref[...], sh_ref[...], sc_ref[...])
    z = _dot(h.astype(BF16), w_ref[...])
    q = z[:, :D_MODEL]
    k = z[:, D_MODEL:2 * D_MODEL]
    v = z[:, 2 * D_MODEL:]
    if emit_f32:
        kf_ref, vf_ref = refs[pos:pos + 2]
        kf_ref[...] = k
        vf_ref[...] = v
    if rope:
        reps = D_MODEL // LANES
        cos = jnp.concatenate([cos_ref[...]] * reps, axis=1)
        sin = jnp.concatenate([sin_ref[...]] * reps, axis=1)
        lane = lax.broadcasted_iota(jnp.int32, (1, D_MODEL), 1)
        first = (lane & 16) == 0

        def rot(t):
            partner = jnp.where(first, pltpu.roll(t, D_MODEL - 16, 1), pltpu.roll(t, 16, 1))
            return t * cos + partner * sin

        q = rot(q)
        k = rot(k)
    q_ref[...] = (q * (DA_DIM ** -0.5 * LOG2E)).astype(BF16)
    k_ref[...] = k.astype(BF16)
    v_ref[...] = v.astype(BF16)


def _da_proj(x, mods, norm_g, w_bf16, lat, rope_tabs=None):
    rope = rope_tabs is not None
    emit_f32 = not lat
    in_specs = [
        _row_spec(D_MODEL),
        _full_spec((1, D_MODEL)),
        _mod_spec(0, lat),
        _mod_spec(1, lat),
        _full_spec((D_MODEL, 3 * D_MODEL)),
    ]
    args = [x, norm_g, mods, mods, w_bf16]
    if rope:
        tiles_per_batch = DEC_SEQ // ROW_TILE
        tab_spec = pl.BlockSpec((ROW_TILE, LANES), lambda i: (i % tiles_per_batch, 0))
        in_specs += [tab_spec, tab_spec]
        args += list(rope_tabs)
    out_specs = [_row_spec(D_MODEL)] * 3
    out_shape = [jax.ShapeDtypeStruct((N_TOK, D_MODEL), BF16)] * 3
    if emit_f32:
        out_specs += [_row_spec(D_MODEL)] * 2
        out_shape += [jax.ShapeDtypeStruct((N_TOK, D_MODEL), F32)] * 2
    return pl.pallas_call(
        functools.partial(_da_proj_kernel, rope=rope, emit_f32=emit_f32),
        grid=(N_ROW_TILES,),
        in_specs=in_specs,
        out_specs=out_specs,
        out_shape=out_shape,
        compiler_params=_cp(48),
        name="da_proj_lat" if lat else "da_proj_ctx",
    )(*args)


def _rope_tables():
    t = jnp.arange(DEC_SEQ)
    rows = (t // GRID_W).astype(F32)
    cols = (t % GRID_W).astype(F32)
    half = DA_DIM // 4
    freqs = ROPE_BASE ** (-jnp.arange(half, dtype=F32) / half)
    ang_r = rows[:, None] * freqs
    ang_c = cols[:, None] * freqs
    cos64 = jnp.concatenate([jnp.cos(ang_r)] * 2 + [jnp.cos(ang_c)] * 2, axis=1)
    sin64 = jnp.concatenate([-jnp.sin(ang_r), jnp.sin(ang_r), -jnp.sin(ang_c), jnp.sin(ang_c)], axis=1)
    reps = LANES // DA_DIM
    return jnp.concatenate([cos64] * reps, axis=1), jnp.concatenate([sin64] * reps, axis=1)


KEY_CHUNK = 256
ATTN_Q_TILE = 128


def _attn_kernel(lq1_ref, lk1_ref, lq2_ref, lk2_ref, gs_ref, q_ref, *refs, lam_init, seg_lens):
    n_seg = len(seg_lens)
    kv_refs, o_ref = refs[:2 * n_seg], refs[2 * n_seg]
    s_refs = refs[2 * n_seg + 1:]
    lam = (jnp.exp(jnp.sum(lq1_ref[...] * lk1_ref[...], axis=-1, keepdims=True))
           - jnp.exp(jnp.sum(lq2_ref[...] * lk2_ref[...], axis=-1, keepdims=True)) + lam_init)
    lane = lax.broadcasted_iota(jnp.int32, (1, DA_VDIM), 1)
    first = lane < DA_DIM
    gs = gs_ref[...]
    chunks = [(s, r0) for s, n in enumerate(seg_lens) for r0 in range(0, n, KEY_CHUNK)]
    half = KEY_CHUNK // 2

    def softmax_pv(qm, sl, s_ref):
        m_acc = None
        for c, (seg, r0) in enumerate(chunks):
            kc = kv_refs[2 * seg][r0:r0 + KEY_CHUNK, sl].astype(BF16)
            s = _dot_nt(qm, kc)
            s_ref[:, c * KEY_CHUNK:(c + 1) * KEY_CHUNK] = s
            tile_max = jnp.maximum(s[:, :half], s[:, half:])
            m_acc = tile_max if m_acc is None else jnp.maximum(m_acc, tile_max)
        m = jnp.max(m_acc, axis=-1, keepdims=True)
        l_acc = None
        o = None
        for c, (seg, r0) in enumerate(chunks):
            p = jnp.exp2(s_ref[:, c * KEY_CHUNK:(c + 1) * KEY_CHUNK] - m)
            tile_sum = p[:, :half] + p[:, half:]
            l_acc = tile_sum if l_acc is None else l_acc + tile_sum
            pv = _dot(p.astype(BF16), kv_refs[2 * seg + 1][r0:r0 + KEY_CHUNK, sl].astype(BF16))
            o = pv if o is None else o + pv
        return o, jnp.sum(l_acc, axis=-1, keepdims=True)

    tq = q_ref.shape[0]
    for h in range(DA_HEADS):
        sl = slice(DA_VDIM * h, DA_VDIM * (h + 1))
        qh = q_ref[:, sl]
        zero = jnp.zeros_like(qh)
        qq = jnp.concatenate([jnp.where(first, qh, zero), jnp.where(first, zero, qh)], axis=0)
        oo, ll = softmax_pv(qq, sl, s_refs[h % 2])
        o = oo[:tq] * (1.0 / ll[:tq]) - oo[tq:] * (lam / ll[tq:])
        o_ref[:, sl] = (_rmsnorm(o, gs) * (1.0 - lam_init)).astype(BF16)


def _attention(lam_vecs, g_sub, q, kv_segs, n_batch, t_q, q_tile, lam_init, name):
    nq = t_q // q_tile
    vec_spec = pl.BlockSpec((1, DA_DIM), lambda b, i: (0, 0))
    kv_specs, kv_args = [], []
    for k, v, t_k in kv_segs:
        kv_specs += [pl.BlockSpec((t_k, D_MODEL), lambda b, i: (b, 0))] * 2
        kv_args += [k, v]
    seg_lens = tuple(t_k for _, _, t_k in kv_segs)
    return pl.pallas_call(
        functools.partial(_attn_kernel, lam_init=lam_init, seg_lens=seg_lens),
        grid=(n_batch, nq),
        in_specs=[vec_spec] * 4 + [
            pl.BlockSpec((1, DA_VDIM), lambda b, i: (0, 0)),
            pl.BlockSpec((q_tile, D_MODEL), lambda b, i: (b * nq + i, 0)),
        ] + kv_specs,
        out_specs=pl.BlockSpec((q_tile, D_MODEL), lambda b, i: (b * nq + i, 0)),
        out_shape=jax.ShapeDtypeStruct((n_batch * t_q, D_MODEL), BF16),
        scratch_shapes=[pltpu.VMEM((2 * q_tile, sum(seg_lens)), F32)] * 2,
        compiler_params=_cp(56),
        name=name,
    )(*lam_vecs, g_sub, q, *kv_args)


def _outproj_kernel(u_ref, w_ref, x_ref, gate_ref, g_ref, sh_ref, sc_ref, wr_ref, x1_ref, hs_ref, lt_ref):
    x1 = x_ref[...] + gate_ref[...] * _dot(u_ref[...], w_ref[...])
    x1_ref[...] = x1
    h2 = _modulate(x1, g_ref[...], sh_ref[...], sc_ref[...])
    _to_slabs(hs_ref, h2, ROW_TILE)
    lt_ref[...] = _dot_nt(wr_ref[...], h2.astype(BF16))


def _outproj(u, w_bf16, x, mods, norm_g, wr_t, lat, name):
    return pl.pallas_call(
        _outproj_kernel,
        grid=(N_ROW_TILES,),
        in_specs=[
            _row_spec(D_MODEL),
            _full_spec((D_MODEL, D_MODEL)),
            _row_spec(D_MODEL),
            _mod_spec(2, lat),
            _full_spec((1, D_MODEL)),
            _mod_spec(3, lat),
            _mod_spec(4, lat),
            _full_spec((LANES, D_MODEL)),
        ],
        out_specs=[
            _row_spec(D_MODEL),
            pl.BlockSpec((ROW_TILE * TOK_SLABS, LANES), lambda i: (i, 0)),
            pl.BlockSpec((LANES, ROW_TILE), lambda i: (0, i)),
        ],
        out_shape=[
            jax.ShapeDtypeStruct((N_TOK, D_MODEL), F32),
            jax.ShapeDtypeStruct((N_TOK * TOK_SLABS, LANES), F32),
            jax.ShapeDtypeStruct((LANES, N_TOK), F32),
        ],
        compiler_params=_cp(48),
        name=name,
    )(u, w_bf16, x, mods, norm_g, mods, mods, wr_t)


N_TOK_BLOCKS = N_TOK // LANES
STACK_ROWS = N_TOK_BLOCKS * N_EXPERTS


def _router_kernel(lt_ref, tri_ref, blk_ref, idx_ref, gate_ref):
    lt = lt_ref[...]
    e = jnp.exp(lt - jnp.max(lt, axis=0, keepdims=True))
    aff = e / jnp.sum(e, axis=0, keepdims=True)
    aff = jnp.where(aff >= F32_MIN_NORMAL, aff, 0.0)

    def count_ge(x):
        return jnp.sum((aff >= x).astype(F32), axis=1, keepdims=True)

    def step(mid_of):
        def body(_, c):
            lo, hi = c
            mid = jnp.minimum(jnp.maximum(mid_of(lo, hi), lo), hi)
            ok = count_ge(mid) >= float(CAPACITY)
            return jnp.where(ok, mid, lo), jnp.where(ok, hi, mid)
        return body

    lo = jnp.full((N_EXPERTS, 1), F32_MIN_NORMAL, F32)
    hi = jnp.full((N_EXPERTS, 1), 2.0, F32)
    lo, hi = lax.fori_loop(0, 8, step(lambda a, b: jnp.sqrt(a * b)), (lo, hi))
    lo, hi = lax.fori_loop(0, 40, step(lambda a, b: a + (b - a) * 0.5), (lo, hi))
    kth = jnp.max(jnp.where(aff < hi, aff, 0.0), axis=1, keepdims=True)
    thr = jnp.where(count_ge(lo) >= float(CAPACITY), kth, 0.0)

    tri = tri_ref[...]
    blk = blk_ref[...]

    def stack(x):
        return jnp.concatenate([x[:, LANES * b:LANES * (b + 1)] for b in range(N_TOK_BLOCKS)], axis=0)

    def unstack(x):
        return jnp.concatenate([x[N_EXPERTS * b:N_EXPERTS * (b + 1), :] for b in range(N_TOK_BLOCKS)], axis=1)

    def cumsum_stacked(mask_st):
        win = _dot(mask_st.astype(BF16), tri)
        tot = win[:, LANES - 1:LANES]
        off = _dot(blk, jnp.broadcast_to(tot, (STACK_ROWS, LANES)).astype(BF16))[:, :1]
        return win, off, tot

    gt = aff > thr
    eq = aff == thr
    need = float(CAPACITY) - jnp.sum(gt.astype(F32), axis=1, keepdims=True)
    eq_f = eq.astype(F32)
    ewin, eoff, _ = cumsum_stacked(stack(eq_f))
    eq_before = unstack(ewin + eoff) - eq_f
    sel = jnp.logical_or(gt, jnp.logical_and(eq, eq_before < need))
    cwin, coff, ctot = cumsum_stacked(stack(sel.astype(F32)))
    cend = coff + ctot
    cwin_b = cwin.astype(BF16)
    a_hi, a_mid, a_lo = _split3(stack(aff))

    row = lax.broadcasted_iota(jnp.int32, (STACK_ROWS, 1), 0)
    row_f = row.astype(F32)
    slot = lax.broadcasted_iota(jnp.int32, (1, CAPACITY), 1).astype(F32)
    lane_f = lax.broadcasted_iota(jnp.int32, (LANES, 1), 0).astype(F32)

    def per_expert(ex, carry):
        mine = (row & (N_EXPERTS - 1)) == ex
        before = jnp.logical_and(mine, cend <= slot)
        nblk = jnp.sum(before.astype(F32), axis=0, keepdims=True)
        base = jnp.sum(jnp.where(before, ctot, 0.0), axis=0, keepdims=True)
        target = nblk * float(N_EXPERTS) + jnp.asarray(ex, F32)
        pick = jnp.where(row_f == target, 1.0, 0.0).astype(BF16)
        cnt_in = _dot_tn(cwin_b, pick)
        lane_idx = jnp.sum((cnt_in <= slot - base).astype(F32), axis=0, keepdims=True)
        idx_ref[pl.ds(ex, 1), :] = (nblk * float(LANES) + lane_idx).astype(jnp.int32)
        aff_blk = _dot_tn(a_hi, pick) + _dot_tn(a_mid, pick) + _dot_tn(a_lo, pick)
        gate_ref[pl.ds(ex, 1), :] = jnp.sum(jnp.where(lane_f == lane_idx, aff_blk, 0.0), axis=0, keepdims=True)
        return carry

    lax.fori_loop(0, N_EXPERTS, per_expert, 0)


def _cumsum_consts():
    l = np.arange(LANES)
    tri = (l[:, None] <= l[None, :]).astype(np.float32)
    r = np.arange(STACK_ROWS)
    same = (r[:, None] % N_EXPERTS) == (r[None, :] % N_EXPERTS)
    earlier = (r[None, :] // N_EXPERTS) < (r[:, None] // N_EXPERTS)
    blk = (same & earlier).astype(np.float32)
    return jnp.asarray(tri, BF16), jnp.asarray(blk, BF16)


def _router(logits_t, tri, blk, name):
    return pl.pallas_call(
        _router_kernel,
        grid=(1,),
        in_specs=[
            pl.BlockSpec((N_EXPERTS, N_TOK), lambda i: (0, 0)),
            _full_spec((LANES, LANES)),
            _full_spec((STACK_ROWS, STACK_ROWS)),
        ],
        out_specs=[_full_spec((N_EXPERTS, CAPACITY))] * 2,
        out_shape=[jax.ShapeDtypeStruct((N_EXPERTS, CAPACITY), jnp.int32),
                   jax.ShapeDtypeStruct((N_EXPERTS, CAPACITY), F32)],
        compiler_params=_cp(48),
        name=name,
    )(logits_t, tri, blk)


GATHER_UNROLL = 8


def _gather_kernel(idx_ref, hs_ref, xe_ref, tile_ref):
    ex = pl.program_id(0)

    def body(i, carry):
        for u in range(GATHER_UNROLL):
            s = i * GATHER_UNROLL + u
            t = idx_ref[ex * CAPACITY + s]
            slab = hs_ref[pl.ds(pl.multiple_of(t * TOK_SLABS, TOK_SLABS), TOK_SLABS), :]
            tile_ref[pl.ds(s, TOK_SLABS, stride=XE_STRIDE), :] = slab
        return carry

    lax.fori_loop(0, CAPACITY // GATHER_UNROLL, body, 0)
    xe_ref[...] = jnp.concatenate(
        [tile_ref[pl.ds(c * XE_STRIDE, CAPACITY), :] for c in range(TOK_SLABS)], axis=1).astype(BF16)


def _gather(idx_flat, hs, name):
    return pl.pallas_call(
        _gather_kernel,
        grid_spec=pltpu.PrefetchScalarGridSpec(
            num_scalar_prefetch=1,
            grid=(N_EXPERTS,),
            in_specs=[pl.BlockSpec((N_TOK * TOK_SLABS, LANES), lambda e, idx: (0, 0), pipeline_mode=pl.Buffered(1))],
            out_specs=pl.BlockSpec((None, CAPACITY, D_MODEL), lambda e, idx: (e, 0, 0)),
            scratch_shapes=[pltpu.VMEM((TOK_SLABS * XE_STRIDE, LANES), F32)],
        ),
        out_shape=jax.ShapeDtypeStruct((N_EXPERTS, CAPACITY, D_MODEL), BF16),
        compiler_params=_cp(40),
        name=name,
    )(idx_flat, hs)


def _ffn_kernel(xc_ref, xl_ref, gc_ref, gl_ref, w1_ref, w3_ref, w2_ref, yc_ref, yl_ref, xe_ref, acc_ref):
    f = pl.program_id(1)

    @pl.when(f == 0)
    def _():
        xe_ref[:CAPACITY, :] = xc_ref[...]
        xe_ref[CAPACITY:, :] = xl_ref[...]
        acc_ref[...] = jnp.zeros_like(acc_ref)

    x = xe_ref[...]
    h1 = _dot(x, w1_ref[...].astype(BF16))
    h3 = _dot(x, w3_ref[...].astype(BF16))
    hid = (h1 * jax.nn.sigmoid(h1) * h3).astype(BF16)
    acc_ref[...] += _dot(hid, w2_ref[...].astype(BF16))

    @pl.when(f == pl.num_programs(1) - 1)
    def _():
        _to_slabs(yc_ref, acc_ref[:CAPACITY, :] * gc_ref[...], CAPACITY)
        _to_slabs(yl_ref, acc_ref[CAPACITY:, :] * gl_ref[...], CAPACITY)


def _expert_ffn(xe_ctx, xe_lat, gate_ctx, gate_lat, w1, w3, w2, layer, name):
    nf = D_EXPERT // FF_TILE
    x_spec = pl.BlockSpec((None, CAPACITY, D_MODEL), lambda e, f: (e, 0, 0))
    g_spec = pl.BlockSpec((None, CAPACITY, 1), lambda e, f: (e, 0, 0))
    y_spec = pl.BlockSpec((None, CAPACITY * TOK_SLABS, LANES), lambda e, f: (e, 0, 0))
    y_shape = jax.ShapeDtypeStruct((N_EXPERTS, CAPACITY * TOK_SLABS, LANES), F32)
    return pl.pallas_call(
        _ffn_kernel,
        grid=(N_EXPERTS, nf),
        in_specs=[
            x_spec, x_spec, g_spec, g_spec,
            pl.BlockSpec((None, None, D_MODEL, FF_TILE), lambda e, f: (layer, e, 0, f)),
            pl.BlockSpec((None, None, D_MODEL, FF_TILE), lambda e, f: (layer, e, 0, f)),
            pl.BlockSpec((None, None, FF_TILE, D_MODEL), lambda e, f: (layer, e, f, 0)),
        ],
        out_specs=[y_spec, y_spec],
        out_shape=[y_shape, y_shape],
        scratch_shapes=[
            pltpu.VMEM((2 * CAPACITY, D_MODEL), BF16),
            pltpu.VMEM((2 * CAPACITY, D_MODEL), F32),
        ],
        compiler_params=_cp(56),
        name=name,
    )(xe_ctx, xe_lat, gate_ctx, gate_lat, w1, w3, w2)


SCATTER_UNROLL = 8


def _combine_kernel(idx_ref, ye_ref, y_ref):
    ex = pl.program_id(0)

    @pl.when(ex == 0)
    def _():
        y_ref[...] = jnp.zeros_like(y_ref)

    def body(i, carry):
        upd = []
        for u in range(SCATTER_UNROLL):
            s = i * SCATTER_UNROLL + u
            t = idx_ref[ex * CAPACITY + s]
            rows = pl.ds(pl.multiple_of(t * TOK_SLABS, TOK_SLABS), TOK_SLABS)
            src = ye_ref[pl.ds(pl.multiple_of(s * TOK_SLABS, TOK_SLABS), TOK_SLABS), :]
            upd.append((rows, y_ref[rows, :] + src))
        for rows, val in upd:
            y_ref[rows, :] = val
        return carry

    lax.fori_loop(0, CAPACITY // SCATTER_UNROLL, body, 0)


def _combine(idx_flat, ye, name):
    return pl.pallas_call(
        _combine_kernel,
        grid_spec=pltpu.PrefetchScalarGridSpec(
            num_scalar_prefetch=1,
            grid=(N_EXPERTS,),
            in_specs=[pl.BlockSpec((None, CAPACITY * TOK_SLABS, LANES), lambda e, idx: (e, 0, 0))],
            out_specs=pl.BlockSpec((N_TOK * TOK_SLABS, LANES), lambda e, idx: (0, 0)),
        ),
        out_shape=jax.ShapeDtypeStruct((N_TOK * TOK_SLABS, LANES), F32),
        compiler_params=_cp(56),
        name=name,
    )(idx_flat, ye)


def _moe(hs_ctx, lt_ctx, hs_lat, lt_lat, tri, blk, w1, w3, w2, layer):
    tag = "l%d" % layer
    routed = []
    for hs, lt, name in ((hs_ctx, lt_ctx, "ctx"), (hs_lat, lt_lat, "lat")):
        idx, gate = _router(lt, tri, blk, "router_%s_%s" % (tag, name))
        idx_flat = idx.reshape(N_EXPERTS * CAPACITY)
        xe = _gather(idx_flat, hs, "gather_%s_%s" % (tag, name))
        routed.append((idx_flat, xe, gate.reshape(N_EXPERTS, CAPACITY, 1)))
    (idx_c, xe_c, g_c), (idx_l, xe_l, g_l) = routed
    ye_c, ye_l = _expert_ffn(xe_c, xe_l, g_c, g_l, w1, w3, w2, layer, "ffn_" + tag)
    return _combine(idx_c, ye_c, "combine_%s_ctx" % tag), _combine(idx_l, ye_l, "combine_%s_lat" % tag)


def _gla_proj_kernel(xp_ref, yt_ref, gp_ref, g_ref, sh_ref, sc_ref, w_ref, wg_ref, bg_ref,
                     x_ref, q_ref, k_ref, v_ref, r_ref, lgf_ref, lgb_ref):
    x = xp_ref[...] + gp_ref[...] * _from_slabs(yt_ref, ROW_TILE)
    x_ref[...] = x
    h = _modulate(x, g_ref[...], sh_ref[...], sc_ref[...])
    z = _dot(h.astype(BF16), w_ref[...])
    nq = GLA_HEADS * GLA_DK
    nv = GLA_HEADS * GLA_DV
    q_ref[...] = z[:, :nq] * (GLA_DK ** -0.5)
    k_ref[...] = z[:, nq:2 * nq]
    v_ref[...] = z[:, 2 * nq:2 * nq + nv].astype(BF16)
    r_ref[...] = z[:, 2 * nq + nv:2 * nq + 2 * nv].astype(BF16)
    zg = z[:, 2 * nq + 2 * nv:].astype(BF16)
    a = _dot(zg, wg_ref[...]) + bg_ref[...]
    ls = (jnp.minimum(a, 0.0) - jnp.log1p(jnp.exp(-jnp.abs(a)))) * (1.0 / GLA_GATE_NORM)
    lgf_ref[...] = ls[:, :nq]
    lgb_ref[...] = ls[:, nq:]


def _gla_proj(x_prev, y_slabs, mods_prev, mods, norm_g, w_bf16, wg, bg, lat, name):
    nq = GLA_HEADS * GLA_DK
    nv = GLA_HEADS * GLA_DV
    n_in = w_bf16.shape[1]
    return pl.pallas_call(
        _gla_proj_kernel,
        grid=(N_ROW_TILES,),
        in_specs=[
            _row_spec(D_MODEL),
            pl.BlockSpec((ROW_TILE * TOK_SLABS, LANES), lambda i: (i, 0)),
            _mod_spec(5, lat),
            _full_spec((1, D_MODEL)),
            _mod_spec(0, lat),
            _mod_spec(1, lat),
            _full_spec((D_MODEL, n_in)),
            _full_spec((LANES, 2 * nq)),
            _full_spec((1, 2 * nq)),
        ],
        out_specs=[_row_spec(D_MODEL), _row_spec(nq), _row_spec(nq), _row_spec(nv), _row_spec(nv),
                   _row_spec(nq), _row_spec(nq)],
        out_shape=[
            jax.ShapeDtypeStruct((N_TOK, D_MODEL), F32),
            jax.ShapeDtypeStruct((N_TOK, nq), F32),
            jax.ShapeDtypeStruct((N_TOK, nq), F32),
            jax.ShapeDtypeStruct((N_TOK, nv), BF16),
            jax.ShapeDtypeStruct((N_TOK, nv), BF16),
            jax.ShapeDtypeStruct((N_TOK, nq), F32),
            jax.ShapeDtypeStruct((N_TOK, nq), F32),
        ],
        compiler_params=_cp(48),
        name=name,
    )(x_prev, y_slabs, mods_prev, norm_g, mods, mods, w_bf16, wg, bg)


GLA_LEVELS = (32, 16, 8)
GLA_UNROLL = 8
GLA_GROUP = 4 * GLA_CHUNK
GLA_SAFE_DECAY = 60.0
NEG_BIG = -1e30


def _gla_intra_fast(qe, kc, b2, fwd):
    c = GLA_CHUNK
    row = lax.broadcasted_iota(jnp.int32, (c, 1), 0)
    col = lax.broadcasted_iota(jnp.int32, (1, c), 1)
    keep = (col <= row) if fwd else (col >= row)
    kq = (kc * jnp.exp2(-b2)).astype(BF16)
    return jnp.where(keep, _dot_nt(qe, kq), 0.0)


def _gla_intra_robust(qc, kc, b2, fwd):
    c = GLA_CHUNK
    row = lax.broadcasted_iota(jnp.int32, (c, 1), 0)
    col = lax.broadcasted_iota(jnp.int32, (1, c), 1)
    a = jnp.zeros((c, c), F32)
    for g in GLA_LEVELS:
        odd = ((row >> int(math.log2(g))) & 1) == 1
        later = odd if fwd else jnp.logical_not(odd)
        refs = []
        for p in range(c // (2 * g)):
            r0 = 2 * g * p + (g - 1 if fwd else g)
            refs.append(jnp.broadcast_to(b2[r0:r0 + 1], (2 * g, GLA_DK)))
        ref = jnp.concatenate(refs, axis=0) if len(refs) > 1 else refs[0]
        qt = qc * jnp.exp2(jnp.where(later, b2 - ref, NEG_BIG))
        kt = kc * jnp.exp2(jnp.where(later, NEG_BIG, ref - b2))
        same_parent = (row >> int(math.log2(2 * g))) == (col >> int(math.log2(2 * g)))
        a = a + jnp.where(same_parent, _dot_nt(qt.astype(BF16), kt.astype(BF16)), 0.0)

    sub = lax.broadcasted_iota(jnp.int32, (SUBLANES, 1), 0)
    strips = []
    for blk in range(c // SUBLANES):
        r0 = SUBLANES * blk
        qb = qc[r0:r0 + SUBLANES]
        bb = b2[r0:r0 + SUBLANES]
        strip = jnp.zeros((SUBLANES, c), F32)
        for jj in range(SUBLANES):
            j = r0 + jj
            cond = (sub >= jj) if fwd else (sub <= jj)
            t = qb * kc[j:j + 1] * jnp.exp2(jnp.where(cond, bb - b2[j:j + 1], NEG_BIG))
            strip = jnp.where(col == j, jnp.sum(t, axis=1, keepdims=True), strip)
        strips.append(strip)
    return a + jnp.concatenate(strips, axis=0)


def _gla_state_increment(kc, vc, b2f, b2b):
    c = GLA_CHUNK
    btf = b2f[c - 1:c]
    btb = b2b[0:1]
    kd = jnp.concatenate([(kc * jnp.exp2(btf - b2f)).astype(BF16), (kc * jnp.exp2(btb - b2b)).astype(BF16)], axis=1)
    return _dot_tn(vc, kd), jnp.concatenate([jnp.exp2(btf), jnp.exp2(btb)], axis=1)


def _gla_scores(qc, kc, b2f, b2b, fast):
    qe_f = (qc * jnp.exp2(b2f)).astype(BF16)
    qe_b = (qc * jnp.exp2(b2b)).astype(BF16)
    if fast:
        a = _gla_intra_fast(qe_f, kc, b2f, True) + _gla_intra_fast(qe_b, kc, b2b, False)
    else:
        a = _gla_intra_robust(qc, kc, b2f, True) + _gla_intra_robust(qc, kc, b2b, False)
    return jnp.concatenate([qe_f, qe_b], axis=1), a.astype(BF16)


def _gla_kernel(q_ref, k_ref, v_ref, r_ref, gf_ref, gb_ref, s0f_ref, s0b_ref, go_ref, trif_ref, trib_ref,
                u_ref, sf_ref, sb_ref,
                b2f_ref, b2b_ref, qe_ref, oi_ref, kv_ref, dec_ref, sst_ref, st_ref, *, t_len):
    c = GLA_CHUNK
    n = t_len // c

    def rows_of(ci):
        return pl.ds(pl.multiple_of(ci * c, c), c)

    def state_rows_of(ci):
        return pl.ds(pl.multiple_of(ci * GLA_DV, GLA_DV), GLA_DV)

    n_groups = t_len // GLA_GROUP
    cum_unroll = 2 if n_groups % 2 == 0 else 1

    def cum_body(i, carry):
        sums = []
        for u in range(cum_unroll):
            rows = pl.ds(pl.multiple_of((i * cum_unroll + u) * GLA_GROUP, GLA_GROUP), GLA_GROUP)
            for g_ref, tri_ref, b_ref in ((gf_ref, trif_ref, b2f_ref), (gb_ref, trib_ref, b2b_ref)):
                s3 = _dot(tri_ref[...], jnp.concatenate(_split3(g_ref[rows, :]), axis=1))
                sums.append((b_ref, rows, s3))
        for b_ref, rows, s3 in sums:
            b_ref[rows, :] = (s3[:, :GLA_DK] + s3[:, GLA_DK:2 * GLA_DK] + s3[:, 2 * GLA_DK:]) * LOG2E
        return carry

    lax.fori_loop(0, n_groups // cum_unroll, cum_body, 0)

    def chunk_total(g_ref):
        return jnp.min(jnp.sum(g_ref[...].reshape(n, c, GLA_DK), axis=1))

    safe = jnp.minimum(chunk_total(gf_ref), chunk_total(gb_ref)) >= -GLA_SAFE_DECAY

    def make_local(fast, unroll):
        def body(i, carry):
            chunks = [i * unroll + u for u in range(unroll)]
            for ci in chunks:
                rows = rows_of(ci)
                kv, dec = _gla_state_increment(k_ref[rows, :], v_ref[rows, :], b2f_ref[rows, :], b2b_ref[rows, :])
                kv_ref[state_rows_of(ci), :] = kv
                dec_ref[pl.ds(ci, 1), :] = dec
            scores = []
            for ci in chunks:
                rows = rows_of(ci)
                qe, a = _gla_scores(q_ref[rows, :], k_ref[rows, :], b2f_ref[rows, :], b2b_ref[rows, :], fast)
                qe_ref[rows, :] = qe
                scores.append(a)
            for ci, a in zip(chunks, scores):
                rows = rows_of(ci)
                oi_ref[rows, :] = _dot(a, v_ref[rows, :])
            return carry
        return body

    unroll = min(GLA_UNROLL, n)

    @pl.when(safe)
    def _():
        lax.fori_loop(0, n // unroll, make_local(True, unroll), 0)

    @pl.when(jnp.logical_not(safe))
    def _():
        lax.fori_loop(0, n, make_local(False, 1), 0)

    st_ref[...] = jnp.concatenate([s0f_ref[...].T, s0b_ref[...].T], axis=1)
    fl = slice(0, GLA_DK)
    bl = slice(GLA_DK, 2 * GLA_DK)

    def rec_body(i, carry):
        cf = i
        cb = n - 1 - i
        st = st_ref[...]
        sst_ref[state_rows_of(cf), fl] = st[:, fl].astype(BF16)
        sst_ref[state_rows_of(cb), bl] = st[:, bl].astype(BF16)
        dec = jnp.concatenate([dec_ref[pl.ds(cf, 1), :][:, fl], dec_ref[pl.ds(cb, 1), :][:, bl]], axis=1)
        kv = jnp.concatenate([kv_ref[state_rows_of(cf), fl], kv_ref[state_rows_of(cb), bl]], axis=1)
        st_ref[...] = st * dec + kv
        return carry

    lax.fori_loop(0, n, rec_body, 0)
    sf_ref[...] = st_ref[:, fl].T
    sb_ref[...] = st_ref[:, bl].T

    go = go_ref[...]

    def out_body(i, carry):
        chunks = [i * unroll + u for u in range(unroll)]
        inter = [_dot_nt(qe_ref[rows_of(ci), :], sst_ref[state_rows_of(ci), :]) for ci in chunks]
        for ci, o_inter in zip(chunks, inter):
            rows = rows_of(ci)
            r = r_ref[rows, :].astype(F32)
            u_ref[rows, :] = (_rmsnorm(oi_ref[rows, :] + o_inter, go) * (r * jax.nn.sigmoid(r))).astype(BF16)
        return carry

    lax.fori_loop(0, n // unroll, out_body, 0)


def _gla_tri():
    i = np.arange(GLA_GROUP)
    same = (i[:, None] // GLA_CHUNK) == (i[None, :] // GLA_CHUNK)
    fwd = (same & (i[None, :] <= i[:, None])).astype(np.float32)
    bwd = (same & (i[None, :] >= i[:, None])).astype(np.float32)
    return jnp.asarray(fwd, BF16), jnp.asarray(bwd, BF16)


def _gla(q, k, v, r, lgf, lgb, s0f, s0b, g_out, n_batch, t_len, name):
    trif, trib = _gla_tri()
    n_chunks = t_len // GLA_CHUNK
    qk_spec = pl.BlockSpec((None, t_len, GLA_DK), lambda b, h: (b, 0, h))
    v_spec = pl.BlockSpec((None, t_len, GLA_DV), lambda b, h: (b, 0, h))
    s_spec = pl.BlockSpec((None, None, GLA_DK, GLA_DV), lambda b, h: (b, h, 0, 0))
    const = lambda shape: pl.BlockSpec(shape, lambda b, h: (0,) * len(shape))
    s_shape = jax.ShapeDtypeStruct((n_batch, GLA_HEADS, GLA_DK, GLA_DV), F32)
    return pl.pallas_call(
        functools.partial(_gla_kernel, t_len=t_len),
        grid=(n_batch, GLA_HEADS),
        in_specs=[qk_spec, qk_spec, v_spec, v_spec, qk_spec, qk_spec, s_spec, s_spec,
                  const((1, GLA_DV)), const((GLA_GROUP, GLA_GROUP)), const((GLA_GROUP, GLA_GROUP))],
        out_specs=[v_spec, s_spec, s_spec],
        out_shape=[jax.ShapeDtypeStruct((n_batch, t_len, GLA_HEADS * GLA_DV), BF16), s_shape, s_shape],
        scratch_shapes=[
            pltpu.VMEM((t_len, GLA_DK), F32), pltpu.VMEM((t_len, GLA_DK), F32),
            pltpu.VMEM((t_len, 2 * GLA_DK), BF16),
            pltpu.VMEM((t_len, GLA_DV), F32),
            pltpu.VMEM((n_chunks * GLA_DV, 2 * GLA_DK), F32),
            pltpu.VMEM((max(n_chunks, SUBLANES), 2 * GLA_DK), F32),
            pltpu.VMEM((n_chunks * GLA_DV, 2 * GLA_DK), BF16),
            pltpu.VMEM((GLA_DV, 2 * GLA_DK), F32),
        ],
        compiler_params=_cp(48),
        name=name,
    )(q, k, v, r, lgf, lgb, s0f, s0b, g_out, trif, trib)


def _final_kernel(xp_ref, yt_ref, gp_ref, fg_ref, o_ref):
    x = xp_ref[...] + gp_ref[...] * _from_slabs(yt_ref, ROW_TILE)
    o_ref[...] = _rmsnorm(x, fg_ref[...])


def _final(x_prev, y_slabs, mods_prev, final_g, lat, name):
    return pl.pallas_call(
        _final_kernel,
        grid=(N_ROW_TILES,),
        in_specs=[
            _row_spec(D_MODEL),
            pl.BlockSpec((ROW_TILE * TOK_SLABS, LANES), lambda i: (i, 0)),
            _mod_spec(5, lat),
            _full_spec((1, D_MODEL)),
        ],
        out_specs=_row_spec(D_MODEL),
        out_shape=jax.ShapeDtypeStruct((N_TOK, D_MODEL), F32),
        compiler_params=_cp(32),
        name=name,
    )(x_prev, y_slabs, mods_prev, final_g)


def kernel(x_prompt, x_sample, cache_k, cache_v, state_fwd, state_bwd, c, c_ctx, w_mod, b_mod, norm_g,
           da_w_in, da_w_out, da_lam_q1, da_lam_k1, da_lam_q2, da_lam_k2, da_g_sub, gla_w_in, gla_w_gf2,
           gla_b_gf, gla_w_gb2, gla_b_gb, gla_g_out, gla_w_out, moe_w_router, moe_w1, moe_w3, moe_w2, final_g):
    n_ctx_b = x_prompt.shape[0]
    n_lat_b = x_sample.shape[0]
    x_ctx = x_prompt.reshape(N_TOK, D_MODEL)
    x_lat = x_sample.reshape(N_TOK, D_MODEL)

    cvec = jnp.zeros((SUBLANES, D_MODEL), F32).at[0].set(c_ctx).at[1:1 + n_lat_b].set(c)
    mods = _mod_params(cvec, w_mod, b_mod)
    tri, blk = _cumsum_consts()
    wr_t = [jnp.zeros((LANES, D_MODEL), BF16).at[:N_EXPERTS].set(moe_w_router[i].T.astype(BF16))
            for i in range(DEPTH)]
    ng = norm_g.reshape(DEPTH, 2, 1, D_MODEL)

    lam_init = 0.8 - 0.6 * math.exp(-0.3 * 0)
    w_in = da_w_in[0].astype(BF16)
    w_out = da_w_out[0].astype(BF16)
    lam_vecs = [v[0].reshape(1, DA_DIM) for v in (da_lam_q1, da_lam_k1, da_lam_q2, da_lam_k2)]
    g_sub = da_g_sub[0].reshape(1, DA_VDIM)

    qc, kc, vc, kf, vf = _da_proj(x_ctx, mods[0], ng[0, 0], w_in, lat=False)
    ql, kl, vl = _da_proj(x_lat, mods[0], ng[0, 0], w_in, lat=True, rope_tabs=_rope_tables())
    past = cache_k.shape[2]
    ck = cache_k[:, 0].reshape(n_lat_b * past, D_MODEL)
    cv = cache_v[:, 0].reshape(n_lat_b * past, D_MODEL)
    u_ctx = _attention(lam_vecs, g_sub, qc, [(kc, vc, SEQ)], n_ctx_b, SEQ, ATTN_Q_TILE, lam_init, "attn_ctx")
    u_lat = _attention(lam_vecs, g_sub, ql, [(kl, vl, DEC_SEQ), (ck, cv, past)], n_lat_b, DEC_SEQ, ATTN_Q_TILE,
                       lam_init, "attn_lat")

    x1_ctx, hs_ctx, lt_ctx = _outproj(u_ctx, w_out, x_ctx, mods[0], ng[0, 1], wr_t[0], False, "outproj0_ctx")
    x1_lat, hs_lat, lt_lat = _outproj(u_lat, w_out, x_lat, mods[0], ng[0, 1], wr_t[0], True, "outproj0_lat")
    y_ctx, y_lat = _moe(hs_ctx, lt_ctx, hs_lat, lt_lat, tri, blk, moe_w1, moe_w3, moe_w2, 0)

    nq = GLA_HEADS * GLA_DK
    gla_in = gla_w_in.shape[2]
    n_pad = (-gla_in) % LANES
    w_in1 = jnp.pad(gla_w_in[0], ((0, 0), (0, n_pad))).astype(BF16)
    wg = jnp.zeros((LANES, 2 * nq), F32)
    wg = wg.at[:GLA_GATE_RANK, :nq].set(gla_w_gf2[0]).at[GLA_GATE_RANK:2 * GLA_GATE_RANK, nq:].set(gla_w_gb2[0])
    wg = wg.astype(BF16)
    bg = jnp.concatenate([gla_b_gf[0], gla_b_gb[0]]).reshape(1, 2 * nq)
    w_out1 = gla_w_out[0].astype(BF16)
    g_out = gla_g_out[0].reshape(1, GLA_DV)

    def gla_side(x1, y, lat, n_b, t_len, s0f, s0b, tag):
        x2, q, k, v, r, lgf, lgb = _gla_proj(x1, y, mods[0], mods[1], ng[1, 0], w_in1, wg, bg, lat, "gla_proj_" + tag)
        sh3 = lambda a: a.reshape(n_b, t_len, a.shape[-1])
        u, sf, sb = _gla(sh3(q), sh3(k), sh3(v), sh3(r), sh3(lgf), sh3(lgb), s0f, s0b, g_out, n_b, t_len, "gla_" + tag)
        return x2, u.reshape(N_TOK, D_MODEL), sf, sb

    zeros_s = jnp.zeros((n_ctx_b, GLA_HEADS, GLA_DK, GLA_DV), F32)
    x2_ctx, ug_ctx, sf, sb = gla_side(x1_ctx, y_ctx, False, n_ctx_b, SEQ, zeros_s, zeros_s, "ctx")
    x2_lat, ug_lat, _, _ = gla_side(x1_lat, y_lat, True, n_lat_b, DEC_SEQ, state_fwd[:, 0], state_bwd[:, 0], "lat")

    x3_ctx, hs_ctx, lt_ctx = _outproj(ug_ctx, w_out1, x2_ctx, mods[1], ng[1, 1], wr_t[1], False, "outproj1_ctx")
    x3_lat, hs_lat, lt_lat = _outproj(ug_lat, w_out1, x2_lat, mods[1], ng[1, 1], wr_t[1], True, "outproj1_lat")
    y_ctx, y_lat = _moe(hs_ctx, lt_ctx, hs_lat, lt_lat, tri, blk, moe_w1, moe_w3, moe_w2, 1)

    fg = final_g.reshape(1, D_MODEL)
    y_prompt = _final(x3_ctx, y_ctx, mods[1], fg, False, "final_ctx").reshape(x_prompt.shape)
    y_sample = _final(x3_lat, y_lat, mods[1], fg, True, "final_lat").reshape(x_sample.shape)
    new_k = kf.reshape(n_ctx_b, 1, SEQ, 2 * DA_HEADS, DA_DIM)
    new_v = vf.reshape(n_ctx_b, 1, SEQ, DA_HEADS, DA_VDIM)
    return (y_prompt, y_sample, new_k, new_v, sf[:, None], sb[:, None])
```

```python
import functools
import math

import numpy as np
import jax
import jax.numpy as jnp
from jax import lax
from jax.experimental import pallas as pl
from jax.experimental.pallas import tpu as pltpu

F32 = jnp.float32
BF16 = jnp.bfloat16

D_MODEL = 1024
DEPTH = 2
SEQ = 256
DEC_SEQ = 2048
GRID_W = 64
N_TOK = 4096
DA_HEADS = 8
DA_DIM = 64
DA_VDIM = 128
ROPE_BASE = 10000.0
GLA_HEADS = 4
GLA_DK = 128
GLA_DV = 256
GLA_GATE_RANK = 16
GLA_GATE_NORM = 16.0
GLA_CHUNK = 64
N_EXPERTS = 16
CAPACITY = 512
D_EXPERT = 2048
EPS = 1e-6
F32_MIN_NORMAL = 2.0 ** -126
LOG2E = 1.4426950408889634

LANES = 128
SUBLANES = 8
ROW_TILE = 512
FF_TILE = 512
TOK_SLABS = D_MODEL // LANES
XE_STRIDE = CAPACITY + SUBLANES
MIB = 1024 * 1024


def _cp(vmem_mib, sem=None):
    return pltpu.CompilerParams(vmem_limit_bytes=vmem_mib * MIB, dimension_semantics=sem)


def _dot(a, b):
    return jnp.dot(a, b, preferred_element_type=F32)


def _dot_nt(a, b):
    return lax.dot_general(a, b, (((1,), (1,)), ((), ())), preferred_element_type=F32)


def _dot_tn(a, b):
    return lax.dot_general(a, b, (((0,), (0,)), ((), ())), preferred_element_type=F32)


def _rmsnorm(x, g):
    return x * lax.rsqrt(jnp.mean(x * x, axis=-1, keepdims=True) + EPS) * g


def _modulate(x, g, shift, scale):
    return _rmsnorm(x, g) * (1.0 + scale) + shift


def _split3(x):
    hi = x.astype(BF16)
    r = x - hi.astype(F32)
    mid = r.astype(BF16)
    lo = (r - mid.astype(F32)).astype(BF16)
    return hi, mid, lo


def _from_slabs(ref, rows):
    return jnp.concatenate([ref[pl.ds(c, rows, stride=TOK_SLABS), :] for c in range(TOK_SLABS)], axis=1)


def _to_slabs(ref, val, rows):
    for c in range(TOK_SLABS):
        ref[pl.ds(c, rows, stride=TOK_SLABS), :] = val[:, LANES * c:LANES * (c + 1)]


def _mod_kernel(c_ref, w_ref, b_ref, o_ref):
    c = c_ref[...]
    s = c * jax.nn.sigmoid(c)
    w = w_ref[...]
    s_hi = s.astype(BF16)
    s_lo = (s - s_hi.astype(F32)).astype(BF16)
    w_hi = w.astype(BF16)
    w_lo = (w - w_hi.astype(F32)).astype(BF16)
    o_ref[...] = _dot(s_hi, w_hi) + _dot(s_hi, w_lo) + _dot(s_lo, w_hi) + b_ref[...]


def _mod_params(cvec, w_mod, b_mod):
    n6 = 6 * D_MODEL
    out = pl.pallas_call(
        _mod_kernel,
        grid=(DEPTH, 6),
        in_specs=[
            pl.BlockSpec((SUBLANES, D_MODEL), lambda i, j: (0, 0)),
            pl.BlockSpec((None, D_MODEL, D_MODEL), lambda i, j: (i, 0, j)),
            pl.BlockSpec((None, 1, D_MODEL), lambda i, j: (i, 0, j)),
        ],
        out_specs=pl.BlockSpec((None, SUBLANES, D_MODEL), lambda i, j: (i, 0, j)),
        out_shape=jax.ShapeDtypeStruct((DEPTH, SUBLANES, n6), F32),
        compiler_params=_cp(32),
        name="mod_params",
    )(cvec, w_mod, b_mod.reshape(DEPTH, 1, n6))
    return out.reshape(DEPTH, SUBLANES * 6, 1, D_MODEL)


def _mod_spec(k, lat):
    tiles_per_batch = DEC_SEQ // ROW_TILE
    if lat:
        return pl.BlockSpec((None, 1, D_MODEL), lambda i: ((1 + i // tiles_per_batch) * 6 + k, 0, 0))
    return pl.BlockSpec((None, 1, D_MODEL), lambda i: (k, 0, 0))


def _row_spec(width, dtype_rows=ROW_TILE):
    return pl.BlockSpec((dtype_rows, width), lambda i: (i, 0))


def _full_spec(shape):
    nd = len(shape)
    return pl.BlockSpec(shape, lambda i: (0,) * nd)


N_ROW_TILES = N_TOK // ROW_TILE


def _da_proj_kernel(*refs, rope, emit_f32):
    x_ref, g_ref, sh_ref, sc_ref, w_ref = refs[:5]
    pos = 5
    if rope:
        cos_ref, sin_ref = refs[pos:pos + 2]
        pos += 2
    q_ref, k_ref, v_ref = refs[pos:pos + 3]
    pos += 3
    h = _modulate(x_ref[...], g_ref[...], sh_ref[...], sc_ref[...])
    z = _dot(h.astype(BF16), w_ref[...])
    q = z[:, :D_MODEL]
    k = z[:, D_MODEL:2 * D_MODEL]
    v = z[:, 2 * D_MODEL:]
    if emit_f32:
        kf_ref, vf_ref = refs[pos:pos + 2]
        kf_ref[...] = k
        vf_ref[...] = v
    if rope:
        reps = D_MODEL // LANES
        cos = jnp.concatenate([cos_ref[...]] * reps, axis=1)
        sin = jnp.concatenate([sin_ref[...]] * reps, axis=1)
        lane = lax.broadcasted_iota(jnp.int32, (1, D_MODEL), 1)
        first = (lane & 16) == 0

        def rot(t):
            partner = jnp.where(first, pltpu.roll(t, D_MODEL - 16, 1), pltpu.roll(t, 16, 1))
            return t * cos + partner * sin

        q = rot(q)
        k = rot(k)
    q_ref[...] = (q * (DA_DIM ** -0.5 * LOG2E)).astype(BF16)
    k_ref[...] = k.astype(BF16)
    v_ref[...] = v.astype(BF16)


def _da_proj(x, mods, norm_g, w_bf16, lat, rope_tabs=None):
    rope = rope_tabs is not None
    emit_f32 = not lat
    in_specs = [
        _row_spec(D_MODEL),
        _full_spec((1, D_MODEL)),
        _mod_spec(0, lat),
        _mod_spec(1, lat),
        _full_spec((D_MODEL, 3 * D_MODEL)),
    ]
    args = [x, norm_g, mods, mods, w_bf16]
    if rope:
        tiles_per_batch = DEC_SEQ // ROW_TILE
        tab_spec = pl.BlockSpec((ROW_TILE, LANES), lambda i: (i % tiles_per_batch, 0))
        in_specs += [tab_spec, tab_spec]
        args += list(rope_tabs)
    out_specs = [_row_spec(D_MODEL)] * 3
    out_shape = [jax.ShapeDtypeStruct((N_TOK, D_MODEL), BF16)] * 3
    if emit_f32:
        out_specs += [_row_spec(D_MODEL)] * 2
        out_shape += [jax.ShapeDtypeStruct((N_TOK, D_MODEL), F32)] * 2
    return pl.pallas_call(
        functools.partial(_da_proj_kernel, rope=rope, emit_f32=emit_f32),
        grid=(N_ROW_TILES,),
        in_specs=in_specs,
        out_specs=out_specs,
        out_shape=out_shape,
        compiler_params=_cp(48),
        name="da_proj_lat" if lat else "da_proj_ctx",
    )(*args)


def _rope_tables():
    t = jnp.arange(DEC_SEQ)
    rows = (t // GRID_W).astype(F32)
    cols = (t % GRID_W).astype(F32)
    half = DA_DIM // 4
    freqs = ROPE_BASE ** (-jnp.arange(half, dtype=F32) / half)
    ang_r = rows[:, None] * freqs
    ang_c = cols[:, None] * freqs
    cos64 = jnp.concatenate([jnp.cos(ang_r)] * 2 + [jnp.cos(ang_c)] * 2, axis=1)
    sin64 = jnp.concatenate([-jnp.sin(ang_r), jnp.sin(ang_r), -jnp.sin(ang_c), jnp.sin(ang_c)], axis=1)
    reps = LANES // DA_DIM
    return jnp.concatenate([cos64] * reps, axis=1), jnp.concatenate([sin64] * reps, axis=1)


KEY_CHUNK = 256
ATTN_Q_TILE = 128


def _attn_kernel(lq1_ref, lk1_ref, lq2_ref, lk2_ref, gs_ref, q_ref, *refs, lam_init, seg_lens):
    n_seg = len(seg_lens)
    kv_refs, o_ref = refs[:2 * n_seg], refs[2 * n_seg]
    s_refs = refs[2 * n_seg + 1:]
    lam = (jnp.exp(jnp.sum(lq1_ref[...] * lk1_ref[...], axis=-1, keepdims=True))
           - jnp.exp(jnp.sum(lq2_ref[...] * lk2_ref[...], axis=-1, keepdims=True)) + lam_init)
    lane = lax.broadcasted_iota(jnp.int32, (1, DA_VDIM), 1)
    first = lane < DA_DIM
    gs_col = gs_ref[...]
    chunks = [(s, r0) for s, n in enumerate(seg_lens) for r0 in range(0, n, KEY_CHUNK)]
    tq = q_ref.shape[0]
    groups = KEY_CHUNK // SUBLANES

    def softmax_pv_t(qq, sl, s_ref):
        m_acc = None
        for c, (seg, r0) in enumerate(chunks):
            kc = kv_refs[2 * seg][r0:r0 + KEY_CHUNK, sl].astype(BF16)
            st = _dot_nt(kc, qq)
            s_ref[c * KEY_CHUNK:(c + 1) * KEY_CHUNK, :] = st
            tile_max = jnp.max(st.reshape(groups, SUBLANES, 2 * tq), axis=0)
            m_acc = tile_max if m_acc is None else jnp.maximum(m_acc, tile_max)
        m = jnp.max(m_acc, axis=0, keepdims=True)
        l_acc = None
        ot = None
        for c, (seg, r0) in enumerate(chunks):
            p = jnp.exp2(s_ref[c * KEY_CHUNK:(c + 1) * KEY_CHUNK, :] - m)
            tile_sum = jnp.sum(p.reshape(groups, SUBLANES, 2 * tq), axis=0)
            l_acc = tile_sum if l_acc is None else l_acc + tile_sum
            pv = _dot_tn(kv_refs[2 * seg + 1][r0:r0 + KEY_CHUNK, sl].astype(BF16), p.astype(BF16))
            ot = pv if ot is None else ot + pv
        return ot, jnp.sum(l_acc, axis=0, keepdims=True)

    for h in range(DA_HEADS):
        sl = slice(DA_VDIM * h, DA_VDIM * (h + 1))
        qh = q_ref[:, sl]
        zero = jnp.zeros_like(qh)
        qq = jnp.concatenate([jnp.where(first, qh, zero), jnp.where(first, zero, qh)], axis=0)
        ot, l = softmax_pv_t(qq, sl, s_refs[h % 2])
        o_t = ot[:, :tq] * (1.0 / l[:, :tq]) - ot[:, tq:] * (lam / l[:, tq:])
        scale = lax.rsqrt(jnp.mean(o_t * o_t, axis=0, keepdims=True) + EPS) * (1.0 - lam_init)
        o_ref[:, sl] = (o_t * scale * gs_col).T.astype(BF16)


def _attention(lam_vecs, g_sub, q, kv_segs, n_batch, t_q, q_tile, lam_init, name):
    nq = t_q // q_tile
    vec_spec = pl.BlockSpec((1, DA_DIM), lambda b, i: (0, 0))
    kv_specs, kv_args = [], []
    for k, v, t_k in kv_segs:
        kv_specs += [pl.BlockSpec((t_k, D_MODEL), lambda b, i: (b, 0))] * 2
        kv_args += [k, v]
    seg_lens = tuple(t_k for _, _, t_k in kv_segs)
    return pl.pallas_call(
        functools.partial(_attn_kernel, lam_init=lam_init, seg_lens=seg_lens),
        grid=(n_batch, nq),
        in_specs=[vec_spec] * 4 + [
            pl.BlockSpec((DA_VDIM, 1), lambda b, i: (0, 0)),
            pl.BlockSpec((q_tile, D_MODEL), lambda b, i: (b * nq + i, 0)),
        ] + kv_specs,
        out_specs=pl.BlockSpec((q_tile, D_MODEL), lambda b, i: (b * nq + i, 0)),
        out_shape=jax.ShapeDtypeStruct((n_batch * t_q, D_MODEL), BF16),
        scratch_shapes=[pltpu.VMEM((sum(seg_lens), 2 * q_tile), F32)] * 2,
        compiler_params=_cp(56),
        name=name,
    )(*lam_vecs, g_sub.reshape(DA_VDIM, 1), q, *kv_args)


def _outproj_kernel(u_ref, w_ref, x_ref, gate_ref, g_ref, sh_ref, sc_ref, wr_ref, x1_ref, hs_ref, lt_ref):
    x1 = x_ref[...] + gate_ref[...] * _dot(u_ref[...], w_ref[...])
    x1_ref[...] = x1
    h2 = _modulate(x1, g_ref[...], sh_ref[...], sc_ref[...])
    _to_slabs(hs_ref, h2, ROW_TILE)
    lt_ref[...] = _dot_nt(wr_ref[...], h2.astype(BF16))


def _outproj(u, w_bf16, x, mods, norm_g, wr_t, lat, name):
    return pl.pallas_call(
        _outproj_kernel,
        grid=(N_ROW_TILES,),
        in_specs=[
            _row_spec(D_MODEL),
            _full_spec((D_MODEL, D_MODEL)),
            _row_spec(D_MODEL),
            _mod_spec(2, lat),
            _full_spec((1, D_MODEL)),
            _mod_spec(3, lat),
            _mod_spec(4, lat),
            _full_spec((LANES, D_MODEL)),
        ],
        out_specs=[
            _row_spec(D_MODEL),
            pl.BlockSpec((ROW_TILE * TOK_SLABS, LANES), lambda i: (i, 0)),
            pl.BlockSpec((LANES, ROW_TILE), lambda i: (0, i)),
        ],
        out_shape=[
            jax.ShapeDtypeStruct((N_TOK, D_MODEL), F32),
            jax.ShapeDtypeStruct((N_TOK * TOK_SLABS, LANES), F32),
            jax.ShapeDtypeStruct((LANES, N_TOK), F32),
        ],
        compiler_params=_cp(48),
        name=name,
    )(u, w_bf16, x, mods, norm_g, mods, mods, wr_t)


N_TOK_BLOCKS = N_TOK // LANES
STACK_ROWS = N_TOK_BLOCKS * N_EXPERTS


def _router_kernel(lt_ref, tri_ref, blk_ref, idx_ref, gate_ref):
    lt = lt_ref[...]
    e = jnp.exp(lt - jnp.max(lt, axis=0, keepdims=True))
    aff = e / jnp.sum(e, axis=0, keepdims=True)
    aff = jnp.where(aff >= F32_MIN_NORMAL, aff, 0.0)

    def count_ge(x):
        return jnp.sum((aff >= x).astype(F32), axis=1, keepdims=True)

    def step(mid_of):
        def body(_, c):
            lo, hi = c
            mid = jnp.minimum(jnp.maximum(mid_of(lo, hi), lo), hi)
            ok = count_ge(mid) >= float(CAPACITY)
            return jnp.where(ok, mid, lo), jnp.where(ok, hi, mid)
        return body

    lo = jnp.full((N_EXPERTS, 1), F32_MIN_NORMAL, F32)
    hi = jnp.full((N_EXPERTS, 1), 2.0, F32)
    lo, hi = lax.fori_loop(0, 8, step(lambda a, b: jnp.sqrt(a * b)), (lo, hi))
    lo, hi = lax.fori_loop(0, 28, step(lambda a, b: a + (b - a) * 0.5), (lo, hi))
    kth = jnp.max(jnp.where(aff < hi, aff, 0.0), axis=1, keepdims=True)
    thr = jnp.where(count_ge(lo) >= float(CAPACITY), kth, 0.0)

    tri = tri_ref[...]
    blk = blk_ref[...]

    def stack(x):
        return jnp.concatenate([x[:, LANES * b:LANES * (b + 1)] for b in range(N_TOK_BLOCKS)], axis=0)

    def unstack(x):
        return jnp.concatenate([x[N_EXPERTS * b:N_EXPERTS * (b + 1), :] for b in range(N_TOK_BLOCKS)], axis=1)

    def cumsum_stacked(mask_st):
        win = _dot(mask_st.astype(BF16), tri)
        tot = win[:, LANES - 1:LANES]
        off = _dot(blk, jnp.broadcast_to(tot, (STACK_ROWS, LANES)).astype(BF16))[:, :1]
        return win, off, tot

    gt = aff > thr
    eq = aff == thr
    need = float(CAPACITY) - jnp.sum(gt.astype(F32), axis=1, keepdims=True)
    eq_f = eq.astype(F32)
    ewin, eoff, _ = cumsum_stacked(stack(eq_f))
    eq_before = unstack(ewin + eoff) - eq_f
    sel = jnp.logical_or(gt, jnp.logical_and(eq, eq_before < need))
    cwin, coff, ctot = cumsum_stacked(stack(sel.astype(F32)))
    cend = coff + ctot
    cwin_b = cwin.astype(BF16)
    a_hi, a_mid, a_lo = _split3(stack(aff))

    row = lax.broadcasted_iota(jnp.int32, (STACK_ROWS, 1), 0)
    row_f = row.astype(F32)
    slot = lax.broadcasted_iota(jnp.int32, (1, CAPACITY), 1).astype(F32)
    lane_f = lax.broadcasted_iota(jnp.int32, (LANES, 1), 0).astype(F32)

    def per_expert(ex, carry):
        mine = (row & (N_EXPERTS - 1)) == ex
        before = jnp.logical_and(mine, cend <= slot)
        nblk = jnp.sum(before.astype(F32), axis=0, keepdims=True)
        base = jnp.sum(jnp.where(before, ctot, 0.0), axis=0, keepdims=True)
        target = nblk * float(N_EXPERTS) + jnp.asarray(ex, F32)
        pick = jnp.where(row_f == target, 1.0, 0.0).astype(BF16)
        cnt_in = _dot_tn(cwin_b, pick)
        lane_idx = jnp.sum((cnt_in <= slot - base).astype(F32), axis=0, keepdims=True)
        idx_ref[pl.ds(ex, 1), :] = (nblk * float(LANES) + lane_idx).astype(jnp.int32)
        aff_blk = _dot_tn(a_hi, pick) + _dot_tn(a_mid, pick) + _dot_tn(a_lo, pick)
        gate_ref[pl.ds(ex, 1), :] = jnp.sum(jnp.where(lane_f == lane_idx, aff_blk, 0.0), axis=0, keepdims=True)
        return carry

    lax.fori_loop(0, N_EXPERTS, per_expert, 0)


def _cumsum_consts():
    l = np.arange(LANES)
    tri = (l[:, None] <= l[None, :]).astype(np.float32)
    r = np.arange(STACK_ROWS)
    same = (r[:, None] % N_EXPERTS) == (r[None, :] % N_EXPERTS)
    earlier = (r[None, :] // N_EXPERTS) < (r[:, None] // N_EXPERTS)
    blk = (same & earlier).astype(np.float32)
    return jnp.asarray(tri, BF16), jnp.asarray(blk, BF16)


def _router(logits_t, tri, blk, name):
    return pl.pallas_call(
        _router_kernel,
        grid=(1,),
        in_specs=[
            pl.BlockSpec((N_EXPERTS, N_TOK), lambda i: (0, 0)),
            _full_spec((LANES, LANES)),
            _full_spec((STACK_ROWS, STACK_ROWS)),
        ],
        out_specs=[_full_spec((N_EXPERTS, CAPACITY))] * 2,
        out_shape=[jax.ShapeDtypeStruct((N_EXPERTS, CAPACITY), jnp.int32),
                   jax.ShapeDtypeStruct((N_EXPERTS, CAPACITY), F32)],
        compiler_params=_cp(48),
        name=name,
    )(logits_t, tri, blk)


GATHER_UNROLL = 8


def _gather_kernel(idx_ref, hs_ref, xe_ref, tile_ref):
    ex = pl.program_id(0)

    def body(i, carry):
        for u in range(GATHER_UNROLL):
            s = i * GATHER_UNROLL + u
            t = idx_ref[ex * CAPACITY + s]
            slab = hs_ref[pl.ds(pl.multiple_of(t * TOK_SLABS, TOK_SLABS), TOK_SLABS), :]
            tile_ref[pl.ds(s, TOK_SLABS, stride=XE_STRIDE), :] = slab
        return carry

    lax.fori_loop(0, CAPACITY // GATHER_UNROLL, body, 0)
    xe_ref[...] = jnp.concatenate(
        [tile_ref[pl.ds(c * XE_STRIDE, CAPACITY), :] for c in range(TOK_SLABS)], axis=1).astype(BF16)


def _gather(idx_flat, hs, name):
    return pl.pallas_call(
        _gather_kernel,
        grid_spec=pltpu.PrefetchScalarGridSpec(
            num_scalar_prefetch=1,
            grid=(N_EXPERTS,),
            in_specs=[pl.BlockSpec((N_TOK * TOK_SLABS, LANES), lambda e, idx: (0, 0), pipeline_mode=pl.Buffered(1))],
            out_specs=pl.BlockSpec((None, CAPACITY, D_MODEL), lambda e, idx: (e, 0, 0)),
            scratch_shapes=[pltpu.VMEM((TOK_SLABS * XE_STRIDE, LANES), F32)],
        ),
        out_shape=jax.ShapeDtypeStruct((N_EXPERTS, CAPACITY, D_MODEL), BF16),
        compiler_params=_cp(40),
        name=name,
    )(idx_flat, hs)


def _ffn_kernel(xc_ref, xl_ref, gc_ref, gl_ref, w1_ref, w3_ref, w2_ref, yc_ref, yl_ref, xe_ref, acc_ref):
    f = pl.program_id(1)

    @pl.when(f == 0)
    def _():
        xe_ref[:CAPACITY, :] = xc_ref[...]
        xe_ref[CAPACITY:, :] = xl_ref[...]
        acc_ref[...] = jnp.zeros_like(acc_ref)

    x = xe_ref[...]
    h1 = _dot(x, w1_ref[...].astype(BF16))
    h3 = _dot(x, w3_ref[...].astype(BF16))
    hid = (h1 * jax.nn.sigmoid(h1) * h3).astype(BF16)
    acc_ref[...] += _dot(hid, w2_ref[...].astype(BF16))

    @pl.when(f == pl.num_programs(1) - 1)
    def _():
        _to_slabs(yc_ref, acc_ref[:CAPACITY, :] * gc_ref[...], CAPACITY)
        _to_slabs(yl_ref, acc_ref[CAPACITY:, :] * gl_ref[...], CAPACITY)


def _expert_ffn(xe_ctx, xe_lat, gate_ctx, gate_lat, w1, w3, w2, layer, name):
    nf = D_EXPERT // FF_TILE
    x_spec = pl.BlockSpec((None, CAPACITY, D_MODEL), lambda e, f: (e, 0, 0))
    g_spec = pl.BlockSpec((None, CAPACITY, 1), lambda e, f: (e, 0, 0))
    y_spec = pl.BlockSpec((None, CAPACITY * TOK_SLABS, LANES), lambda e, f: (e, 0, 0))
    y_shape = jax.ShapeDtypeStruct((N_EXPERTS, CAPACITY * TOK_SLABS, LANES), F32)
    return pl.pallas_call(
        _ffn_kernel,
        grid=(N_EXPERTS, nf),
        in_specs=[
            x_spec, x_spec, g_spec, g_spec,
            pl.BlockSpec((None, None, D_MODEL, FF_TILE), lambda e, f: (layer, e, 0, f)),
            pl.BlockSpec((None, None, D_MODEL, FF_TILE), lambda e, f: (layer, e, 0, f)),
            pl.BlockSpec((None, None, FF_TILE, D_MODEL), lambda e, f: (layer, e, f, 0)),
        ],
        out_specs=[y_spec, y_spec],
        out_shape=[y_shape, y_shape],
        scratch_shapes=[
            pltpu.VMEM((2 * CAPACITY, D_MODEL), BF16),
            pltpu.VMEM((2 * CAPACITY, D_MODEL), F32),
        ],
        compiler_params=_cp(56),
        name=name,
    )(xe_ctx, xe_lat, gate_ctx, gate_lat, w1, w3, w2)


SCATTER_UNROLL = 8


def _combine_kernel(idx_ref, ye_ref, y_ref):
    ex = pl.program_id(0)

    @pl.when(ex == 0)
    def _():
        y_ref[...] = jnp.zeros_like(y_ref)

    def body(i, carry):
        upd = []
        for u in range(SCATTER_UNROLL):
            s = i * SCATTER_UNROLL + u
            t = idx_ref[ex * CAPACITY + s]
            rows = pl.ds(pl.multiple_of(t * TOK_SLABS, TOK_SLABS), TOK_SLABS)
            src = ye_ref[pl.ds(pl.multiple_of(s * TOK_SLABS, TOK_SLABS), TOK_SLABS), :]
            upd.append((rows, y_ref[rows, :] + src))
        for rows, val in upd:
            y_ref[rows, :] = val
        return carry

    lax.fori_loop(0, CAPACITY // SCATTER_UNROLL, body, 0)


def _combine(idx_flat, ye, name):
    return pl.pallas_call(
        _combine_kernel,
        grid_spec=pltpu.PrefetchScalarGridSpec(
            num_scalar_prefetch=1,
            grid=(N_EXPERTS,),
            in_specs=[pl.BlockSpec((None, CAPACITY * TOK_SLABS, LANES), lambda e, idx: (e, 0, 0))],
            out_specs=pl.BlockSpec((N_TOK * TOK_SLABS, LANES), lambda e, idx: (0, 0)),
        ),
        out_shape=jax.ShapeDtypeStruct((N_TOK * TOK_SLABS, LANES), F32),
        compiler_params=_cp(56),
        name=name,
    )(idx_flat, ye)


def _moe(hs_ctx, lt_ctx, hs_lat, lt_lat, tri, blk, w1, w3, w2, layer):
    tag = "l%d" % layer
    routed = []
    for hs, lt, name in ((hs_ctx, lt_ctx, "ctx"), (hs_lat, lt_lat, "lat")):
        idx, gate = _router(lt, tri, blk, "router_%s_%s" % (tag, name))
        idx_flat = idx.reshape(N_EXPERTS * CAPACITY)
        xe = _gather(idx_flat, hs, "gather_%s_%s" % (tag, name))
        routed.append((idx_flat, xe, gate.reshape(N_EXPERTS, CAPACITY, 1)))
    (idx_c, xe_c, g_c), (idx_l, xe_l, g_l) = routed
    ye_c, ye_l = _expert_ffn(xe_c, xe_l, g_c, g_l, w1, w3, w2, layer, "ffn_" + tag)
    return _combine(idx_c, ye_c, "combine_%s_ctx" % tag), _combine(idx_l, ye_l, "combine_%s_lat" % tag)


def _gla_proj_kernel(xp_ref, yt_ref, gp_ref, g_ref, sh_ref, sc_ref, w_ref, wg_ref, bg_ref,
                     x_ref, q_ref, k_ref, v_ref, r_ref, lgf_ref, lgb_ref):
    x = xp_ref[...] + gp_ref[...] * _from_slabs(yt_ref, ROW_TILE)
    x_ref[...] = x
    h = _modulate(x, g_ref[...], sh_ref[...], sc_ref[...])
    z = _dot(h.astype(BF16), w_ref[...])
    nq = GLA_HEADS * GLA_DK
    nv = GLA_HEADS * GLA_DV
    q_ref[...] = z[:, :nq] * (GLA_DK ** -0.5)
    k_ref[...] = z[:, nq:2 * nq]
    v_ref[...] = z[:, 2 * nq:2 * nq + nv].astype(BF16)
    r_ref[...] = z[:, 2 * nq + nv:2 * nq + 2 * nv].astype(BF16)
    zg = z[:, 2 * nq + 2 * nv:].astype(BF16)
    a = _dot(zg, wg_ref[...]) + bg_ref[...]
    ls = (jnp.minimum(a, 0.0) - jnp.log1p(jnp.exp(-jnp.abs(a)))) * (1.0 / GLA_GATE_NORM)
    lgf_ref[...] = ls[:, :nq]
    lgb_ref[...] = ls[:, nq:]


def _gla_proj(x_prev, y_slabs, mods_prev, mods, norm_g, w_bf16, wg, bg, lat, name):
    nq = GLA_HEADS * GLA_DK
    nv = GLA_HEADS * GLA_DV
    n_in = w_bf16.shape[1]
    return pl.pallas_call(
        _gla_proj_kernel,
        grid=(N_ROW_TILES,),
        in_specs=[
            _row_spec(D_MODEL),
            pl.BlockSpec((ROW_TILE * TOK_SLABS, LANES), lambda i: (i, 0)),
            _mod_spec(5, lat),
            _full_spec((1, D_MODEL)),
            _mod_spec(0, lat),
            _mod_spec(1, lat),
            _full_spec((D_MODEL, n_in)),
            _full_spec((LANES, 2 * nq)),
            _full_spec((1, 2 * nq)),
        ],
        out_specs=[_row_spec(D_MODEL), _row_spec(nq), _row_spec(nq), _row_spec(nv), _row_spec(nv),
                   _row_spec(nq), _row_spec(nq)],
        out_shape=[
            jax.ShapeDtypeStruct((N_TOK, D_MODEL), F32),
            jax.ShapeDtypeStruct((N_TOK, nq), F32),
            jax.ShapeDtypeStruct((N_TOK, nq), F32),
            jax.ShapeDtypeStruct((N_TOK, nv), BF16),
            jax.ShapeDtypeStruct((N_TOK, nv), BF16),
            jax.ShapeDtypeStruct((N_TOK, nq), F32),
            jax.ShapeDtypeStruct((N_TOK, nq), F32),
        ],
        compiler_params=_cp(48),
        name=name,
    )(x_prev, y_slabs, mods_prev, norm_g, mods, mods, w_bf16, wg, bg)


GLA_LEVELS = (32, 16, 8)
GLA_UNROLL = 8
GLA_GROUP = 4 * GLA_CHUNK
GLA_SAFE_DECAY = 60.0
NEG_BIG = -1e30


def _gla_intra_fast(qe, kc, b2, fwd):
    c = GLA_CHUNK
    row = lax.broadcasted_iota(jnp.int32, (c, 1), 0)
    col = lax.broadcasted_iota(jnp.int32, (1, c), 1)
    keep = (col <= row) if fwd else (col >= row)
    kq = (kc * jnp.exp2(-b2)).astype(BF16)
    return jnp.where(keep, _dot_nt(qe, kq), 0.0)


def _gla_intra_robust(qc, kc, b2, fwd):
    c = GLA_CHUNK
    row = lax.broadcasted_iota(jnp.int32, (c, 1), 0)
    col = lax.broadcasted_iota(jnp.int32, (1, c), 1)
    a = jnp.zeros((c, c), F32)
    for g in GLA_LEVELS:
        odd = ((row >> int(math.log2(g))) & 1) == 1
        later = odd if fwd else jnp.logical_not(odd)
        refs = []
        for p in range(c // (2 * g)):
            r0 = 2 * g * p + (g - 1 if fwd else g)
            refs.append(jnp.broadcast_to(b2[r0:r0 + 1], (2 * g, GLA_DK)))
        ref = jnp.concatenate(refs, axis=0) if len(refs) > 1 else refs[0]
        qt = qc * jnp.exp2(jnp.where(later, b2 - ref, NEG_BIG))
        kt = kc * jnp.exp2(jnp.where(later, NEG_BIG, ref - b2))
        same_parent = (row >> int(math.log2(2 * g))) == (col >> int(math.log2(2 * g)))
        a = a + jnp.where(same_parent, _dot_nt(qt.astype(BF16), kt.astype(BF16)), 0.0)

    sub = lax.broadcasted_iota(jnp.int32, (SUBLANES, 1), 0)
    strips = []
    for blk in range(c // SUBLANES):
        r0 = SUBLANES * blk
        qb = qc[r0:r0 + SUBLANES]
        bb = b2[r0:r0 + SUBLANES]
        strip = jnp.zeros((SUBLANES, c), F32)
        for jj in range(SUBLANES):
            j = r0 + jj
            cond = (sub >= jj) if fwd else (sub <= jj)
            t = qb * kc[j:j + 1] * jnp.exp2(jnp.where(cond, bb - b2[j:j + 1], NEG_BIG))
            strip = jnp.where(col == j, jnp.sum(t, axis=1, keepdims=True), strip)
        strips.append(strip)
    return a + jnp.concatenate(strips, axis=0)


def _gla_state_increment(kc, vc, b2f, b2b):
    c = GLA_CHUNK
    btf = b2f[c - 1:c]
    btb = b2b[0:1]
    kd = jnp.concatenate([(kc * jnp.exp2(btf - b2f)).astype(BF16), (kc * jnp.exp2(btb - b2b)).astype(BF16)], axis=1)
    return _dot_tn(vc, kd), jnp.concatenate([jnp.exp2(btf), jnp.exp2(btb)], axis=1)


def _gla_scores(qc, kc, b2f, b2b, fast):
    qe_f = (qc * jnp.exp2(b2f)).astype(BF16)
    qe_b = (qc * jnp.exp2(b2b)).astype(BF16)
    if fast:
        a = _gla_intra_fast(qe_f, kc, b2f, True) + _gla_intra_fast(qe_b, kc, b2b, False)
    else:
        a = _gla_intra_robust(qc, kc, b2f, True) + _gla_intra_robust(qc, kc, b2b, False)
    return jnp.concatenate([qe_f, qe_b], axis=1), a.astype(BF16)


def _gla_kernel(q_ref, k_ref, v_ref, r_ref, gf_ref, gb_ref, s0f_ref, s0b_ref, go_ref, trif_ref, trib_ref,
                u_ref, sf_ref, sb_ref,
                b2f_ref, b2b_ref, qe_ref, oi_ref, kv_ref, dec_ref, sst_ref, st_ref, *, t_len):
    c = GLA_CHUNK
    n = t_len // c

    def rows_of(ci):
        return pl.ds(pl.multiple_of(ci * c, c), c)

    def state_rows_of(ci):
        return pl.ds(pl.multiple_of(ci * GLA_DV, GLA_DV), GLA_DV)

    n_groups = t_len // GLA_GROUP
    cum_unroll = 2 if n_groups % 2 == 0 else 1

    def cum_body(i, carry):
        sums = []
        for u in range(cum_unroll):
            rows = pl.ds(pl.multiple_of((i * cum_unroll + u) * GLA_GROUP, GLA_GROUP), GLA_GROUP)
            for g_ref, tri_ref, b_ref in ((gf_ref, trif_ref, b2f_ref), (gb_ref, trib_ref, b2b_ref)):
                s3 = _dot(tri_ref[...], jnp.concatenate(_split3(g_ref[rows, :]), axis=1))
                sums.append((b_ref, rows, s3))
        for b_ref, rows, s3 in sums:
            b_ref[rows, :] = (s3[:, :GLA_DK] + s3[:, GLA_DK:2 * GLA_DK] + s3[:, 2 * GLA_DK:]) * LOG2E
        return carry

    lax.fori_loop(0, n_groups // cum_unroll, cum_body, 0)

    def chunk_total(g_ref):
        return jnp.min(jnp.sum(g_ref[...].reshape(n, c, GLA_DK), axis=1))

    safe = jnp.minimum(chunk_total(gf_ref), chunk_total(gb_ref)) >= -GLA_SAFE_DECAY

    def make_local(fast, unroll):
        def body(i, carry):
            chunks = [i * unroll + u for u in range(unroll)]
            for ci in chunks:
                rows = rows_of(ci)
                kv, dec = _gla_state_increment(k_ref[rows, :], v_ref[rows, :], b2f_ref[rows, :], b2b_ref[rows, :])
                kv_ref[state_rows_of(ci), :] = kv
                dec_ref[pl.ds(ci, 1), :] = dec
            scores = []
            for ci in chunks:
                rows = rows_of(ci)
                qe, a = _gla_scores(q_ref[rows, :], k_ref[rows, :], b2f_ref[rows, :], b2b_ref[rows, :], fast)
                qe_ref[rows, :] = qe
                scores.append(a)
            for ci, a in zip(chunks, scores):
                rows = rows_of(ci)
                oi_ref[rows, :] = _dot(a, v_ref[rows, :])
            return carry
        return body

    unroll = min(GLA_UNROLL, n)

    @pl.when(safe)
    def _():
        lax.fori_loop(0, n // unroll, make_local(True, unroll), 0)

    @pl.when(jnp.logical_not(safe))
    def _():
        lax.fori_loop(0, n, make_local(False, 1), 0)

    st_ref[...] = jnp.concatenate([s0f_ref[...].T, s0b_ref[...].T], axis=1)
    fl = slice(0, GLA_DK)
    bl = slice(GLA_DK, 2 * GLA_DK)

    def rec_body(i, carry):
        cf = i
        cb = n - 1 - i
        st = st_ref[...]
        sst_ref[state_rows_of(cf), fl] = st[:, fl].astype(BF16)
        sst_ref[state_rows_of(cb), bl] = st[:, bl].astype(BF16)
        dec = jnp.concatenate([dec_ref[pl.ds(cf, 1), :][:, fl], dec_ref[pl.ds(cb, 1), :][:, bl]], axis=1)
        kv = jnp.concatenate([kv_ref[state_rows_of(cf), fl], kv_ref[state_rows_of(cb), bl]], axis=1)
        st_ref[...] = st * dec + kv
        return carry

    lax.fori_loop(0, n, rec_body, 0)
    sf_ref[...] = st_ref[:, fl].T
    sb_ref[...] = st_ref[:, bl].T

    go = go_ref[...]

    def out_body(i, carry):
        chunks = [i * unroll + u for u in range(unroll)]
        inter = [_dot_nt(qe_ref[rows_of(ci), :], sst_ref[state_rows_of(ci), :]) for ci in chunks]
        for ci, o_inter in zip(chunks, inter):
            rows = rows_of(ci)
            r = r_ref[rows, :].astype(F32)
            u_ref[rows, :] = (_rmsnorm(oi_ref[rows, :] + o_inter, go) * (r * jax.nn.sigmoid(r))).astype(BF16)
        return carry

    lax.fori_loop(0, n // unroll, out_body, 0)


def _gla_tri():
    i = np.arange(GLA_GROUP)
    same = (i[:, None] // GLA_CHUNK) == (i[None, :] // GLA_CHUNK)
    fwd = (same & (i[None, :] <= i[:, None])).astype(np.float32)
    bwd = (same & (i[None, :] >= i[:, None])).astype(np.float32)
    return jnp.asarray(fwd, BF16), jnp.asarray(bwd, BF16)


def _gla(q, k, v, r, lgf, lgb, s0f, s0b, g_out, n_batch, t_len, name):
    trif, trib = _gla_tri()
    n_chunks = t_len // GLA_CHUNK
    qk_spec = pl.BlockSpec((None, t_len, GLA_DK), lambda b, h: (b, 0, h))
    v_spec = pl.BlockSpec((None, t_len, GLA_DV), lambda b, h: (b, 0, h))
    s_spec = pl.BlockSpec((None, None, GLA_DK, GLA_DV), lambda b, h: (b, h, 0, 0))
    const = lambda shape: pl.BlockSpec(shape, lambda b, h: (0,) * len(shape))
    s_shape = jax.ShapeDtypeStruct((n_batch, GLA_HEADS, GLA_DK, GLA_DV), F32)
    return pl.pallas_call(
        functools.partial(_gla_kernel, t_len=t_len),
        grid=(n_batch, GLA_HEADS),
        in_specs=[qk_spec, qk_spec, v_spec, v_spec, qk_spec, qk_spec, s_spec, s_spec,
                  const((1, GLA_DV)), const((GLA_GROUP, GLA_GROUP)), const((GLA_GROUP, GLA_GROUP))],
        out_specs=[v_spec, s_spec, s_spec],
        out_shape=[jax.ShapeDtypeStruct((n_batch, t_len, GLA_HEADS * GLA_DV), BF16), s_shape, s_shape],
        scratch_shapes=[
            pltpu.VMEM((t_len, GLA_DK), F32), pltpu.VMEM((t_len, GLA_DK), F32),
            pltpu.VMEM((t_len, 2 * GLA_DK), BF16),
            pltpu.VMEM((t_len, GLA_DV), F32),
            pltpu.VMEM((n_chunks * GLA_DV, 2 * GLA_DK), F32),
            pltpu.VMEM((max(n_chunks, SUBLANES), 2 * GLA_DK), F32),
            pltpu.VMEM((n_chunks * GLA_DV, 2 * GLA_DK), BF16),
            pltpu.VMEM((GLA_DV, 2 * GLA_DK), F32),
        ],
        compiler_params=_cp(48),
        name=name,
    )(q, k, v, r, lgf, lgb, s0f, s0b, g_out, trif, trib)


def _final_kernel(xp_ref, yt_ref, gp_ref, fg_ref, o_ref):
    x = xp_ref[...] + gp_ref[...] * _from_slabs(yt_ref, ROW_TILE)
    o_ref[...] = _rmsnorm(x, fg_ref[...])


def _final(x_prev, y_slabs, mods_prev, final_g, lat, name):
    return pl.pallas_call(
        _final_kernel,
        grid=(N_ROW_TILES,),
        in_specs=[
            _row_spec(D_MODEL),
            pl.BlockSpec((ROW_TILE * TOK_SLABS, LANES), lambda i: (i, 0)),
            _mod_spec(5, lat),
            _full_spec((1, D_MODEL)),
        ],
        out_specs=_row_spec(D_MODEL),
        out_shape=jax.ShapeDtypeStruct((N_TOK, D_MODEL), F32),
        compiler_params=_cp(32),
        name=name,
    )(x_prev, y_slabs, mods_prev, final_g)


def kernel(x_prompt, x_sample, cache_k, cache_v, state_fwd, state_bwd, c, c_ctx, w_mod, b_mod, norm_g,
           da_w_in, da_w_out, da_lam_q1, da_lam_k1, da_lam_q2, da_lam_k2, da_g_sub, gla_w_in, gla_w_gf2,
           gla_b_gf, gla_w_gb2, gla_b_gb, gla_g_out, gla_w_out, moe_w_router, moe_w1, moe_w3, moe_w2, final_g):
    n_ctx_b = x_prompt.shape[0]
    n_lat_b = x_sample.shape[0]
    x_ctx = x_prompt.reshape(N_TOK, D_MODEL)
    x_lat = x_sample.reshape(N_TOK, D_MODEL)

    cvec = jnp.zeros((SUBLANES, D_MODEL), F32).at[0].set(c_ctx).at[1:1 + n_lat_b].set(c)
    mods = _mod_params(cvec, w_mod, b_mod)
    tri, blk = _cumsum_consts()
    wr_t = [jnp.zeros((LANES, D_MODEL), BF16).at[:N_EXPERTS].set(moe_w_router[i].T.astype(BF16))
            for i in range(DEPTH)]
    ng = norm_g.reshape(DEPTH, 2, 1, D_MODEL)

    lam_init = 0.8 - 0.6 * math.exp(-0.3 * 0)
    w_in = da_w_in[0].astype(BF16)
    w_out = da_w_out[0].astype(BF16)
    lam_vecs = [v[0].reshape(1, DA_DIM) for v in (da_lam_q1, da_lam_k1, da_lam_q2, da_lam_k2)]
    g_sub = da_g_sub[0].reshape(1, DA_VDIM)

    qc, kc, vc, kf, vf = _da_proj(x_ctx, mods[0], ng[0, 0], w_in, lat=False)
    ql, kl, vl = _da_proj(x_lat, mods[0], ng[0, 0], w_in, lat=True, rope_tabs=_rope_tables())
    past = cache_k.shape[2]
    ck = cache_k[:, 0].reshape(n_lat_b * past, D_MODEL)
    cv = cache_v[:, 0].reshape(n_lat_b * past, D_MODEL)
    u_ctx = _attention(lam_vecs, g_sub, qc, [(kc, vc, SEQ)], n_ctx_b, SEQ, ATTN_Q_TILE, lam_init, "attn_ctx")
    u_lat = _attention(lam_vecs, g_sub, ql, [(kl, vl, DEC_SEQ), (ck, cv, past)], n_lat_b, DEC_SEQ, ATTN_Q_TILE,
                       lam_init, "attn_lat")

    x1_ctx, hs_ctx, lt_ctx = _outproj(u_ctx, w_out, x_ctx, mods[0], ng[0, 1], wr_t[0], False, "outproj0_ctx")
    x1_lat, hs_lat, lt_lat = _outproj(u_lat, w_out, x_lat, mods[0], ng[0, 1], wr_t[0], True, "outproj0_lat")
    y_ctx, y_lat = _moe(hs_ctx, lt_ctx, hs_lat, lt_lat, tri, blk, moe_w1, moe_w3, moe_w2, 0)

    nq = GLA_HEADS * GLA_DK
    gla_in = gla_w_in.shape[2]
    n_pad = (-gla_in) % LANES
    w_in1 = jnp.pad(gla_w_in[0], ((0, 0), (0, n_pad))).astype(BF16)
    wg = jnp.zeros((LANES, 2 * nq), F32)
    wg = wg.at[:GLA_GATE_RANK, :nq].set(gla_w_gf2[0]).at[GLA_GATE_RANK:2 * GLA_GATE_RANK, nq:].set(gla_w_gb2[0])
    wg = wg.astype(BF16)
    bg = jnp.concatenate([gla_b_gf[0], gla_b_gb[0]]).reshape(1, 2 * nq)
    w_out1 = gla_w_out[0].astype(BF16)
    g_out = gla_g_out[0].reshape(1, GLA_DV)

    def gla_side(x1, y, lat, n_b, t_len, s0f, s0b, tag):
        x2, q, k, v, r, lgf, lgb = _gla_proj(x1, y, mods[0], mods[1], ng[1, 0], w_in1, wg, bg, lat, "gla_proj_" + tag)
        sh3 = lambda a: a.reshape(n_b, t_len, a.shape[-1])
        u, sf, sb = _gla(sh3(q), sh3(k), sh3(v), sh3(r), sh3(lgf), sh3(lgb), s0f, s0b, g_out, n_b, t_len, "gla_" + tag)
        return x2, u.reshape(N_TOK, D_MODEL), sf, sb

    zeros_s = jnp.zeros((n_ctx_b, GLA_HEADS, GLA_DK, GLA_DV), F32)
    x2_ctx, ug_ctx, sf, sb = gla_side(x1_ctx, y_ctx, False, n_ctx_b, SEQ, zeros_s, zeros_s, "ctx")
    x2_lat, ug_lat, _, _ = gla_side(x1_lat, y_lat, True, n_lat_b, DEC_SEQ, state_fwd[:, 0], state_bwd[:, 0], "lat")

    x3_ctx, hs_ctx, lt_ctx = _outproj(ug_ctx, w_out1, x2_ctx, mods[1], ng[1, 1], wr_t[1], False, "outproj1_ctx")
    x3_lat, hs_lat, lt_lat = _outproj(ug_lat, w_out1, x2_lat, mods[1], ng[1, 1], wr_t[1], True, "outproj1_lat")
    y_ctx, y_lat = _moe(hs_ctx, lt_ctx, hs_lat, lt_lat, tri, blk, moe_w1, moe_w3, moe_w2, 1)

    fg = final_g.reshape(1, D_MODEL)
    y_prompt = _final(x3_ctx, y_ctx, mods[1], fg, False, "final_ctx").reshape(x_prompt.shape)
    y_sample = _final(x3_lat, y_lat, mods[1], fg, True, "final_lat").reshape(x_sample.shape)
    new_k = kf.reshape(n_ctx_b, 1, SEQ, 2 * DA_HEADS, DA_DIM)
    new_v = vf.reshape(n_ctx_b, 1, SEQ, DA_HEADS, DA_VDIM)
    return (y_prompt, y_sample, new_k, new_v, sf[:, None], sb[:, None])
```

```python
import functools
import math

import numpy as np
import jax
import jax.numpy as jnp
from jax import lax
from jax.experimental import pallas as pl
from jax.experimental.pallas import tpu as pltpu

F32 = jnp.float32
BF16 = jnp.bfloat16

D_MODEL = 1024
DEPTH = 2
SEQ = 256
DEC_SEQ = 2048
GRID_W = 64
N_TOK = 4096
DA_HEADS = 8
DA_DIM = 64
DA_VDIM = 128
ROPE_BASE = 10000.0
GLA_HEADS = 4
GLA_DK = 128
GLA_DV = 256
GLA_GATE_RANK = 16
GLA_GATE_NORM = 16.0
GLA_CHUNK = 64
N_EXPERTS = 16
CAPACITY = 512
D_EXPERT = 2048
EPS = 1e-6
F32_MIN_NORMAL = 2.0 ** -126
LOG2E = 1.4426950408889634

LANES = 128
SUBLANES = 8
ROW_TILE = 512
FF_TILE = 512
TOK_SLABS = D_MODEL // LANES
XE_STRIDE = CAPACITY + SUBLANES
MIB = 1024 * 1024


def _cp(vmem_mib, sem=None):
    return pltpu.CompilerParams(vmem_limit_bytes=vmem_mib * MIB, dimension_semantics=sem)


def _dot(a, b):
    return jnp.dot(a, b, preferred_element_type=F32)


def _dot_nt(a, b):
    return lax.dot_general(a, b, (((1,), (1,)), ((), ())), preferred_element_type=F32)


def _dot_tn(a, b):
    return lax.dot_general(a, b, (((0,), (0,)), ((), ())), preferred_element_type=F32)


def _rmsnorm(x, g):
    return x * lax.rsqrt(jnp.mean(x * x, axis=-1, keepdims=True) + EPS) * g


def _modulate(x, g, shift, scale):
    return _rmsnorm(x, g) * (1.0 + scale) + shift


def _split3(x):
    hi = x.astype(BF16)
    r = x - hi.astype(F32)
    mid = r.astype(BF16)
    lo = (r - mid.astype(F32)).astype(BF16)
    return hi, mid, lo


def _from_slabs(ref, rows, base=0):
    return jnp.concatenate([ref[pl.ds(base + c, rows, stride=TOK_SLABS), :] for c in range(TOK_SLABS)], axis=1)


def _to_slabs(ref, val, rows):
    for c in range(TOK_SLABS):
        ref[pl.ds(c, rows, stride=TOK_SLABS), :] = val[:, LANES * c:LANES * (c + 1)]


def _mod_kernel(c_ref, w_ref, b_ref, o_ref):
    c = c_ref[...]
    s = c * jax.nn.sigmoid(c)
    w = w_ref[...]
    s_hi = s.astype(BF16)
    s_lo = (s - s_hi.astype(F32)).astype(BF16)
    w_hi = w.astype(BF16)
    w_lo = (w - w_hi.astype(F32)).astype(BF16)
    o_ref[...] = _dot(s_hi, w_hi) + _dot(s_hi, w_lo) + _dot(s_lo, w_hi) + b_ref[...]


def _mod_params(cvec, w_mod, b_mod):
    n6 = 6 * D_MODEL
    out = pl.pallas_call(
        _mod_kernel,
        grid=(DEPTH, 6),
        in_specs=[
            pl.BlockSpec((SUBLANES, D_MODEL), lambda i, j: (0, 0)),
            pl.BlockSpec((None, D_MODEL, D_MODEL), lambda i, j: (i, 0, j)),
            pl.BlockSpec((None, 1, D_MODEL), lambda i, j: (i, 0, j)),
        ],
        out_specs=pl.BlockSpec((None, SUBLANES, D_MODEL), lambda i, j: (i, 0, j)),
        out_shape=jax.ShapeDtypeStruct((DEPTH, SUBLANES, n6), F32),
        compiler_params=_cp(32),
        name="mod_params",
    )(cvec, w_mod, b_mod.reshape(DEPTH, 1, n6))
    return out.reshape(DEPTH, SUBLANES * 6, 1, D_MODEL)


def _mod_spec(k, lat):
    tiles_per_batch = DEC_SEQ // ROW_TILE
    if lat:
        return pl.BlockSpec((None, 1, D_MODEL), lambda i: ((1 + i // tiles_per_batch) * 6 + k, 0, 0))
    return pl.BlockSpec((None, 1, D_MODEL), lambda i: (k, 0, 0))


def _row_spec(width, dtype_rows=ROW_TILE):
    return pl.BlockSpec((dtype_rows, width), lambda i: (i, 0))


def _full_spec(shape):
    nd = len(shape)
    return pl.BlockSpec(shape, lambda i: (0,) * nd)


N_ROW_TILES = N_TOK // ROW_TILE


def _weight_spec(rows, cols):
    return pl.BlockSpec((None, rows, cols), lambda i: (0, 0, 0), pipeline_mode=pl.Buffered(1))


def _cast_once(w_ref, wb_ref):
    @pl.when(pl.program_id(0) == 0)
    def _():
        wb_ref[...] = w_ref[...].astype(BF16)


def _da_proj_kernel(*refs, rope, emit_f32):
    x_ref, g_ref, sh_ref, sc_ref, w_ref = refs[:5]
    pos = 5
    if rope:
        cos_ref, sin_ref = refs[pos:pos + 2]
        pos += 2
    q_ref, k_ref, v_ref = refs[pos:pos + 3]
    pos += 3
    wb_ref = refs[-1]
    _cast_once(w_ref, wb_ref)
    h = _modulate(x_ref[...], g_ref[...], sh_ref[...], sc_ref[...])
    z = _dot(h.astype(BF16), wb_ref[...])
    q = z[:, :D_MODEL]
    k = z[:, D_MODEL:2 * D_MODEL]
    v = z[:, 2 * D_MODEL:]
    if emit_f32:
        kf_ref, vf_ref = refs[pos:pos + 2]
        kf_ref[...] = k
        vf_ref[...] = v
    if rope:
        reps = D_MODEL // LANES
        cos = jnp.concatenate([cos_ref[...]] * reps, axis=1)
        sin = jnp.concatenate([sin_ref[...]] * reps, axis=1)
        lane = lax.broadcasted_iota(jnp.int32, (1, D_MODEL), 1)
        first = (lane & 16) == 0

        def rot(t):
            partner = jnp.where(first, pltpu.roll(t, D_MODEL - 16, 1), pltpu.roll(t, 16, 1))
            return t * cos + partner * sin

        q = rot(q)
        k = rot(k)
    q_ref[...] = (q * (DA_DIM ** -0.5 * LOG2E)).astype(BF16)
    k_ref[...] = k.astype(BF16)
    v_ref[...] = v.astype(BF16)


def _da_proj(x, mods, norm_g, w_in, lat, rope_tabs=None):
    rope = rope_tabs is not None
    emit_f32 = not lat
    in_specs = [
        _row_spec(D_MODEL),
        _full_spec((1, D_MODEL)),
        _mod_spec(0, lat),
        _mod_spec(1, lat),
        _weight_spec(D_MODEL, 3 * D_MODEL),
    ]
    args = [x, norm_g, mods, mods, w_in]
    if rope:
        tiles_per_batch = DEC_SEQ // ROW_TILE
        tab_spec = pl.BlockSpec((ROW_TILE, LANES), lambda i: (i % tiles_per_batch, 0))
        in_specs += [tab_spec, tab_spec]
        args += list(rope_tabs)
    out_specs = [_row_spec(D_MODEL)] * 3
    out_shape = [jax.ShapeDtypeStruct((N_TOK, D_MODEL), BF16)] * 3
    if emit_f32:
        out_specs += [_row_spec(D_MODEL)] * 2
        out_shape += [jax.ShapeDtypeStruct((N_TOK, D_MODEL), F32)] * 2
    return pl.pallas_call(
        functools.partial(_da_proj_kernel, rope=rope, emit_f32=emit_f32),
        grid=(N_ROW_TILES,),
        in_specs=in_specs,
        out_specs=out_specs,
        out_shape=out_shape,
        scratch_shapes=[pltpu.VMEM((D_MODEL, 3 * D_MODEL), BF16)],
        compiler_params=_cp(56),
        name="da_proj_lat" if lat else "da_proj_ctx",
    )(*args)


def _rope_tables():
    t = np.arange(DEC_SEQ)
    rows = (t // GRID_W).astype(np.float32)
    cols = (t % GRID_W).astype(np.float32)
    half = DA_DIM // 4
    freqs = (np.float32(ROPE_BASE) ** (-np.arange(half, dtype=np.float32) / np.float32(half))).astype(np.float32)
    ang_r = rows[:, None] * freqs
    ang_c = cols[:, None] * freqs
    cos64 = np.concatenate([np.cos(ang_r)] * 2 + [np.cos(ang_c)] * 2, axis=1)
    sin64 = np.concatenate([-np.sin(ang_r), np.sin(ang_r), -np.sin(ang_c), np.sin(ang_c)], axis=1)
    reps = LANES // DA_DIM
    return (jnp.asarray(np.concatenate([cos64] * reps, axis=1), F32),
            jnp.asarray(np.concatenate([sin64] * reps, axis=1), F32))


ATTN_Q_TILE = 256


def _attn_kernel(lq1_ref, lk1_ref, lq2_ref, lk2_ref, gs_ref, q_ref, *refs, lam_init, seg_lens):
    n_seg = len(seg_lens)
    kv_refs, o_ref = refs[:2 * n_seg], refs[2 * n_seg]
    lam =(jnp.exp(jnp.sum(lq1_ref[...] * lk1_ref[...], axis=-1, keepdims=True))
           - jnp.exp(jnp.sum(lq2_ref[...] * lk2_ref[...], axis=-1, keepdims=True)) + lam_init)
    lane = lax.broadcasted_iota(jnp.int32, (1, DA_VDIM), 1)
    first = lane < DA_DIM
    gs = gs_ref[...]
    tq = q_ref.shape[0]

    def softmax_pv(qq, ks, vs):
        ss = [_dot_nt(qq, kk) for kk in ks]
        m = functools.reduce(jnp.maximum, [jnp.max(s, axis=-1, keepdims=True) for s in ss])
        ps = [jnp.exp2(s - m) for s in ss]
        l = functools.reduce(jnp.add, [jnp.sum(p, axis=-1, keepdims=True) for p in ps])
        o = functools.reduce(jnp.add, [_dot(p.astype(BF16), vv) for p, vv in zip(ps, vs)])
        return o, l

    for h in range(DA_HEADS):
        sl = slice(DA_VDIM * h, DA_VDIM * (h + 1))
        qh = q_ref[:, sl]
        ks = [kv_refs[2 * s][:, sl].astype(BF16) for s in range(n_seg)]
        vs = [kv_refs[2 * s + 1][:, sl].astype(BF16) for s in range(n_seg)]
        zero = jnp.zeros_like(qh)
        qq = jnp.concatenate([jnp.where(first, qh, zero), jnp.where(first, zero, qh)], axis=0)
        oo, ll = softmax_pv(qq, ks, vs)
        o = oo[:tq] * (1.0 / ll[:tq]) - oo[tq:] * (lam / ll[tq:])
        o_ref[:, sl] = (_rmsnorm(o, gs) * (1.0 - lam_init)).astype(BF16)


def _attention(lam_vecs, g_sub, q, kv_segs, n_batch, t_q, q_tile, lam_init, name):
    nq = t_q // q_tile
    vec_spec = pl.BlockSpec((1, DA_DIM), lambda b, i: (0, 0))
    kv_specs, kv_args = [], []
    for k, v, t_k in kv_segs:
        kv_specs += [pl.BlockSpec((t_k, D_MODEL), lambda b, i: (b, 0))] * 2
        kv_args += [k, v]
    seg_lens = tuple(t_k for _, _, t_k in kv_segs)
    return pl.pallas_call(
        functools.partial(_attn_kernel, lam_init=lam_init, seg_lens=seg_lens),
        grid=(n_batch, nq),
        in_specs=[vec_spec] * 4 + [
            pl.BlockSpec((1, DA_VDIM), lambda b, i: (0, 0)),
            pl.BlockSpec((q_tile, D_MODEL), lambda b, i: (b * nq + i, 0)),
        ] + kv_specs,
        out_specs=pl.BlockSpec((q_tile, D_MODEL), lambda b, i: (b * nq + i, 0)),
        out_shape=jax.ShapeDtypeStruct((n_batch * t_q, D_MODEL), BF16),
        compiler_params=_cp(56),
        name=name,
    )(*lam_vecs, g_sub, q, *kv_args)


def _outproj_kernel(u_ref, w_ref, x_ref, gate_ref, g_ref, sh_ref, sc_ref, wr_ref, x1_ref, hs_ref, lt_ref, wb_ref):
    _cast_once(w_ref, wb_ref)
    x1 = x_ref[...] + gate_ref[...] * _dot(u_ref[...], wb_ref[...])
    x1_ref[...] = x1
    h2 = _modulate(x1, g_ref[...], sh_ref[...], sc_ref[...])
    _to_slabs(hs_ref, h2, ROW_TILE)
    lt_ref[...] = _dot_nt(wr_ref[...], h2.astype(BF16))


def _outproj(u, w_out, x, mods, norm_g, wr_t, lat, name):
    return pl.pallas_call(
        _outproj_kernel,
        grid=(N_ROW_TILES,),
        in_specs=[
            _row_spec(D_MODEL),
            _weight_spec(D_MODEL, D_MODEL),
            _row_spec(D_MODEL),
            _mod_spec(2, lat),
            _full_spec((1, D_MODEL)),
            _mod_spec(3, lat),
            _mod_spec(4, lat),
            _full_spec((LANES, D_MODEL)),
        ],
        out_specs=[
            _row_spec(D_MODEL),
            pl.BlockSpec((ROW_TILE * TOK_SLABS, LANES), lambda i: (i, 0)),
            pl.BlockSpec((LANES, ROW_TILE), lambda i: (0, i)),
        ],
        out_shape=[
            jax.ShapeDtypeStruct((N_TOK, D_MODEL), F32),
            jax.ShapeDtypeStruct((N_TOK * TOK_SLABS, LANES), F32),
            jax.ShapeDtypeStruct((LANES, N_TOK), F32),
        ],
        scratch_shapes=[pltpu.VMEM((D_MODEL, D_MODEL), BF16)],
        compiler_params=_cp(48),
        name=name,
    )(u, w_out, x, mods, norm_g, mods, mods, wr_t)


N_TOK_BLOCKS = N_TOK // LANES
STACK_ROWS = N_TOK_BLOCKS * N_EXPERTS


def _router_kernel(lt_ref, tri_ref, blk_ref, idx_ref, gate_ref):
    lt = lt_ref[...]
    e = jnp.exp(lt - jnp.max(lt, axis=0, keepdims=True))
    aff = e / jnp.sum(e, axis=0, keepdims=True)
    aff = jnp.where(aff >= F32_MIN_NORMAL, aff, 0.0)

    def count_ge(x):
        return jnp.sum((aff >= x).astype(F32), axis=1, keepdims=True)

    def step(mid_of):
        def body(_, c):
            lo, hi = c
            mid = jnp.minimum(jnp.maximum(mid_of(lo, hi), lo), hi)
            ok = count_ge(mid) >= float(CAPACITY)
            return jnp.where(ok, mid, lo), jnp.where(ok, hi, mid)
        return body

    lo = jnp.full((N_EXPERTS, 1), F32_MIN_NORMAL, F32)
    hi = jnp.full((N_EXPERTS, 1), 2.0, F32)
    lo, hi = lax.fori_loop(0, 8, step(lambda a, b: jnp.sqrt(a * b)), (lo, hi))
    lo, hi = lax.fori_loop(0, 28, step(lambda a, b: a + (b - a) * 0.5), (lo, hi))
    kth = jnp.max(jnp.where(aff < hi, aff, 0.0), axis=1, keepdims=True)
    thr = jnp.where(count_ge(lo) >= float(CAPACITY), kth, 0.0)

    tri = tri_ref[...]
    blk = blk_ref[...]

    def stack(x):
        return jnp.concatenate([x[:, LANES * b:LANES * (b + 1)] for b in range(N_TOK_BLOCKS)], axis=0)

    def unstack(x):
        return jnp.concatenate([x[N_EXPERTS * b:N_EXPERTS * (b + 1), :] for b in range(N_TOK_BLOCKS)], axis=1)

    def cumsum_stacked(mask_st):
        win = _dot(mask_st.astype(BF16), tri)
        tot = win[:, LANES - 1:LANES]
        off = _dot(blk, jnp.broadcast_to(tot, (STACK_ROWS, LANES)).astype(BF16))[:, :1]
        return win, off, tot

    gt = aff > thr
    eq = aff == thr
    need = float(CAPACITY) - jnp.sum(gt.astype(F32), axis=1, keepdims=True)
    eq_f = eq.astype(F32)
    ewin, eoff, _ = cumsum_stacked(stack(eq_f))
    eq_before = unstack(ewin + eoff) - eq_f
    sel = jnp.logical_or(gt, jnp.logical_and(eq, eq_before < need))
    cwin, coff, ctot = cumsum_stacked(stack(sel.astype(F32)))
    cend = coff + ctot
    cwin_b = cwin.astype(BF16)
    a_hi, a_mid, a_lo = _split3(stack(aff))

    row = lax.broadcasted_iota(jnp.int32, (STACK_ROWS, 1), 0)
    row_f = row.astype(F32)
    slot = lax.broadcasted_iota(jnp.int32, (1, CAPACITY), 1).astype(F32)
    lane_f = lax.broadcasted_iota(jnp.int32, (LANES, 1), 0).astype(F32)

    def per_expert(ex, carry):
        mine = (row & (N_EXPERTS - 1)) == ex
        before = jnp.logical_and(mine, cend <= slot)
        nblk = jnp.sum(before.astype(F32), axis=0, keepdims=True)
        base = jnp.sum(jnp.where(before, ctot, 0.0), axis=0, keepdims=True)
        target = nblk * float(N_EXPERTS) + jnp.asarray(ex, F32)
        pick = jnp.where(row_f == target, 1.0, 0.0).astype(BF16)
        cnt_in = _dot_tn(cwin_b, pick)
        lane_idx = jnp.sum((cnt_in <= slot - base).astype(F32), axis=0, keepdims=True)
        idx_ref[pl.ds(ex, 1), :] = (nblk * float(LANES) + lane_idx).astype(jnp.int32)
        aff_blk = _dot_tn(a_hi, pick) + _dot_tn(a_mid, pick) + _dot_tn(a_lo, pick)
        gate_ref[pl.ds(ex, 1), :] = jnp.sum(jnp.where(lane_f == lane_idx, aff_blk, 0.0), axis=0, keepdims=True)
        return carry

    lax.fori_loop(0, N_EXPERTS, per_expert, 0)


def _cumsum_consts():
    l = np.arange(LANES)
    tri = (l[:, None] <= l[None, :]).astype(np.float32)
    r = np.arange(STACK_ROWS)
    same = (r[:, None] % N_EXPERTS) == (r[None, :] % N_EXPERTS)
    earlier = (r[None, :] // N_EXPERTS) < (r[:, None] // N_EXPERTS)
    blk = (same & earlier).astype(np.float32)
    return jnp.asarray(tri, BF16), jnp.asarray(blk, BF16)


def _router(logits_t, tri, blk, name):
    return pl.pallas_call(
        _router_kernel,
        grid=(1,),
        in_specs=[
            pl.BlockSpec((N_EXPERTS, N_TOK), lambda i: (0, 0)),
            _full_spec((LANES, LANES)),
            _full_spec((STACK_ROWS, STACK_ROWS)),
        ],
        out_specs=[_full_spec((N_EXPERTS, CAPACITY))] * 2,
        out_shape=[jax.ShapeDtypeStruct((N_EXPERTS, CAPACITY), jnp.int32),
                   jax.ShapeDtypeStruct((N_EXPERTS, CAPACITY), F32)],
        compiler_params=_cp(48),
        name=name,
    )(logits_t, tri, blk)


GATHER_UNROLL = 8


def _gather_kernel(idx_ref, hs_ref, xe_ref, tile_ref):
    ex = pl.program_id(0)

    def body(i, carry):
        for u in range(GATHER_UNROLL):
            s = i * GATHER_UNROLL + u
            t = idx_ref[ex * CAPACITY + s]
            slab = hs_ref[pl.ds(pl.multiple_of(t * TOK_SLABS, TOK_SLABS), TOK_SLABS), :]
            tile_ref[pl.ds(s, TOK_SLABS, stride=XE_STRIDE), :] = slab
        return carry

    lax.fori_loop(0, CAPACITY // GATHER_UNROLL, body, 0)
    xe_ref[...] = jnp.concatenate(
        [tile_ref[pl.ds(c * XE_STRIDE, CAPACITY), :] for c in range(TOK_SLABS)], axis=1).astype(BF16)


def _gather(idx_flat, hs, name):
    return pl.pallas_call(
        _gather_kernel,
        grid_spec=pltpu.PrefetchScalarGridSpec(
            num_scalar_prefetch=1,
            grid=(N_EXPERTS,),
            in_specs=[pl.BlockSpec((N_TOK * TOK_SLABS, LANES), lambda e, idx: (0, 0), pipeline_mode=pl.Buffered(1))],
            out_specs=pl.BlockSpec((None, CAPACITY, D_MODEL), lambda e, idx: (e, 0, 0)),
            scratch_shapes=[pltpu.VMEM((TOK_SLABS * XE_STRIDE, LANES), F32)],
        ),
        out_shape=jax.ShapeDtypeStruct((N_EXPERTS, CAPACITY, D_MODEL), BF16),
        compiler_params=_cp(40),
        name=name,
    )(idx_flat, hs)


def _ffn_kernel(xc_ref, xl_ref, gc_ref, gl_ref, w1_ref, w3_ref, w2_ref, yc_ref, yl_ref, xe_ref, acc_ref):
    f = pl.program_id(1)

    @pl.when(f == 0)
    def _():
        xe_ref[:CAPACITY, :] = xc_ref[...]
        xe_ref[CAPACITY:, :] = xl_ref[...]
        acc_ref[...] = jnp.zeros_like(acc_ref)

    x = xe_ref[...]
    h1 = _dot(x, w1_ref[...].astype(BF16))
    h3 = _dot(x, w3_ref[...].astype(BF16))
    hid = (h1 * jax.nn.sigmoid(h1) * h3).astype(BF16)
    acc_ref[...] += _dot(hid, w2_ref[...].astype(BF16))

    @pl.when(f == pl.num_programs(1) - 1)
    def _():
        _to_slabs(yc_ref, acc_ref[:CAPACITY, :] * gc_ref[...], CAPACITY)
        _to_slabs(yl_ref, acc_ref[CAPACITY:, :] * gl_ref[...], CAPACITY)


def _expert_ffn(xe_ctx, xe_lat, gate_ctx, gate_lat, w1, w3, w2, layer, name):
    nf = D_EXPERT // FF_TILE
    x_spec = pl.BlockSpec((None, CAPACITY, D_MODEL), lambda e, f: (e, 0, 0))
    g_spec = pl.BlockSpec((None, CAPACITY, 1), lambda e, f: (e, 0, 0))
    y_spec = pl.BlockSpec((None, CAPACITY * TOK_SLABS, LANES), lambda e, f: (e, 0, 0))
    y_shape = jax.ShapeDtypeStruct((N_EXPERTS, CAPACITY * TOK_SLABS, LANES), F32)
    return pl.pallas_call(
        _ffn_kernel,
        grid=(N_EXPERTS, nf),
        in_specs=[
            x_spec, x_spec, g_spec, g_spec,
            pl.BlockSpec((None, None, D_MODEL, FF_TILE), lambda e, f: (layer, e, 0, f)),
            pl.BlockSpec((None, None, D_MODEL, FF_TILE), lambda e, f: (layer, e, 0, f)),
            pl.BlockSpec((None, None, FF_TILE, D_MODEL), lambda e, f: (layer, e, f, 0)),
        ],
        out_specs=[y_spec, y_spec],
        out_shape=[y_shape, y_shape],
        scratch_shapes=[
            pltpu.VMEM((2 * CAPACITY, D_MODEL), BF16),
            pltpu.VMEM((2 * CAPACITY, D_MODEL), F32),
        ],
        compiler_params=_cp(56),
        name=name,
    )(xe_ctx, xe_lat, gate_ctx, gate_lat, w1, w3, w2)


SCATTER_UNROLL = 8


def _scatter_expert(idx_ref, ye_ref, y_ref, ex):
    def body(i, carry):
        upd = []
        for u in range(SCATTER_UNROLL):
            s = i * SCATTER_UNROLL + u
            t = idx_ref[ex * CAPACITY + s]
            rows = pl.ds(pl.multiple_of(t * TOK_SLABS, TOK_SLABS), TOK_SLABS)
            src = ye_ref[pl.ds(pl.multiple_of(s * TOK_SLABS, TOK_SLABS), TOK_SLABS), :]
            upd.append((rows, y_ref[rows, :] + src))
        for rows, val in upd:
            y_ref[rows, :] = val
        return carry

    lax.fori_loop(0, CAPACITY // SCATTER_UNROLL, body, 0)


def _combine_kernel(idx_ref, ye_ref, y_ref):
    ex = pl.program_id(0)

    @pl.when(ex == 0)
    def _():
        y_ref[...] = jnp.zeros_like(y_ref)

    _scatter_expert(idx_ref, ye_ref, y_ref, ex)


def _combine_final_kernel(idx_ref, ye_ref, xp_ref, gp_ref, fg_ref, o_ref, y_ref):
    i = pl.program_id(0)

    @pl.when(i == 0)
    def _():
        y_ref[...] = jnp.zeros_like(y_ref)

    @pl.when(i < N_EXPERTS)
    def _():
        _scatter_expert(idx_ref, ye_ref, y_ref, i)

    @pl.when(i >= N_EXPERTS)
    def _():
        slabs_per_tile = ROW_TILE * TOK_SLABS
        base = pl.multiple_of((i - N_EXPERTS) * slabs_per_tile, slabs_per_tile)
        x = xp_ref[...] + gp_ref[...] * _from_slabs(y_ref, ROW_TILE, base)
        o_ref[...] = _rmsnorm(x, fg_ref[...])


def _combine_final(idx_flat, ye, x_prev, mods_prev, final_g, lat, name):
    tiles_per_batch = DEC_SEQ // ROW_TILE

    def tile(i):
        return jnp.maximum(i - N_EXPERTS, 0)

    def gate_row(i, idx):
        return ((1 + tile(i) // tiles_per_batch) * 6 + 5 if lat else 5, 0, 0)

    return pl.pallas_call(
        _combine_final_kernel,
        grid_spec=pltpu.PrefetchScalarGridSpec(
            num_scalar_prefetch=1,
            grid=(N_EXPERTS + N_ROW_TILES,),
            in_specs=[
                pl.BlockSpec((None, CAPACITY * TOK_SLABS, LANES), lambda i, idx: (jnp.minimum(i, N_EXPERTS - 1), 0, 0)),
                pl.BlockSpec((ROW_TILE, D_MODEL), lambda i, idx: (tile(i), 0)),
                pl.BlockSpec((None, 1, D_MODEL), gate_row),
                pl.BlockSpec((1, D_MODEL), lambda i, idx: (0, 0)),
            ],
            out_specs=pl.BlockSpec((ROW_TILE, D_MODEL), lambda i, idx: (tile(i), 0)),
            scratch_shapes=[pltpu.VMEM((N_TOK * TOK_SLABS, LANES), F32)],
        ),
        out_shape=jax.ShapeDtypeStruct((N_TOK, D_MODEL), F32),
        compiler_params=_cp(40),
        name=name,
    )(idx_flat, ye, x_prev, mods_prev, final_g)


def _combine(idx_flat, ye, name):
    return pl.pallas_call(
        _combine_kernel,
        grid_spec=pltpu.PrefetchScalarGridSpec(
            num_scalar_prefetch=1,
            grid=(N_EXPERTS,),
            in_specs=[pl.BlockSpec((None, CAPACITY * TOK_SLABS, LANES), lambda e, idx: (e, 0, 0))],
            out_specs=pl.BlockSpec((N_TOK * TOK_SLABS, LANES), lambda e, idx: (0, 0)),
        ),
        out_shape=jax.ShapeDtypeStruct((N_TOK * TOK_SLABS, LANES), F32),
        compiler_params=_cp(56),
        name=name,
    )(idx_flat, ye)


def _moe(hs_ctx, lt_ctx, hs_lat, lt_lat, tri, blk, w1, w3, w2, layer):
    tag = "l%d" % layer
    routed = []
    for hs, lt, name in ((hs_ctx, lt_ctx, "ctx"), (hs_lat, lt_lat, "lat")):
        idx, gate = _router(lt, tri, blk, "router_%s_%s" % (tag, name))
        idx_flat = idx.reshape(N_EXPERTS * CAPACITY)
        xe = _gather(idx_flat, hs, "gather_%s_%s" % (tag, name))
        routed.append((idx_flat, xe, gate.reshape(N_EXPERTS, CAPACITY, 1)))
    (idx_c, xe_c, g_c), (idx_l, xe_l, g_l) = routed
    ye_c, ye_l = _expert_ffn(xe_c, xe_l, g_c, g_l, w1, w3, w2, layer, "ffn_" + tag)
    return (idx_c, ye_c), (idx_l, ye_l)


def _gla_proj_kernel(xp_ref, yt_ref, gp_ref, g_ref, sh_ref, sc_ref, w_ref, wt_ref, wg_ref, bg_ref,
                     x_ref, q_ref, k_ref, v_ref, r_ref, lgf_ref, lgb_ref, wb_ref):
    _cast_once(w_ref, wb_ref)
    x = xp_ref[...] + gp_ref[...] * _from_slabs(yt_ref, ROW_TILE)
    x_ref[...] = x
    h = _modulate(x, g_ref[...], sh_ref[...], sc_ref[...]).astype(BF16)
    z = _dot(h, wb_ref[...])
    nq = GLA_HEADS * GLA_DK
    nv = GLA_HEADS * GLA_DV
    q_ref[...] = z[:, :nq] * (GLA_DK ** -0.5)
    k_ref[...] = z[:, nq:2 * nq]
    v_ref[...] = z[:, 2 * nq:2 * nq + nv].astype(BF16)
    r_ref[...] = z[:, 2 * nq + nv:].astype(BF16)
    zg = _dot(h, wt_ref[...]).astype(BF16)
    a = _dot(zg, wg_ref[...]) + bg_ref[...]
    ls = (jnp.minimum(a, 0.0) - jnp.log1p(jnp.exp(-jnp.abs(a)))) * (1.0 / GLA_GATE_NORM)
    lgf_ref[...] = ls[:, :nq]
    lgb_ref[...] = ls[:, nq:]


def _gla_proj(x_prev, y_slabs, mods_prev, mods, norm_g, w_in, w_tail, wg, bg, lat, name):
    nq = GLA_HEADS * GLA_DK
    nv = GLA_HEADS * GLA_DV
    n_main = 2 * nq + 2 * nv
    return pl.pallas_call(
        _gla_proj_kernel,
        grid=(N_ROW_TILES,),
        in_specs=[
            _row_spec(D_MODEL),
            pl.BlockSpec((ROW_TILE * TOK_SLABS, LANES), lambda i: (i, 0)),
            _mod_spec(5, lat),
            _full_spec((1, D_MODEL)),
            _mod_spec(0, lat),
            _mod_spec(1, lat),
            _weight_spec(D_MODEL, n_main),
            _full_spec((D_MODEL, LANES)),
            _full_spec((LANES, 2 * nq)),
            _full_spec((1, 2 * nq)),
        ],
        out_specs=[_row_spec(D_MODEL), _row_spec(nq), _row_spec(nq), _row_spec(nv), _row_spec(nv),
                   _row_spec(nq), _row_spec(nq)],
        out_shape=[
            jax.ShapeDtypeStruct((N_TOK, D_MODEL), F32),
            jax.ShapeDtypeStruct((N_TOK, nq), F32),
            jax.ShapeDtypeStruct((N_TOK, nq), F32),
            jax.ShapeDtypeStruct((N_TOK, nv), BF16),
            jax.ShapeDtypeStruct((N_TOK, nv), BF16),
            jax.ShapeDtypeStruct((N_TOK, nq), F32),
            jax.ShapeDtypeStruct((N_TOK, nq), F32),
        ],
        scratch_shapes=[pltpu.VMEM((D_MODEL, n_main), BF16)],
        compiler_params=_cp(56),
        name=name,
    )(x_prev, y_slabs, mods_prev, norm_g, mods, mods, w_in, w_tail, wg, bg)


GLA_LEVELS = (32, 16, 8)
GLA_UNROLL = 8
GLA_GROUP = 4 * GLA_CHUNK
GLA_SAFE_DECAY = 60.0
NEG_BIG = -1e30


def _gla_intra_fast(qe, kc, b2, fwd):
    c = GLA_CHUNK
    row = lax.broadcasted_iota(jnp.int32, (c, 1), 0)
    col = lax.broadcasted_iota(jnp.int32, (1, c), 1)
    keep = (col <= row) if fwd else (col >= row)
    kq = (kc * jnp.exp2(-b2)).astype(BF16)
    return jnp.where(keep, _dot_nt(qe, kq), 0.0)


def _gla_intra_robust(qc, kc, b2, fwd):
    c = GLA_CHUNK
    row = lax.broadcasted_iota(jnp.int32, (c, 1), 0)
    col = lax.broadcasted_iota(jnp.int32, (1, c), 1)
    a = jnp.zeros((c, c), F32)
    for g in GLA_LEVELS:
        odd = ((row >> int(math.log2(g))) & 1) == 1
        later = odd if fwd else jnp.logical_not(odd)
        refs = []
        for p in range(c // (2 * g)):
            r0 = 2 * g * p + (g - 1 if fwd else g)
            refs.append(jnp.broadcast_to(b2[r0:r0 + 1], (2 * g, GLA_DK)))
        ref = jnp.concatenate(refs, axis=0) if len(refs) > 1 else refs[0]
        qt = qc * jnp.exp2(jnp.where(later, b2 - ref, NEG_BIG))
        kt = kc * jnp.exp2(jnp.where(later, NEG_BIG, ref - b2))
        same_parent = (row >> int(math.log2(2 * g))) == (col >> int(math.log2(2 * g)))
        a = a + jnp.where(same_parent, _dot_nt(qt.astype(BF16), kt.astype(BF16)), 0.0)

    sub = lax.broadcasted_iota(jnp.int32, (SUBLANES, 1), 0)
    strips = []
    for blk in range(c // SUBLANES):
        r0 = SUBLANES * blk
        qb = qc[r0:r0 + SUBLANES]
        bb = b2[r0:r0 + SUBLANES]
        strip = jnp.zeros((SUBLANES, c), F32)
        for jj in range(SUBLANES):
            j = r0 + jj
            cond = (sub >= jj) if fwd else (sub <= jj)
            t = qb * kc[j:j + 1] * jnp.exp2(jnp.where(cond, bb - b2[j:j + 1], NEG_BIG))
            strip = jnp.where(col == j, jnp.sum(t, axis=1, keepdims=True), strip)
        strips.append(strip)
    return a + jnp.concatenate(strips, axis=0)


def _gla_state_increment(kc, vc, b2f, b2b):
    c = GLA_CHUNK
    btf = b2f[c - 1:c]
    btb = b2b[0:1]
    kd = jnp.concatenate([(kc * jnp.exp2(btf - b2f)).astype(BF16), (kc * jnp.exp2(btb - b2b)).astype(BF16)], axis=1)
    return _dot_tn(vc, kd), jnp.concatenate([jnp.exp2(btf), jnp.exp2(btb)], axis=1)


def _gla_scores(qc, kc, b2f, b2b, fast):
    qe_f = (qc * jnp.exp2(b2f)).astype(BF16)
    qe_b = (qc * jnp.exp2(b2b)).astype(BF16)
    if fast:
        a = _gla_intra_fast(qe_f, kc, b2f, True) + _gla_intra_fast(qe_b, kc, b2b, False)
    else:
        a = _gla_intra_robust(qc, kc, b2f, True) + _gla_intra_robust(qc, kc, b2b, False)
    return jnp.concatenate([qe_f, qe_b], axis=1), a.astype(BF16)


def _gla_kernel(q_ref, k_ref, v_ref, r_ref, gf_ref, gb_ref, s0f_ref, s0b_ref, go_ref, trif_ref, trib_ref,
                u_ref, sf_ref, sb_ref,
                b2f_ref, b2b_ref, qe_ref, oi_ref, kv_ref, dec_ref, sst_ref, st_ref, *, t_len):
    c = GLA_CHUNK
    n = t_len // c

    def rows_of(ci):
        return pl.ds(pl.multiple_of(ci * c, c), c)

    def state_rows_of(ci):
        return pl.ds(pl.multiple_of(ci * GLA_DV, GLA_DV), GLA_DV)

    n_groups = t_len // GLA_GROUP
    cum_unroll = 2 if n_groups % 2 == 0 else 1

    def cum_body(i, carry):
        sums = []
        for u in range(cum_unroll):
            rows = pl.ds(pl.multiple_of((i * cum_unroll + u) * GLA_GROUP, GLA_GROUP), GLA_GROUP)
            for g_ref, tri_ref, b_ref in ((gf_ref, trif_ref, b2f_ref), (gb_ref, trib_ref, b2b_ref)):
                s3 = _dot(tri_ref[...], jnp.concatenate(_split3(g_ref[rows, :]), axis=1))
                sums.append((b_ref, rows, s3))
        for b_ref, rows, s3 in sums:
            b_ref[rows, :] = (s3[:, :GLA_DK] + s3[:, GLA_DK:2 * GLA_DK] + s3[:, 2 * GLA_DK:]) * LOG2E
        return carry

    lax.fori_loop(0, n_groups // cum_unroll, cum_body, 0)

    def chunk_total(g_ref):
        return jnp.min(jnp.sum(g_ref[...].reshape(n, c, GLA_DK), axis=1))

    safe = jnp.minimum(chunk_total(gf_ref), chunk_total(gb_ref)) >= -GLA_SAFE_DECAY

    def make_local(fast, unroll):
        def body(i, carry):
            chunks = [i * unroll + u for u in range(unroll)]
            for ci in chunks:
                rows = rows_of(ci)
                kv, dec = _gla_state_increment(k_ref[rows, :], v_ref[rows, :], b2f_ref[rows, :], b2b_ref[rows, :])
                kv_ref[state_rows_of(ci), :] = kv
                dec_ref[pl.ds(ci, 1), :] = dec
            scores = []
            for ci in chunks:
                rows = rows_of(ci)
                qe, a = _gla_scores(q_ref[rows, :], k_ref[rows, :], b2f_ref[rows, :], b2b_ref[rows, :], fast)
                qe_ref[rows, :] = qe
                scores.append(a)
            for ci, a in zip(chunks, scores):
                rows = rows_of(ci)
                oi_ref[rows, :] = _dot(a, v_ref[rows, :])
            return carry
        return body

    unroll = min(GLA_UNROLL, n)

    @pl.when(safe)
    def _():
        lax.fori_loop(0, n // unroll, make_local(True, unroll), 0)

    @pl.when(jnp.logical_not(safe))
    def _():
        lax.fori_loop(0, n, make_local(False, 1), 0)

    st_ref[...] = jnp.concatenate([s0f_ref[...].T, s0b_ref[...].T], axis=1)
    fl = slice(0, GLA_DK)
    bl = slice(GLA_DK, 2 * GLA_DK)

    def rec_body(i, carry):
        cf = i
        cb = n - 1 - i
        st = st_ref[...]
        sst_ref[state_rows_of(cf), fl] = st[:, fl].astype(BF16)
        sst_ref[state_rows_of(cb), bl] = st[:, bl].astype(BF16)
        dec = jnp.concatenate([dec_ref[pl.ds(cf, 1), :][:, fl], dec_ref[pl.ds(cb, 1), :][:, bl]], axis=1)
        kv = jnp.concatenate([kv_ref[state_rows_of(cf), fl], kv_ref[state_rows_of(cb), bl]], axis=1)
        st_ref[...] = st * dec + kv
        return carry

    lax.fori_loop(0, n, rec_body, 0)
    sf_ref[...] = st_ref[:, fl].T
    sb_ref[...] = st_ref[:, bl].T

    go = go_ref[...]

    def out_body(i, carry):
        chunks = [i * unroll + u for u in range(unroll)]
        inter = [_dot_nt(qe_ref[rows_of(ci), :], sst_ref[state_rows_of(ci), :]) for ci in chunks]
        for ci, o_inter in zip(chunks, inter):
            rows = rows_of(ci)
            r = r_ref[rows, :].astype(F32)
            u_ref[rows, :] = (_rmsnorm(oi_ref[rows, :] + o_inter, go) * (r * jax.nn.sigmoid(r))).astype(BF16)
        return carry

    lax.fori_loop(0, n // unroll, out_body, 0)


def _gla_tri():
    i = np.arange(GLA_GROUP)
    same = (i[:, None] // GLA_CHUNK) == (i[None, :] // GLA_CHUNK)
    fwd = (same & (i[None, :] <= i[:, None])).astype(np.float32)
    bwd = (same & (i[None, :] >= i[:, None])).astype(np.float32)
    return jnp.asarray(fwd, BF16), jnp.asarray(bwd, BF16)


def _gla(q, k, v, r, lgf, lgb, s0f, s0b, g_out, n_batch, t_len, name):
    trif, trib = _gla_tri()
    n_chunks = t_len // GLA_CHUNK
    qk_spec = pl.BlockSpec((None, t_len, GLA_DK), lambda b, h: (b, 0, h))
    v_spec = pl.BlockSpec((None, t_len, GLA_DV), lambda b, h: (b, 0, h))
    s_spec = pl.BlockSpec((None, None, GLA_DK, GLA_DV), lambda b, h: (b, h, 0, 0))
    const = lambda shape: pl.BlockSpec(shape, lambda b, h: (0,) * len(shape))
    s_shape = jax.ShapeDtypeStruct((n_batch, GLA_HEADS, GLA_DK, GLA_DV), F32)
    return pl.pallas_call(
        functools.partial(_gla_kernel, t_len=t_len),
        grid=(n_batch, GLA_HEADS),
        in_specs=[qk_spec, qk_spec, v_spec, v_spec, qk_spec, qk_spec, s_spec, s_spec,
                  const((1, GLA_DV)), const((GLA_GROUP, GLA_GROUP)), const((GLA_GROUP, GLA_GROUP))],
        out_specs=[v_spec, s_spec, s_spec],
        out_shape=[jax.ShapeDtypeStruct((n_batch, t_len, GLA_HEADS * GLA_DV), BF16), s_shape, s_shape],
        scratch_shapes=[
            pltpu.VMEM((t_len, GLA_DK), F32), pltpu.VMEM((t_len, GLA_DK), F32),
            pltpu.VMEM((t_len, 2 * GLA_DK), BF16),
            pltpu.VMEM((t_len, GLA_DV), F32),
            pltpu.VMEM((n_chunks * GLA_DV, 2 * GLA_DK), F32),
            pltpu.VMEM((max(n_chunks, SUBLANES), 2 * GLA_DK), F32),
            pltpu.VMEM((n_chunks * GLA_DV, 2 * GLA_DK), BF16),
            pltpu.VMEM((GLA_DV, 2 * GLA_DK), F32),
        ],
        compiler_params=_cp(48),
        name=name,
    )(q, k, v, r, lgf, lgb, s0f, s0b, g_out, trif, trib)


def kernel(x_prompt, x_sample, cache_k, cache_v, state_fwd, state_bwd, c, c_ctx, w_mod, b_mod, norm_g,
           da_w_in, da_w_out, da_lam_q1, da_lam_k1, da_lam_q2, da_lam_k2, da_g_sub, gla_w_in, gla_w_gf2,
           gla_b_gf, gla_w_gb2, gla_b_gb, gla_g_out, gla_w_out, moe_w_router, moe_w1, moe_w3, moe_w2, final_g):
    n_ctx_b = x_prompt.shape[0]
    n_lat_b = x_sample.shape[0]
    x_ctx = x_prompt.reshape(N_TOK, D_MODEL)
    x_lat = x_sample.reshape(N_TOK, D_MODEL)

    cvec = jnp.zeros((SUBLANES, D_MODEL), F32).at[0].set(c_ctx).at[1:1 + n_lat_b].set(c)
    mods = _mod_params(cvec, w_mod, b_mod)
    tri, blk = _cumsum_consts()
    wr_t = [jnp.zeros((LANES, D_MODEL), BF16).at[:N_EXPERTS].set(moe_w_router[i].T.astype(BF16))
            for i in range(DEPTH)]
    ng = norm_g.reshape(DEPTH, 2, 1, D_MODEL)

    lam_init = 0.8 - 0.6 * math.exp(-0.3 * 0)
    w_in = da_w_in
    w_out = da_w_out
    lam_vecs = [v[0].reshape(1, DA_DIM) for v in (da_lam_q1, da_lam_k1, da_lam_q2, da_lam_k2)]
    g_sub = da_g_sub[0].reshape(1, DA_VDIM)

    qc, kc, vc, kf, vf = _da_proj(x_ctx, mods[0], ng[0, 0], w_in, lat=False)
    ql, kl, vl = _da_proj(x_lat, mods[0], ng[0, 0], w_in, lat=True, rope_tabs=_rope_tables())
    past = cache_k.shape[2]
    ck = cache_k[:, 0].reshape(n_lat_b * past, D_MODEL)
    cv = cache_v[:, 0].reshape(n_lat_b * past, D_MODEL)
    u_ctx = _attention(lam_vecs, g_sub, qc, [(kc, vc, SEQ)], n_ctx_b, SEQ, ATTN_Q_TILE, lam_init, "attn_ctx")
    u_lat = _attention(lam_vecs, g_sub, ql, [(kl, vl, DEC_SEQ), (ck, cv, past)], n_lat_b, DEC_SEQ, ATTN_Q_TILE,
                       lam_init, "attn_lat")

    x1_ctx, hs_ctx, lt_ctx = _outproj(u_ctx, w_out, x_ctx, mods[0], ng[0, 1], wr_t[0], False, "outproj0_ctx")
    x1_lat, hs_lat, lt_lat = _outproj(u_lat, w_out, x_lat, mods[0], ng[0, 1], wr_t[0], True, "outproj0_lat")
    (idx_c, ye_c), (idx_l, ye_l) = _moe(hs_ctx, lt_ctx, hs_lat, lt_lat, tri, blk, moe_w1, moe_w3, moe_w2, 0)
    y_ctx = _combine(idx_c, ye_c, "combine_l0_ctx")
    y_lat = _combine(idx_l, ye_l, "combine_l0_lat")

    nq = GLA_HEADS * GLA_DK
    n_main = 2 * nq + 2 * GLA_HEADS * GLA_DV
    w_tail = jnp.zeros((D_MODEL, LANES), F32).at[:, :2 * GLA_GATE_RANK].set(gla_w_in[0][:, n_main:]).astype(BF16)
    wg = jnp.zeros((LANES, 2 * nq), F32)
    wg = wg.at[:GLA_GATE_RANK, :nq].set(gla_w_gf2[0]).at[GLA_GATE_RANK:2 * GLA_GATE_RANK, nq:].set(gla_w_gb2[0])
    wg = wg.astype(BF16)
    bg = jnp.concatenate([gla_b_gf[0], gla_b_gb[0]]).reshape(1, 2 * nq)
    w_out1 = gla_w_out
    g_out = gla_g_out[0].reshape(1, GLA_DV)

    def gla_side(x1, y, lat, n_b, t_len, s0f, s0b, tag):
        x2, q, k, v, r, lgf, lgb = _gla_proj(x1, y, mods[0], mods[1], ng[1, 0], gla_w_in, w_tail, wg, bg, lat,
                                             "gla_proj_" + tag)
        sh3 = lambda a: a.reshape(n_b, t_len, a.shape[-1])
        u, sf, sb = _gla(sh3(q), sh3(k), sh3(v), sh3(r), sh3(lgf), sh3(lgb), s0f, s0b, g_out, n_b, t_len, "gla_" + tag)
        return x2, u.reshape(N_TOK, D_MODEL), sf, sb

    zeros_s = jnp.zeros((n_ctx_b, GLA_HEADS, GLA_DK, GLA_DV), F32)
    x2_ctx, ug_ctx, sf, sb = gla_side(x1_ctx, y_ctx, False, n_ctx_b, SEQ, zeros_s, zeros_s, "ctx")
    x2_lat, ug_lat, _, _ = gla_side(x1_lat, y_lat, True, n_lat_b, DEC_SEQ, state_fwd[:, 0], state_bwd[:, 0], "lat")

    x3_ctx, hs_ctx, lt_ctx = _outproj(ug_ctx, w_out1, x2_ctx, mods[1], ng[1, 1], wr_t[1], False, "outproj1_ctx")
    x3_lat, hs_lat, lt_lat = _outproj(ug_lat, w_out1, x2_lat, mods[1], ng[1, 1], wr_t[1], True, "outproj1_lat")
    (idx_c, ye_c), (idx_l, ye_l) = _moe(hs_ctx, lt_ctx, hs_lat, lt_lat, tri, blk, moe_w1, moe_w3, moe_w2, 1)

    fg = final_g.reshape(1, D_MODEL)
    y_prompt = _combine_final(idx_c, ye_c, x3_ctx, mods[1], fg, False, "final_ctx").reshape(x_prompt.shape)
    y_sample = _combine_final(idx_l, ye_l, x3_lat, mods[1], fg, True, "final_lat").reshape(x_sample.shape)
    new_k = kf.reshape(n_ctx_b, 1, SEQ, 2 * DA_HEADS, DA_DIM)
    new_v = vf.reshape(n_ctx_b, 1, SEQ, DA_HEADS, DA_VDIM)
    return (y_prompt, y_sample, new_k, new_v, sf[:, None], sb[:, None])
```

```python
import functools
import math

import numpy as np
import jax
import jax.numpy as jnp
from jax import lax
from jax.experimental import pallas as pl
from jax.experimental.pallas import tpu as pltpu

F32 = jnp.float32
BF16 = jnp.bfloat16

D_MODEL = 1024
DEPTH = 2
SEQ = 256
DEC_SEQ = 2048
GRID_W = 64
N_TOK = 4096
DA_HEADS = 8
DA_DIM = 64
DA_VDIM = 128
ROPE_BASE = 10000.0
GLA_HEADS = 4
GLA_DK = 128
GLA_DV = 256
GLA_GATE_RANK = 16
GLA_GATE_NORM = 16.0
GLA_CHUNK = 64
N_EXPERTS = 16
CAPACITY = 512
D_EXPERT = 2048
EPS = 1e-6
F32_MIN_NORMAL = 2.0 ** -126
LOG2E = 1.4426950408889634

LANES = 128
SUBLANES = 8
ROW_TILE = 512
FF_TILE = 512
TOK_SLABS = D_MODEL // LANES
XE_STRIDE = CAPACITY + SUBLANES
MIB = 1024 * 1024


def _cp(vmem_mib, sem=None):
    return pltpu.CompilerParams(vmem_limit_bytes=vmem_mib * MIB, dimension_semantics=sem)


def _dot(a, b):
    return jnp.dot(a, b, preferred_element_type=F32)


def _dot_nt(a, b):
    return lax.dot_general(a, b, (((1,), (1,)), ((), ())), preferred_element_type=F32)


def _dot_tn(a, b):
    return lax.dot_general(a, b, (((0,), (0,)), ((), ())), preferred_element_type=F32)


def _rmsnorm(x, g):
    return x * lax.rsqrt(jnp.mean(x * x, axis=-1, keepdims=True) + EPS) * g


def _modulate(x, g, shift, scale):
    return _rmsnorm(x, g) * (1.0 + scale) + shift


def _split3(x):
    hi = x.astype(BF16)
    r = x - hi.astype(F32)
    mid = r.astype(BF16)
    lo = (r - mid.astype(F32)).astype(BF16)
    return hi, mid, lo


def _from_slabs(ref, rows, base=0):
    return jnp.concatenate([ref[pl.ds(base + c, rows, stride=TOK_SLABS), :] for c in range(TOK_SLABS)], axis=1)


def _to_slabs(ref, val, rows):
    for c in range(TOK_SLABS):
        ref[pl.ds(c, rows, stride=TOK_SLABS), :] = val[:, LANES * c:LANES * (c + 1)]


def _mod_kernel(c_ref, w_ref, b_ref, o_ref):
    c = c_ref[...]
    s = c * jax.nn.sigmoid(c)
    w = w_ref[...]
    s_hi = s.astype(BF16)
    s_lo = (s - s_hi.astype(F32)).astype(BF16)
    w_hi = w.astype(BF16)
    w_lo = (w - w_hi.astype(F32)).astype(BF16)
    o_ref[...] = _dot(s_hi, w_hi) + _dot(s_hi, w_lo) + _dot(s_lo, w_hi) + b_ref[...]


def _mod_params(cvec, w_mod, b_mod):
    n6 = 6 * D_MODEL
    out = pl.pallas_call(
        _mod_kernel,
        grid=(DEPTH, 6),
        in_specs=[
            pl.BlockSpec((SUBLANES, D_MODEL), lambda i, j: (0, 0)),
            pl.BlockSpec((None, D_MODEL, D_MODEL), lambda i, j: (i, 0, j)),
            pl.BlockSpec((None, 1, D_MODEL), lambda i, j: (i, 0, j)),
        ],
        out_specs=pl.BlockSpec((None, SUBLANES, D_MODEL), lambda i, j: (i, 0, j)),
        out_shape=jax.ShapeDtypeStruct((DEPTH, SUBLANES, n6), F32),
        compiler_params=_cp(32),
        name="mod_params",
    )(cvec, w_mod, b_mod.reshape(DEPTH, 1, n6))
    return out.reshape(DEPTH, SUBLANES * 6, 1, D_MODEL)


def _mod_spec(k, lat):
    tiles_per_batch = DEC_SEQ // ROW_TILE
    if lat:
        return pl.BlockSpec((None, 1, D_MODEL), lambda i: ((1 + i // tiles_per_batch) * 6 + k, 0, 0))
    return pl.BlockSpec((None, 1, D_MODEL), lambda i: (k, 0, 0))


def _row_spec(width, dtype_rows=ROW_TILE):
    return pl.BlockSpec((dtype_rows, width), lambda i: (i, 0))


def _full_spec(shape):
    nd = len(shape)
    return pl.BlockSpec(shape, lambda i: (0,) * nd)


N_ROW_TILES = N_TOK // ROW_TILE


def _weight_spec(rows, cols):
    return pl.BlockSpec((None, rows, cols), lambda i: (0, 0, 0), pipeline_mode=pl.Buffered(1))


def _cast_once(w_ref, wb_ref):
    @pl.when(pl.program_id(0) == 0)
    def _():
        wb_ref[...] = w_ref[...].astype(BF16)


def _da_proj_kernel(*refs, rope, emit_f32):
    x_ref, g_ref, sh_ref, sc_ref, w_ref = refs[:5]
    pos = 5
    if rope:
        cos_ref, sin_ref = refs[pos:pos + 2]
        pos += 2
    q_ref, k_ref, v_ref = refs[pos:pos + 3]
    pos += 3
    wb_ref = refs[-1]
    _cast_once(w_ref, wb_ref)
    h = _modulate(x_ref[...], g_ref[...], sh_ref[...], sc_ref[...])
    z = _dot(h.astype(BF16), wb_ref[...])
    q = z[:, :D_MODEL]
    k = z[:, D_MODEL:2 * D_MODEL]
    v = z[:, 2 * D_MODEL:]
    if emit_f32:
        kf_ref, vf_ref = refs[pos:pos + 2]
        kf_ref[...] = k
        vf_ref[...] = v
    if rope:
        reps = D_MODEL // LANES
        cos = jnp.concatenate([cos_ref[...]] * reps, axis=1)
        sin = jnp.concatenate([sin_ref[...]] * reps, axis=1)
        lane = lax.broadcasted_iota(jnp.int32, (1, D_MODEL), 1)
        first = (lane & 16) == 0

        def rot(t):
            partner = jnp.where(first, pltpu.roll(t, D_MODEL - 16, 1), pltpu.roll(t, 16, 1))
            return t * cos + partner * sin

        q = rot(q)
        k = rot(k)
    q_ref[...] = (q * (DA_DIM ** -0.5 * LOG2E)).astype(BF16)
    k_ref[...] = k.astype(BF16)
    v_ref[...] = v.astype(BF16)


def _da_proj(x, mods, norm_g, w_in, lat, rope_tabs=None):
    rope = rope_tabs is not None
    emit_f32 = not lat
    in_specs = [
        _row_spec(D_MODEL),
        _full_spec((1, D_MODEL)),
        _mod_spec(0, lat),
        _mod_spec(1, lat),
        _weight_spec(D_MODEL, 3 * D_MODEL),
    ]
    args = [x, norm_g, mods, mods, w_in]
    if rope:
        tiles_per_batch = DEC_SEQ // ROW_TILE
        tab_spec = pl.BlockSpec((ROW_TILE, LANES), lambda i: (i % tiles_per_batch, 0))
        in_specs += [tab_spec, tab_spec]
        args += list(rope_tabs)
    out_specs = [_row_spec(D_MODEL)] * 3
    out_shape = [jax.ShapeDtypeStruct((N_TOK, D_MODEL), BF16)] * 3
    if emit_f32:
        out_specs += [_row_spec(D_MODEL)] * 2
        out_shape += [jax.ShapeDtypeStruct((N_TOK, D_MODEL), F32)] * 2
    return pl.pallas_call(
        functools.partial(_da_proj_kernel, rope=rope, emit_f32=emit_f32),
        grid=(N_ROW_TILES,),
        in_specs=in_specs,
        out_specs=out_specs,
        out_shape=out_shape,
        scratch_shapes=[pltpu.VMEM((D_MODEL, 3 * D_MODEL), BF16)],
        compiler_params=_cp(56),
        name="da_proj_lat" if lat else "da_proj_ctx",
    )(*args)


def _rope_tables():
    t = np.arange(DEC_SEQ)
    rows = (t // GRID_W).astype(np.float32)
    cols = (t % GRID_W).astype(np.float32)
    half = DA_DIM // 4
    freqs = (np.float32(ROPE_BASE) ** (-np.arange(half, dtype=np.float32) / np.float32(half))).astype(np.float32)
    ang_r = rows[:, None] * freqs
    ang_c = cols[:, None] * freqs
    cos64 = np.concatenate([np.cos(ang_r)] * 2 + [np.cos(ang_c)] * 2, axis=1)
    sin64 = np.concatenate([-np.sin(ang_r), np.sin(ang_r), -np.sin(ang_c), np.sin(ang_c)], axis=1)
    reps = LANES // DA_DIM
    return (jnp.asarray(np.concatenate([cos64] * reps, axis=1), F32),
            jnp.asarray(np.concatenate([sin64] * reps, axis=1), F32))


ATTN_Q_TILE = 256


def _attn_kernel(lq1_ref, lk1_ref, lq2_ref, lk2_ref, gs_ref, q_ref, *refs, lam_init, seg_lens):
    n_seg = len(seg_lens)
    kv_refs, o_ref = refs[:2 * n_seg], refs[2 * n_seg]
    lam =(jnp.exp(jnp.sum(lq1_ref[...] * lk1_ref[...], axis=-1, keepdims=True))
           - jnp.exp(jnp.sum(lq2_ref[...] * lk2_ref[...], axis=-1, keepdims=True)) + lam_init)
    lane = lax.broadcasted_iota(jnp.int32, (1, DA_VDIM), 1)
    first = lane < DA_DIM
    gs = gs_ref[...]
    tq = q_ref.shape[0]

    def softmax_pv(qq, ks, vs):
        ss = [_dot_nt(qq, kk) for kk in ks]
        m = functools.reduce(jnp.maximum, [jnp.max(s, axis=-1, keepdims=True) for s in ss])
        ps = [jnp.exp2(s - m) for s in ss]
        l = functools.reduce(jnp.add, [jnp.sum(p, axis=-1, keepdims=True) for p in ps])
        o = functools.reduce(jnp.add, [_dot(p.astype(BF16), vv) for p, vv in zip(ps, vs)])
        return o, l

    for h in range(DA_HEADS):
        sl = slice(DA_VDIM * h, DA_VDIM * (h + 1))
        qh = q_ref[:, sl]
        ks = [kv_refs[2 * s][:, sl].astype(BF16) for s in range(n_seg)]
        vs = [kv_refs[2 * s + 1][:, sl].astype(BF16) for s in range(n_seg)]
        zero = jnp.zeros_like(qh)
        qq = jnp.concatenate([jnp.where(first, qh, zero), jnp.where(first, zero, qh)], axis=0)
        oo, ll = softmax_pv(qq, ks, vs)
        o = oo[:tq] * (1.0 / ll[:tq]) - oo[tq:] * (lam / ll[tq:])
        o_ref[:, sl] = (_rmsnorm(o, gs) * (1.0 - lam_init)).astype(BF16)


def _attention(lam_vecs, g_sub, q, kv_segs, n_batch, t_q, q_tile, lam_init, name):
    nq = t_q // q_tile
    vec_spec = pl.BlockSpec((1, DA_DIM), lambda b, i: (0, 0))
    kv_specs, kv_args = [], []
    for k, v, t_k in kv_segs:
        kv_specs += [pl.BlockSpec((t_k, D_MODEL), lambda b, i: (b, 0))] * 2
        kv_args += [k, v]
    seg_lens = tuple(t_k for _, _, t_k in kv_segs)
    return pl.pallas_call(
        functools.partial(_attn_kernel, lam_init=lam_init, seg_lens=seg_lens),
        grid=(n_batch, nq),
        in_specs=[vec_spec] * 4 + [
            pl.BlockSpec((1, DA_VDIM), lambda b, i: (0, 0)),
            pl.BlockSpec((q_tile, D_MODEL), lambda b, i: (b * nq + i, 0)),
        ] + kv_specs,
        out_specs=pl.BlockSpec((q_tile, D_MODEL), lambda b, i: (b * nq + i, 0)),
        out_shape=jax.ShapeDtypeStruct((n_batch * t_q, D_MODEL), BF16),
        compiler_params=_cp(56),
        name=name,
    )(*lam_vecs, g_sub, q, *kv_args)


def _outproj_kernel(u_ref, w_ref, x_ref, gate_ref, g_ref, sh_ref, sc_ref, wr_ref, x1_ref, hs_ref, lt_ref, wb_ref):
    _cast_once(w_ref, wb_ref)
    x1 = x_ref[...] + gate_ref[...] * _dot(u_ref[...], wb_ref[...])
    x1_ref[...] = x1
    h2 = _modulate(x1, g_ref[...], sh_ref[...], sc_ref[...])
    _to_slabs(hs_ref, h2, ROW_TILE)
    lt_ref[...] = _dot_nt(wr_ref[...], h2.astype(BF16))


def _outproj(u, w_out, x, mods, norm_g, wr_t, lat, name):
    return pl.pallas_call(
        _outproj_kernel,
        grid=(N_ROW_TILES,),
        in_specs=[
            _row_spec(D_MODEL),
            _weight_spec(D_MODEL, D_MODEL),
            _row_spec(D_MODEL),
            _mod_spec(2, lat),
            _full_spec((1, D_MODEL)),
            _mod_spec(3, lat),
            _mod_spec(4, lat),
            _full_spec((LANES, D_MODEL)),
        ],
        out_specs=[
            _row_spec(D_MODEL),
            pl.BlockSpec((ROW_TILE * TOK_SLABS, LANES), lambda i: (i, 0)),
            pl.BlockSpec((LANES, ROW_TILE), lambda i: (0, i)),
        ],
        out_shape=[
            jax.ShapeDtypeStruct((N_TOK, D_MODEL), F32),
            jax.ShapeDtypeStruct((N_TOK * TOK_SLABS, LANES), F32),
            jax.ShapeDtypeStruct((LANES, N_TOK), F32),
        ],
        scratch_shapes=[pltpu.VMEM((D_MODEL, D_MODEL), BF16)],
        compiler_params=_cp(48),
        name=name,
    )(u, w_out, x, mods, norm_g, mods, mods, wr_t)


N_TOK_BLOCKS = N_TOK // LANES
STACK_ROWS = N_TOK_BLOCKS * N_EXPERTS


def _router_kernel(lt_ref, tri_ref, blk_ref, idx_ref, gate_ref, cwin_ref, aff_ref, cend_ref, ctot_ref):
    lt = lt_ref[...]
    e = jnp.exp(lt - jnp.max(lt, axis=0, keepdims=True))
    aff = e / jnp.sum(e, axis=0, keepdims=True)
    aff = jnp.where(aff >= F32_MIN_NORMAL, aff, 0.0)

    def count_ge(x):
        return jnp.sum((aff >= x).astype(F32), axis=1, keepdims=True)

    def step(mid_of):
        def body(_, c):
            lo, hi = c
            mid = jnp.minimum(jnp.maximum(mid_of(lo, hi), lo), hi)
            ok = count_ge(mid) >= float(CAPACITY)
            return jnp.where(ok, mid, lo), jnp.where(ok, hi, mid)
        return body

    lo = jnp.full((N_EXPERTS, 1), F32_MIN_NORMAL, F32)
    hi = jnp.full((N_EXPERTS, 1), 2.0, F32)
    lo, hi = lax.fori_loop(0, 8, step(lambda a, b: jnp.sqrt(a * b)), (lo, hi))
    lo, hi = lax.fori_loop(0, 28, step(lambda a, b: a + (b - a) * 0.5), (lo, hi))
    kth = jnp.max(jnp.where(aff < hi, aff, 0.0), axis=1, keepdims=True)
    thr = jnp.where(count_ge(lo) >= float(CAPACITY), kth, 0.0)

    tri = tri_ref[...]
    blk = blk_ref[...]

    def stack(x):
        return jnp.concatenate([x[:, LANES * b:LANES * (b + 1)] for b in range(N_TOK_BLOCKS)], axis=0)

    def unstack(x):
        return jnp.concatenate([x[N_EXPERTS * b:N_EXPERTS * (b + 1), :] for b in range(N_TOK_BLOCKS)], axis=1)

    def cumsum_stacked(mask_st):
        win = _dot(mask_st.astype(BF16), tri)
        tot = win[:, LANES - 1:LANES]
        off = _dot(blk, jnp.broadcast_to(tot, (STACK_ROWS, LANES)).astype(BF16))[:, :1]
        return win, off, tot

    gt = aff > thr
    eq = aff == thr
    need = float(CAPACITY) - jnp.sum(gt.astype(F32), axis=1, keepdims=True)
    eq_f = eq.astype(F32)
    ewin, eoff, _ = cumsum_stacked(stack(eq_f))
    eq_before = unstack(ewin + eoff) - eq_f
    sel = jnp.logical_or(gt, jnp.logical_and(eq, eq_before < need))
    cwin, coff, ctot = cumsum_stacked(stack(sel.astype(F32)))
    cwin_ref[...] = cwin
    aff_ref[...] = stack(aff)
    cend_ref[...] = jnp.broadcast_to(coff + ctot, (STACK_ROWS, LANES))
    ctot_ref[...] = jnp.broadcast_to(ctot, (STACK_ROWS, LANES))

    blk_id = lax.broadcasted_iota(jnp.int32, (N_TOK_BLOCKS, 1), 0).astype(F32)
    slot = lax.broadcasted_iota(jnp.int32, (1, CAPACITY), 1).astype(F32)
    lane_f = lax.broadcasted_iota(jnp.int32, (LANES, 1), 0).astype(F32)

    def per_expert(ex):
        rows = pl.ds(ex, N_TOK_BLOCKS, stride=N_EXPERTS)
        cend = cend_ref[rows, :][:, :1]
        ctot_e = ctot_ref[rows, :][:, :1]
        before = cend <= slot
        nblk = jnp.sum(before.astype(F32), axis=0, keepdims=True)
        base = jnp.sum(jnp.where(before, ctot_e, 0.0), axis=0, keepdims=True)
        pick = jnp.where(blk_id == nblk, 1.0, 0.0).astype(BF16)
        cnt_in = _dot_tn(cwin_ref[rows, :].astype(BF16), pick)
        lane_idx = jnp.sum((cnt_in <= slot - base).astype(F32), axis=0, keepdims=True)
        idx_ref[pl.ds(ex, 1), :] = (nblk * float(LANES) + lane_idx).astype(jnp.int32)
        a_hi, a_mid, a_lo = _split3(aff_ref[rows, :])
        aff_blk = _dot_tn(a_hi, pick) + _dot_tn(a_mid, pick) + _dot_tn(a_lo, pick)
        gate_ref[pl.ds(ex, 1), :] = jnp.sum(jnp.where(lane_f == lane_idx, aff_blk, 0.0), axis=0, keepdims=True)

    def expert_pair(i, carry):
        per_expert(2 * i)
        per_expert(2 * i + 1)
        return carry

    lax.fori_loop(0, N_EXPERTS // 2, expert_pair, 0)


def _cumsum_consts():
    l = np.arange(LANES)
    tri = (l[:, None] <= l[None, :]).astype(np.float32)
    r = np.arange(STACK_ROWS)
    same = (r[:, None] % N_EXPERTS) == (r[None, :] % N_EXPERTS)
    earlier = (r[None, :] // N_EXPERTS) < (r[:, None] // N_EXPERTS)
    blk = (same & earlier).astype(np.float32)
    return jnp.asarray(tri, BF16), jnp.asarray(blk, BF16)


def _router(logits_t, tri, blk, name):
    return pl.pallas_call(
        _router_kernel,
        grid=(1,),
        in_specs=[
            pl.BlockSpec((N_EXPERTS, N_TOK), lambda i: (0, 0)),
            _full_spec((LANES, LANES)),
            _full_spec((STACK_ROWS, STACK_ROWS)),
        ],
        out_specs=[_full_spec((N_EXPERTS, CAPACITY))] * 2,
        out_shape=[jax.ShapeDtypeStruct((N_EXPERTS, CAPACITY), jnp.int32),
                   jax.ShapeDtypeStruct((N_EXPERTS, CAPACITY), F32)],
        scratch_shapes=[pltpu.VMEM((STACK_ROWS, LANES), F32)] * 4,
        compiler_params=_cp(48),
        name=name,
    )(logits_t, tri, blk)


GATHER_UNROLL = 8


def _gather_kernel(idx_ref, hs_ref, xe_ref, tile_ref):
    ex = pl.program_id(0)

    def body(i, carry):
        for u in range(GATHER_UNROLL):
            s = i * GATHER_UNROLL + u
            t = idx_ref[ex * CAPACITY + s]
            slab = hs_ref[pl.ds(pl.multiple_of(t * TOK_SLABS, TOK_SLABS), TOK_SLABS), :]
            tile_ref[pl.ds(s, TOK_SLABS, stride=XE_STRIDE), :] = slab
        return carry

    lax.fori_loop(0, CAPACITY // GATHER_UNROLL, body, 0)
    xe_ref[...] = jnp.concatenate(
        [tile_ref[pl.ds(c * XE_STRIDE, CAPACITY), :] for c in range(TOK_SLABS)], axis=1).astype(BF16)


def _gather(idx_flat, hs, name):
    return pl.pallas_call(
        _gather_kernel,
        grid_spec=pltpu.PrefetchScalarGridSpec(
            num_scalar_prefetch=1,
            grid=(N_EXPERTS,),
            in_specs=[pl.BlockSpec((N_TOK * TOK_SLABS, LANES), lambda e, idx: (0, 0), pipeline_mode=pl.Buffered(1))],
            out_specs=pl.BlockSpec((None, CAPACITY, D_MODEL), lambda e, idx: (e, 0, 0)),
            scratch_shapes=[pltpu.VMEM((TOK_SLABS * XE_STRIDE, LANES), F32)],
        ),
        out_shape=jax.ShapeDtypeStruct((N_EXPERTS, CAPACITY, D_MODEL), BF16),
        compiler_params=_cp(40),
        name=name,
    )(idx_flat, hs)


def _ffn_kernel(xc_ref, xl_ref, gc_ref, gl_ref, w1_ref, w3_ref, w2_ref, yc_ref, yl_ref, xe_ref, acc_ref):
    f = pl.program_id(1)

    @pl.when(f == 0)
    def _():
        xe_ref[:CAPACITY, :] = xc_ref[...]
        xe_ref[CAPACITY:, :] = xl_ref[...]
        acc_ref[...] = jnp.zeros_like(acc_ref)

    x = xe_ref[...]
    h1 = _dot(x, w1_ref[...].astype(BF16))
    h3 = _dot(x, w3_ref[...].astype(BF16))
    hid = (h1 * jax.nn.sigmoid(h1) * h3).astype(BF16)
    acc_ref[...] += _dot(hid, w2_ref[...].astype(BF16))

    @pl.when(f == pl.num_programs(1) - 1)
    def _():
        _to_slabs(yc_ref, acc_ref[:CAPACITY, :] * gc_ref[...], CAPACITY)
        _to_slabs(yl_ref, acc_ref[CAPACITY:, :] * gl_ref[...], CAPACITY)


def _expert_ffn(xe_ctx, xe_lat, gate_ctx, gate_lat, w1, w3, w2, layer, name):
    nf = D_EXPERT // FF_TILE
    x_spec = pl.BlockSpec((None, CAPACITY, D_MODEL), lambda e, f: (e, 0, 0))
    g_spec = pl.BlockSpec((None, CAPACITY, 1), lambda e, f: (e, 0, 0))
    y_spec = pl.BlockSpec((None, CAPACITY * TOK_SLABS, LANES), lambda e, f: (e, 0, 0))
    y_shape = jax.ShapeDtypeStruct((N_EXPERTS, CAPACITY * TOK_SLABS, LANES), F32)
    return pl.pallas_call(
        _ffn_kernel,
        grid=(N_EXPERTS, nf),
        in_specs=[
            x_spec, x_spec, g_spec, g_spec,
            pl.BlockSpec((None, None, D_MODEL, FF_TILE), lambda e, f: (layer, e, 0, f)),
            pl.BlockSpec((None, None, D_MODEL, FF_TILE), lambda e, f: (layer, e, 0, f)),
            pl.BlockSpec((None, None, FF_TILE, D_MODEL), lambda e, f: (layer, e, f, 0)),
        ],
        out_specs=[y_spec, y_spec],
        out_shape=[y_shape, y_shape],
        scratch_shapes=[
            pltpu.VMEM((2 * CAPACITY, D_MODEL), BF16),
            pltpu.VMEM((2 * CAPACITY, D_MODEL), F32),
        ],
        compiler_params=_cp(56),
        name=name,
    )(xe_ctx, xe_lat, gate_ctx, gate_lat, w1, w3, w2)


SCATTER_UNROLL = 8


def _scatter_expert(idx_ref, ye_ref, y_ref, ex):
    def body(i, carry):
        upd = []
        for u in range(SCATTER_UNROLL):
            s = i * SCATTER_UNROLL + u
            t = idx_ref[ex * CAPACITY + s]
            rows = pl.ds(pl.multiple_of(t * TOK_SLABS, TOK_SLABS), TOK_SLABS)
            src = ye_ref[pl.ds(pl.multiple_of(s * TOK_SLABS, TOK_SLABS), TOK_SLABS), :]
            upd.append((rows, y_ref[rows, :] + src))
        for rows, val in upd:
            y_ref[rows, :] = val
        return carry

    lax.fori_loop(0, CAPACITY // SCATTER_UNROLL, body, 0)


def _combine_kernel(idx_ref, ye_ref, y_ref):
    ex = pl.program_id(0)

    @pl.when(ex == 0)
    def _():
        y_ref[...] = jnp.zeros_like(y_ref)

    _scatter_expert(idx_ref, ye_ref, y_ref, ex)


def _combine_final_kernel(idx_ref, ye_ref, xp_ref, gp_ref, fg_ref, o_ref, y_ref):
    i = pl.program_id(0)

    @pl.when(i == 0)
    def _():
        y_ref[...] = jnp.zeros_like(y_ref)

    @pl.when(i < N_EXPERTS)
    def _():
        _scatter_expert(idx_ref, ye_ref, y_ref, i)

    @pl.when(i >= N_EXPERTS)
    def _():
        slabs_per_tile = ROW_TILE * TOK_SLABS
        base = pl.multiple_of((i - N_EXPERTS) * slabs_per_tile, slabs_per_tile)
        x = xp_ref[...] + gp_ref[...] * _from_slabs(y_ref, ROW_TILE, base)
        o_ref[...] = _rmsnorm(x, fg_ref[...])


def _combine_final(idx_flat, ye, x_prev, mods_prev, final_g, lat, name):
    tiles_per_batch = DEC_SEQ // ROW_TILE

    def tile(i):
        return jnp.maximum(i - N_EXPERTS, 0)

    def gate_row(i, idx):
        return ((1 + tile(i) // tiles_per_batch) * 6 + 5 if lat else 5, 0, 0)

    return pl.pallas_call(
        _combine_final_kernel,
        grid_spec=pltpu.PrefetchScalarGridSpec(
            num_scalar_prefetch=1,
            grid=(N_EXPERTS + N_ROW_TILES,),
            in_specs=[
                pl.BlockSpec((None, CAPACITY * TOK_SLABS, LANES), lambda i, idx: (jnp.minimum(i, N_EXPERTS - 1), 0, 0)),
                pl.BlockSpec((ROW_TILE, D_MODEL), lambda i, idx: (tile(i), 0)),
                pl.BlockSpec((None, 1, D_MODEL), gate_row),
                pl.BlockSpec((1, D_MODEL), lambda i, idx: (0, 0)),
            ],
            out_specs=pl.BlockSpec((ROW_TILE, D_MODEL), lambda i, idx: (tile(i), 0)),
            scratch_shapes=[pltpu.VMEM((N_TOK * TOK_SLABS, LANES), F32)],
        ),
        out_shape=jax.ShapeDtypeStruct((N_TOK, D_MODEL), F32),
        compiler_params=_cp(40),
        name=name,
    )(idx_flat, ye, x_prev, mods_prev, final_g)


def _combine(idx_flat, ye, name):
    return pl.pallas_call(
        _combine_kernel,
        grid_spec=pltpu.PrefetchScalarGridSpec(
            num_scalar_prefetch=1,
            grid=(N_EXPERTS,),
            in_specs=[pl.BlockSpec((None, CAPACITY * TOK_SLABS, LANES), lambda e, idx: (e, 0, 0))],
            out_specs=pl.BlockSpec((N_TOK * TOK_SLABS, LANES), lambda e, idx: (0, 0)),
        ),
        out_shape=jax.ShapeDtypeStruct((N_TOK * TOK_SLABS, LANES), F32),
        compiler_params=_cp(56),
        name=name,
    )(idx_flat, ye)


def _moe(hs_ctx, lt_ctx, hs_lat, lt_lat, tri, blk, w1, w3, w2, layer):
    tag = "l%d" % layer
    routed = []
    for hs, lt, name in ((hs_ctx, lt_ctx, "ctx"), (hs_lat, lt_lat, "lat")):
        idx, gate = _router(lt, tri, blk, "router_%s_%s" % (tag, name))
        idx_flat = idx.reshape(N_EXPERTS * CAPACITY)
        xe = _gather(idx_flat, hs, "gather_%s_%s" % (tag, name))
        routed.append((idx_flat, xe, gate.reshape(N_EXPERTS, CAPACITY, 1)))
    (idx_c, xe_c, g_c), (idx_l, xe_l, g_l) = routed
    ye_c, ye_l = _expert_ffn(xe_c, xe_l, g_c, g_l, w1, w3, w2, layer, "ffn_" + tag)
    return (idx_c, ye_c), (idx_l, ye_l)


def _gla_proj_kernel(xp_ref, yt_ref, gp_ref, g_ref, sh_ref, sc_ref, w_ref, wt_ref, wg_ref, bg_ref,
                     x_ref, q_ref, k_ref, v_ref, r_ref, lgf_ref, lgb_ref, wb_ref):
    _cast_once(w_ref, wb_ref)
    x = xp_ref[...] + gp_ref[...] * _from_slabs(yt_ref, ROW_TILE)
    x_ref[...] = x
    h = _modulate(x, g_ref[...], sh_ref[...], sc_ref[...]).astype(BF16)
    z = _dot(h, wb_ref[...])
    nq = GLA_HEADS * GLA_DK
    nv = GLA_HEADS * GLA_DV
    q_ref[...] = z[:, :nq] * (GLA_DK ** -0.5)
    k_ref[...] = z[:, nq:2 * nq]
    v_ref[...] = z[:, 2 * nq:2 * nq + nv].astype(BF16)
    r_ref[...] = z[:, 2 * nq + nv:].astype(BF16)
    zg = _dot(h, wt_ref[...]).astype(BF16)
    a = _dot(zg, wg_ref[...]) + bg_ref[...]
    ls = (jnp.minimum(a, 0.0) - jnp.log1p(jnp.exp(-jnp.abs(a)))) * (1.0 / GLA_GATE_NORM)
    lgf_ref[...] = ls[:, :nq]
    lgb_ref[...] = ls[:, nq:]


def _gla_proj(x_prev, y_slabs, mods_prev, mods, norm_g, w_in, w_tail, wg, bg, lat, name):
    nq = GLA_HEADS * GLA_DK
    nv = GLA_HEADS * GLA_DV
    n_main = 2 * nq + 2 * nv
    return pl.pallas_call(
        _gla_proj_kernel,
        grid=(N_ROW_TILES,),
        in_specs=[
            _row_spec(D_MODEL),
            pl.BlockSpec((ROW_TILE * TOK_SLABS, LANES), lambda i: (i, 0)),
            _mod_spec(5, lat),
            _full_spec((1, D_MODEL)),
            _mod_spec(0, lat),
            _mod_spec(1, lat),
            _weight_spec(D_MODEL, n_main),
            _full_spec((D_MODEL, LANES)),
            _full_spec((LANES, 2 * nq)),
            _full_spec((1, 2 * nq)),
        ],
        out_specs=[_row_spec(D_MODEL), _row_spec(nq), _row_spec(nq), _row_spec(nv), _row_spec(nv),
                   _row_spec(nq), _row_spec(nq)],
        out_shape=[
            jax.ShapeDtypeStruct((N_TOK, D_MODEL), F32),
            jax.ShapeDtypeStruct((N_TOK, nq), F32),
            jax.ShapeDtypeStruct((N_TOK, nq), F32),
            jax.ShapeDtypeStruct((N_TOK, nv), BF16),
            jax.ShapeDtypeStruct((N_TOK, nv), BF16),
            jax.ShapeDtypeStruct((N_TOK, nq), F32),
            jax.ShapeDtypeStruct((N_TOK, nq), F32),
        ],
        scratch_shapes=[pltpu.VMEM((D_MODEL, n_main), BF16)],
        compiler_params=_cp(56),
        name=name,
    )(x_prev, y_slabs, mods_prev, norm_g, mods, mods, w_in, w_tail, wg, bg)


GLA_LEVELS = (32, 16, 8)
GLA_UNROLL = 8
GLA_GROUP = 4 * GLA_CHUNK
GLA_SAFE_DECAY = 60.0
NEG_BIG = -1e30


def _gla_intra_fast(qe, kc, b2, fwd):
    c = GLA_CHUNK
    row = lax.broadcasted_iota(jnp.int32, (c, 1), 0)
    col = lax.broadcasted_iota(jnp.int32, (1, c), 1)
    keep = (col <= row) if fwd else (col >= row)
    kq = (kc * jnp.exp2(-b2)).astype(BF16)
    return jnp.where(keep, _dot_nt(qe, kq), 0.0)


def _gla_intra_robust(qc, kc, b2, fwd):
    c = GLA_CHUNK
    row = lax.broadcasted_iota(jnp.int32, (c, 1), 0)
    col = lax.broadcasted_iota(jnp.int32, (1, c), 1)
    a = jnp.zeros((c, c), F32)
    for g in GLA_LEVELS:
        odd = ((row >> int(math.log2(g))) & 1) == 1
        later = odd if fwd else jnp.logical_not(odd)
        refs = []
        for p in range(c // (2 * g)):
            r0 = 2 * g * p + (g - 1 if fwd else g)
            refs.append(jnp.broadcast_to(b2[r0:r0 + 1], (2 * g, GLA_DK)))
        ref = jnp.concatenate(refs, axis=0) if len(refs) > 1 else refs[0]
        qt = qc * jnp.exp2(jnp.where(later, b2 - ref, NEG_BIG))
        kt = kc * jnp.exp2(jnp.where(later, NEG_BIG, ref - b2))
        same_parent = (row >> int(math.log2(2 * g))) == (col >> int(math.log2(2 * g)))
        a = a + jnp.where(same_parent, _dot_nt(qt.astype(BF16), kt.astype(BF16)), 0.0)

    sub = lax.broadcasted_iota(jnp.int32, (SUBLANES, 1), 0)
    strips = []
    for blk in range(c // SUBLANES):
        r0 = SUBLANES * blk
        qb = qc[r0:r0 + SUBLANES]
        bb = b2[r0:r0 + SUBLANES]
        strip = jnp.zeros((SUBLANES, c), F32)
        for jj in range(SUBLANES):
            j = r0 + jj
            cond = (sub >= jj) if fwd else (sub <= jj)
            t = qb * kc[j:j + 1] * jnp.exp2(jnp.where(cond, bb - b2[j:j + 1], NEG_BIG))
            strip = jnp.where(col == j, jnp.sum(t, axis=1, keepdims=True), strip)
        strips.append(strip)
    return a + jnp.concatenate(strips, axis=0)


def _gla_state_increment(kc, vc, b2f, b2b):
    c = GLA_CHUNK
    btf = b2f[c - 1:c]
    btb = b2b[0:1]
    kd = jnp.concatenate([(kc * jnp.exp2(btf - b2f)).astype(BF16), (kc * jnp.exp2(btb - b2b)).astype(BF16)], axis=1)
    return _dot_tn(vc, kd), jnp.concatenate([jnp.exp2(btf), jnp.exp2(btb)], axis=1)


def _gla_scores(qc, kc, b2f, b2b, fast):
    qe_f = (qc * jnp.exp2(b2f)).astype(BF16)
    qe_b = (qc * jnp.exp2(b2b)).astype(BF16)
    if fast:
        a = _gla_intra_fast(qe_f, kc, b2f, True) + _gla_intra_fast(qe_b, kc, b2b, False)
    else:
        a = _gla_intra_robust(qc, kc, b2f, True) + _gla_intra_robust(qc, kc, b2b, False)
    return jnp.concatenate([qe_f, qe_b], axis=1), a.astype(BF16)


def _gla_kernel(q_ref, k_ref, v_ref, r_ref, gf_ref, gb_ref, s0f_ref, s0b_ref, go_ref, trif_ref, trib_ref,
                u_ref, sf_ref, sb_ref, *scratch, t_len, heads):
    for hh in range(heads):
        dk = slice(hh * GLA_DK, (hh + 1) * GLA_DK)
        dv = slice(hh * GLA_DV, (hh + 1) * GLA_DV)
        _gla_head(q_ref.at[:, dk], k_ref.at[:, dk], v_ref.at[:, dv], r_ref.at[:, dv], gf_ref.at[:, dk],
                  gb_ref.at[:, dk], s0f_ref.at[hh], s0b_ref.at[hh], go_ref, trif_ref, trib_ref,
                  u_ref.at[:, dv], sf_ref.at[hh], sb_ref.at[hh], *scratch, t_len=t_len)


def _gla_head(q_ref, k_ref, v_ref, r_ref, gf_ref, gb_ref, s0f_ref, s0b_ref, go_ref, trif_ref, trib_ref,
                u_ref, sf_ref, sb_ref,
                b2f_ref, b2b_ref, qe_ref, oi_ref, kv_ref, dec_ref, sst_ref, st_ref, *, t_len):
    c = GLA_CHUNK
    n = t_len // c

    def rows_of(ci):
        return pl.ds(pl.multiple_of(ci * c, c), c)

    def state_rows_of(ci):
        return pl.ds(pl.multiple_of(ci * GLA_DV, GLA_DV), GLA_DV)

    n_groups = t_len // GLA_GROUP
    cum_unroll = 2 if n_groups % 2 == 0 else 1

    def cum_body(i, carry):
        sums = []
        for u in range(cum_unroll):
            rows = pl.ds(pl.multiple_of((i * cum_unroll + u) * GLA_GROUP, GLA_GROUP), GLA_GROUP)
            for g_ref, tri_ref, b_ref in ((gf_ref, trif_ref, b2f_ref), (gb_ref, trib_ref, b2b_ref)):
                s3 = _dot(tri_ref[...], jnp.concatenate(_split3(g_ref[rows, :]), axis=1))
                sums.append((b_ref, rows, s3))
        for b_ref, rows, s3 in sums:
            b_ref[rows, :] = (s3[:, :GLA_DK] + s3[:, GLA_DK:2 * GLA_DK] + s3[:, 2 * GLA_DK:]) * LOG2E
        return carry

    lax.fori_loop(0, n_groups // cum_unroll, cum_body, 0)

    def chunk_total(g_ref):
        return jnp.min(jnp.sum(g_ref[...].reshape(n, c, GLA_DK), axis=1))

    safe = jnp.minimum(chunk_total(gf_ref), chunk_total(gb_ref)) >= -GLA_SAFE_DECAY

    def make_local(fast, unroll):
        def body(i, carry):
            chunks = [i * unroll + u for u in range(unroll)]
            for ci in chunks:
                rows = rows_of(ci)
                kv, dec = _gla_state_increment(k_ref[rows, :], v_ref[rows, :], b2f_ref[rows, :], b2b_ref[rows, :])
                kv_ref[state_rows_of(ci), :] = kv
                dec_ref[pl.ds(ci, 1), :] = dec
            scores = []
            for ci in chunks:
                rows = rows_of(ci)
                qe, a = _gla_scores(q_ref[rows, :], k_ref[rows, :], b2f_ref[rows, :], b2b_ref[rows, :], fast)
                qe_ref[rows, :] = qe
                scores.append(a)
            for ci, a in zip(chunks, scores):
                rows = rows_of(ci)
                oi_ref[rows, :] = _dot(a, v_ref[rows, :])
            return carry
        return body

    unroll = min(GLA_UNROLL, n)

    @pl.when(safe)
    def _():
        lax.fori_loop(0, n // unroll, make_local(True, unroll), 0)

    @pl.when(jnp.logical_not(safe))
    def _():
        lax.fori_loop(0, n, make_local(False, 1), 0)

    st_ref[...] = jnp.concatenate([s0f_ref[...].T, s0b_ref[...].T], axis=1)
    fl = slice(0, GLA_DK)
    bl = slice(GLA_DK, 2 * GLA_DK)

    def rec_body(i, carry):
        cf = i
        cb = n - 1 - i
        st = st_ref[...]
        sst_ref[state_rows_of(cf), fl] = st[:, fl].astype(BF16)
        sst_ref[state_rows_of(cb), bl] = st[:, bl].astype(BF16)
        dec = jnp.concatenate([dec_ref[pl.ds(cf, 1), :][:, fl], dec_ref[pl.ds(cb, 1), :][:, bl]], axis=1)
        kv = jnp.concatenate([kv_ref[state_rows_of(cf), fl], kv_ref[state_rows_of(cb), bl]], axis=1)
        st_ref[...] = st * dec + kv
        return carry

    lax.fori_loop(0, n, rec_body, 0)
    sf_ref[...] = st_ref[:, fl].T
    sb_ref[...] = st_ref[:, bl].T

    go = go_ref[...]

    def out_body(i, carry):
        chunks = [i * unroll + u for u in range(unroll)]
        inter = [_dot_nt(qe_ref[rows_of(ci), :], sst_ref[state_rows_of(ci), :]) for ci in chunks]
        for ci, o_inter in zip(chunks, inter):
            rows = rows_of(ci)
            r = r_ref[rows, :].astype(F32)
            u_ref[rows, :] = (_rmsnorm(oi_ref[rows, :] + o_inter, go) * (r * jax.nn.sigmoid(r))).astype(BF16)
        return carry

    lax.fori_loop(0, n // unroll, out_body, 0)


def _gla_tri():
    i = np.arange(GLA_GROUP)
    same = (i[:, None] // GLA_CHUNK) == (i[None, :] // GLA_CHUNK)
    fwd = (same & (i[None, :] <= i[:, None])).astype(np.float32)
    bwd = (same & (i[None, :] >= i[:, None])).astype(np.float32)
    return jnp.asarray(fwd, BF16), jnp.asarray(bwd, BF16)


def _gla(q, k, v, r, lgf, lgb, s0f, s0b, g_out, n_batch, t_len, heads, name):
    trif, trib = _gla_tri()
    n_chunks = t_len // GLA_CHUNK
    qk_spec = pl.BlockSpec((None, t_len, heads * GLA_DK), lambda b, h: (b, 0, h))
    v_spec = pl.BlockSpec((None, t_len, heads * GLA_DV), lambda b, h: (b, 0, h))
    s_spec = pl.BlockSpec((None, heads, GLA_DK, GLA_DV), lambda b, h: (b, h, 0, 0))
    const = lambda shape: pl.BlockSpec(shape, lambda b, h: (0,) * len(shape))
    s_shape = jax.ShapeDtypeStruct((n_batch, GLA_HEADS, GLA_DK, GLA_DV), F32)
    return pl.pallas_call(
        functools.partial(_gla_kernel, t_len=t_len, heads=heads),
        grid=(n_batch, GLA_HEADS // heads),
        in_specs=[qk_spec, qk_spec, v_spec, v_spec, qk_spec, qk_spec, s_spec, s_spec,
                  const((1, GLA_DV)), const((GLA_GROUP, GLA_GROUP)), const((GLA_GROUP, GLA_GROUP))],
        out_specs=[v_spec, s_spec, s_spec],
        out_shape=[jax.ShapeDtypeStruct((n_batch, t_len, GLA_HEADS * GLA_DV), BF16), s_shape, s_shape],
        scratch_shapes=[
            pltpu.VMEM((t_len, GLA_DK), F32), pltpu.VMEM((t_len, GLA_DK), F32),
            pltpu.VMEM((t_len, 2 * GLA_DK), BF16),
            pltpu.VMEM((t_len, GLA_DV), F32),
            pltpu.VMEM((n_chunks * GLA_DV, 2 * GLA_DK), F32),
            pltpu.VMEM((max(n_chunks, SUBLANES), 2 * GLA_DK), F32),
            pltpu.VMEM((n_chunks * GLA_DV, 2 * GLA_DK), BF16),
            pltpu.VMEM((GLA_DV, 2 * GLA_DK), F32),
        ],
        compiler_params=_cp(48),
        name=name,
    )(q, k, v, r, lgf, lgb, s0f, s0b, g_out, trif, trib)


def kernel(x_prompt, x_sample, cache_k, cache_v, state_fwd, state_bwd, c, c_ctx, w_mod, b_mod, norm_g,
           da_w_in, da_w_out, da_lam_q1, da_lam_k1, da_lam_q2, da_lam_k2, da_g_sub, gla_w_in, gla_w_gf2,
           gla_b_gf, gla_w_gb2, gla_b_gb, gla_g_out, gla_w_out, moe_w_router, moe_w1, moe_w3, moe_w2, final_g):
    n_ctx_b = x_prompt.shape[0]
    n_lat_b = x_sample.shape[0]
    x_ctx = x_prompt.reshape(N_TOK, D_MODEL)
    x_lat = x_sample.reshape(N_TOK, D_MODEL)

    cvec = jnp.zeros((SUBLANES, D_MODEL), F32).at[0].set(c_ctx).at[1:1 + n_lat_b].set(c)
    mods = _mod_params(cvec, w_mod, b_mod)
    tri, blk = _cumsum_consts()
    wr_t = [jnp.zeros((LANES, D_MODEL), BF16).at[:N_EXPERTS].set(moe_w_router[i].T.astype(BF16))
            for i in range(DEPTH)]
    ng = norm_g.reshape(DEPTH, 2, 1, D_MODEL)

    lam_init = 0.8 - 0.6 * math.exp(-0.3 * 0)
    w_in = da_w_in
    w_out = da_w_out
    lam_vecs = [v[0].reshape(1, DA_DIM) for v in (da_lam_q1, da_lam_k1, da_lam_q2, da_lam_k2)]
    g_sub = da_g_sub[0].reshape(1, DA_VDIM)

    qc, kc, vc, kf, vf = _da_proj(x_ctx, mods[0], ng[0, 0], w_in, lat=False)
    ql, kl, vl = _da_proj(x_lat, mods[0], ng[0, 0], w_in, lat=True, rope_tabs=_rope_tables())
    past = cache_k.shape[2]
    ck = cache_k[:, 0].reshape(n_lat_b * past, D_MODEL)
    cv = cache_v[:, 0].reshape(n_lat_b * past, D_MODEL)
    u_ctx = _attention(lam_vecs, g_sub, qc, [(kc, vc, SEQ)], n_ctx_b, SEQ, ATTN_Q_TILE, lam_init, "attn_ctx")
    u_lat = _attention(lam_vecs, g_sub, ql, [(kl, vl, DEC_SEQ), (ck, cv, past)], n_lat_b, DEC_SEQ, ATTN_Q_TILE,
                       lam_init, "attn_lat")

    x1_ctx, hs_ctx, lt_ctx = _outproj(u_ctx, w_out, x_ctx, mods[0], ng[0, 1], wr_t[0], False, "outproj0_ctx")
    x1_lat, hs_lat, lt_lat = _outproj(u_lat, w_out, x_lat, mods[0], ng[0, 1], wr_t[0], True, "outproj0_lat")
    (idx_c, ye_c), (idx_l, ye_l) = _moe(hs_ctx, lt_ctx, hs_lat, lt_lat, tri, blk, moe_w1, moe_w3, moe_w2, 0)
    y_ctx = _combine(idx_c, ye_c, "combine_l0_ctx")
    y_lat = _combine(idx_l, ye_l, "combine_l0_lat")

    nq = GLA_HEADS * GLA_DK
    n_main = 2 * nq + 2 * GLA_HEADS * GLA_DV
    w_tail = jnp.zeros((D_MODEL, LANES), F32).at[:, :2 * GLA_GATE_RANK].set(gla_w_in[0][:, n_main:]).astype(BF16)
    wg = jnp.zeros((LANES, 2 * nq), F32)
    wg = wg.at[:GLA_GATE_RANK, :nq].set(gla_w_gf2[0]).at[GLA_GATE_RANK:2 * GLA_GATE_RANK, nq:].set(gla_w_gb2[0])
    wg = wg.astype(BF16)
    bg = jnp.concatenate([gla_b_gf[0], gla_b_gb[0]]).reshape(1, 2 * nq)
    w_out1 = gla_w_out
    g_out = gla_g_out[0].reshape(1, GLA_DV)

    def gla_side(x1, y, lat, n_b, t_len, s0f, s0b, tag):
        x2, q, k, v, r, lgf, lgb = _gla_proj(x1, y, mods[0], mods[1], ng[1, 0], gla_w_in, w_tail, wg, bg, lat,
                                             "gla_proj_" + tag)
        sh3 = lambda a: a.reshape(n_b, t_len, a.shape[-1])
        heads = GLA_HEADS if t_len <= GLA_GROUP else 1
        u, sf, sb = _gla(sh3(q), sh3(k), sh3(v), sh3(r), sh3(lgf), sh3(lgb), s0f, s0b, g_out, n_b, t_len, heads,
                         "gla_" + tag)
        return x2, u.reshape(N_TOK, D_MODEL), sf, sb

    zeros_s = jnp.zeros((n_ctx_b, GLA_HEADS, GLA_DK, GLA_DV), F32)
    x2_ctx, ug_ctx, sf, sb = gla_side(x1_ctx, y_ctx, False, n_ctx_b, SEQ, zeros_s, zeros_s, "ctx")
    x2_lat, ug_lat, _, _ = gla_side(x1_lat, y_lat, True, n_lat_b, DEC_SEQ, state_fwd[:, 0], state_bwd[:, 0], "lat")

    x3_ctx, hs_ctx, lt_ctx = _outproj(ug_ctx, w_out1, x2_ctx, mods[1], ng[1, 1], wr_t[1], False, "outproj1_ctx")
    x3_lat, hs_lat, lt_lat = _outproj(ug_lat, w_out1, x2_lat, mods[1], ng[1, 1], wr_t[1], True, "outproj1_lat")
    (idx_c, ye_c), (idx_l, ye_l) = _moe(hs_ctx, lt_ctx, hs_lat, lt_lat, tri, blk, moe_w1, moe_w3, moe_w2, 1)

    fg = final_g.reshape(1, D_MODEL)
    y_prompt = _combine_final(idx_c, ye_c, x3_ctx, mods[1], fg, False, "final_ctx").reshape(x_prompt.shape)
    y_sample = _combine_final(idx_l, ye_l, x3_lat, mods[1], fg, True, "final_lat").reshape(x_sample.shape)
    new_k = kf.reshape(n_ctx_b, 1, SEQ, 2 * DA_HEADS, DA_DIM)
    new_v = vf.reshape(n_ctx_b, 1, SEQ, DA_HEADS, DA_VDIM)
    return (y_prompt, y_sample, new_k, new_v, sf[:, None], sb[:, None])
```

```python
import functools
import math

import numpy as np
import jax
import jax.numpy as jnp
from jax import lax
from jax.experimental import pallas as pl
from jax.experimental.pallas import tpu as pltpu

F32 = jnp.float32
BF16 = jnp.bfloat16

D_MODEL = 1024
DEPTH = 2
SEQ = 256
DEC_SEQ = 2048
GRID_W = 64
N_TOK = 4096
DA_HEADS = 8
DA_DIM = 64
DA_VDIM = 128
ROPE_BASE = 10000.0
GLA_HEADS = 4
GLA_DK = 128
GLA_DV = 256
GLA_GATE_RANK = 16
GLA_GATE_NORM = 16.0
GLA_CHUNK = 64
N_EXPERTS = 16
CAPACITY = 512
D_EXPERT = 2048
EPS = 1e-6
F32_MIN_NORMAL = 2.0 ** -126
LOG2E = 1.4426950408889634

LANES = 128
SUBLANES = 8
ROW_TILE = 512
FF_TILE = 512
TOK_SLABS = D_MODEL // LANES
XE_STRIDE = CAPACITY + SUBLANES
MIB = 1024 * 1024


def _cp(vmem_mib, sem=None):
    return pltpu.CompilerParams(vmem_limit_bytes=vmem_mib * MIB, dimension_semantics=sem)


def _dot(a, b):
    return jnp.dot(a, b, preferred_element_type=F32)


def _dot_nt(a, b):
    return lax.dot_general(a, b, (((1,), (1,)), ((), ())), preferred_element_type=F32)


def _dot_tn(a, b):
    return lax.dot_general(a, b, (((0,), (0,)), ((), ())), preferred_element_type=F32)


def _rmsnorm(x, g):
    return x * lax.rsqrt(jnp.mean(x * x, axis=-1, keepdims=True) + EPS) * g


def _modulate(x, g, shift, scale):
    return _rmsnorm(x, g) * (1.0 + scale) + shift


def _split3(x):
    hi = x.astype(BF16)
    r = x - hi.astype(F32)
    mid = r.astype(BF16)
    lo = (r - mid.astype(F32)).astype(BF16)
    return hi, mid, lo


def _from_slabs(ref, rows, base=0):
    return jnp.concatenate([ref[pl.ds(base + c, rows, stride=TOK_SLABS), :] for c in range(TOK_SLABS)], axis=1)


def _to_slabs(ref, val, rows):
    for c in range(TOK_SLABS):
        ref[pl.ds(c, rows, stride=TOK_SLABS), :] = val[:, LANES * c:LANES * (c + 1)]


def _mod_kernel(c_ref, w_ref, b_ref, o_ref):
    c = c_ref[...]
    s = c * jax.nn.sigmoid(c)
    w = w_ref[...]
    s_hi = s.astype(BF16)
    s_lo = (s - s_hi.astype(F32)).astype(BF16)
    w_hi = w.astype(BF16)
    w_lo = (w - w_hi.astype(F32)).astype(BF16)
    o_ref[...] = _dot(s_hi, w_hi) + _dot(s_hi, w_lo) + _dot(s_lo, w_hi) + b_ref[...]


def _mod_params(cvec, w_mod, b_mod):
    n6 = 6 * D_MODEL
    out = pl.pallas_call(
        _mod_kernel,
        grid=(DEPTH, 6),
        in_specs=[
            pl.BlockSpec((SUBLANES, D_MODEL), lambda i, j: (0, 0)),
            pl.BlockSpec((None, D_MODEL, D_MODEL), lambda i, j: (i, 0, j)),
            pl.BlockSpec((None, 1, D_MODEL), lambda i, j: (i, 0, j)),
        ],
        out_specs=pl.BlockSpec((None, SUBLANES, D_MODEL), lambda i, j: (i, 0, j)),
        out_shape=jax.ShapeDtypeStruct((DEPTH, SUBLANES, n6), F32),
        compiler_params=_cp(32),
        name="mod_params",
    )(cvec, w_mod, b_mod.reshape(DEPTH, 1, n6))
    return out.reshape(DEPTH, SUBLANES * 6, 1, D_MODEL)


def _mod_spec(k, lat):
    tiles_per_batch = DEC_SEQ // ROW_TILE
    if lat:
        return pl.BlockSpec((None, 1, D_MODEL), lambda i: ((1 + i // tiles_per_batch) * 6 + k, 0, 0))
    return pl.BlockSpec((None, 1, D_MODEL), lambda i: (k, 0, 0))


def _row_spec(width, dtype_rows=ROW_TILE):
    return pl.BlockSpec((dtype_rows, width), lambda i: (i, 0))


def _full_spec(shape):
    nd = len(shape)
    return pl.BlockSpec(shape, lambda i: (0,) * nd)


N_ROW_TILES = N_TOK // ROW_TILE


def _weight_spec(rows, cols):
    return pl.BlockSpec((None, rows, cols), lambda i: (0, 0, 0), pipeline_mode=pl.Buffered(1))


def _cast_once(w_ref, wb_ref):
    @pl.when(pl.program_id(0) == 0)
    def _():
        wb_ref[...] = w_ref[...].astype(BF16)


def _da_proj_kernel(*refs, rope, emit_f32):
    x_ref, g_ref, sh_ref, sc_ref, w_ref = refs[:5]
    pos = 5
    if rope:
        cos_ref, sin_ref = refs[pos:pos + 2]
        pos += 2
    q_ref, k_ref, v_ref = refs[pos:pos + 3]
    pos += 3
    wb_ref = refs[-1]
    _cast_once(w_ref, wb_ref)
    h = _modulate(x_ref[...], g_ref[...], sh_ref[...], sc_ref[...])
    z = _dot(h.astype(BF16), wb_ref[...])
    q = z[:, :D_MODEL]
    k = z[:, D_MODEL:2 * D_MODEL]
    v = z[:, 2 * D_MODEL:]
    if emit_f32:
        kf_ref, vf_ref = refs[pos:pos + 2]
        kf_ref[...] = k
        vf_ref[...] = v
    if rope:
        reps = D_MODEL // LANES
        cos = jnp.concatenate([cos_ref[...]] * reps, axis=1)
        sin = jnp.concatenate([sin_ref[...]] * reps, axis=1)
        lane = lax.broadcasted_iota(jnp.int32, (1, D_MODEL), 1)
        first = (lane & 16) == 0

        def rot(t):
            partner = jnp.where(first, pltpu.roll(t, D_MODEL - 16, 1), pltpu.roll(t, 16, 1))
            return t * cos + partner * sin

        q = rot(q)
        k = rot(k)
    q_ref[...] = (q * (DA_DIM ** -0.5 * LOG2E)).astype(BF16)
    k_ref[...] = k.astype(BF16)
    v_ref[...] = v.astype(BF16)


def _da_proj(x, mods, norm_g, w_in, lat, rope_tabs=None):
    rope = rope_tabs is not None
    emit_f32 = not lat
    in_specs = [
        _row_spec(D_MODEL),
        _full_spec((1, D_MODEL)),
        _mod_spec(0, lat),
        _mod_spec(1, lat),
        _weight_spec(D_MODEL, 3 * D_MODEL),
    ]
    args = [x, norm_g, mods, mods, w_in]
    if rope:
        tiles_per_batch = DEC_SEQ // ROW_TILE
        tab_spec = pl.BlockSpec((ROW_TILE, LANES), lambda i: (i % tiles_per_batch, 0))
        in_specs += [tab_spec, tab_spec]
        args += list(rope_tabs)
    out_specs = [_row_spec(D_MODEL)] * 3
    out_shape = [jax.ShapeDtypeStruct((N_TOK, D_MODEL), BF16)] * 3
    if emit_f32:
        out_specs += [_row_spec(D_MODEL)] * 2
        out_shape += [jax.ShapeDtypeStruct((N_TOK, D_MODEL), F32)] * 2
    return pl.pallas_call(
        functools.partial(_da_proj_kernel, rope=rope, emit_f32=emit_f32),
        grid=(N_ROW_TILES,),
        in_specs=in_specs,
        out_specs=out_specs,
        out_shape=out_shape,
        scratch_shapes=[pltpu.VMEM((D_MODEL, 3 * D_MODEL), BF16)],
        compiler_params=_cp(56),
        name="da_proj_lat" if lat else "da_proj_ctx",
    )(*args)


def _rope_tables():
    t = np.arange(DEC_SEQ)
    rows = (t // GRID_W).astype(np.float32)
    cols = (t % GRID_W).astype(np.float32)
    half = DA_DIM // 4
    freqs = (np.float32(ROPE_BASE) ** (-np.arange(half, dtype=np.float32) / np.float32(half))).astype(np.float32)
    ang_r = rows[:, None] * freqs
    ang_c = cols[:, None] * freqs
    cos64 = np.concatenate([np.cos(ang_r)] * 2 + [np.cos(ang_c)] * 2, axis=1)
    sin64 = np.concatenate([-np.sin(ang_r), np.sin(ang_r), -np.sin(ang_c), np.sin(ang_c)], axis=1)
    reps = LANES // DA_DIM
    return (jnp.asarray(np.concatenate([cos64] * reps, axis=1), F32),
            jnp.asarray(np.concatenate([sin64] * reps, axis=1), F32))


ATTN_Q_TILE = 256


def _attn_kernel(lq1_ref, lk1_ref, lq2_ref, lk2_ref, gs_ref, q_ref, *refs, lam_init, seg_lens):
    n_seg = len(seg_lens)
    kv_refs, o_ref = refs[:2 * n_seg], refs[2 * n_seg]
    lam =(jnp.exp(jnp.sum(lq1_ref[...] * lk1_ref[...], axis=-1, keepdims=True))
           - jnp.exp(jnp.sum(lq2_ref[...] * lk2_ref[...], axis=-1, keepdims=True)) + lam_init)
    lane = lax.broadcasted_iota(jnp.int32, (1, DA_VDIM), 1)
    first = lane < DA_DIM
    gs = gs_ref[...]
    tq = q_ref.shape[0]

    def softmax_pv(qq, ks, vs):
        ss = [_dot_nt(qq, kk) for kk in ks]
        m = functools.reduce(jnp.maximum, [jnp.max(s, axis=-1, keepdims=True) for s in ss])
        ps = [jnp.exp2(s - m) for s in ss]
        l = functools.reduce(jnp.add, [jnp.sum(p, axis=-1, keepdims=True) for p in ps])
        o = functools.reduce(jnp.add, [_dot(p.astype(BF16), vv) for p, vv in zip(ps, vs)])
        return o, l

    for h in range(DA_HEADS):
        sl = slice(DA_VDIM * h, DA_VDIM * (h + 1))
        qh = q_ref[:, sl]
        ks = [kv_refs[2 * s][:, sl].astype(BF16) for s in range(n_seg)]
        vs = [kv_refs[2 * s + 1][:, sl].astype(BF16) for s in range(n_seg)]
        zero = jnp.zeros_like(qh)
        qq = jnp.concatenate([jnp.where(first, qh, zero), jnp.where(first, zero, qh)], axis=0)
        oo, ll = softmax_pv(qq, ks, vs)
        o = oo[:tq] * (1.0 / ll[:tq]) - oo[tq:] * (lam / ll[tq:])
        o_ref[:, sl] = (_rmsnorm(o, gs) * (1.0 - lam_init)).astype(BF16)


def _attention(lam_vecs, g_sub, q, kv_segs, n_batch, t_q, q_tile, lam_init, name):
    nq = t_q // q_tile
    vec_spec = pl.BlockSpec((1, DA_DIM), lambda b, i: (0, 0))
    kv_specs, kv_args = [], []
    for k, v, t_k in kv_segs:
        kv_specs += [pl.BlockSpec((t_k, D_MODEL), lambda b, i: (b, 0))] * 2
        kv_args += [k, v]
    seg_lens = tuple(t_k for _, _, t_k in kv_segs)
    return pl.pallas_call(
        functools.partial(_attn_kernel, lam_init=lam_init, seg_lens=seg_lens),
        grid=(n_batch, nq),
        in_specs=[vec_spec] * 4 + [
            pl.BlockSpec((1, DA_VDIM), lambda b, i: (0, 0)),
            pl.BlockSpec((q_tile, D_MODEL), lambda b, i: (b * nq + i, 0)),
        ] + kv_specs,
        out_specs=pl.BlockSpec((q_tile, D_MODEL), lambda b, i: (b * nq + i, 0)),
        out_shape=jax.ShapeDtypeStruct((n_batch * t_q, D_MODEL), BF16),
        compiler_params=_cp(56),
        name=name,
    )(*lam_vecs, g_sub, q, *kv_args)


def _outproj_kernel(u_ref, w_ref, x_ref, gate_ref, g_ref, sh_ref, sc_ref, wr_ref, x1_ref, hs_ref, lt_ref, wb_ref):
    _cast_once(w_ref, wb_ref)
    x1 = x_ref[...] + gate_ref[...] * _dot(u_ref[...], wb_ref[...])
    x1_ref[...] = x1
    h2 = _modulate(x1, g_ref[...], sh_ref[...], sc_ref[...])
    _to_slabs(hs_ref, h2, ROW_TILE)
    lt_ref[...] = _dot_nt(wr_ref[...], h2.astype(BF16))


def _outproj(u, w_out, x, mods, norm_g, wr_t, lat, name):
    return pl.pallas_call(
        _outproj_kernel,
        grid=(N_ROW_TILES,),
        in_specs=[
            _row_spec(D_MODEL),
            _weight_spec(D_MODEL, D_MODEL),
            _row_spec(D_MODEL),
            _mod_spec(2, lat),
            _full_spec((1, D_MODEL)),
            _mod_spec(3, lat),
            _mod_spec(4, lat),
            _full_spec((LANES, D_MODEL)),
        ],
        out_specs=[
            _row_spec(D_MODEL),
            pl.BlockSpec((ROW_TILE * TOK_SLABS, LANES), lambda i: (i, 0)),
            pl.BlockSpec((LANES, ROW_TILE), lambda i: (0, i)),
        ],
        out_shape=[
            jax.ShapeDtypeStruct((N_TOK, D_MODEL), F32),
            jax.ShapeDtypeStruct((N_TOK * TOK_SLABS, LANES), F32),
            jax.ShapeDtypeStruct((LANES, N_TOK), F32),
        ],
        scratch_shapes=[pltpu.VMEM((D_MODEL, D_MODEL), BF16)],
        compiler_params=_cp(48),
        name=name,
    )(u, w_out, x, mods, norm_g, mods, mods, wr_t)


N_TOK_BLOCKS = N_TOK // LANES
STACK_ROWS = N_TOK_BLOCKS * N_EXPERTS


def _router_kernel(lt_ref, tri_ref, blk_ref, idx_ref, gate_ref, cwin_ref, aff_ref, cend_ref, ctot_ref):
    lt = lt_ref[...]
    e = jnp.exp(lt - jnp.max(lt, axis=0, keepdims=True))
    aff = e / jnp.sum(e, axis=0, keepdims=True)
    aff = jnp.where(aff >= F32_MIN_NORMAL, aff, 0.0)

    def count_ge(x):
        return jnp.sum((aff >= x).astype(F32), axis=1, keepdims=True)

    def step(mid_of):
        def body(_, c):
            lo, hi = c
            mid = jnp.minimum(jnp.maximum(mid_of(lo, hi), lo), hi)
            ok = count_ge(mid) >= float(CAPACITY)
            return jnp.where(ok, mid, lo), jnp.where(ok, hi, mid)
        return body

    lo = jnp.full((N_EXPERTS, 1), F32_MIN_NORMAL, F32)
    hi = jnp.full((N_EXPERTS, 1), 2.0, F32)
    lo, hi = lax.fori_loop(0, 8, step(lambda a, b: jnp.sqrt(a * b)), (lo, hi))
    lo, hi = lax.fori_loop(0, 28, step(lambda a, b: a + (b - a) * 0.5), (lo, hi))
    kth = jnp.max(jnp.where(aff < hi, aff, 0.0), axis=1, keepdims=True)
    thr = jnp.where(count_ge(lo) >= float(CAPACITY), kth, 0.0)

    tri = tri_ref[...]
    blk = blk_ref[...]

    def stack(x):
        return jnp.concatenate([x[:, LANES * b:LANES * (b + 1)] for b in range(N_TOK_BLOCKS)], axis=0)

    def unstack(x):
        return jnp.concatenate([x[N_EXPERTS * b:N_EXPERTS * (b + 1), :] for b in range(N_TOK_BLOCKS)], axis=1)

    def cumsum_stacked(mask_st):
        win = _dot(mask_st.astype(BF16), tri)
        tot = win[:, LANES - 1:LANES]
        off = _dot(blk, jnp.broadcast_to(tot, (STACK_ROWS, LANES)).astype(BF16))[:, :1]
        return win, off, tot

    gt = aff > thr
    eq = aff == thr
    need = float(CAPACITY) - jnp.sum(gt.astype(F32), axis=1, keepdims=True)
    eq_f = eq.astype(F32)
    ewin, eoff, _ = cumsum_stacked(stack(eq_f))
    eq_before = unstack(ewin + eoff) - eq_f
    sel = jnp.logical_or(gt, jnp.logical_and(eq, eq_before < need))
    cwin, coff, ctot = cumsum_stacked(stack(sel.astype(F32)))
    cwin_ref[...] = cwin
    aff_ref[...] = stack(aff)
    cend_ref[...] = jnp.broadcast_to(coff + ctot, (STACK_ROWS, LANES))
    ctot_ref[...] = jnp.broadcast_to(ctot, (STACK_ROWS, LANES))

    blk_id = lax.broadcasted_iota(jnp.int32, (N_TOK_BLOCKS, 1), 0).astype(F32)
    slot = lax.broadcasted_iota(jnp.int32, (1, CAPACITY), 1).astype(F32)
    lane_f = lax.broadcasted_iota(jnp.int32, (LANES, 1), 0).astype(F32)

    def per_expert(ex):
        rows = pl.ds(ex, N_TOK_BLOCKS, stride=N_EXPERTS)
        cend = cend_ref[rows, :][:, :1]
        ctot_e = ctot_ref[rows, :][:, :1]
        before = cend <= slot
        nblk = jnp.sum(before.astype(F32), axis=0, keepdims=True)
        base = jnp.sum(jnp.where(before, ctot_e, 0.0), axis=0, keepdims=True)
        pick = jnp.where(blk_id == nblk, 1.0, 0.0).astype(BF16)
        cnt_in = _dot_tn(cwin_ref[rows, :].astype(BF16), pick)
        lane_idx = jnp.sum((cnt_in <= slot - base).astype(F32), axis=0, keepdims=True)
        idx_ref[pl.ds(ex, 1), :] = (nblk * float(LANES) + lane_idx).astype(jnp.int32)
        a_hi, a_mid, a_lo = _split3(aff_ref[rows, :])
        aff_blk = _dot_tn(a_hi, pick) + _dot_tn(a_mid, pick) + _dot_tn(a_lo, pick)
        gate_ref[pl.ds(ex, 1), :] = jnp.sum(jnp.where(lane_f == lane_idx, aff_blk, 0.0), axis=0, keepdims=True)

    def expert_pair(i, carry):
        per_expert(2 * i)
        per_expert(2 * i + 1)
        return carry

    lax.fori_loop(0, N_EXPERTS // 2, expert_pair, 0)


def _cumsum_consts():
    l = np.arange(LANES)
    tri = (l[:, None] <= l[None, :]).astype(np.float32)
    r = np.arange(STACK_ROWS)
    same = (r[:, None] % N_EXPERTS) == (r[None, :] % N_EXPERTS)
    earlier = (r[None, :] // N_EXPERTS) < (r[:, None] // N_EXPERTS)
    blk = (same & earlier).astype(np.float32)
    return jnp.asarray(tri, BF16), jnp.asarray(blk, BF16)


def _router(logits_t, tri, blk, name):
    return pl.pallas_call(
        _router_kernel,
        grid=(1,),
        in_specs=[
            pl.BlockSpec((N_EXPERTS, N_TOK), lambda i: (0, 0)),
            _full_spec((LANES, LANES)),
            _full_spec((STACK_ROWS, STACK_ROWS)),
        ],
        out_specs=[_full_spec((N_EXPERTS, CAPACITY))] * 2,
        out_shape=[jax.ShapeDtypeStruct((N_EXPERTS, CAPACITY), jnp.int32),
                   jax.ShapeDtypeStruct((N_EXPERTS, CAPACITY), F32)],
        scratch_shapes=[pltpu.VMEM((STACK_ROWS, LANES), F32)] * 4,
        compiler_params=_cp(48),
        name=name,
    )(logits_t, tri, blk)


GATHER_UNROLL = 8


def _gather_kernel(idx_ref, hs_ref, xe_ref, tile_ref):
    ex = pl.program_id(0)

    def body(i, carry):
        for u in range(GATHER_UNROLL):
            s = i * GATHER_UNROLL + u
            t = idx_ref[ex * CAPACITY + s]
            slab = hs_ref[pl.ds(pl.multiple_of(t * TOK_SLABS, TOK_SLABS), TOK_SLABS), :]
            tile_ref[pl.ds(s, TOK_SLABS, stride=XE_STRIDE), :] = slab
        return carry

    lax.fori_loop(0, CAPACITY // GATHER_UNROLL, body, 0)
    xe_ref[...] = jnp.concatenate(
        [tile_ref[pl.ds(c * XE_STRIDE, CAPACITY), :] for c in range(TOK_SLABS)], axis=1).astype(BF16)


def _gather(idx_flat, hs, name):
    return pl.pallas_call(
        _gather_kernel,
        grid_spec=pltpu.PrefetchScalarGridSpec(
            num_scalar_prefetch=1,
            grid=(N_EXPERTS,),
            in_specs=[pl.BlockSpec((N_TOK * TOK_SLABS, LANES), lambda e, idx: (0, 0), pipeline_mode=pl.Buffered(1))],
            out_specs=pl.BlockSpec((None, CAPACITY, D_MODEL), lambda e, idx: (e, 0, 0)),
            scratch_shapes=[pltpu.VMEM((TOK_SLABS * XE_STRIDE, LANES), F32)],
        ),
        out_shape=jax.ShapeDtypeStruct((N_EXPERTS, CAPACITY, D_MODEL), BF16),
        compiler_params=_cp(40),
        name=name,
    )(idx_flat, hs)


def _ffn_kernel(xc_ref, xl_ref, gc_ref, gl_ref, w1_ref, w3_ref, w2_ref, yc_ref, yl_ref, xe_ref, acc_ref):
    f = pl.program_id(1)

    @pl.when(f == 0)
    def _():
        xe_ref[:CAPACITY, :] = xc_ref[...]
        xe_ref[CAPACITY:, :] = xl_ref[...]
        acc_ref[...] = jnp.zeros_like(acc_ref)

    x = xe_ref[...]
    h1 = _dot(x, w1_ref[...].astype(BF16))
    h3 = _dot(x, w3_ref[...].astype(BF16))
    hid = (h1 * jax.nn.sigmoid(h1) * h3).astype(BF16)
    acc_ref[...] += _dot(hid, w2_ref[...].astype(BF16))

    @pl.when(f == pl.num_programs(1) - 1)
    def _():
        _to_slabs(yc_ref, acc_ref[:CAPACITY, :] * gc_ref[...], CAPACITY)
        _to_slabs(yl_ref, acc_ref[CAPACITY:, :] * gl_ref[...], CAPACITY)


def _expert_ffn(xe_ctx, xe_lat, gate_ctx, gate_lat, w1, w3, w2, layer, name):
    nf = D_EXPERT // FF_TILE
    x_spec = pl.BlockSpec((None, CAPACITY, D_MODEL), lambda e, f: (e, 0, 0))
    g_spec = pl.BlockSpec((None, CAPACITY, 1), lambda e, f: (e, 0, 0))
    y_spec = pl.BlockSpec((None, CAPACITY * TOK_SLABS, LANES), lambda e, f: (e, 0, 0))
    y_shape = jax.ShapeDtypeStruct((N_EXPERTS, CAPACITY * TOK_SLABS, LANES), F32)
    return pl.pallas_call(
        _ffn_kernel,
        grid=(N_EXPERTS, nf),
        in_specs=[
            x_spec, x_spec, g_spec, g_spec,
            pl.BlockSpec((None, None, D_MODEL, FF_TILE), lambda e, f: (layer, e, 0, f)),
            pl.BlockSpec((None, None, D_MODEL, FF_TILE), lambda e, f: (layer, e, 0, f)),
            pl.BlockSpec((None, None, FF_TILE, D_MODEL), lambda e, f: (layer, e, f, 0)),
        ],
        out_specs=[y_spec, y_spec],
        out_shape=[y_shape, y_shape],
        scratch_shapes=[
            pltpu.VMEM((2 * CAPACITY, D_MODEL), BF16),
            pltpu.VMEM((2 * CAPACITY, D_MODEL), F32),
        ],
        compiler_params=_cp(56),
        name=name,
    )(xe_ctx, xe_lat, gate_ctx, gate_lat, w1, w3, w2)


SCATTER_UNROLL = 8


def _scatter_expert(idx_ref, ye_ref, y_ref, ex):
    def body(i, carry):
        upd = []
        for u in range(SCATTER_UNROLL):
            s = i * SCATTER_UNROLL + u
            t = idx_ref[ex * CAPACITY + s]
            rows = pl.ds(pl.multiple_of(t * TOK_SLABS, TOK_SLABS), TOK_SLABS)
            src = ye_ref[pl.ds(pl.multiple_of(s * TOK_SLABS, TOK_SLABS), TOK_SLABS), :]
            upd.append((rows, y_ref[rows, :] + src))
        for rows, val in upd:
            y_ref[rows, :] = val
        return carry

    lax.fori_loop(0, CAPACITY // SCATTER_UNROLL, body, 0)


def _combine_kernel(idx_ref, ye_ref, y_ref):
    ex = pl.program_id(0)

    @pl.when(ex == 0)
    def _():
        y_ref[...] = jnp.zeros_like(y_ref)

    _scatter_expert(idx_ref, ye_ref, y_ref, ex)


def _combine_final_kernel(idx_ref, ye_ref, xp_ref, gp_ref, fg_ref, o_ref, y_ref):
    i = pl.program_id(0)

    @pl.when(i == 0)
    def _():
        y_ref[...] = jnp.zeros_like(y_ref)

    @pl.when(i < N_EXPERTS)
    def _():
        _scatter_expert(idx_ref, ye_ref, y_ref, i)

    @pl.when(i >= N_EXPERTS)
    def _():
        slabs_per_tile = ROW_TILE * TOK_SLABS
        base = pl.multiple_of((i - N_EXPERTS) * slabs_per_tile, slabs_per_tile)
        x = xp_ref[...] + gp_ref[...] * _from_slabs(y_ref, ROW_TILE, base)
        o_ref[...] = _rmsnorm(x, fg_ref[...])


def _combine_final(idx_flat, ye, x_prev, mods_prev, final_g, lat, name):
    tiles_per_batch = DEC_SEQ // ROW_TILE

    def tile(i):
        return jnp.maximum(i - N_EXPERTS, 0)

    def gate_row(i, idx):
        return ((1 + tile(i) // tiles_per_batch) * 6 + 5 if lat else 5, 0, 0)

    return pl.pallas_call(
        _combine_final_kernel,
        grid_spec=pltpu.PrefetchScalarGridSpec(
            num_scalar_prefetch=1,
            grid=(N_EXPERTS + N_ROW_TILES,),
            in_specs=[
                pl.BlockSpec((None, CAPACITY * TOK_SLABS, LANES), lambda i, idx: (jnp.minimum(i, N_EXPERTS - 1), 0, 0)),
                pl.BlockSpec((ROW_TILE, D_MODEL), lambda i, idx: (tile(i), 0)),
                pl.BlockSpec((None, 1, D_MODEL), gate_row),
                pl.BlockSpec((1, D_MODEL), lambda i, idx: (0, 0)),
            ],
            out_specs=pl.BlockSpec((ROW_TILE, D_MODEL), lambda i, idx: (tile(i), 0)),
            scratch_shapes=[pltpu.VMEM((N_TOK * TOK_SLABS, LANES), F32)],
        ),
        out_shape=jax.ShapeDtypeStruct((N_TOK, D_MODEL), F32),
        compiler_params=_cp(40),
        name=name,
    )(idx_flat, ye, x_prev, mods_prev, final_g)


def _combine(idx_flat, ye, name):
    return pl.pallas_call(
        _combine_kernel,
        grid_spec=pltpu.PrefetchScalarGridSpec(
            num_scalar_prefetch=1,
            grid=(N_EXPERTS,),
            in_specs=[pl.BlockSpec((None, CAPACITY * TOK_SLABS, LANES), lambda e, idx: (e, 0, 0))],
            out_specs=pl.BlockSpec((N_TOK * TOK_SLABS, LANES), lambda e, idx: (0, 0)),
        ),
        out_shape=jax.ShapeDtypeStruct((N_TOK * TOK_SLABS, LANES), F32),
        compiler_params=_cp(56),
        name=name,
    )(idx_flat, ye)


def _moe(hs_ctx, lt_ctx, hs_lat, lt_lat, tri, blk, w1, w3, w2, layer):
    tag = "l%d" % layer
    routed = []
    for hs, lt, name in ((hs_ctx, lt_ctx, "ctx"), (hs_lat, lt_lat, "lat")):
        idx, gate = _router(lt, tri, blk, "router_%s_%s" % (tag, name))
        idx_flat = idx.reshape(N_EXPERTS * CAPACITY)
        xe = _gather(idx_flat, hs, "gather_%s_%s" % (tag, name))
        routed.append((idx_flat, xe, gate.reshape(N_EXPERTS, CAPACITY, 1)))
    (idx_c, xe_c, g_c), (idx_l, xe_l, g_l) = routed
    ye_c, ye_l = _expert_ffn(xe_c, xe_l, g_c, g_l, w1, w3, w2, layer, "ffn_" + tag)
    return (idx_c, ye_c), (idx_l, ye_l)


def _gla_proj_kernel(xp_ref, yt_ref, gp_ref, g_ref, sh_ref, sc_ref, w_ref, wt_ref, wg_ref, bg_ref,
                     x_ref, q_ref, k_ref, v_ref, r_ref, lgf_ref, lgb_ref, wb_ref):
    _cast_once(w_ref, wb_ref)
    x = xp_ref[...] + gp_ref[...] * _from_slabs(yt_ref, ROW_TILE)
    x_ref[...] = x
    h = _modulate(x, g_ref[...], sh_ref[...], sc_ref[...]).astype(BF16)
    z = _dot(h, wb_ref[...])
    nq = GLA_HEADS * GLA_DK
    nv = GLA_HEADS * GLA_DV
    q_ref[...] = z[:, :nq] * (GLA_DK ** -0.5)
    k_ref[...] = z[:, nq:2 * nq]
    v_ref[...] = z[:, 2 * nq:2 * nq + nv].astype(BF16)
    r_ref[...] = z[:, 2 * nq + nv:].astype(BF16)
    zg = _dot(h, wt_ref[...]).astype(BF16)
    a = _dot(zg, wg_ref[...]) + bg_ref[...]
    ls = (jnp.minimum(a, 0.0) - jnp.log1p(jnp.exp(-jnp.abs(a)))) * (1.0 / GLA_GATE_NORM)
    lgf_ref[...] = ls[:, :nq]
    lgb_ref[...] = ls[:, nq:]


def _gla_proj(x_prev, y_slabs, mods_prev, mods, norm_g, w_in, w_tail, wg, bg, lat, name):
    nq = GLA_HEADS * GLA_DK
    nv = GLA_HEADS * GLA_DV
    n_main = 2 * nq + 2 * nv
    return pl.pallas_call(
        _gla_proj_kernel,
        grid=(N_ROW_TILES,),
        in_specs=[
            _row_spec(D_MODEL),
            pl.BlockSpec((ROW_TILE * TOK_SLABS, LANES), lambda i: (i, 0)),
            _mod_spec(5, lat),
            _full_spec((1, D_MODEL)),
            _mod_spec(0, lat),
            _mod_spec(1, lat),
            _weight_spec(D_MODEL, n_main),
            _full_spec((D_MODEL, LANES)),
            _full_spec((LANES, 2 * nq)),
            _full_spec((1, 2 * nq)),
        ],
        out_specs=[_row_spec(D_MODEL), _row_spec(nq), _row_spec(nq), _row_spec(nv), _row_spec(nv),
                   _row_spec(nq), _row_spec(nq)],
        out_shape=[
            jax.ShapeDtypeStruct((N_TOK, D_MODEL), F32),
            jax.ShapeDtypeStruct((N_TOK, nq), F32),
            jax.ShapeDtypeStruct((N_TOK, nq), F32),
            jax.ShapeDtypeStruct((N_TOK, nv), BF16),
            jax.ShapeDtypeStruct((N_TOK, nv), BF16),
            jax.ShapeDtypeStruct((N_TOK, nq), F32),
            jax.ShapeDtypeStruct((N_TOK, nq), F32),
        ],
        scratch_shapes=[pltpu.VMEM((D_MODEL, n_main), BF16)],
        compiler_params=_cp(56),
        name=name,
    )(x_prev, y_slabs, mods_prev, norm_g, mods, mods, w_in, w_tail, wg, bg)


GLA_LEVELS = (32, 16, 8)
GLA_UNROLL = 8
GLA_GROUP = 4 * GLA_CHUNK
GLA_SAFE_DECAY = 60.0
NEG_BIG = -1e30


def _gla_intra_fast(qe, kc, b2, fwd):
    c = GLA_CHUNK
    row = lax.broadcasted_iota(jnp.int32, (c, 1), 0)
    col = lax.broadcasted_iota(jnp.int32, (1, c), 1)
    keep = (col <= row) if fwd else (col >= row)
    kq = (kc * jnp.exp2(-b2)).astype(BF16)
    return jnp.where(keep, _dot_nt(qe, kq), 0.0)


def _gla_intra_robust(qc, kc, b2, fwd):
    c = GLA_CHUNK
    row = lax.broadcasted_iota(jnp.int32, (c, 1), 0)
    col = lax.broadcasted_iota(jnp.int32, (1, c), 1)
    a = jnp.zeros((c, c), F32)
    for g in GLA_LEVELS:
        odd = ((row >> int(math.log2(g))) & 1) == 1
        later = odd if fwd else jnp.logical_not(odd)
        refs = []
        for p in range(c // (2 * g)):
            r0 = 2 * g * p + (g - 1 if fwd else g)
            refs.append(jnp.broadcast_to(b2[r0:r0 + 1], (2 * g, GLA_DK)))
        ref = jnp.concatenate(refs, axis=0) if len(refs) > 1 else refs[0]
        qt = qc * jnp.exp2(jnp.where(later, b2 - ref, NEG_BIG))
        kt = kc * jnp.exp2(jnp.where(later, NEG_BIG, ref - b2))
        same_parent = (row >> int(math.log2(2 * g))) == (col >> int(math.log2(2 * g)))
        a = a + jnp.where(same_parent, _dot_nt(qt.astype(BF16), kt.astype(BF16)), 0.0)

    sub = lax.broadcasted_iota(jnp.int32, (SUBLANES, 1), 0)
    strips = []
    for blk in range(c // SUBLANES):
        r0 = SUBLANES * blk
        qb = qc[r0:r0 + SUBLANES]
        bb = b2[r0:r0 + SUBLANES]
        strip = jnp.zeros((SUBLANES, c), F32)
        for jj in range(SUBLANES):
            j = r0 + jj
            cond = (sub >= jj) if fwd else (sub <= jj)
            t = qb * kc[j:j + 1] * jnp.exp2(jnp.where(cond, bb - b2[j:j + 1], NEG_BIG))
            strip = jnp.where(col == j, jnp.sum(t, axis=1, keepdims=True), strip)
        strips.append(strip)
    return a + jnp.concatenate(strips, axis=0)


def _gla_state_increment(kc, vc, b2f, b2b):
    c = GLA_CHUNK
    btf = b2f[c - 1:c]
    btb = b2b[0:1]
    kd = jnp.concatenate([(kc * jnp.exp2(btf - b2f)).astype(BF16), (kc * jnp.exp2(btb - b2b)).astype(BF16)], axis=1)
    return _dot_tn(vc, kd), jnp.concatenate([jnp.exp2(btf), jnp.exp2(btb)], axis=1)


def _gla_scores(qc, kc, b2f, b2b, fast):
    qe_f = (qc * jnp.exp2(b2f)).astype(BF16)
    qe_b = (qc * jnp.exp2(b2b)).astype(BF16)
    if fast:
        a = _gla_intra_fast(qe_f, kc, b2f, True) + _gla_intra_fast(qe_b, kc, b2b, False)
    else:
        a = _gla_intra_robust(qc, kc, b2f, True) + _gla_intra_robust(qc, kc, b2b, False)
    return jnp.concatenate([qe_f, qe_b], axis=1), a.astype(BF16)


def _gla_kernel(q_ref, k_ref, v_ref, r_ref, gf_ref, gb_ref, s0f_ref, s0b_ref, go_ref, trif_ref, trib_ref,
                u_ref, sf_ref, sb_ref, *scratch, t_len, heads):
    c = GLA_CHUNK
    n = t_len // c
    per_head = len(scratch) // heads
    hd = []
    for hh in range(heads):
        dk = slice(hh * GLA_DK, (hh + 1) * GLA_DK)
        dv = slice(hh * GLA_DV, (hh + 1) * GLA_DV)
        b2f, b2b, qe, oi, kv, dec, sst, st = scratch[hh * per_head:(hh + 1) * per_head]
        hd.append(dict(q=q_ref.at[:, dk], k=k_ref.at[:, dk], v=v_ref.at[:, dv], r=r_ref.at[:, dv],
                       gf=gf_ref.at[:, dk], gb=gb_ref.at[:, dk], s0f=s0f_ref.at[hh], s0b=s0b_ref.at[hh],
                       u=u_ref.at[:, dv], sf=sf_ref.at[hh], sb=sb_ref.at[hh],
                       b2f=b2f, b2b=b2b, qe=qe, oi=oi, kv=kv, dec=dec, sst=sst, st=st))

    def rows_of(ci):
        return pl.ds(pl.multiple_of(ci * c, c), c)

    def state_rows_of(ci):
        return pl.ds(pl.multiple_of(ci * GLA_DV, GLA_DV), GLA_DV)

    n_groups = t_len // GLA_GROUP
    cum_unroll = 2 if n_groups % 2 == 0 else 1

    def cum_body(i, carry):
        sums = []
        for h in hd:
            for u in range(cum_unroll):
                rows = pl.ds(pl.multiple_of((i * cum_unroll + u) * GLA_GROUP, GLA_GROUP), GLA_GROUP)
                for g_ref, tri_ref, b_ref in ((h["gf"], trif_ref, h["b2f"]), (h["gb"], trib_ref, h["b2b"])):
                    s3 = _dot(tri_ref[...], jnp.concatenate(_split3(g_ref[rows, :]), axis=1))
                    sums.append((b_ref, rows, s3))
        for b_ref, rows, s3 in sums:
            b_ref[rows, :] = (s3[:, :GLA_DK] + s3[:, GLA_DK:2 * GLA_DK] + s3[:, 2 * GLA_DK:]) * LOG2E
        return carry

    lax.fori_loop(0, n_groups // cum_unroll, cum_body, 0)

    def chunk_total(g_ref):
        return jnp.min(jnp.sum(g_ref[...].reshape(n, c, GLA_DK), axis=1))

    totals = [chunk_total(h[g]) for h in hd for g in ("gf", "gb")]
    safe = functools.reduce(jnp.minimum, totals) >= -GLA_SAFE_DECAY

    def make_local(fast, unroll):
        def body(i, carry):
            work = [(h, i * unroll + u) for h in hd for u in range(unroll)]
            for h, ci in work:
                rows = rows_of(ci)
                kv, dec = _gla_state_increment(h["k"][rows, :], h["v"][rows, :], h["b2f"][rows, :], h["b2b"][rows, :])
                h["kv"][state_rows_of(ci), :] = kv
                h["dec"][pl.ds(ci, 1), :] = dec
            scores = []
            for h, ci in work:
                rows = rows_of(ci)
                qe, a = _gla_scores(h["q"][rows, :], h["k"][rows, :], h["b2f"][rows, :], h["b2b"][rows, :], fast)
                h["qe"][rows, :] = qe
                scores.append(a)
            for (h, ci), a in zip(work, scores):
                rows = rows_of(ci)
                h["oi"][rows, :] = _dot(a, h["v"][rows, :])
            return carry
        return body

    unroll = min(GLA_UNROLL, n)

    @pl.when(safe)
    def _():
        lax.fori_loop(0, n // unroll, make_local(True, unroll), 0)

    @pl.when(jnp.logical_not(safe))
    def _():
        lax.fori_loop(0, n, make_local(False, 1), 0)

    fl = slice(0, GLA_DK)
    bl = slice(GLA_DK, 2 * GLA_DK)
    for h in hd:
        h["st"][...] = jnp.concatenate([h["s0f"][...].T, h["s0b"][...].T], axis=1)

    def rec_body(i, carry):
        cf = i
        cb = n - 1 - i
        for h in hd:
            st = h["st"][...]
            h["sst"][state_rows_of(cf), fl] = st[:, fl].astype(BF16)
            h["sst"][state_rows_of(cb), bl] = st[:, bl].astype(BF16)
            dec = jnp.concatenate([h["dec"][pl.ds(cf, 1), :][:, fl], h["dec"][pl.ds(cb, 1), :][:, bl]], axis=1)
            kv = jnp.concatenate([h["kv"][state_rows_of(cf), fl], h["kv"][state_rows_of(cb), bl]], axis=1)
            h["st"][...] = st * dec + kv
        return carry

    lax.fori_loop(0, n, rec_body, 0)
    for h in hd:
        h["sf"][...] = h["st"][:, fl].T
        h["sb"][...] = h["st"][:, bl].T

    go = go_ref[...]

    def out_body(i, carry):
        work = [(h, i * unroll + u) for h in hd for u in range(unroll)]
        inter = [_dot_nt(h["qe"][rows_of(ci), :], h["sst"][state_rows_of(ci), :]) for h, ci in work]
        for (h, ci), o_inter in zip(work, inter):
            rows = rows_of(ci)
            r = h["r"][rows, :].astype(F32)
            h["u"][rows, :] = (_rmsnorm(h["oi"][rows, :] + o_inter, go) * (r * jax.nn.sigmoid(r))).astype(BF16)
        return carry

    lax.fori_loop(0, n // unroll, out_body, 0)


def _gla_tri():
    i = np.arange(GLA_GROUP)
    same = (i[:, None] // GLA_CHUNK) == (i[None, :] // GLA_CHUNK)
    fwd = (same & (i[None, :] <= i[:, None])).astype(np.float32)
    bwd = (same & (i[None, :] >= i[:, None])).astype(np.float32)
    return jnp.asarray(fwd, BF16), jnp.asarray(bwd, BF16)


def _gla(q, k, v, r, lgf, lgb, s0f, s0b, g_out, n_batch, t_len, heads, name):
    trif, trib = _gla_tri()
    n_chunks = t_len // GLA_CHUNK
    qk_spec = pl.BlockSpec((None, t_len, heads * GLA_DK), lambda b, h: (b, 0, h))
    v_spec = pl.BlockSpec((None, t_len, heads * GLA_DV), lambda b, h: (b, 0, h))
    s_spec = pl.BlockSpec((None, heads, GLA_DK, GLA_DV), lambda b, h: (b, h, 0, 0))
    const = lambda shape: pl.BlockSpec(shape, lambda b, h: (0,) * len(shape))
    s_shape = jax.ShapeDtypeStruct((n_batch, GLA_HEADS, GLA_DK, GLA_DV), F32)
    return pl.pallas_call(
        functools.partial(_gla_kernel, t_len=t_len, heads=heads),
        grid=(n_batch, GLA_HEADS // heads),
        in_specs=[qk_spec, qk_spec, v_spec, v_spec, qk_spec, qk_spec, s_spec, s_spec,
                  const((1, GLA_DV)), const((GLA_GROUP, GLA_GROUP)), const((GLA_GROUP, GLA_GROUP))],
        out_specs=[v_spec, s_spec, s_spec],
        out_shape=[jax.ShapeDtypeStruct((n_batch, t_len, GLA_HEADS * GLA_DV), BF16), s_shape, s_shape],
        scratch_shapes=[
            pltpu.VMEM((t_len, GLA_DK), F32), pltpu.VMEM((t_len, GLA_DK), F32),
            pltpu.VMEM((t_len, 2 * GLA_DK), BF16),
            pltpu.VMEM((t_len, GLA_DV), F32),
            pltpu.VMEM((n_chunks * GLA_DV, 2 * GLA_DK), F32),
            pltpu.VMEM((max(n_chunks, SUBLANES), 2 * GLA_DK), F32),
            pltpu.VMEM((n_chunks * GLA_DV, 2 * GLA_DK), BF16),
            pltpu.VMEM((GLA_DV, 2 * GLA_DK), F32),
        ] * heads,
        compiler_params=_cp(48),
        name=name,
    )(q, k, v, r, lgf, lgb, s0f, s0b, g_out, trif, trib)


def kernel(x_prompt, x_sample, cache_k, cache_v, state_fwd, state_bwd, c, c_ctx, w_mod, b_mod, norm_g,
           da_w_in, da_w_out, da_lam_q1, da_lam_k1, da_lam_q2, da_lam_k2, da_g_sub, gla_w_in, gla_w_gf2,
           gla_b_gf, gla_w_gb2, gla_b_gb, gla_g_out, gla_w_out, moe_w_router, moe_w1, moe_w3, moe_w2, final_g):
    n_ctx_b = x_prompt.shape[0]
    n_lat_b = x_sample.shape[0]
    x_ctx = x_prompt.reshape(N_TOK, D_MODEL)
    x_lat = x_sample.reshape(N_TOK, D_MODEL)

    cvec = jnp.zeros((SUBLANES, D_MODEL), F32).at[0].set(c_ctx).at[1:1 + n_lat_b].set(c)
    mods = _mod_params(cvec, w_mod, b_mod)
    tri, blk = _cumsum_consts()
    wr_t = [jnp.zeros((LANES, D_MODEL), BF16).at[:N_EXPERTS].set(moe_w_router[i].T.astype(BF16))
            for i in range(DEPTH)]
    ng = norm_g.reshape(DEPTH, 2, 1, D_MODEL)

    lam_init = 0.8 - 0.6 * math.exp(-0.3 * 0)
    w_in = da_w_in
    w_out = da_w_out
    lam_vecs = [v[0].reshape(1, DA_DIM) for v in (da_lam_q1, da_lam_k1, da_lam_q2, da_lam_k2)]
    g_sub = da_g_sub[0].reshape(1, DA_VDIM)

    qc, kc, vc, kf, vf = _da_proj(x_ctx, mods[0], ng[0, 0], w_in, lat=False)
    ql, kl, vl = _da_proj(x_lat, mods[0], ng[0, 0], w_in, lat=True, rope_tabs=_rope_tables())
    past = cache_k.shape[2]
    ck = cache_k[:, 0].reshape(n_lat_b * past, D_MODEL)
    cv = cache_v[:, 0].reshape(n_lat_b * past, D_MODEL)
    u_ctx = _attention(lam_vecs, g_sub, qc, [(kc, vc, SEQ)], n_ctx_b, SEQ, ATTN_Q_TILE, lam_init, "attn_ctx")
    u_lat = _attention(lam_vecs, g_sub, ql, [(kl, vl, DEC_SEQ), (ck, cv, past)], n_lat_b, DEC_SEQ, ATTN_Q_TILE,
                       lam_init, "attn_lat")

    x1_ctx, hs_ctx, lt_ctx = _outproj(u_ctx, w_out, x_ctx, mods[0], ng[0, 1], wr_t[0], False, "outproj0_ctx")
    x1_lat, hs_lat, lt_lat = _outproj(u_lat, w_out, x_lat, mods[0], ng[0, 1], wr_t[0], True, "outproj0_lat")
    (idx_c, ye_c), (idx_l, ye_l) = _moe(hs_ctx, lt_ctx, hs_lat, lt_lat, tri, blk, moe_w1, moe_w3, moe_w2, 0)
    y_ctx = _combine(idx_c, ye_c, "combine_l0_ctx")
    y_lat = _combine(idx_l, ye_l, "combine_l0_lat")

    nq = GLA_HEADS * GLA_DK
    n_main = 2 * nq + 2 * GLA_HEADS * GLA_DV
    w_tail = jnp.zeros((D_MODEL, LANES), F32).at[:, :2 * GLA_GATE_RANK].set(gla_w_in[0][:, n_main:]).astype(BF16)
    wg = jnp.zeros((LANES, 2 * nq), F32)
    wg = wg.at[:GLA_GATE_RANK, :nq].set(gla_w_gf2[0]).at[GLA_GATE_RANK:2 * GLA_GATE_RANK, nq:].set(gla_w_gb2[0])
    wg = wg.astype(BF16)
    bg = jnp.concatenate([gla_b_gf[0], gla_b_gb[0]]).reshape(1, 2 * nq)
    w_out1 = gla_w_out
    g_out = gla_g_out[0].reshape(1, GLA_DV)

    def gla_side(x1, y, lat, n_b, t_len, s0f, s0b, tag):
        x2, q, k, v, r, lgf, lgb = _gla_proj(x1, y, mods[0], mods[1], ng[1, 0], gla_w_in, w_tail, wg, bg, lat,
                                             "gla_proj_" + tag)
        sh3 = lambda a: a.reshape(n_b, t_len, a.shape[-1])
        heads = GLA_HEADS if t_len <= GLA_GROUP else 1
        u, sf, sb = _gla(sh3(q), sh3(k), sh3(v), sh3(r), sh3(lgf), sh3(lgb), s0f, s0b, g_out, n_b, t_len, heads,
                         "gla_" + tag)
        return x2, u.reshape(N_TOK, D_MODEL), sf, sb

    zeros_s = jnp.zeros((n_ctx_b, GLA_HEADS, GLA_DK, GLA_DV), F32)
    x2_ctx, ug_ctx, sf, sb = gla_side(x1_ctx, y_ctx, False, n_ctx_b, SEQ, zeros_s, zeros_s, "ctx")
    x2_lat, ug_lat, _, _ = gla_side(x1_lat, y_lat, True, n_lat_b, DEC_SEQ, state_fwd[:, 0], state_bwd[:, 0], "lat")

    x3_ctx, hs_ctx, lt_ctx = _outproj(ug_ctx, w_out1, x2_ctx, mods[1], ng[1, 1], wr_t[1], False, "outproj1_ctx")
    x3_lat, hs_lat, lt_lat = _outproj(ug_lat, w_out1, x2_lat, mods[1], ng[1, 1], wr_t[1], True, "outproj1_lat")
    (idx_c, ye_c), (idx_l, ye_l) = _moe(hs_ctx, lt_ctx, hs_lat, lt_lat, tri, blk, moe_w1, moe_w3, moe_w2, 1)

    fg = final_g.reshape(1, D_MODEL)
    y_prompt = _combine_final(idx_c, ye_c, x3_ctx, mods[1], fg, False, "final_ctx").reshape(x_prompt.shape)
    y_sample = _combine_final(idx_l, ye_l, x3_lat, mods[1], fg, True, "final_lat").reshape(x_sample.shape)
    new_k = kf.reshape(n_ctx_b, 1, SEQ, 2 * DA_HEADS, DA_DIM)
    new_v = vf.reshape(n_ctx_b, 1, SEQ, DA_HEADS, DA_VDIM)
    return (y_prompt, y_sample, new_k, new_v, sf[:, None], sb[:, None])
```

```python
import functools
import math

import numpy as np
import jax
import jax.numpy as jnp
from jax import lax
from jax.experimental import pallas as pl
from jax.experimental.pallas import tpu as pltpu

F32 = jnp.float32
BF16 = jnp.bfloat16

D_MODEL = 1024
DEPTH = 2
SEQ = 256
DEC_SEQ = 2048
GRID_W = 64
N_TOK = 4096
DA_HEADS = 8
DA_DIM = 64
DA_VDIM = 128
ROPE_BASE = 10000.0
GLA_HEADS = 4
GLA_DK = 128
GLA_DV = 256
GLA_GATE_RANK = 16
GLA_GATE_NORM = 16.0
GLA_CHUNK = 64
N_EXPERTS = 16
CAPACITY = 512
D_EXPERT = 2048
EPS = 1e-6
F32_MIN_NORMAL = 2.0 ** -126
LOG2E = 1.4426950408889634

LANES = 128
SUBLANES = 8
ROW_TILE = 512
FF_TILE = 512
TOK_SLABS = D_MODEL // LANES
XE_STRIDE = CAPACITY + SUBLANES
MIB = 1024 * 1024


def _cp(vmem_mib, sem=None):
    return pltpu.CompilerParams(vmem_limit_bytes=vmem_mib * MIB, dimension_semantics=sem)


def _dot(a, b):
    return jnp.dot(a, b, preferred_element_type=F32)


def _dot_nt(a, b):
    return lax.dot_general(a, b, (((1,), (1,)), ((), ())), preferred_element_type=F32)


def _dot_tn(a, b):
    return lax.dot_general(a, b, (((0,), (0,)), ((), ())), preferred_element_type=F32)


def _rmsnorm(x, g):
    return x * lax.rsqrt(jnp.mean(x * x, axis=-1, keepdims=True) + EPS) * g


def _modulate(x, g, shift, scale):
    return _rmsnorm(x, g) * (1.0 + scale) + shift


def _split3(x):
    hi = x.astype(BF16)
    r = x - hi.astype(F32)
    mid = r.astype(BF16)
    lo = (r - mid.astype(F32)).astype(BF16)
    return hi, mid, lo


def _from_slabs(ref, rows, base=0):
    return jnp.concatenate([ref[pl.ds(base + c, rows, stride=TOK_SLABS), :] for c in range(TOK_SLABS)], axis=1)


def _to_slabs(ref, val, rows):
    for c in range(TOK_SLABS):
        ref[pl.ds(c, rows, stride=TOK_SLABS), :] = val[:, LANES * c:LANES * (c + 1)]


def _mod_kernel(c_ref, w_ref, b_ref, o_ref):
    c = c_ref[...]
    s = c * jax.nn.sigmoid(c)
    w = w_ref[...]
    s_hi = s.astype(BF16)
    s_lo = (s - s_hi.astype(F32)).astype(BF16)
    w_hi = w.astype(BF16)
    w_lo = (w - w_hi.astype(F32)).astype(BF16)
    o_ref[...] = _dot(s_hi, w_hi) + _dot(s_hi, w_lo) + _dot(s_lo, w_hi) + b_ref[...]


def _mod_params(cvec, w_mod, b_mod):
    n6 = 6 * D_MODEL
    out = pl.pallas_call(
        _mod_kernel,
        grid=(DEPTH, 6),
        in_specs=[
            pl.BlockSpec((SUBLANES, D_MODEL), lambda i, j: (0, 0)),
            pl.BlockSpec((None, D_MODEL, D_MODEL), lambda i, j: (i, 0, j)),
            pl.BlockSpec((None, 1, D_MODEL), lambda i, j: (i, 0, j)),
        ],
        out_specs=pl.BlockSpec((None, SUBLANES, D_MODEL), lambda i, j: (i, 0, j)),
        out_shape=jax.ShapeDtypeStruct((DEPTH, SUBLANES, n6), F32),
        compiler_params=_cp(32),
        name="mod_params",
    )(cvec, w_mod, b_mod.reshape(DEPTH, 1, n6))
    return out.reshape(DEPTH, SUBLANES * 6, 1, D_MODEL)


def _mod_spec(k, lat):
    tiles_per_batch = DEC_SEQ // ROW_TILE
    if lat:
        return pl.BlockSpec((None, 1, D_MODEL), lambda i: ((1 + i // tiles_per_batch) * 6 + k, 0, 0))
    return pl.BlockSpec((None, 1, D_MODEL), lambda i: (k, 0, 0))


def _row_spec(width, dtype_rows=ROW_TILE):
    return pl.BlockSpec((dtype_rows, width), lambda i: (i, 0))


def _full_spec(shape):
    nd = len(shape)
    return pl.BlockSpec(shape, lambda i: (0,) * nd)


N_ROW_TILES = N_TOK // ROW_TILE


def _weight_spec(rows, cols):
    return pl.BlockSpec((None, rows, cols), lambda i: (0, 0, 0), pipeline_mode=pl.Buffered(1))


def _cast_once(w_ref, wb_ref):
    @pl.when(pl.program_id(0) == 0)
    def _():
        wb_ref[...] = w_ref[...].astype(BF16)


def _da_proj_kernel(*refs, rope, emit_f32):
    x_ref, g_ref, sh_ref, sc_ref, w_ref = refs[:5]
    pos = 5
    if rope:
        cos_ref, sin_ref = refs[pos:pos + 2]
        pos += 2
    q_ref, k_ref, v_ref = refs[pos:pos + 3]
    pos += 3
    wb_ref = refs[-1]
    _cast_once(w_ref, wb_ref)
    h = _modulate(x_ref[...], g_ref[...], sh_ref[...], sc_ref[...])
    z = _dot(h.astype(BF16), wb_ref[...])
    q = z[:, :D_MODEL]
    k = z[:, D_MODEL:2 * D_MODEL]
    v = z[:, 2 * D_MODEL:]
    if emit_f32:
        kf_ref, vf_ref = refs[pos:pos + 2]
        kf_ref[...] = k
        vf_ref[...] = v
    if rope:
        reps = D_MODEL // LANES
        cos = jnp.concatenate([cos_ref[...]] * reps, axis=1)
        sin = jnp.concatenate([sin_ref[...]] * reps, axis=1)
        lane = lax.broadcasted_iota(jnp.int32, (1, D_MODEL), 1)
        first = (lane & 16) == 0

        def rot(t):
            partner = jnp.where(first, pltpu.roll(t, D_MODEL - 16, 1), pltpu.roll(t, 16, 1))
            return t * cos + partner * sin

        q = rot(q)
        k = rot(k)
    q_ref[...] = (q * (DA_DIM ** -0.5 * LOG2E)).astype(BF16)
    k_ref[...] = k.astype(BF16)
    v_ref[...] = v.astype(BF16)


def _da_proj(x, mods, norm_g, w_in, lat, rope_tabs=None):
    rope = rope_tabs is not None
    emit_f32 = not lat
    in_specs = [
        _row_spec(D_MODEL),
        _full_spec((1, D_MODEL)),
        _mod_spec(0, lat),
        _mod_spec(1, lat),
        _weight_spec(D_MODEL, 3 * D_MODEL),
    ]
    args = [x, norm_g, mods, mods, w_in]
    if rope:
        tiles_per_batch = DEC_SEQ // ROW_TILE
        tab_spec = pl.BlockSpec((ROW_TILE, LANES), lambda i: (i % tiles_per_batch, 0))
        in_specs += [tab_spec, tab_spec]
        args += list(rope_tabs)
    out_specs = [_row_spec(D_MODEL)] * 3
    out_shape = [jax.ShapeDtypeStruct((N_TOK, D_MODEL), BF16)] * 3
    if emit_f32:
        out_specs += [_row_spec(D_MODEL)] * 2
        out_shape += [jax.ShapeDtypeStruct((N_TOK, D_MODEL), F32)] * 2
    return pl.pallas_call(
        functools.partial(_da_proj_kernel, rope=rope, emit_f32=emit_f32),
        grid=(N_ROW_TILES,),
        in_specs=in_specs,
        out_specs=out_specs,
        out_shape=out_shape,
        scratch_shapes=[pltpu.VMEM((D_MODEL, 3 * D_MODEL), BF16)],
        compiler_params=_cp(56),
        name="da_proj_lat" if lat else "da_proj_ctx",
    )(*args)


def _rope_tables():
    t = np.arange(DEC_SEQ)
    rows = (t // GRID_W).astype(np.float32)
    cols = (t % GRID_W).astype(np.float32)
    half = DA_DIM // 4
    freqs = (np.float32(ROPE_BASE) ** (-np.arange(half, dtype=np.float32) / np.float32(half))).astype(np.float32)
    ang_r = rows[:, None] * freqs
    ang_c = cols[:, None] * freqs
    cos64 = np.concatenate([np.cos(ang_r)] * 2 + [np.cos(ang_c)] * 2, axis=1)
    sin64 = np.concatenate([-np.sin(ang_r), np.sin(ang_r), -np.sin(ang_c), np.sin(ang_c)], axis=1)
    reps = LANES // DA_DIM
    return (jnp.asarray(np.concatenate([cos64] * reps, axis=1), F32),
            jnp.asarray(np.concatenate([sin64] * reps, axis=1), F32))


ATTN_Q_TILE = 256


def _attn_kernel(lq1_ref, lk1_ref, lq2_ref, lk2_ref, gs_ref, q_ref, *refs, lam_init, seg_lens):
    n_seg = len(seg_lens)
    kv_refs, o_ref = refs[:2 * n_seg], refs[2 * n_seg]
    lam =(jnp.exp(jnp.sum(lq1_ref[...] * lk1_ref[...], axis=-1, keepdims=True))
           - jnp.exp(jnp.sum(lq2_ref[...] * lk2_ref[...], axis=-1, keepdims=True)) + lam_init)
    lane = lax.broadcasted_iota(jnp.int32, (1, DA_VDIM), 1)
    first = lane < DA_DIM
    gs = gs_ref[...]
    tq = q_ref.shape[0]

    def softmax_pv(qq, ks, vs):
        ss = [_dot_nt(qq, kk) for kk in ks]
        m = functools.reduce(jnp.maximum, [jnp.max(s, axis=-1, keepdims=True) for s in ss])
        ps = [jnp.exp2(s - m) for s in ss]
        l = functools.reduce(jnp.add, [jnp.sum(p, axis=-1, keepdims=True) for p in ps])
        o = functools.reduce(jnp.add, [_dot(p.astype(BF16), vv) for p, vv in zip(ps, vs)])
        return o, l

    for h in range(DA_HEADS):
        sl = slice(DA_VDIM * h, DA_VDIM * (h + 1))
        qh = q_ref[:, sl]
        ks = [kv_refs[2 * s][:, sl].astype(BF16) for s in range(n_seg)]
        vs = [(r[:, h, :] if len(r.shape) == 3 else r[:, sl]).astype(BF16) for r in kv_refs[1::2]]
        zero = jnp.zeros_like(qh)
        qq = jnp.concatenate([jnp.where(first, qh, zero), jnp.where(first, zero, qh)], axis=0)
        oo, ll = softmax_pv(qq, ks, vs)
        o = oo[:tq] * (1.0 / ll[:tq]) - oo[tq:] * (lam / ll[tq:])
        o_ref[:, sl] = (_rmsnorm(o, gs) * (1.0 - lam_init)).astype(BF16)


def _attention(lam_vecs, g_sub, q, kv_segs, n_batch, t_q, q_tile, lam_init, name):
    nq = t_q // q_tile
    vec_spec = pl.BlockSpec((1, DA_DIM), lambda b, i: (0, 0))
    flat_spec = lambda t_k: pl.BlockSpec((t_k, D_MODEL), lambda b, i: (b, 0))
    kv_specs, kv_args = [], []
    for k, v, t_k in kv_segs:
        v_spec = flat_spec(t_k) if v.ndim == 2 else pl.BlockSpec(
            (None, None, t_k, DA_HEADS, DA_VDIM), lambda b, i: (b, 0, 0, 0, 0))
        kv_specs += [flat_spec(t_k), v_spec]
        kv_args += [k, v]
    seg_lens = tuple(t_k for _, _, t_k in kv_segs)
    return pl.pallas_call(
        functools.partial(_attn_kernel, lam_init=lam_init, seg_lens=seg_lens),
        grid=(n_batch, nq),
        in_specs=[vec_spec] * 4 + [
            pl.BlockSpec((1, DA_VDIM), lambda b, i: (0, 0)),
            pl.BlockSpec((q_tile, D_MODEL), lambda b, i: (b * nq + i, 0)),
        ] + kv_specs,
        out_specs=pl.BlockSpec((q_tile, D_MODEL), lambda b, i: (b * nq + i, 0)),
        out_shape=jax.ShapeDtypeStruct((n_batch * t_q, D_MODEL), BF16),
        compiler_params=_cp(56),
        name=name,
    )(*lam_vecs, g_sub, q, *kv_args)


def _outproj_kernel(u_ref, w_ref, x_ref, gate_ref, g_ref, sh_ref, sc_ref, wr_ref, x1_ref, hs_ref, lt_ref, wb_ref):
    _cast_once(w_ref, wb_ref)
    x1 = x_ref[...] + gate_ref[...] * _dot(u_ref[...], wb_ref[...])
    x1_ref[...] = x1
    h2 = _modulate(x1, g_ref[...], sh_ref[...], sc_ref[...])
    _to_slabs(hs_ref, h2, ROW_TILE)
    lt_ref[...] = _dot_nt(wr_ref[...], h2.astype(BF16))


def _outproj(u, w_out, x, mods, norm_g, wr_t, lat, name):
    return pl.pallas_call(
        _outproj_kernel,
        grid=(N_ROW_TILES,),
        in_specs=[
            _row_spec(D_MODEL),
            _weight_spec(D_MODEL, D_MODEL),
            _row_spec(D_MODEL),
            _mod_spec(2, lat),
            _full_spec((1, D_MODEL)),
            _mod_spec(3, lat),
            _mod_spec(4, lat),
            _full_spec((LANES, D_MODEL)),
        ],
        out_specs=[
            _row_spec(D_MODEL),
            pl.BlockSpec((ROW_TILE * TOK_SLABS, LANES), lambda i: (i, 0)),
            pl.BlockSpec((LANES, ROW_TILE), lambda i: (0, i)),
        ],
        out_shape=[
            jax.ShapeDtypeStruct((N_TOK, D_MODEL), F32),
            jax.ShapeDtypeStruct((N_TOK * TOK_SLABS, LANES), F32),
            jax.ShapeDtypeStruct((LANES, N_TOK), F32),
        ],
        scratch_shapes=[pltpu.VMEM((D_MODEL, D_MODEL), BF16)],
        compiler_params=_cp(48),
        name=name,
    )(u, w_out, x, mods, norm_g, mods, mods, wr_t)


N_TOK_BLOCKS = N_TOK // LANES
STACK_ROWS = N_TOK_BLOCKS * N_EXPERTS


def _router_kernel(lt_ref, tri_ref, blk_ref, idx_ref, gate_ref, cwin_ref, aff_ref, cend_ref, ctot_ref):
    lt = lt_ref[...]
    e = jnp.exp(lt - jnp.max(lt, axis=0, keepdims=True))
    aff = e / jnp.sum(e, axis=0, keepdims=True)
    aff = jnp.where(aff >= F32_MIN_NORMAL, aff, 0.0)

    def count_ge(x):
        return jnp.sum((aff >= x).astype(F32), axis=1, keepdims=True)

    def step(mid_of):
        def body(_, c):
            lo, hi = c
            mid = jnp.minimum(jnp.maximum(mid_of(lo, hi), lo), hi)
            ok = count_ge(mid) >= float(CAPACITY)
            return jnp.where(ok, mid, lo), jnp.where(ok, hi, mid)
        return body

    lo = jnp.full((N_EXPERTS, 1), F32_MIN_NORMAL, F32)
    hi = jnp.full((N_EXPERTS, 1), 2.0, F32)
    lo, hi = lax.fori_loop(0, 8, step(lambda a, b: jnp.sqrt(a * b)), (lo, hi))
    lo, hi = lax.fori_loop(0, 28, step(lambda a, b: a + (b - a) * 0.5), (lo, hi))
    kth = jnp.max(jnp.where(aff < hi, aff, 0.0), axis=1, keepdims=True)
    thr = jnp.where(count_ge(lo) >= float(CAPACITY), kth, 0.0)

    tri = tri_ref[...]
    blk = blk_ref[...]

    def stack(x):
        return jnp.concatenate([x[:, LANES * b:LANES * (b + 1)] for b in range(N_TOK_BLOCKS)], axis=0)

    def unstack(x):
        return jnp.concatenate([x[N_EXPERTS * b:N_EXPERTS * (b + 1), :] for b in range(N_TOK_BLOCKS)], axis=1)

    def cumsum_stacked(mask_st):
        win = _dot(mask_st.astype(BF16), tri)
        tot = win[:, LANES - 1:LANES]
        off = _dot(blk, jnp.broadcast_to(tot, (STACK_ROWS, LANES)).astype(BF16))[:, :1]
        return win, off, tot

    gt = aff > thr
    eq = aff == thr
    need = float(CAPACITY) - jnp.sum(gt.astype(F32), axis=1, keepdims=True)
    eq_f = eq.astype(F32)
    ewin, eoff, _ = cumsum_stacked(stack(eq_f))
    eq_before = unstack(ewin + eoff) - eq_f
    sel = jnp.logical_or(gt, jnp.logical_and(eq, eq_before < need))
    cwin, coff, ctot = cumsum_stacked(stack(sel.astype(F32)))
    cwin_ref[...] = cwin
    aff_ref[...] = stack(aff)
    cend_ref[...] = jnp.broadcast_to(coff + ctot, (STACK_ROWS, LANES))
    ctot_ref[...] = jnp.broadcast_to(ctot, (STACK_ROWS, LANES))

    blk_id = lax.broadcasted_iota(jnp.int32, (N_TOK_BLOCKS, 1), 0).astype(F32)
    slot = lax.broadcasted_iota(jnp.int32, (1, CAPACITY), 1).astype(F32)
    lane_f = lax.broadcasted_iota(jnp.int32, (LANES, 1), 0).astype(F32)

    def per_expert(ex):
        rows = pl.ds(ex, N_TOK_BLOCKS, stride=N_EXPERTS)
        cend = cend_ref[rows, :][:, :1]
        ctot_e = ctot_ref[rows, :][:, :1]
        before = cend <= slot
        nblk = jnp.sum(before.astype(F32), axis=0, keepdims=True)
        base = jnp.sum(jnp.where(before, ctot_e, 0.0), axis=0, keepdims=True)
        pick = jnp.where(blk_id == nblk, 1.0, 0.0).astype(BF16)
        cnt_in = _dot_tn(cwin_ref[rows, :].astype(BF16), pick)
        lane_idx = jnp.sum((cnt_in <= slot - base).astype(F32), axis=0, keepdims=True)
        idx_ref[pl.ds(ex, 1), :] = ((nblk * float(LANES) + lane_idx) * float(TOK_SLABS)).astype(jnp.int32)
        a_hi, a_mid, a_lo = _split3(aff_ref[rows, :])
        aff_blk = _dot_tn(a_hi, pick) + _dot_tn(a_mid, pick) + _dot_tn(a_lo, pick)
        gate_ref[pl.ds(ex, 1), :] = jnp.sum(jnp.where(lane_f == lane_idx, aff_blk, 0.0), axis=0, keepdims=True)

    def expert_pair(i, carry):
        per_expert(2 * i)
        per_expert(2 * i + 1)
        return carry

    lax.fori_loop(0, N_EXPERTS // 2, expert_pair, 0)


def _cumsum_consts():
    l = np.arange(LANES)
    tri = (l[:, None] <= l[None, :]).astype(np.float32)
    r = np.arange(STACK_ROWS)
    same = (r[:, None] % N_EXPERTS) == (r[None, :] % N_EXPERTS)
    earlier = (r[None, :] // N_EXPERTS) < (r[:, None] // N_EXPERTS)
    blk = (same & earlier).astype(np.float32)
    return jnp.asarray(tri, BF16), jnp.asarray(blk, BF16)


def _router(logits_t, tri, blk, name):
    return pl.pallas_call(
        _router_kernel,
        grid=(1,),
        in_specs=[
            pl.BlockSpec((N_EXPERTS, N_TOK), lambda i: (0, 0)),
            _full_spec((LANES, LANES)),
            _full_spec((STACK_ROWS, STACK_ROWS)),
        ],
        out_specs=[_full_spec((N_EXPERTS, CAPACITY))] * 2,
        out_shape=[jax.ShapeDtypeStruct((N_EXPERTS, CAPACITY), jnp.int32),
                   jax.ShapeDtypeStruct((N_EXPERTS, CAPACITY), F32)],
        scratch_shapes=[pltpu.VMEM((STACK_ROWS, LANES), F32)] * 4,
        compiler_params=_cp(48),
        name=name,
    )(logits_t, tri, blk)


GATHER_UNROLL = 16


def _gather_kernel(off_ref, hs_ref, xe_ref, tile_ref):
    ex = pl.program_id(0)

    def body(i, carry):
        for u in range(GATHER_UNROLL):
            s = i * GATHER_UNROLL + u
            off = pl.multiple_of(off_ref[ex * CAPACITY + s], TOK_SLABS)
            tile_ref[pl.ds(s, TOK_SLABS, stride=XE_STRIDE), :] = hs_ref[pl.ds(off, TOK_SLABS), :]
        return carry

    lax.fori_loop(0, CAPACITY // GATHER_UNROLL, body, 0)
    xe_ref[...] = jnp.concatenate(
        [tile_ref[pl.ds(c * XE_STRIDE, CAPACITY), :] for c in range(TOK_SLABS)], axis=1).astype(BF16)


def _gather(idx_flat, hs, name):
    return pl.pallas_call(
        _gather_kernel,
        grid_spec=pltpu.PrefetchScalarGridSpec(
            num_scalar_prefetch=1,
            grid=(N_EXPERTS,),
            in_specs=[pl.BlockSpec((N_TOK * TOK_SLABS, LANES), lambda e, idx: (0, 0), pipeline_mode=pl.Buffered(1))],
            out_specs=pl.BlockSpec((None, CAPACITY, D_MODEL), lambda e, idx: (e, 0, 0)),
            scratch_shapes=[pltpu.VMEM((TOK_SLABS * XE_STRIDE, LANES), F32)],
        ),
        out_shape=jax.ShapeDtypeStruct((N_EXPERTS, CAPACITY, D_MODEL), BF16),
        compiler_params=_cp(40),
        name=name,
    )(idx_flat, hs)


def _ffn_kernel(xc_ref, xl_ref, gc_ref, gl_ref, w1_ref, w3_ref, w2_ref, yc_ref, yl_ref, xe_ref, acc_ref):
    f = pl.program_id(1)

    @pl.when(f == 0)
    def _():
        xe_ref[:CAPACITY, :] = xc_ref[...]
        xe_ref[CAPACITY:, :] = xl_ref[...]
        acc_ref[...] = jnp.zeros_like(acc_ref)

    x = xe_ref[...]
    h1 = _dot(x, w1_ref[...].astype(BF16))
    h3 = _dot(x, w3_ref[...].astype(BF16))
    hid = (h1 * jax.nn.sigmoid(h1) * h3).astype(BF16)
    acc_ref[...] += _dot(hid, w2_ref[...].astype(BF16))

    @pl.when(f == pl.num_programs(1) - 1)
    def _():
        _to_slabs(yc_ref, acc_ref[:CAPACITY, :] * gc_ref[...], CAPACITY)
        _to_slabs(yl_ref, acc_ref[CAPACITY:, :] * gl_ref[...], CAPACITY)


def _expert_ffn(xe_ctx, xe_lat, gate_ctx, gate_lat, w1, w3, w2, layer, name):
    nf = D_EXPERT // FF_TILE
    x_spec = pl.BlockSpec((None, CAPACITY, D_MODEL), lambda e, f: (e, 0, 0))
    g_spec = pl.BlockSpec((None, CAPACITY, 1), lambda e, f: (e, 0, 0))
    y_spec = pl.BlockSpec((None, CAPACITY * TOK_SLABS, LANES), lambda e, f: (e, 0, 0))
    y_shape = jax.ShapeDtypeStruct((N_EXPERTS, CAPACITY * TOK_SLABS, LANES), F32)
    return pl.pallas_call(
        _ffn_kernel,
        grid=(N_EXPERTS, nf),
        in_specs=[
            x_spec, x_spec, g_spec, g_spec,
            pl.BlockSpec((None, None, D_MODEL, FF_TILE), lambda e, f: (layer, e, 0, f)),
            pl.BlockSpec((None, None, D_MODEL, FF_TILE), lambda e, f: (layer, e, 0, f)),
            pl.BlockSpec((None, None, FF_TILE, D_MODEL), lambda e, f: (layer, e, f, 0)),
        ],
        out_specs=[y_spec, y_spec],
        out_shape=[y_shape, y_shape],
        scratch_shapes=[
            pltpu.VMEM((2 * CAPACITY, D_MODEL), BF16),
            pltpu.VMEM((2 * CAPACITY, D_MODEL), F32),
        ],
        compiler_params=_cp(56),
        name=name,
    )(xe_ctx, xe_lat, gate_ctx, gate_lat, w1, w3, w2)


SCATTER_UNROLL = 16


def _scatter_expert(off_ref, ye_ref, y_ref, ex):
    def body(i, carry):
        upd = []
        for u in range(SCATTER_UNROLL):
            s = i * SCATTER_UNROLL + u
            rows = pl.ds(pl.multiple_of(off_ref[ex * CAPACITY + s], TOK_SLABS), TOK_SLABS)
            src = ye_ref[pl.ds(pl.multiple_of(s * TOK_SLABS, TOK_SLABS), TOK_SLABS), :]
            upd.append((rows, y_ref[rows, :] + src))
        for rows, val in upd:
            y_ref[rows, :] = val
        return carry

    lax.fori_loop(0, CAPACITY // SCATTER_UNROLL, body, 0)


def _combine_kernel(idx_ref, ye_ref, y_ref):
    ex = pl.program_id(0)

    @pl.when(ex == 0)
    def _():
        y_ref[...] = jnp.zeros_like(y_ref)

    _scatter_expert(idx_ref, ye_ref, y_ref, ex)


def _combine_final_kernel(idx_ref, ye_ref, xp_ref, gp_ref, fg_ref, o_ref, y_ref):
    i = pl.program_id(0)

    @pl.when(i == 0)
    def _():
        y_ref[...] = jnp.zeros_like(y_ref)

    @pl.when(i < N_EXPERTS)
    def _():
        _scatter_expert(idx_ref, ye_ref, y_ref, i)

    @pl.when(i >= N_EXPERTS)
    def _():
        slabs_per_tile = ROW_TILE * TOK_SLABS
        base = pl.multiple_of((i - N_EXPERTS) * slabs_per_tile, slabs_per_tile)
        x = xp_ref[...] + gp_ref[...] * _from_slabs(y_ref, ROW_TILE, base)
        o_ref[...] = _rmsnorm(x, fg_ref[...])


def _combine_final(idx_flat, ye, x_prev, mods_prev, final_g, lat, name):
    tiles_per_batch = DEC_SEQ // ROW_TILE

    def tile(i):
        return jnp.maximum(i - N_EXPERTS, 0)

    def gate_row(i, idx):
        return ((1 + tile(i) // tiles_per_batch) * 6 + 5 if lat else 5, 0, 0)

    return pl.pallas_call(
        _combine_final_kernel,
        grid_spec=pltpu.PrefetchScalarGridSpec(
            num_scalar_prefetch=1,
            grid=(N_EXPERTS + N_ROW_TILES,),
            in_specs=[
                pl.BlockSpec((None, CAPACITY * TOK_SLABS, LANES), lambda i, idx: (jnp.minimum(i, N_EXPERTS - 1), 0, 0)),
                pl.BlockSpec((ROW_TILE, D_MODEL), lambda i, idx: (tile(i), 0)),
                pl.BlockSpec((None, 1, D_MODEL), gate_row),
                pl.BlockSpec((1, D_MODEL), lambda i, idx: (0, 0)),
            ],
            out_specs=pl.BlockSpec((ROW_TILE, D_MODEL), lambda i, idx: (tile(i), 0)),
            scratch_shapes=[pltpu.VMEM((N_TOK * TOK_SLABS, LANES), F32)],
        ),
        out_shape=jax.ShapeDtypeStruct((N_TOK, D_MODEL), F32),
        compiler_params=_cp(40),
        name=name,
    )(idx_flat, ye, x_prev, mods_prev, final_g)


def _combine(idx_flat, ye, name):
    return pl.pallas_call(
        _combine_kernel,
        grid_spec=pltpu.PrefetchScalarGridSpec(
            num_scalar_prefetch=1,
            grid=(N_EXPERTS,),
            in_specs=[pl.BlockSpec((None, CAPACITY * TOK_SLABS, LANES), lambda e, idx: (e, 0, 0))],
            out_specs=pl.BlockSpec((N_TOK * TOK_SLABS, LANES), lambda e, idx: (0, 0)),
        ),
        out_shape=jax.ShapeDtypeStruct((N_TOK * TOK_SLABS, LANES), F32),
        compiler_params=_cp(56),
        name=name,
    )(idx_flat, ye)


def _moe(hs_ctx, lt_ctx, hs_lat, lt_lat, tri, blk, w1, w3, w2, layer):
    tag = "l%d" % layer
    routed = []
    for hs, lt, name in ((hs_ctx, lt_ctx, "ctx"), (hs_lat, lt_lat, "lat")):
        idx, gate = _router(lt, tri, blk, "router_%s_%s" % (tag, name))
        idx_flat = idx.reshape(N_EXPERTS * CAPACITY)
        xe = _gather(idx_flat, hs, "gather_%s_%s" % (tag, name))
        routed.append((idx_flat, xe, gate.reshape(N_EXPERTS, CAPACITY, 1)))
    (idx_c, xe_c, g_c), (idx_l, xe_l, g_l) = routed
    ye_c, ye_l = _expert_ffn(xe_c, xe_l, g_c, g_l, w1, w3, w2, layer, "ffn_" + tag)
    return (idx_c, ye_c), (idx_l, ye_l)


def _gla_proj_kernel(xp_ref, yt_ref, gp_ref, g_ref, sh_ref, sc_ref, w_ref, wt_ref, wg_ref, bg_ref,
                     x_ref, q_ref, k_ref, v_ref, r_ref, lgf_ref, lgb_ref, wb_ref):
    _cast_once(w_ref, wb_ref)
    x = xp_ref[...] + gp_ref[...] * _from_slabs(yt_ref, ROW_TILE)
    x_ref[...] = x
    h = _modulate(x, g_ref[...], sh_ref[...], sc_ref[...]).astype(BF16)
    z = _dot(h, wb_ref[...])
    nq = GLA_HEADS * GLA_DK
    nv = GLA_HEADS * GLA_DV
    q_ref[...] = z[:, :nq] * (GLA_DK ** -0.5)
    k_ref[...] = z[:, nq:2 * nq]
    v_ref[...] = z[:, 2 * nq:2 * nq + nv].astype(BF16)
    r_ref[...] = z[:, 2 * nq + nv:].astype(BF16)
    zg = _dot(h, wt_ref[...]).astype(BF16)
    a = _dot(zg, wg_ref[...]) + bg_ref[...]
    ls = (jnp.minimum(a, 0.0) - jnp.log1p(jnp.exp(-jnp.abs(a)))) * (1.0 / GLA_GATE_NORM)
    lgf_ref[...] = ls[:, :nq]
    lgb_ref[...] = ls[:, nq:]


def _gla_proj(x_prev, y_slabs, mods_prev, mods, norm_g, w_in, w_tail, wg, bg, lat, name):
    nq = GLA_HEADS * GLA_DK
    nv = GLA_HEADS * GLA_DV
    n_main = 2 * nq + 2 * nv
    return pl.pallas_call(
        _gla_proj_kernel,
        grid=(N_ROW_TILES,),
        in_specs=[
            _row_spec(D_MODEL),
            pl.BlockSpec((ROW_TILE * TOK_SLABS, LANES), lambda i: (i, 0)),
            _mod_spec(5, lat),
            _full_spec((1, D_MODEL)),
            _mod_spec(0, lat),
            _mod_spec(1, lat),
            _weight_spec(D_MODEL, n_main),
            _full_spec((D_MODEL, LANES)),
            _full_spec((LANES, 2 * nq)),
            _full_spec((1, 2 * nq)),
        ],
        out_specs=[_row_spec(D_MODEL), _row_spec(nq), _row_spec(nq), _row_spec(nv), _row_spec(nv),
                   _row_spec(nq), _row_spec(nq)],
        out_shape=[
            jax.ShapeDtypeStruct((N_TOK, D_MODEL), F32),
            jax.ShapeDtypeStruct((N_TOK, nq), F32),
            jax.ShapeDtypeStruct((N_TOK, nq), F32),
            jax.ShapeDtypeStruct((N_TOK, nv), BF16),
            jax.ShapeDtypeStruct((N_TOK, nv), BF16),
            jax.ShapeDtypeStruct((N_TOK, nq), F32),
            jax.ShapeDtypeStruct((N_TOK, nq), F32),
        ],
        scratch_shapes=[pltpu.VMEM((D_MODEL, n_main), BF16)],
        compiler_params=_cp(56),
        name=name,
    )(x_prev, y_slabs, mods_prev, norm_g, mods, mods, w_in, w_tail, wg, bg)


GLA_LEVELS = (32, 16, 8)
GLA_UNROLL = 8
GLA_GROUP = 4 * GLA_CHUNK
GLA_SAFE_DECAY = 60.0
NEG_BIG = -1e30


def _gla_intra_fast(qe, kc, b2, fwd):
    c = GLA_CHUNK
    row = lax.broadcasted_iota(jnp.int32, (c, 1), 0)
    col = lax.broadcasted_iota(jnp.int32, (1, c), 1)
    keep = (col <= row) if fwd else (col >= row)
    kq = (kc * jnp.exp2(-b2)).astype(BF16)
    return jnp.where(keep, _dot_nt(qe, kq), 0.0)


def _gla_intra_robust(qc, kc, b2, fwd):
    c = GLA_CHUNK
    row = lax.broadcasted_iota(jnp.int32, (c, 1), 0)
    col = lax.broadcasted_iota(jnp.int32, (1, c), 1)
    a = jnp.zeros((c, c), F32)
    for g in GLA_LEVELS:
        odd = ((row >> int(math.log2(g))) & 1) == 1
        later = odd if fwd else jnp.logical_not(odd)
        refs = []
        for p in range(c // (2 * g)):
            r0 = 2 * g * p + (g - 1 if fwd else g)
            refs.append(jnp.broadcast_to(b2[r0:r0 + 1], (2 * g, GLA_DK)))
        ref = jnp.concatenate(refs, axis=0) if len(refs) > 1 else refs[0]
        qt = qc * jnp.exp2(jnp.where(later, b2 - ref, NEG_BIG))
        kt = kc * jnp.exp2(jnp.where(later, NEG_BIG, ref - b2))
        same_parent = (row >> int(math.log2(2 * g))) == (col >> int(math.log2(2 * g)))
        a = a + jnp.where(same_parent, _dot_nt(qt.astype(BF16), kt.astype(BF16)), 0.0)

    sub = lax.broadcasted_iota(jnp.int32, (SUBLANES, 1), 0)
    strips = []
    for blk in range(c // SUBLANES):
        r0 = SUBLANES * blk
        qb = qc[r0:r0 + SUBLANES]
        bb = b2[r0:r0 + SUBLANES]
        strip = jnp.zeros((SUBLANES, c), F32)
        for jj in range(SUBLANES):
            j = r0 + jj
            cond = (sub >= jj) if fwd else (sub <= jj)
            t = qb * kc[j:j + 1] * jnp.exp2(jnp.where(cond, bb - b2[j:j + 1], NEG_BIG))
            strip = jnp.where(col == j, jnp.sum(t, axis=1, keepdims=True), strip)
        strips.append(strip)
    return a + jnp.concatenate(strips, axis=0)


def _gla_state_increment(kc, vc, b2f, b2b):
    c = GLA_CHUNK
    btf = b2f[c - 1:c]
    btb = b2b[0:1]
    kd = jnp.concatenate([(kc * jnp.exp2(btf - b2f)).astype(BF16), (kc * jnp.exp2(btb - b2b)).astype(BF16)], axis=1)
    return _dot_tn(vc, kd), jnp.concatenate([jnp.exp2(btf), jnp.exp2(btb)], axis=1)


def _gla_scores(qc, kc, b2f, b2b, fast):
    qe_f = (qc * jnp.exp2(b2f)).astype(BF16)
    qe_b = (qc * jnp.exp2(b2b)).astype(BF16)
    if fast:
        a = _gla_intra_fast(qe_f, kc, b2f, True) + _gla_intra_fast(qe_b, kc, b2b, False)
    else:
        a = _gla_intra_robust(qc, kc, b2f, True) + _gla_intra_robust(qc, kc, b2b, False)
    return jnp.concatenate([qe_f, qe_b], axis=1), a.astype(BF16)


def _gla_kernel(q_ref, k_ref, v_ref, r_ref, gf_ref, gb_ref, *refs, t_len, heads, zero_state):
    if zero_state:
        s0f_ref = s0b_ref = None
    else:
        s0f_ref, s0b_ref = refs[:2]
        refs = refs[2:]
    go_ref, trif_ref, trib_ref, u_ref, sf_ref, sb_ref = refs[:6]
    scratch = refs[6:]
    _gla_passes(q_ref, k_ref, v_ref, r_ref, gf_ref, gb_ref, s0f_ref, s0b_ref, go_ref, trif_ref, trib_ref,
                u_ref, sf_ref, sb_ref, *scratch, t_len=t_len, heads=heads)


def _gla_passes(q_ref, k_ref, v_ref, r_ref, gf_ref, gb_ref, s0f_ref, s0b_ref, go_ref, trif_ref, trib_ref,
                u_ref, sf_ref, sb_ref, *scratch, t_len, heads):
    c = GLA_CHUNK
    n = t_len // c
    per_head = len(scratch) // heads
    hd = []
    for hh in range(heads):
        dk = slice(hh * GLA_DK, (hh + 1) * GLA_DK)
        dv = slice(hh * GLA_DV, (hh + 1) * GLA_DV)
        b2f, b2b, qe, oi, kv, dec, sst, st = scratch[hh * per_head:(hh + 1) * per_head]
        hd.append(dict(q=q_ref.at[:, dk], k=k_ref.at[:, dk], v=v_ref.at[:, dv], r=r_ref.at[:, dv],
                       gf=gf_ref.at[:, dk], gb=gb_ref.at[:, dk],
                       s0f=None if s0f_ref is None else s0f_ref.at[hh],
                       s0b=None if s0b_ref is None else s0b_ref.at[hh],
                       u=u_ref.at[:, dv], sf=sf_ref.at[hh], sb=sb_ref.at[hh],
                       b2f=b2f, b2b=b2b, qe=qe, oi=oi, kv=kv, dec=dec, sst=sst, st=st))

    def rows_of(ci):
        return pl.ds(pl.multiple_of(ci * c, c), c)

    def state_rows_of(ci):
        return pl.ds(pl.multiple_of(ci * GLA_DV, GLA_DV), GLA_DV)

    n_groups = t_len // GLA_GROUP
    cum_unroll = 2 if n_groups % 2 == 0 else 1

    def cum_body(i, carry):
        sums = []
        for h in hd:
            for u in range(cum_unroll):
                rows = pl.ds(pl.multiple_of((i * cum_unroll + u) * GLA_GROUP, GLA_GROUP), GLA_GROUP)
                for g_ref, tri_ref, b_ref in ((h["gf"], trif_ref, h["b2f"]), (h["gb"], trib_ref, h["b2b"])):
                    s3 = _dot(tri_ref[...], jnp.concatenate(_split3(g_ref[rows, :]), axis=1))
                    sums.append((b_ref, rows, s3))
        for b_ref, rows, s3 in sums:
            b_ref[rows, :] = (s3[:, :GLA_DK] + s3[:, GLA_DK:2 * GLA_DK] + s3[:, 2 * GLA_DK:]) * LOG2E
        return carry

    lax.fori_loop(0, n_groups // cum_unroll, cum_body, 0)

    def chunk_total(g_ref):
        return jnp.min(jnp.sum(g_ref[...].reshape(n, c, GLA_DK), axis=1))

    totals = [chunk_total(h[g]) for h in hd for g in ("gf", "gb")]
    safe = functools.reduce(jnp.minimum, totals) >= -GLA_SAFE_DECAY

    def make_local(fast, unroll):
        def body(i, carry):
            work = [(h, i * unroll + u) for h in hd for u in range(unroll)]
            for h, ci in work:
                rows = rows_of(ci)
                kv, dec = _gla_state_increment(h["k"][rows, :], h["v"][rows, :], h["b2f"][rows, :], h["b2b"][rows, :])
                h["kv"][state_rows_of(ci), :] = kv
                h["dec"][pl.ds(ci, 1), :] = dec
            scores = []
            for h, ci in work:
                rows = rows_of(ci)
                qe, a = _gla_scores(h["q"][rows, :], h["k"][rows, :], h["b2f"][rows, :], h["b2b"][rows, :], fast)
                h["qe"][rows, :] = qe
                scores.append(a)
            for (h, ci), a in zip(work, scores):
                rows = rows_of(ci)
                h["oi"][rows, :] = _dot(a, h["v"][rows, :])
            return carry
        return body

    unroll = min(GLA_UNROLL, n)

    @pl.when(safe)
    def _():
        lax.fori_loop(0, n // unroll, make_local(True, unroll), 0)

    @pl.when(jnp.logical_not(safe))
    def _():
        lax.fori_loop(0, n, make_local(False, 1), 0)

    fl = slice(0, GLA_DK)
    bl = slice(GLA_DK, 2 * GLA_DK)
    for h in hd:
        if h["s0f"] is None:
            h["st"][...] = jnp.zeros_like(h["st"])
        else:
            h["st"][...] = jnp.concatenate([h["s0f"][...].T, h["s0b"][...].T], axis=1)

    def rec_body(i, carry):
        cf = i
        cb = n - 1 - i
        for h in hd:
            st = h["st"][...]
            h["sst"][state_rows_of(cf), fl] = st[:, fl].astype(BF16)
            h["sst"][state_rows_of(cb), bl] = st[:, bl].astype(BF16)
            dec = jnp.concatenate([h["dec"][pl.ds(cf, 1), :][:, fl], h["dec"][pl.ds(cb, 1), :][:, bl]], axis=1)
            kv = jnp.concatenate([h["kv"][state_rows_of(cf), fl], h["kv"][state_rows_of(cb), bl]], axis=1)
            h["st"][...] = st * dec + kv
        return carry

    lax.fori_loop(0, n, rec_body, 0)
    for h in hd:
        h["sf"][...] = h["st"][:, fl].T
        h["sb"][...] = h["st"][:, bl].T

    go = go_ref[...]

    def out_body(i, carry):
        work = [(h, i * unroll + u) for h in hd for u in range(unroll)]
        inter = [_dot_nt(h["qe"][rows_of(ci), :], h["sst"][state_rows_of(ci), :]) for h, ci in work]
        for (h, ci), o_inter in zip(work, inter):
            rows = rows_of(ci)
            r = h["r"][rows, :].astype(F32)
            h["u"][rows, :] = (_rmsnorm(h["oi"][rows, :] + o_inter, go) * (r * jax.nn.sigmoid(r))).astype(BF16)
        return carry

    lax.fori_loop(0, n // unroll, out_body, 0)


def _gla_tri():
    i = np.arange(GLA_GROUP)
    same = (i[:, None] // GLA_CHUNK) == (i[None, :] // GLA_CHUNK)
    fwd = (same & (i[None, :] <= i[:, None])).astype(np.float32)
    bwd = (same & (i[None, :] >= i[:, None])).astype(np.float32)
    return jnp.asarray(fwd, BF16), jnp.asarray(bwd, BF16)


def _gla(q, k, v, r, lgf, lgb, s0f, s0b, g_out, n_batch, t_len, heads, name):
    zero_state = s0f is None
    trif, trib = _gla_tri()
    n_chunks = t_len // GLA_CHUNK
    qk_spec = pl.BlockSpec((None, t_len, heads * GLA_DK), lambda b, h: (b, 0, h))
    v_spec = pl.BlockSpec((None, t_len, heads * GLA_DV), lambda b, h: (b, 0, h))
    s_spec = pl.BlockSpec((None, heads, GLA_DK, GLA_DV), lambda b, h: (b, h, 0, 0))
    const = lambda shape: pl.BlockSpec(shape, lambda b, h: (0,) * len(shape))
    s_shape = jax.ShapeDtypeStruct((n_batch, GLA_HEADS, GLA_DK, GLA_DV), F32)
    states = [] if zero_state else [s0f, s0b]
    return pl.pallas_call(
        functools.partial(_gla_kernel, t_len=t_len, heads=heads, zero_state=zero_state),
        grid=(n_batch, GLA_HEADS // heads),
        in_specs=[qk_spec, qk_spec, v_spec, v_spec, qk_spec, qk_spec] + [s_spec] * len(states)
        + [const((1, GLA_DV)), const((GLA_GROUP, GLA_GROUP)), const((GLA_GROUP, GLA_GROUP))],
        out_specs=[v_spec, s_spec, s_spec],
        out_shape=[jax.ShapeDtypeStruct((n_batch, t_len, GLA_HEADS * GLA_DV), BF16), s_shape, s_shape],
        scratch_shapes=[
            pltpu.VMEM((t_len, GLA_DK), F32), pltpu.VMEM((t_len, GLA_DK), F32),
            pltpu.VMEM((t_len, 2 * GLA_DK), BF16),
            pltpu.VMEM((t_len, GLA_DV), F32),
            pltpu.VMEM((n_chunks * GLA_DV, 2 * GLA_DK), F32),
            pltpu.VMEM((max(n_chunks, SUBLANES), 2 * GLA_DK), F32),
            pltpu.VMEM((n_chunks * GLA_DV, 2 * GLA_DK), BF16),
            pltpu.VMEM((GLA_DV, 2 * GLA_DK), F32),
        ] * heads,
        compiler_params=_cp(48),
        name=name,
    )(q, k, v, r, lgf, lgb, *states, g_out, trif, trib)


def kernel(x_prompt, x_sample, cache_k, cache_v, state_fwd, state_bwd, c, c_ctx, w_mod, b_mod, norm_g,
           da_w_in, da_w_out, da_lam_q1, da_lam_k1, da_lam_q2, da_lam_k2, da_g_sub, gla_w_in, gla_w_gf2,
           gla_b_gf, gla_w_gb2, gla_b_gb, gla_g_out, gla_w_out, moe_w_router, moe_w1, moe_w3, moe_w2, final_g):
    n_ctx_b = x_prompt.shape[0]
    n_lat_b = x_sample.shape[0]
    x_ctx = x_prompt.reshape(N_TOK, D_MODEL)
    x_lat = x_sample.reshape(N_TOK, D_MODEL)

    cvec = jnp.zeros((SUBLANES, D_MODEL), F32).at[0].set(c_ctx).at[1:1 + n_lat_b].set(c)
    mods = _mod_params(cvec, w_mod, b_mod)
    tri, blk = _cumsum_consts()
    wr_t = [jnp.zeros((LANES, D_MODEL), BF16).at[:N_EXPERTS].set(moe_w_router[i].T.astype(BF16))
            for i in range(DEPTH)]
    ng = norm_g.reshape(DEPTH, 2, 1, D_MODEL)

    lam_init = 0.8 - 0.6 * math.exp(-0.3 * 0)
    w_in = da_w_in
    w_out = da_w_out
    lam_vecs = [v[0].reshape(1, DA_DIM) for v in (da_lam_q1, da_lam_k1, da_lam_q2, da_lam_k2)]
    g_sub = da_g_sub[0].reshape(1, DA_VDIM)

    qc, kc, vc, kf, vf = _da_proj(x_ctx, mods[0], ng[0, 0], w_in, lat=False)
    ql, kl, vl = _da_proj(x_lat, mods[0], ng[0, 0], w_in, lat=True, rope_tabs=_rope_tables())
    past = cache_k.shape[2]
    ck = cache_k[:, 0].reshape(n_lat_b * past, D_MODEL)
    cv = cache_v
    u_ctx = _attention(lam_vecs, g_sub, qc, [(kc, vc, SEQ)], n_ctx_b, SEQ, ATTN_Q_TILE, lam_init, "attn_ctx")
    u_lat = _attention(lam_vecs, g_sub, ql, [(kl, vl, DEC_SEQ), (ck, cv, past)], n_lat_b, DEC_SEQ, ATTN_Q_TILE,
                       lam_init, "attn_lat")

    x1_ctx, hs_ctx, lt_ctx = _outproj(u_ctx, w_out, x_ctx, mods[0], ng[0, 1], wr_t[0], False, "outproj0_ctx")
    x1_lat, hs_lat, lt_lat = _outproj(u_lat, w_out, x_lat, mods[0], ng[0, 1], wr_t[0], True, "outproj0_lat")
    (idx_c, ye_c), (idx_l, ye_l) = _moe(hs_ctx, lt_ctx, hs_lat, lt_lat, tri, blk, moe_w1, moe_w3, moe_w2, 0)
    y_ctx = _combine(idx_c, ye_c, "combine_l0_ctx")
    y_lat = _combine(idx_l, ye_l, "combine_l0_lat")

    nq = GLA_HEADS * GLA_DK
    n_main = 2 * nq + 2 * GLA_HEADS * GLA_DV
    w_tail = jnp.zeros((D_MODEL, LANES), F32).at[:, :2 * GLA_GATE_RANK].set(gla_w_in[0][:, n_main:]).astype(BF16)
    wg = jnp.zeros((LANES, 2 * nq), F32)
    wg = wg.at[:GLA_GATE_RANK, :nq].set(gla_w_gf2[0]).at[GLA_GATE_RANK:2 * GLA_GATE_RANK, nq:].set(gla_w_gb2[0])
    wg = wg.astype(BF16)
    bg = jnp.concatenate([gla_b_gf[0], gla_b_gb[0]]).reshape(1, 2 * nq)
    w_out1 = gla_w_out
    g_out = gla_g_out[0].reshape(1, GLA_DV)

    def gla_side(x1, y, lat, n_b, t_len, s0f, s0b, tag):
        x2, q, k, v, r, lgf, lgb = _gla_proj(x1, y, mods[0], mods[1], ng[1, 0], gla_w_in, w_tail, wg, bg, lat,
                                             "gla_proj_" + tag)
        sh3 = lambda a: a.reshape(n_b, t_len, a.shape[-1])
        heads = GLA_HEADS if t_len <= GLA_GROUP else 1
        u, sf, sb = _gla(sh3(q), sh3(k), sh3(v), sh3(r), sh3(lgf), sh3(lgb), s0f, s0b, g_out, n_b, t_len, heads,
                         "gla_" + tag)
        return x2, u.reshape(N_TOK, D_MODEL), sf, sb

    x2_ctx, ug_ctx, sf, sb = gla_side(x1_ctx, y_ctx, False, n_ctx_b, SEQ, None, None, "ctx")
    x2_lat, ug_lat, _, _ = gla_side(x1_lat, y_lat, True, n_lat_b, DEC_SEQ, state_fwd[:, 0], state_bwd[:, 0], "lat")

    x3_ctx, hs_ctx, lt_ctx = _outproj(ug_ctx, w_out1, x2_ctx, mods[1], ng[1, 1], wr_t[1], False, "outproj1_ctx")
    x3_lat, hs_lat, lt_lat = _outproj(ug_lat, w_out1, x2_lat, mods[1], ng[1, 1], wr_t[1], True, "outproj1_lat")
    (idx_c, ye_c), (idx_l, ye_l) = _moe(hs_ctx, lt_ctx, hs_lat, lt_lat, tri, blk, moe_w1, moe_w3, moe_w2, 1)

    fg = final_g.reshape(1, D_MODEL)
    y_prompt = _combine_final(idx_c, ye_c, x3_ctx, mods[1], fg, False, "final_ctx").reshape(x_prompt.shape)
    y_sample = _combine_final(idx_l, ye_l, x3_lat, mods[1], fg, True, "final_lat").reshape(x_sample.shape)
    new_k = kf.reshape(n_ctx_b, 1, SEQ, 2 * DA_HEADS, DA_DIM)
    new_v = vf.reshape(n_ctx_b, 1, SEQ, DA_HEADS, DA_VDIM)
    return (y_prompt, y_sample, new_k, new_v, sf[:, None], sb[:, None])
```

```python
import functools
import math

import numpy as np
import jax
import jax.numpy as jnp
from jax import lax
from jax.experimental import pallas as pl
from jax.experimental.pallas import tpu as pltpu

F32 = jnp.float32
BF16 = jnp.bfloat16

D_MODEL = 1024
DEPTH = 2
SEQ = 256
DEC_SEQ = 2048
GRID_W = 64
N_TOK = 4096
DA_HEADS = 8
DA_DIM = 64
DA_VDIM = 128
ROPE_BASE = 10000.0
GLA_HEADS = 4
GLA_DK = 128
GLA_DV = 256
GLA_GATE_RANK = 16
GLA_GATE_NORM = 16.0
GLA_CHUNK = 64
N_EXPERTS = 16
CAPACITY = 512
D_EXPERT = 2048
EPS = 1e-6
F32_MIN_NORMAL = 2.0 ** -126
LOG2E = 1.4426950408889634

LANES = 128
SUBLANES = 8
ROW_TILE = 512
FF_TILE = 512
TOK_SLABS = D_MODEL // LANES
XE_STRIDE = CAPACITY + SUBLANES
MIB = 1024 * 1024


def _cp(vmem_mib, sem=None):
    return pltpu.CompilerParams(vmem_limit_bytes=vmem_mib * MIB, dimension_semantics=sem)


def _dot(a, b):
    return jnp.dot(a, b, preferred_element_type=F32)


def _dot_nt(a, b):
    return lax.dot_general(a, b, (((1,), (1,)), ((), ())), preferred_element_type=F32)


def _dot_tn(a, b):
    return lax.dot_general(a, b, (((0,), (0,)), ((), ())), preferred_element_type=F32)


def _rmsnorm(x, g):
    return x * lax.rsqrt(jnp.mean(x * x, axis=-1, keepdims=True) + EPS) * g


def _modulate(x, g, shift, scale):
    return _rmsnorm(x, g) * (1.0 + scale) + shift


def _split3(x):
    hi = x.astype(BF16)
    r = x - hi.astype(F32)
    mid = r.astype(BF16)
    lo = (r - mid.astype(F32)).astype(BF16)
    return hi, mid, lo


def _from_slabs(ref, rows, base=0):
    return jnp.concatenate([ref[pl.ds(base + c, rows, stride=TOK_SLABS), :] for c in range(TOK_SLABS)], axis=1)


def _to_slabs(ref, val, rows):
    for c in range(TOK_SLABS):
        ref[pl.ds(c, rows, stride=TOK_SLABS), :] = val[:, LANES * c:LANES * (c + 1)]


def _mod_kernel(c_ref, w_ref, b_ref, o_ref):
    c = c_ref[...]
    s = c * jax.nn.sigmoid(c)
    w = w_ref[...]
    s_hi = s.astype(BF16)
    s_lo = (s - s_hi.astype(F32)).astype(BF16)
    w_hi = w.astype(BF16)
    w_lo = (w - w_hi.astype(F32)).astype(BF16)
    o_ref[...] = _dot(s_hi, w_hi) + _dot(s_hi, w_lo) + _dot(s_lo, w_hi) + b_ref[...]


def _mod_params(cvec, w_mod, b_mod):
    n6 = 6 * D_MODEL
    out = pl.pallas_call(
        _mod_kernel,
        grid=(DEPTH, 6),
        in_specs=[
            pl.BlockSpec((SUBLANES, D_MODEL), lambda i, j: (0, 0)),
            pl.BlockSpec((None, D_MODEL, D_MODEL), lambda i, j: (i, 0, j)),
            pl.BlockSpec((None, 1, D_MODEL), lambda i, j: (i, 0, j)),
        ],
        out_specs=pl.BlockSpec((None, SUBLANES, D_MODEL), lambda i, j: (i, 0, j)),
        out_shape=jax.ShapeDtypeStruct((DEPTH, SUBLANES, n6), F32),
        compiler_params=_cp(32),
        name="mod_params",
    )(cvec, w_mod, b_mod.reshape(DEPTH, 1, n6))
    return out.reshape(DEPTH, SUBLANES * 6, 1, D_MODEL)


N_ROW_TILES = N_TOK // ROW_TILE
N_TILES = 2 * N_ROW_TILES
N_ALL = 2 * N_TOK


def _tile_group(i):
    return jnp.where(i < N_ROW_TILES, 0, 1 + (i - N_ROW_TILES) // (DEC_SEQ // ROW_TILE))


def _mod_spec(k):
    return pl.BlockSpec((None, 1, D_MODEL), lambda i: (_tile_group(i) * 6 + k, 0, 0))


def _row_spec(width):
    return pl.BlockSpec((ROW_TILE, width), lambda i: (i, 0))


def _ctx_row_spec(width):
    return pl.BlockSpec((ROW_TILE, width), lambda i: (jnp.minimum(i, N_ROW_TILES - 1), 0))


def _lat_row_spec(width):
    return pl.BlockSpec((ROW_TILE, width), lambda i: (jnp.maximum(i - N_ROW_TILES, 0), 0))


def _full_spec(shape):
    nd = len(shape)
    return pl.BlockSpec(shape, lambda i: (0,) * nd)


def _per_set(body):
    i = pl.program_id(0)

    @pl.when(i < N_ROW_TILES)
    def _():
        body(False)

    @pl.when(i >= N_ROW_TILES)
    def _():
        body(True)


def _weight_spec(rows, cols):
    return pl.BlockSpec((None, rows, cols), lambda i: (0, 0, 0), pipeline_mode=pl.Buffered(1))


def _cast_once(w_ref, wb_ref):
    @pl.when(pl.program_id(0) == 0)
    def _():
        wb_ref[...] = w_ref[...].astype(BF16)


def _da_proj_kernel(xc_ref, xl_ref, g_ref, sh_ref, sc_ref, w_ref, cos_ref, sin_ref,
                    q_ref, k_ref, v_ref, kf_ref, vf_ref, wb_ref):
    _cast_once(w_ref, wb_ref)

    def body(latent):
        x_ref = xl_ref if latent else xc_ref
        h = _modulate(x_ref[...], g_ref[...], sh_ref[...], sc_ref[...])
        z = _dot(h.astype(BF16), wb_ref[...])
        q = z[:, :D_MODEL]
        k = z[:, D_MODEL:2 * D_MODEL]
        v = z[:, 2 * D_MODEL:]
        if latent:
            reps = D_MODEL // LANES
            cos = jnp.concatenate([cos_ref[...]] * reps, axis=1)
            sin = jnp.concatenate([sin_ref[...]] * reps, axis=1)
            lane = lax.broadcasted_iota(jnp.int32, (1, D_MODEL), 1)
            first = (lane & 16) == 0

            def rot(t):
                partner = jnp.where(first, pltpu.roll(t, D_MODEL - 16, 1), pltpu.roll(t, 16, 1))
                return t * cos + partner * sin

            q = rot(q)
            k = rot(k)
        else:
            kf_ref[...] = k
            vf_ref[...] = v
        q_ref[...] = (q * (DA_DIM ** -0.5 * LOG2E)).astype(BF16)
        k_ref[...] = k.astype(BF16)
        v_ref[...] = v.astype(BF16)

    _per_set(body)


def _da_proj(x_ctx, x_lat, mods, norm_g, w_in, rope_tabs):
    tiles_per_batch = DEC_SEQ // ROW_TILE
    tab_spec = pl.BlockSpec((ROW_TILE, LANES), lambda i: (jnp.maximum(i - N_ROW_TILES, 0) % tiles_per_batch, 0))
    return pl.pallas_call(
        _da_proj_kernel,
        grid=(N_TILES,),
        in_specs=[
            _ctx_row_spec(D_MODEL),
            _lat_row_spec(D_MODEL),
            _full_spec((1, D_MODEL)),
            _mod_spec(0),
            _mod_spec(1),
            _weight_spec(D_MODEL, 3 * D_MODEL),
            tab_spec, tab_spec,
        ],
        out_specs=[_row_spec(D_MODEL)] * 3 + [_ctx_row_spec(D_MODEL)] * 2,
        out_shape=[jax.ShapeDtypeStruct((N_ALL, D_MODEL), BF16)] * 3 + [jax.ShapeDtypeStruct((N_TOK, D_MODEL), F32)] * 2,
        scratch_shapes=[pltpu.VMEM((D_MODEL, 3 * D_MODEL), BF16)],
        compiler_params=_cp(56),
        name="da_proj",
    )(x_ctx, x_lat, norm_g, mods, mods, w_in, *rope_tabs)


def _rope_tables():
    t = np.arange(DEC_SEQ)
    rows = (t // GRID_W).astype(np.float32)
    cols = (t % GRID_W).astype(np.float32)
    half = DA_DIM // 4
    freqs = (np.float32(ROPE_BASE) ** (-np.arange(half, dtype=np.float32) / np.float32(half))).astype(np.float32)
    ang_r = rows[:, None] * freqs
    ang_c = cols[:, None] * freqs
    cos64 = np.concatenate([np.cos(ang_r)] * 2 + [np.cos(ang_c)] * 2, axis=1)
    sin64 = np.concatenate([-np.sin(ang_r), np.sin(ang_r), -np.sin(ang_c), np.sin(ang_c)], axis=1)
    reps = LANES // DA_DIM
    return (jnp.asarray(np.concatenate([cos64] * reps, axis=1), F32),
            jnp.asarray(np.concatenate([sin64] * reps, axis=1), F32))


ATTN_Q_TILE = 256


def _attn_kernel(lq1_ref, lk1_ref, lq2_ref, lk2_ref, gs_ref, q_ref, *refs, lam_init, seg_lens):
    n_seg = len(seg_lens)
    kv_refs, o_ref = refs[:2 * n_seg], refs[2 * n_seg]
    lam =(jnp.exp(jnp.sum(lq1_ref[...] * lk1_ref[...], axis=-1, keepdims=True))
           - jnp.exp(jnp.sum(lq2_ref[...] * lk2_ref[...], axis=-1, keepdims=True)) + lam_init)
    lane = lax.broadcasted_iota(jnp.int32, (1, DA_VDIM), 1)
    first = lane < DA_DIM
    gs = gs_ref[...]
    tq = q_ref.shape[0]

    def softmax_pv(qq, ks, vs):
        ss = [_dot_nt(qq, kk) for kk in ks]
        m = functools.reduce(jnp.maximum, [jnp.max(s, axis=-1, keepdims=True) for s in ss])
        ps = [jnp.exp2(s - m) for s in ss]
        l = functools.reduce(jnp.add, [jnp.sum(p, axis=-1, keepdims=True) for p in ps])
        o = functools.reduce(jnp.add, [_dot(p.astype(BF16), vv) for p, vv in zip(ps, vs)])
        return o, l

    for h in range(DA_HEADS):
        sl = slice(DA_VDIM * h, DA_VDIM * (h + 1))
        qh = q_ref[:, sl]
        ks = [kv_refs[2 * s][:, sl].astype(BF16) for s in range(n_seg)]
        vs = [kv_refs[2 * s + 1][:, sl].astype(BF16) for s in range(n_seg)]
        zero = jnp.zeros_like(qh)
        qq = jnp.concatenate([jnp.where(first, qh, zero), jnp.where(first, zero, qh)], axis=0)
        oo, ll = softmax_pv(qq, ks, vs)
        o = oo[:tq] * (1.0 / ll[:tq]) - oo[tq:] * (lam / ll[tq:])
        o_ref[:, sl] = (_rmsnorm(o, gs) * (1.0 - lam_init)).astype(BF16)


def _attention(lam_vecs, g_sub, q, q_row0, kv_segs, n_batch, t_q, q_tile, lam_init, name):
    nq = t_q // q_tile
    q0 = q_row0 // q_tile
    vec_spec = pl.BlockSpec((1, DA_DIM), lambda b, i: (0, 0))
    kv_specs, kv_args = [], []
    for k, v, row0, t_k in kv_segs:
        kv_specs += [pl.BlockSpec((t_k, D_MODEL), lambda b, i, b0=row0 // t_k: (b0 + b, 0))] * 2
        kv_args += [k, v]
    seg_lens = tuple(t_k for _, _, _, t_k in kv_segs)
    return pl.pallas_call(
        functools.partial(_attn_kernel, lam_init=lam_init, seg_lens=seg_lens),
        grid=(n_batch, nq),
        in_specs=[vec_spec] * 4 + [
            pl.BlockSpec((1, DA_VDIM), lambda b, i: (0, 0)),
            pl.BlockSpec((q_tile, D_MODEL), lambda b, i: (q0 + b * nq + i, 0)),
        ] + kv_specs,
        out_specs=pl.BlockSpec((q_tile, D_MODEL), lambda b, i: (b * nq + i, 0)),
        out_shape=jax.ShapeDtypeStruct((n_batch * t_q, D_MODEL), BF16),
        compiler_params=_cp(56),
        name=name,
    )(*lam_vecs, g_sub, q, *kv_args)


def _outproj_kernel(*refs, x_stacked):
    uc_ref, ul_ref = refs[:2]
    n_x = 1 if x_stacked else 2
    x_refs = refs[2:2 + n_x]
    w_ref, gate_ref, g_ref, sh_ref, sc_ref, wr_ref, x1_ref, hs_ref, lt_ref, wb_ref = refs[2 + n_x:]
    _cast_once(w_ref, wb_ref)

    def body(latent):
        u_ref = ul_ref if latent else uc_ref
        x_ref = x_refs[0] if x_stacked else x_refs[int(latent)]
        x1 = x_ref[...] + gate_ref[...] * _dot(u_ref[...], wb_ref[...])
        x1_ref[...] = x1
        h2 = _modulate(x1, g_ref[...], sh_ref[...], sc_ref[...])
        _to_slabs(hs_ref, h2, ROW_TILE)
        lt_ref[...] = _dot_nt(wr_ref[...], h2.astype(BF16))

    _per_set(body)


def _outproj(u_ctx, u_lat, xs, w_out, mods, norm_g, wr_t, name):
    x_stacked = len(xs) == 1
    x_specs = [_row_spec(D_MODEL)] if x_stacked else [_ctx_row_spec(D_MODEL), _lat_row_spec(D_MODEL)]
    return pl.pallas_call(
        functools.partial(_outproj_kernel, x_stacked=x_stacked),
        grid=(N_TILES,),
        in_specs=[_ctx_row_spec(D_MODEL), _lat_row_spec(D_MODEL)] + x_specs + [
            _weight_spec(D_MODEL, D_MODEL),
            _mod_spec(2),
            _full_spec((1, D_MODEL)),
            _mod_spec(3),
            _mod_spec(4),
            _full_spec((LANES, D_MODEL)),
        ],
        out_specs=[
            _row_spec(D_MODEL),
            pl.BlockSpec((ROW_TILE * TOK_SLABS, LANES), lambda i: (i, 0)),
            pl.BlockSpec((LANES, ROW_TILE), lambda i: (0, i)),
        ],
        out_shape=[
            jax.ShapeDtypeStruct((N_ALL, D_MODEL), F32),
            jax.ShapeDtypeStruct((N_ALL * TOK_SLABS, LANES), F32),
            jax.ShapeDtypeStruct((LANES, N_ALL), F32),
        ],
        scratch_shapes=[pltpu.VMEM((D_MODEL, D_MODEL), BF16)],
        compiler_params=_cp(48),
        name=name,
    )(u_ctx, u_lat, *xs, w_out, mods, norm_g, mods, mods, wr_t)


N_TOK_BLOCKS = N_TOK // LANES
STACK_ROWS = N_TOK_BLOCKS * N_EXPERTS


def _router_kernel(lt_ref, tri_ref, blk_ref, idx_ref, gate_ref, cwin_ref, aff_ref, cend_ref, ctot_ref):
    lt = lt_ref[...]
    e = jnp.exp(lt - jnp.max(lt, axis=0, keepdims=True))
    aff = e / jnp.sum(e, axis=0, keepdims=True)
    aff = jnp.where(aff >= F32_MIN_NORMAL, aff, 0.0)

    def count_ge(x):
        return jnp.sum((aff >= x).astype(F32), axis=1, keepdims=True)

    def step(mid_of):
        def body(_, c):
            lo, hi = c
            mid = jnp.minimum(jnp.maximum(mid_of(lo, hi), lo), hi)
            ok = count_ge(mid) >= float(CAPACITY)
            return jnp.where(ok, mid, lo), jnp.where(ok, hi, mid)
        return body

    lo = jnp.full((N_EXPERTS, 1), F32_MIN_NORMAL, F32)
    hi = jnp.full((N_EXPERTS, 1), 2.0, F32)
    lo, hi = lax.fori_loop(0, 8, step(lambda a, b: jnp.sqrt(a * b)), (lo, hi))
    lo, hi = lax.fori_loop(0, 28, step(lambda a, b: a + (b - a) * 0.5), (lo, hi))
    kth = jnp.max(jnp.where(aff < hi, aff, 0.0), axis=1, keepdims=True)
    thr = jnp.where(count_ge(lo) >= float(CAPACITY), kth, 0.0)

    tri = tri_ref[...]
    blk = blk_ref[...]

    def stack(x):
        return jnp.concatenate([x[:, LANES * b:LANES * (b + 1)] for b in range(N_TOK_BLOCKS)], axis=0)

    def unstack(x):
        return jnp.concatenate([x[N_EXPERTS * b:N_EXPERTS * (b + 1), :] for b in range(N_TOK_BLOCKS)], axis=1)

    def cumsum_stacked(mask_st):
        win = _dot(mask_st.astype(BF16), tri)
        tot = win[:, LANES - 1:LANES]
        off = _dot(blk, jnp.broadcast_to(tot, (STACK_ROWS, LANES)).astype(BF16))[:, :1]
        return win, off, tot

    gt = aff > thr
    eq = aff == thr
    need = float(CAPACITY) - jnp.sum(gt.astype(F32), axis=1, keepdims=True)
    eq_f = eq.astype(F32)
    ewin, eoff, _ = cumsum_stacked(stack(eq_f))
    eq_before = unstack(ewin + eoff) - eq_f
    sel = jnp.logical_or(gt, jnp.logical_and(eq, eq_before < need))
    cwin, coff, ctot = cumsum_stacked(stack(sel.astype(F32)))
    cwin_ref[...] = cwin
    aff_ref[...] = stack(aff)
    cend_ref[...] = jnp.broadcast_to(coff + ctot, (STACK_ROWS, LANES))
    ctot_ref[...] = jnp.broadcast_to(ctot, (STACK_ROWS, LANES))

    blk_id = lax.broadcasted_iota(jnp.int32, (N_TOK_BLOCKS, 1), 0).astype(F32)
    slot = lax.broadcasted_iota(jnp.int32, (1, CAPACITY), 1).astype(F32)
    lane_f = lax.broadcasted_iota(jnp.int32, (LANES, 1), 0).astype(F32)

    def per_expert(ex):
        rows = pl.ds(ex, N_TOK_BLOCKS, stride=N_EXPERTS)
        cend = cend_ref[rows, :][:, :1]
        ctot_e = ctot_ref[rows, :][:, :1]
        before = cend <= slot
        nblk = jnp.sum(before.astype(F32), axis=0, keepdims=True)
        base = jnp.sum(jnp.where(before, ctot_e, 0.0), axis=0, keepdims=True)
        pick = jnp.where(blk_id == nblk, 1.0, 0.0).astype(BF16)
        cnt_in = _dot_tn(cwin_ref[rows, :].astype(BF16), pick)
        lane_idx = jnp.sum((cnt_in <= slot - base).astype(F32), axis=0, keepdims=True)
        idx_ref[pl.ds(ex, 1), :] = ((nblk * float(LANES) + lane_idx) * float(TOK_SLABS)).astype(jnp.int32)
        a_hi, a_mid, a_lo = _split3(aff_ref[rows, :])
        aff_blk = _dot_tn(a_hi, pick) + _dot_tn(a_mid, pick) + _dot_tn(a_lo, pick)
        gate_ref[pl.ds(ex, 1), :] = jnp.sum(jnp.where(lane_f == lane_idx, aff_blk, 0.0), axis=0, keepdims=True)

    def expert_pair(i, carry):
        per_expert(2 * i)
        per_expert(2 * i + 1)
        return carry

    lax.fori_loop(0, N_EXPERTS // 2, expert_pair, 0)


def _cumsum_consts():
    l = np.arange(LANES)
    tri = (l[:, None] <= l[None, :]).astype(np.float32)
    r = np.arange(STACK_ROWS)
    same = (r[:, None] % N_EXPERTS) == (r[None, :] % N_EXPERTS)
    earlier = (r[None, :] // N_EXPERTS) < (r[:, None] // N_EXPERTS)
    blk = (same & earlier).astype(np.float32)
    return jnp.asarray(tri, BF16), jnp.asarray(blk, BF16)


def _router(logits_t, token_set, tri, blk, name):
    return pl.pallas_call(
        _router_kernel,
        grid=(1,),
        in_specs=[
            pl.BlockSpec((N_EXPERTS, N_TOK), lambda i: (0, token_set)),
            _full_spec((LANES, LANES)),
            _full_spec((STACK_ROWS, STACK_ROWS)),
        ],
        out_specs=[_full_spec((N_EXPERTS, CAPACITY))] * 2,
        out_shape=[jax.ShapeDtypeStruct((N_EXPERTS, CAPACITY), jnp.int32),
                   jax.ShapeDtypeStruct((N_EXPERTS, CAPACITY), F32)],
        scratch_shapes=[pltpu.VMEM((STACK_ROWS, LANES), F32)] * 4,
        compiler_params=_cp(48),
        name=name,
    )(logits_t, tri, blk)


GATHER_UNROLL = 16


def _gather_kernel(off_ref, hs_ref, xe_ref, tile_ref):
    ex = pl.program_id(0)

    def body(i, carry):
        for u in range(GATHER_UNROLL):
            s = i * GATHER_UNROLL + u
            off = pl.multiple_of(off_ref[ex * CAPACITY + s], TOK_SLABS)
            tile_ref[pl.ds(s, TOK_SLABS, stride=XE_STRIDE), :] = hs_ref[pl.ds(off, TOK_SLABS), :]
        return carry

    lax.fori_loop(0, CAPACITY // GATHER_UNROLL, body, 0)
    xe_ref[...] = jnp.concatenate(
        [tile_ref[pl.ds(c * XE_STRIDE, CAPACITY), :] for c in range(TOK_SLABS)], axis=1).astype(BF16)


def _gather(idx_flat, hs, token_set, name):
    return pl.pallas_call(
        _gather_kernel,
        grid_spec=pltpu.PrefetchScalarGridSpec(
            num_scalar_prefetch=1,
            grid=(N_EXPERTS,),
            in_specs=[pl.BlockSpec((N_TOK * TOK_SLABS, LANES), lambda e, idx: (token_set, 0),
                                   pipeline_mode=pl.Buffered(1))],
            out_specs=pl.BlockSpec((None, CAPACITY, D_MODEL), lambda e, idx: (e, 0, 0)),
            scratch_shapes=[pltpu.VMEM((TOK_SLABS * XE_STRIDE, LANES), F32)],
        ),
        out_shape=jax.ShapeDtypeStruct((N_EXPERTS, CAPACITY, D_MODEL), BF16),
        compiler_params=_cp(40),
        name=name,
    )(idx_flat, hs)


def _ffn_kernel(xc_ref, xl_ref, gc_ref, gl_ref, w1_ref, w3_ref, w2_ref, yc_ref, yl_ref, xe_ref, acc_ref):
    f = pl.program_id(1)

    @pl.when(f == 0)
    def _():
        xe_ref[:CAPACITY, :] = xc_ref[...]
        xe_ref[CAPACITY:, :] = xl_ref[...]
        acc_ref[...] = jnp.zeros_like(acc_ref)

    x = xe_ref[...]
    h1 = _dot(x, w1_ref[...].astype(BF16))
    h3 = _dot(x, w3_ref[...].astype(BF16))
    hid = (h1 * jax.nn.sigmoid(h1) * h3).astype(BF16)
    acc_ref[...] += _dot(hid, w2_ref[...].astype(BF16))

    @pl.when(f == pl.num_programs(1) - 1)
    def _():
        _to_slabs(yc_ref, acc_ref[:CAPACITY, :] * gc_ref[...], CAPACITY)
        _to_slabs(yl_ref, acc_ref[CAPACITY:, :] * gl_ref[...], CAPACITY)


def _expert_ffn(xe_ctx, xe_lat, gate_ctx, gate_lat, w1, w3, w2, layer, name):
    nf = D_EXPERT // FF_TILE
    x_spec = pl.BlockSpec((None, CAPACITY, D_MODEL), lambda e, f: (e, 0, 0))
    g_spec = pl.BlockSpec((None, CAPACITY, 1), lambda e, f: (e, 0, 0))
    y_spec = pl.BlockSpec((None, CAPACITY * TOK_SLABS, LANES), lambda e, f: (e, 0, 0))
    y_shape = jax.ShapeDtypeStruct((N_EXPERTS, CAPACITY * TOK_SLABS, LANES), F32)
    return pl.pallas_call(
        _ffn_kernel,
        grid=(N_EXPERTS, nf),
        in_specs=[
            x_spec, x_spec, g_spec, g_spec,
            pl.BlockSpec((None, None, D_MODEL, FF_TILE), lambda e, f: (layer, e, 0, f)),
            pl.BlockSpec((None, None, D_MODEL, FF_TILE), lambda e, f: (layer, e, 0, f)),
            pl.BlockSpec((None, None, FF_TILE, D_MODEL), lambda e, f: (layer, e, f, 0)),
        ],
        out_specs=[y_spec, y_spec],
        out_shape=[y_shape, y_shape],
        scratch_shapes=[
            pltpu.VMEM((2 * CAPACITY, D_MODEL), BF16),
            pltpu.VMEM((2 * CAPACITY, D_MODEL), F32),
        ],
        compiler_params=_cp(56),
        name=name,
    )(xe_ctx, xe_lat, gate_ctx, gate_lat, w1, w3, w2)


SCATTER_UNROLL = 16


def _scatter_expert(off_ref, ye_ref, y_ref, ex):
    def body(i, carry):
        upd = []
        for u in range(SCATTER_UNROLL):
            s = i * SCATTER_UNROLL + u
            rows = pl.ds(pl.multiple_of(off_ref[ex * CAPACITY + s], TOK_SLABS), TOK_SLABS)
            src = ye_ref[pl.ds(pl.multiple_of(s * TOK_SLABS, TOK_SLABS), TOK_SLABS), :]
            upd.append((rows, y_ref[rows, :] + src))
        for rows, val in upd:
            y_ref[rows, :] = val
        return carry

    lax.fori_loop(0, CAPACITY // SCATTER_UNROLL, body, 0)


def _combine_kernel(idx_ref, ye_ref, y_ref):
    ex = pl.program_id(0)

    @pl.when(ex == 0)
    def _():
        y_ref[...] = jnp.zeros_like(y_ref)

    _scatter_expert(idx_ref, ye_ref, y_ref, ex)


def _combine_final_kernel(idx_ref, ye_ref, xp_ref, gp_ref, fg_ref, o_ref, y_ref):
    i = pl.program_id(0)

    @pl.when(i == 0)
    def _():
        y_ref[...] = jnp.zeros_like(y_ref)

    @pl.when(i < N_EXPERTS)
    def _():
        _scatter_expert(idx_ref, ye_ref, y_ref, i)

    @pl.when(i >= N_EXPERTS)
    def _():
        slabs_per_tile = ROW_TILE * TOK_SLABS
        base = pl.multiple_of((i - N_EXPERTS) * slabs_per_tile, slabs_per_tile)
        x = xp_ref[...] + gp_ref[...] * _from_slabs(y_ref, ROW_TILE, base)
        o_ref[...] = _rmsnorm(x, fg_ref[...])


def _combine_final(idx_flat, ye, x_prev, token_set, mods_prev, final_g, name):
    def tile(i):
        return jnp.maximum(i - N_EXPERTS, 0)

    def gate_row(i, idx):
        return (_tile_group(token_set * N_ROW_TILES + tile(i)) * 6 + 5, 0, 0)

    return pl.pallas_call(
        _combine_final_kernel,
        grid_spec=pltpu.PrefetchScalarGridSpec(
            num_scalar_prefetch=1,
            grid=(N_EXPERTS + N_ROW_TILES,),
            in_specs=[
                pl.BlockSpec((None, CAPACITY * TOK_SLABS, LANES), lambda i, idx: (jnp.minimum(i, N_EXPERTS - 1), 0, 0)),
                pl.BlockSpec((ROW_TILE, D_MODEL), lambda i, idx: (token_set * N_ROW_TILES + tile(i), 0)),
                pl.BlockSpec((None, 1, D_MODEL), gate_row),
                pl.BlockSpec((1, D_MODEL), lambda i, idx: (0, 0)),
            ],
            out_specs=pl.BlockSpec((ROW_TILE, D_MODEL), lambda i, idx: (tile(i), 0)),
            scratch_shapes=[pltpu.VMEM((N_TOK * TOK_SLABS, LANES), F32)],
        ),
        out_shape=jax.ShapeDtypeStruct((N_TOK, D_MODEL), F32),
        compiler_params=_cp(40),
        name=name,
    )(idx_flat, ye, x_prev, mods_prev, final_g)


def _combine(idx_flat, ye, name):
    return pl.pallas_call(
        _combine_kernel,
        grid_spec=pltpu.PrefetchScalarGridSpec(
            num_scalar_prefetch=1,
            grid=(N_EXPERTS,),
            in_specs=[pl.BlockSpec((None, CAPACITY * TOK_SLABS, LANES), lambda e, idx: (e, 0, 0))],
            out_specs=pl.BlockSpec((N_TOK * TOK_SLABS, LANES), lambda e, idx: (0, 0)),
        ),
        out_shape=jax.ShapeDtypeStruct((N_TOK * TOK_SLABS, LANES), F32),
        compiler_params=_cp(56),
        name=name,
    )(idx_flat, ye)


def _moe(hs, lt, tri, blk, w1, w3, w2, layer):
    tag = "l%d" % layer
    routed = []
    for token_set, name in enumerate(("ctx", "lat")):
        idx, gate = _router(lt, token_set, tri, blk, "router_%s_%s" % (tag, name))
        idx_flat = idx.reshape(N_EXPERTS * CAPACITY)
        xe = _gather(idx_flat, hs, token_set, "gather_%s_%s" % (tag, name))
        routed.append((idx_flat, xe, gate.reshape(N_EXPERTS, CAPACITY, 1)))
    (idx_c, xe_c, g_c), (idx_l, xe_l, g_l) = routed
    ye_c, ye_l = _expert_ffn(xe_c, xe_l, g_c, g_l, w1, w3, w2, layer, "ffn_" + tag)
    return (idx_c, ye_c), (idx_l, ye_l)


def _gla_proj_kernel(xp_ref, yc_ref, yl_ref, gp_ref, g_ref, sh_ref, sc_ref, w_ref, wt_ref, wg_ref, bg_ref,
                     x_ref, q_ref, k_ref, v_ref, r_ref, lgf_ref, lgb_ref, wb_ref):
    _cast_once(w_ref, wb_ref)

    def body(latent):
        yt_ref = yl_ref if latent else yc_ref
        x = xp_ref[...] + gp_ref[...] * _from_slabs(yt_ref, ROW_TILE)
        x_ref[...] = x
        h = _modulate(x, g_ref[...], sh_ref[...], sc_ref[...]).astype(BF16)
        z = _dot(h, wb_ref[...])
        nq = GLA_HEADS * GLA_DK
        nv = GLA_HEADS * GLA_DV
        q_ref[...] = z[:, :nq] * (GLA_DK ** -0.5)
        k_ref[...] = z[:, nq:2 * nq]
        v_ref[...] = z[:, 2 * nq:2 * nq + nv].astype(BF16)
        r_ref[...] = z[:, 2 * nq + nv:].astype(BF16)
        zg = _dot(h, wt_ref[...]).astype(BF16)
        a = _dot(zg, wg_ref[...]) + bg_ref[...]
        ls = (jnp.minimum(a, 0.0) - jnp.log1p(jnp.exp(-jnp.abs(a)))) * (1.0 / GLA_GATE_NORM)
        lgf_ref[...] = ls[:, :nq]
        lgb_ref[...] = ls[:, nq:]

    _per_set(body)


def _gla_proj(x_prev, y_ctx, y_lat, mods_prev, mods, norm_g, w_in, w_tail, wg, bg):
    nq = GLA_HEADS * GLA_DK
    nv = GLA_HEADS * GLA_DV
    n_main = 2 * nq + 2 * nv
    slabs = ROW_TILE * TOK_SLABS
    return pl.pallas_call(
        _gla_proj_kernel,
        grid=(N_TILES,),
        in_specs=[
            _row_spec(D_MODEL),
            pl.BlockSpec((slabs, LANES), lambda i: (jnp.minimum(i, N_ROW_TILES - 1), 0)),
            pl.BlockSpec((slabs, LANES), lambda i: (jnp.maximum(i - N_ROW_TILES, 0), 0)),
            _mod_spec(5),
            _full_spec((1, D_MODEL)),
            _mod_spec(0),
            _mod_spec(1),
            _weight_spec(D_MODEL, n_main),
            _full_spec((D_MODEL, LANES)),
            _full_spec((LANES, 2 * nq)),
            _full_spec((1, 2 * nq)),
        ],
        out_specs=[_row_spec(D_MODEL), _row_spec(nq), _row_spec(nq), _row_spec(nv), _row_spec(nv),
                   _row_spec(nq), _row_spec(nq)],
        out_shape=[
            jax.ShapeDtypeStruct((N_ALL, D_MODEL), F32),
            jax.ShapeDtypeStruct((N_ALL, nq), F32),
            jax.ShapeDtypeStruct((N_ALL, nq), F32),
            jax.ShapeDtypeStruct((N_ALL, nv), BF16),
            jax.ShapeDtypeStruct((N_ALL, nv), BF16),
            jax.ShapeDtypeStruct((N_ALL, nq), F32),
            jax.ShapeDtypeStruct((N_ALL, nq), F32),
        ],
        scratch_shapes=[pltpu.VMEM((D_MODEL, n_main), BF16)],
        compiler_params=_cp(56),
        name="gla_proj",
    )(x_prev, y_ctx, y_lat, mods_prev, norm_g, mods, mods, w_in, w_tail, wg, bg)


GLA_LEVELS = (32, 16, 8)
GLA_UNROLL = 8
GLA_GROUP = 4 * GLA_CHUNK
GLA_SAFE_DECAY = 60.0
NEG_BIG = -1e30


def _gla_intra_fast(qe, kc, b2, fwd):
    c = GLA_CHUNK
    row = lax.broadcasted_iota(jnp.int32, (c, 1), 0)
    col = lax.broadcasted_iota(jnp.int32, (1, c), 1)
    keep = (col <= row) if fwd else (col >= row)
    kq = (kc * jnp.exp2(-b2)).astype(BF16)
    return jnp.where(keep, _dot_nt(qe, kq), 0.0)


def _gla_intra_robust(qc, kc, b2, fwd):
    c = GLA_CHUNK
    row = lax.broadcasted_iota(jnp.int32, (c, 1), 0)
    col = lax.broadcasted_iota(jnp.int32, (1, c), 1)
    a = jnp.zeros((c, c), F32)
    for g in GLA_LEVELS:
        odd = ((row >> int(math.log2(g))) & 1) == 1
        later = odd if fwd else jnp.logical_not(odd)
        refs = []
        for p in range(c // (2 * g)):
            r0 = 2 * g * p + (g - 1 if fwd else g)
            refs.append(jnp.broadcast_to(b2[r0:r0 + 1], (2 * g, GLA_DK)))
        ref = jnp.concatenate(refs, axis=0) if len(refs) > 1 else refs[0]
        qt = qc * jnp.exp2(jnp.where(later, b2 - ref, NEG_BIG))
        kt = kc * jnp.exp2(jnp.where(later, NEG_BIG, ref - b2))
        same_parent = (row >> int(math.log2(2 * g))) == (col >> int(math.log2(2 * g)))
        a = a + jnp.where(same_parent, _dot_nt(qt.astype(BF16), kt.astype(BF16)), 0.0)

    sub = lax.broadcasted_iota(jnp.int32, (SUBLANES, 1), 0)
    strips = []
    for blk in range(c // SUBLANES):
        r0 = SUBLANES * blk
        qb = qc[r0:r0 + SUBLANES]
        bb = b2[r0:r0 + SUBLANES]
        strip = jnp.zeros((SUBLANES, c), F32)
        for jj in range(SUBLANES):
            j = r0 + jj
            cond = (sub >= jj) if fwd else (sub <= jj)
            t = qb * kc[j:j + 1] * jnp.exp2(jnp.where(cond, bb - b2[j:j + 1], NEG_BIG))
            strip = jnp.where(col == j, jnp.sum(t, axis=1, keepdims=True), strip)
        strips.append(strip)
    return a + jnp.concatenate(strips, axis=0)


def _gla_state_increment(kc, vc, b2f, b2b):
    c = GLA_CHUNK
    btf = b2f[c - 1:c]
    btb = b2b[0:1]
    kd = jnp.concatenate([(kc * jnp.exp2(btf - b2f)).astype(BF16), (kc * jnp.exp2(btb - b2b)).astype(BF16)], axis=1)
    return _dot_tn(vc, kd), jnp.concatenate([jnp.exp2(btf), jnp.exp2(btb)], axis=1)


def _gla_scores(qc, kc, b2f, b2b, fast):
    qe_f = (qc * jnp.exp2(b2f)).astype(BF16)
    qe_b = (qc * jnp.exp2(b2b)).astype(BF16)
    if fast:
        a = _gla_intra_fast(qe_f, kc, b2f, True) + _gla_intra_fast(qe_b, kc, b2b, False)
    else:
        a = _gla_intra_robust(qc, kc, b2f, True) + _gla_intra_robust(qc, kc, b2b, False)
    return jnp.concatenate([qe_f, qe_b], axis=1), a.astype(BF16)


def _gla_kernel(q_ref, k_ref, v_ref, r_ref, gf_ref, gb_ref, *refs, t_len, heads, zero_state):
    if zero_state:
        s0f_ref = s0b_ref = None
    else:
        s0f_ref, s0b_ref = refs[:2]
        refs = refs[2:]
    go_ref, trif_ref, trib_ref, u_ref, sf_ref, sb_ref = refs[:6]
    scratch = refs[6:]
    _gla_passes(q_ref, k_ref, v_ref, r_ref, gf_ref, gb_ref, s0f_ref, s0b_ref, go_ref, trif_ref, trib_ref,
                u_ref, sf_ref, sb_ref, *scratch, t_len=t_len, heads=heads)


def _gla_passes(q_ref, k_ref, v_ref, r_ref, gf_ref, gb_ref, s0f_ref, s0b_ref, go_ref, trif_ref, trib_ref,
                u_ref, sf_ref, sb_ref, *scratch, t_len, heads):
    c = GLA_CHUNK
    n = t_len // c
    per_head = len(scratch) // heads
    hd = []
    for hh in range(heads):
        dk = slice(hh * GLA_DK, (hh + 1) * GLA_DK)
        dv = slice(hh * GLA_DV, (hh + 1) * GLA_DV)
        b2f, b2b, qe, oi, kv, dec, sst, st = scratch[hh * per_head:(hh + 1) * per_head]
        hd.append(dict(q=q_ref.at[:, dk], k=k_ref.at[:, dk], v=v_ref.at[:, dv], r=r_ref.at[:, dv],
                       gf=gf_ref.at[:, dk], gb=gb_ref.at[:, dk],
                       s0f=None if s0f_ref is None else s0f_ref.at[hh],
                       s0b=None if s0b_ref is None else s0b_ref.at[hh],
                       u=u_ref.at[:, dv], sf=sf_ref.at[hh], sb=sb_ref.at[hh],
                       b2f=b2f, b2b=b2b, qe=qe, oi=oi, kv=kv, dec=dec, sst=sst, st=st))

    def rows_of(ci):
        return pl.ds(pl.multiple_of(ci * c, c), c)

    def state_rows_of(ci):
        return pl.ds(pl.multiple_of(ci * GLA_DV, GLA_DV), GLA_DV)

    n_groups = t_len // GLA_GROUP
    cum_unroll = 2 if n_groups % 2 == 0 else 1

    def cum_body(i, carry):
        sums = []
        for h in hd:
            for u in range(cum_unroll):
                rows = pl.ds(pl.multiple_of((i * cum_unroll + u) * GLA_GROUP, GLA_GROUP), GLA_GROUP)
                for g_ref, tri_ref, b_ref in ((h["gf"], trif_ref, h["b2f"]), (h["gb"], trib_ref, h["b2b"])):
                    s3 = _dot(tri_ref[...], jnp.concatenate(_split3(g_ref[rows, :]), axis=1))
                    sums.append((b_ref, rows, s3))
        for b_ref, rows, s3 in sums:
            b_ref[rows, :] = (s3[:, :GLA_DK] + s3[:, GLA_DK:2 * GLA_DK] + s3[:, 2 * GLA_DK:]) * LOG2E
        return carry

    lax.fori_loop(0, n_groups // cum_unroll, cum_body, 0)

    def chunk_total(g_ref):
        return jnp.min(jnp.sum(g_ref[...].reshape(n, c, GLA_DK), axis=1))

    totals = [chunk_total(h[g]) for h in hd for g in ("gf", "gb")]
    safe = functools.reduce(jnp.minimum, totals) >= -GLA_SAFE_DECAY

    def make_local(fast, unroll):
        def body(i, carry):
            work = [(h, i * unroll + u) for h in hd for u in range(unroll)]
            for h, ci in work:
                rows = rows_of(ci)
                kv, dec = _gla_state_increment(h["k"][rows, :], h["v"][rows, :], h["b2f"][rows, :], h["b2b"][rows, :])
                h["kv"][state_rows_of(ci), :] = kv
                h["dec"][pl.ds(ci, 1), :] = dec
            scores = []
            for h, ci in work:
                rows = rows_of(ci)
                qe, a = _gla_scores(h["q"][rows, :], h["k"][rows, :], h["b2f"][rows, :], h["b2b"][rows, :], fast)
                h["qe"][rows, :] = qe
                scores.append(a)
            for (h, ci), a in zip(work, scores):
                rows = rows_of(ci)
                h["oi"][rows, :] = _dot(a, h["v"][rows, :])
            return carry
        return body

    unroll = min(GLA_UNROLL, n)

    @pl.when(safe)
    def _():
        lax.fori_loop(0, n // unroll, make_local(True, unroll), 0)

    @pl.when(jnp.logical_not(safe))
    def _():
        lax.fori_loop(0, n, make_local(False, 1), 0)

    fl = slice(0, GLA_DK)
    bl = slice(GLA_DK, 2 * GLA_DK)
    for h in hd:
        if h["s0f"] is None:
            h["st"][...] = jnp.zeros_like(h["st"])
        else:
            h["st"][...] = jnp.concatenate([h["s0f"][...].T, h["s0b"][...].T], axis=1)

    def rec_body(i, carry):
        cf = i
        cb = n - 1 - i
        for h in hd:
            st = h["st"][...]
            h["sst"][state_rows_of(cf), fl] = st[:, fl].astype(BF16)
            h["sst"][state_rows_of(cb), bl] = st[:, bl].astype(BF16)
            dec = jnp.concatenate([h["dec"][pl.ds(cf, 1), :][:, fl], h["dec"][pl.ds(cb, 1), :][:, bl]], axis=1)
            kv = jnp.concatenate([h["kv"][state_rows_of(cf), fl], h["kv"][state_rows_of(cb), bl]], axis=1)
            h["st"][...] = st * dec + kv
        return carry

    lax.fori_loop(0, n, rec_body, 0)
    for h in hd:
        h["sf"][...] = h["st"][:, fl].T
        h["sb"][...] = h["st"][:, bl].T

    go = go_ref[...]

    def out_body(i, carry):
        work = [(h, i * unroll + u) for h in hd for u in range(unroll)]
        inter = [_dot_nt(h["qe"][rows_of(ci), :], h["sst"][state_rows_of(ci), :]) for h, ci in work]
        for (h, ci), o_inter in zip(work, inter):
            rows = rows_of(ci)
            r = h["r"][rows, :].astype(F32)
            h["u"][rows, :] = (_rmsnorm(h["oi"][rows, :] + o_inter, go) * (r * jax.nn.sigmoid(r))).astype(BF16)
        return carry

    lax.fori_loop(0, n // unroll, out_body, 0)


def _gla_tri():
    i = np.arange(GLA_GROUP)
    same = (i[:, None] // GLA_CHUNK) == (i[None, :] // GLA_CHUNK)
    fwd = (same & (i[None, :] <= i[:, None])).astype(np.float32)
    bwd = (same & (i[None, :] >= i[:, None])).astype(np.float32)
    return jnp.asarray(fwd, BF16), jnp.asarray(bwd, BF16)


def _gla(q, k, v, r, lgf, lgb, batch0, s0f, s0b, g_out, n_batch, t_len, heads, name):
    zero_state = s0f is None
    trif, trib = _gla_tri()
    n_chunks = t_len // GLA_CHUNK
    qk_spec = pl.BlockSpec((None, t_len, heads * GLA_DK), lambda b, h: (batch0 + b, 0, h))
    vin_spec = pl.BlockSpec((None, t_len, heads * GLA_DV), lambda b, h: (batch0 + b, 0, h))
    v_spec = pl.BlockSpec((None, t_len, heads * GLA_DV), lambda b, h: (b, 0, h))
    s_spec = pl.BlockSpec((None, heads, GLA_DK, GLA_DV), lambda b, h: (b, h, 0, 0))
    const = lambda shape: pl.BlockSpec(shape, lambda b, h: (0,) * len(shape))
    s_shape = jax.ShapeDtypeStruct((n_batch, GLA_HEADS, GLA_DK, GLA_DV), F32)
    states = [] if zero_state else [s0f, s0b]
    return pl.pallas_call(
        functools.partial(_gla_kernel, t_len=t_len, heads=heads, zero_state=zero_state),
        grid=(n_batch, GLA_HEADS // heads),
        in_specs=[qk_spec, qk_spec, vin_spec, vin_spec, qk_spec, qk_spec] + [s_spec] * len(states)
        + [const((1, GLA_DV)), const((GLA_GROUP, GLA_GROUP)), const((GLA_GROUP, GLA_GROUP))],
        out_specs=[v_spec, s_spec, s_spec],
        out_shape=[jax.ShapeDtypeStruct((n_batch, t_len, GLA_HEADS * GLA_DV), BF16), s_shape, s_shape],
        scratch_shapes=[
            pltpu.VMEM((t_len, GLA_DK), F32), pltpu.VMEM((t_len, GLA_DK), F32),
            pltpu.VMEM((t_len, 2 * GLA_DK), BF16),
            pltpu.VMEM((t_len, GLA_DV), F32),
            pltpu.VMEM((n_chunks * GLA_DV, 2 * GLA_DK), F32),
            pltpu.VMEM((max(n_chunks, SUBLANES), 2 * GLA_DK), F32),
            pltpu.VMEM((n_chunks * GLA_DV, 2 * GLA_DK), BF16),
            pltpu.VMEM((GLA_DV, 2 * GLA_DK), F32),
        ] * heads,
        compiler_params=_cp(48),
        name=name,
    )(q, k, v, r, lgf, lgb, *states, g_out, trif, trib)


def kernel(x_prompt, x_sample, cache_k, cache_v, state_fwd, state_bwd, c, c_ctx, w_mod, b_mod, norm_g,
           da_w_in, da_w_out, da_lam_q1, da_lam_k1, da_lam_q2, da_lam_k2, da_g_sub, gla_w_in, gla_w_gf2,
           gla_b_gf, gla_w_gb2, gla_b_gb, gla_g_out, gla_w_out, moe_w_router, moe_w1, moe_w3, moe_w2, final_g):
    n_ctx_b = x_prompt.shape[0]
    n_lat_b = x_sample.shape[0]
    x_ctx = x_prompt.reshape(N_TOK, D_MODEL)
    x_lat = x_sample.reshape(N_TOK, D_MODEL)

    cvec = jnp.zeros((SUBLANES, D_MODEL), F32).at[0].set(c_ctx).at[1:1 + n_lat_b].set(c)
    mods = _mod_params(cvec, w_mod, b_mod)
    tri, blk = _cumsum_consts()
    wr_t = [jnp.zeros((LANES, D_MODEL), BF16).at[:N_EXPERTS].set(moe_w_router[i].T.astype(BF16))
            for i in range(DEPTH)]
    ng = norm_g.reshape(DEPTH, 2, 1, D_MODEL)

    lam_init = 0.8 - 0.6 * math.exp(-0.3 * 0)
    w_in = da_w_in
    w_out = da_w_out
    lam_vecs = [v[0].reshape(1, DA_DIM) for v in (da_lam_q1, da_lam_k1, da_lam_q2, da_lam_k2)]
    g_sub = da_g_sub[0].reshape(1, DA_VDIM)

    q, k, v, kf, vf = _da_proj(x_ctx, x_lat, mods[0], ng[0, 0], w_in, _rope_tables())
    past = cache_k.shape[2]
    ck = cache_k[:, 0].reshape(n_lat_b * past, D_MODEL)
    cv = cache_v[:, 0].reshape(n_lat_b * past, D_MODEL)
    u_ctx = _attention(lam_vecs, g_sub, q, 0, [(k, v, 0, SEQ)], n_ctx_b, SEQ, ATTN_Q_TILE, lam_init, "attn_ctx")
    u_lat = _attention(lam_vecs, g_sub, q, N_TOK, [(k, v, N_TOK, DEC_SEQ), (ck, cv, 0, past)], n_lat_b, DEC_SEQ,
                       ATTN_Q_TILE, lam_init, "attn_lat")

    x1, hs, lt = _outproj(u_ctx, u_lat, (x_ctx, x_lat), w_out, mods[0], ng[0, 1], wr_t[0], "outproj0")
    (idx_c, ye_c), (idx_l, ye_l) = _moe(hs, lt, tri, blk, moe_w1, moe_w3, moe_w2, 0)
    y_ctx = _combine(idx_c, ye_c, "combine_l0_ctx")
    y_lat = _combine(idx_l, ye_l, "combine_l0_lat")

    nq = GLA_HEADS * GLA_DK
    n_main = 2 * nq + 2 * GLA_HEADS * GLA_DV
    w_tail = jnp.zeros((D_MODEL, LANES), F32).at[:, :2 * GLA_GATE_RANK].set(gla_w_in[0][:, n_main:]).astype(BF16)
    wg = jnp.zeros((LANES, 2 * nq), F32)
    wg = wg.at[:GLA_GATE_RANK, :nq].set(gla_w_gf2[0]).at[GLA_GATE_RANK:2 * GLA_GATE_RANK, nq:].set(gla_w_gb2[0])
    wg = wg.astype(BF16)
    bg = jnp.concatenate([gla_b_gf[0], gla_b_gb[0]]).reshape(1, 2 * nq)
    w_out1 = gla_w_out
    g_out = gla_g_out[0].reshape(1, GLA_DV)

    x2, *gla_in = _gla_proj(x1, y_ctx, y_lat, mods[0], mods[1], ng[1, 0], gla_w_in, w_tail, wg, bg)

    def gla_side(n_b, t_len, s0f, s0b, tag):
        seqs = [a.reshape(N_ALL // t_len, t_len, a.shape[-1]) for a in gla_in]
        batch0 = 0 if s0f is None else N_TOK // t_len
        heads = GLA_HEADS if t_len <= GLA_GROUP else 1
        u, sf, sb = _gla(*seqs, batch0, s0f, s0b, g_out, n_b, t_len, heads, "gla_" + tag)
        return u.reshape(N_TOK, D_MODEL), sf, sb

    ug_ctx, sf, sb = gla_side(n_ctx_b, SEQ, None, None, "ctx")
    ug_lat, _, _ = gla_side(n_lat_b, DEC_SEQ, state_fwd[:, 0], state_bwd[:, 0], "lat")

    x3, hs, lt = _outproj(ug_ctx, ug_lat, (x2,), w_out1, mods[1], ng[1, 1], wr_t[1], "outproj1")
    (idx_c, ye_c), (idx_l, ye_l) = _moe(hs, lt, tri, blk, moe_w1, moe_w3, moe_w2, 1)

    fg = final_g.reshape(1, D_MODEL)
    y_prompt = _combine_final(idx_c, ye_c, x3, 0, mods[1], fg, "final_ctx").reshape(x_prompt.shape)
    y_sample = _combine_final(idx_l, ye_l, x3, 1, mods[1], fg, "final_lat").reshape(x_sample.shape)
    new_k = kf.reshape(n_ctx_b, 1, SEQ, 2 * DA_HEADS, DA_DIM)
    new_v = vf.reshape(n_ctx_b, 1, SEQ, DA_HEADS, DA_VDIM)
    return (y_prompt, y_sample, new_k, new_v, sf[:, None], sb[:, None])
```

```python
import functools
import math

import numpy as np
import jax
import jax.numpy as jnp
from jax import lax
from jax.experimental import pallas as pl
from jax.experimental.pallas import tpu as pltpu

F32 = jnp.float32
BF16 = jnp.bfloat16

D_MODEL = 1024
DEPTH = 2
SEQ = 256
DEC_SEQ = 2048
GRID_W = 64
N_TOK = 4096
DA_HEADS = 8
DA_DIM = 64
DA_VDIM = 128
ROPE_BASE = 10000.0
GLA_HEADS = 4
GLA_DK = 128
GLA_DV = 256
GLA_GATE_RANK = 16
GLA_GATE_NORM = 16.0
GLA_CHUNK = 64
N_EXPERTS = 16
CAPACITY = 512
D_EXPERT = 2048
EPS = 1e-6
F32_MIN_NORMAL = 2.0 ** -126
LOG2E = 1.4426950408889634

LANES = 128
SUBLANES = 8
ROW_TILE = 512
FF_TILE = 512
TOK_SLABS = D_MODEL // LANES
MIB = 1024 * 1024


def _cp(vmem_mib, sem=None):
    return pltpu.CompilerParams(vmem_limit_bytes=vmem_mib * MIB, dimension_semantics=sem)


def _dot(a, b):
    return jnp.dot(a, b, preferred_element_type=F32)


def _dot_nt(a, b):
    return lax.dot_general(a, b, (((1,), (1,)), ((), ())), preferred_element_type=F32)


def _dot_tn(a, b):
    return lax.dot_general(a, b, (((0,), (0,)), ((), ())), preferred_element_type=F32)


def _rmsnorm(x, g):
    return x * lax.rsqrt(jnp.mean(x * x, axis=-1, keepdims=True) + EPS) * g


def _modulate(x, g, shift, scale):
    return _rmsnorm(x, g) * (1.0 + scale) + shift


def _split3(x):
    hi = x.astype(BF16)
    r = x - hi.astype(F32)
    mid = r.astype(BF16)
    lo = (r - mid.astype(F32)).astype(BF16)
    return hi, mid, lo


def _from_slabs(ref, rows, base=0):
    return jnp.concatenate([ref[pl.ds(base + c, rows, stride=TOK_SLABS), :] for c in range(TOK_SLABS)], axis=1)


def _to_slabs(ref, val, rows):
    for c in range(TOK_SLABS):
        ref[pl.ds(c, rows, stride=TOK_SLABS), :] = val[:, LANES * c:LANES * (c + 1)]


def _mod_kernel(c_ref, w_ref, b_ref, o_ref):
    c = c_ref[...]
    s = c * jax.nn.sigmoid(c)
    w = w_ref[...]
    s_hi = s.astype(BF16)
    s_lo = (s - s_hi.astype(F32)).astype(BF16)
    w_hi = w.astype(BF16)
    w_lo = (w - w_hi.astype(F32)).astype(BF16)
    o_ref[...] = _dot(s_hi, w_hi) + _dot(s_hi, w_lo) + _dot(s_lo, w_hi) + b_ref[...]


def _mod_params(cvec, w_mod, b_mod):
    n6 = 6 * D_MODEL
    out = pl.pallas_call(
        _mod_kernel,
        grid=(DEPTH, 6),
        in_specs=[
            pl.BlockSpec((SUBLANES, D_MODEL), lambda i, j: (0, 0)),
            pl.BlockSpec((None, D_MODEL, D_MODEL), lambda i, j: (i, 0, j)),
            pl.BlockSpec((None, 1, D_MODEL), lambda i, j: (i, 0, j)),
        ],
        out_specs=pl.BlockSpec((None, SUBLANES, D_MODEL), lambda i, j: (i, 0, j)),
        out_shape=jax.ShapeDtypeStruct((DEPTH, SUBLANES, n6), F32),
        compiler_params=_cp(32),
        name="mod_params",
    )(cvec, w_mod, b_mod.reshape(DEPTH, 1, n6))
    return out.reshape(DEPTH, SUBLANES * 6, 1, D_MODEL)


N_ROW_TILES = N_TOK // ROW_TILE
N_TILES = 2 * N_ROW_TILES
N_ALL = 2 * N_TOK


def _tile_group(i):
    return jnp.where(i < N_ROW_TILES, 0, 1 + (i - N_ROW_TILES) // (DEC_SEQ // ROW_TILE))


def _mod_spec(k):
    return pl.BlockSpec((None, 1, D_MODEL), lambda i: (_tile_group(i) * 6 + k, 0, 0))


def _row_spec(width):
    return pl.BlockSpec((ROW_TILE, width), lambda i: (i, 0))


def _ctx_row_spec(width):
    return pl.BlockSpec((ROW_TILE, width), lambda i: (jnp.minimum(i, N_ROW_TILES - 1), 0))


def _lat_row_spec(width):
    return pl.BlockSpec((ROW_TILE, width), lambda i: (jnp.maximum(i - N_ROW_TILES, 0), 0))


def _full_spec(shape):
    nd = len(shape)
    return pl.BlockSpec(shape, lambda i: (0,) * nd)


def _per_set(body):
    i = pl.program_id(0)

    @pl.when(i < N_ROW_TILES)
    def _():
        body(False)

    @pl.when(i >= N_ROW_TILES)
    def _():
        body(True)


def _weight_spec(rows, cols):
    return pl.BlockSpec((None, rows, cols), lambda i: (0, 0, 0), pipeline_mode=pl.Buffered(1))


def _cast_once(w_ref, wb_ref):
    @pl.when(pl.program_id(0) == 0)
    def _():
        wb_ref[...] = w_ref[...].astype(BF16)


def _da_proj_kernel(xc_ref, xl_ref, g_ref, sh_ref, sc_ref, w_ref, cos_ref, sin_ref,
                    q_ref, k_ref, v_ref, kf_ref, vf_ref, wb_ref):
    _cast_once(w_ref, wb_ref)

    def body(latent):
        x_ref = xl_ref if latent else xc_ref
        h = _modulate(x_ref[...], g_ref[...], sh_ref[...], sc_ref[...])
        z = _dot(h.astype(BF16), wb_ref[...])
        q = z[:, :D_MODEL]
        k = z[:, D_MODEL:2 * D_MODEL]
        v = z[:, 2 * D_MODEL:]
        if latent:
            reps = D_MODEL // LANES
            cos = jnp.concatenate([cos_ref[...]] * reps, axis=1)
            sin = jnp.concatenate([sin_ref[...]] * reps, axis=1)
            lane = lax.broadcasted_iota(jnp.int32, (1, D_MODEL), 1)
            first = (lane & 16) == 0

            def rot(t):
                partner = jnp.where(first, pltpu.roll(t, D_MODEL - 16, 1), pltpu.roll(t, 16, 1))
                return t * cos + partner * sin

            q = rot(q)
            k = rot(k)
        else:
            kf_ref[...] = k
            vf_ref[...] = v
        q_ref[...] = (q * (DA_DIM ** -0.5 * LOG2E)).astype(BF16)
        k_ref[...] = k.astype(BF16)
        v_ref[...] = v.astype(BF16)

    _per_set(body)


def _da_proj(x_ctx, x_lat, mods, norm_g, w_in, rope_tabs):
    tiles_per_batch = DEC_SEQ // ROW_TILE
    tab_spec = pl.BlockSpec((ROW_TILE, LANES), lambda i: (jnp.maximum(i - N_ROW_TILES, 0) % tiles_per_batch, 0))
    return pl.pallas_call(
        _da_proj_kernel,
        grid=(N_TILES,),
        in_specs=[
            _ctx_row_spec(D_MODEL),
            _lat_row_spec(D_MODEL),
            _full_spec((1, D_MODEL)),
            _mod_spec(0),
            _mod_spec(1),
            _weight_spec(D_MODEL, 3 * D_MODEL),
            tab_spec, tab_spec,
        ],
        out_specs=[_row_spec(D_MODEL)] * 3 + [_ctx_row_spec(D_MODEL)] * 2,
        out_shape=[jax.ShapeDtypeStruct((N_ALL, D_MODEL), BF16)] * 3 + [jax.ShapeDtypeStruct((N_TOK, D_MODEL), F32)] * 2,
        scratch_shapes=[pltpu.VMEM((D_MODEL, 3 * D_MODEL), BF16)],
        compiler_params=_cp(56),
        name="da_proj",
    )(x_ctx, x_lat, norm_g, mods, mods, w_in, *rope_tabs)


def _rope_tables():
    t = np.arange(DEC_SEQ)
    rows = (t // GRID_W).astype(np.float32)
    cols = (t % GRID_W).astype(np.float32)
    half = DA_DIM // 4
    freqs = (np.float32(ROPE_BASE) ** (-np.arange(half, dtype=np.float32) / np.float32(half))).astype(np.float32)
    ang_r = rows[:, None] * freqs
    ang_c = cols[:, None] * freqs
    cos64 = np.concatenate([np.cos(ang_r)] * 2 + [np.cos(ang_c)] * 2, axis=1)
    sin64 = np.concatenate([-np.sin(ang_r), np.sin(ang_r), -np.sin(ang_c), np.sin(ang_c)], axis=1)
    reps = LANES // DA_DIM
    return (jnp.asarray(np.concatenate([cos64] * reps, axis=1), F32),
            jnp.asarray(np.concatenate([sin64] * reps, axis=1), F32))


ATTN_Q_TILE = 256


def _attn_kernel(lq1_ref, lk1_ref, lq2_ref, lk2_ref, gs_ref, q_ref, *refs, lam_init, seg_lens):
    n_seg = len(seg_lens)
    kv_refs, o_ref = refs[:2 * n_seg], refs[2 * n_seg]
    lam =(jnp.exp(jnp.sum(lq1_ref[...] * lk1_ref[...], axis=-1, keepdims=True))
           - jnp.exp(jnp.sum(lq2_ref[...] * lk2_ref[...], axis=-1, keepdims=True)) + lam_init)
    lane = lax.broadcasted_iota(jnp.int32, (1, DA_VDIM), 1)
    first = lane < DA_DIM
    gs = gs_ref[...]
    tq = q_ref.shape[0]

    def softmax_pv(qq, ks, vs):
        ss = [_dot_nt(qq, kk) for kk in ks]
        m = functools.reduce(jnp.maximum, [jnp.max(s, axis=-1, keepdims=True) for s in ss])
        ps = [jnp.exp2(s - m) for s in ss]
        l = functools.reduce(jnp.add, [jnp.sum(p, axis=-1, keepdims=True) for p in ps])
        o = functools.reduce(jnp.add, [_dot(p.astype(BF16), vv) for p, vv in zip(ps, vs)])
        return o, l

    for h in range(DA_HEADS):
        sl = slice(DA_VDIM * h, DA_VDIM * (h + 1))
        qh = q_ref[:, sl]
        ks = [kv_refs[2 * s][:, sl].astype(BF16) for s in range(n_seg)]
        vs = [kv_refs[2 * s + 1][:, sl].astype(BF16) for s in range(n_seg)]
        zero = jnp.zeros_like(qh)
        qq = jnp.concatenate([jnp.where(first, qh, zero), jnp.where(first, zero, qh)], axis=0)
        oo, ll = softmax_pv(qq, ks, vs)
        o = oo[:tq] * (1.0 / ll[:tq]) - oo[tq:] * (lam / ll[tq:])
        o_ref[:, sl] = (_rmsnorm(o, gs) * (1.0 - lam_init)).astype(BF16)


def _attention(lam_vecs, g_sub, q, q_row0, kv_segs, n_batch, t_q, q_tile, lam_init, name):
    nq = t_q // q_tile
    q0 = q_row0 // q_tile
    vec_spec = pl.BlockSpec((1, DA_DIM), lambda b, i: (0, 0))
    kv_specs, kv_args = [], []
    for k, v, row0, t_k in kv_segs:
        kv_specs += [pl.BlockSpec((t_k, D_MODEL), lambda b, i, b0=row0 // t_k: (b0 + b, 0))] * 2
        kv_args += [k, v]
    seg_lens = tuple(t_k for _, _, _, t_k in kv_segs)
    return pl.pallas_call(
        functools.partial(_attn_kernel, lam_init=lam_init, seg_lens=seg_lens),
        grid=(n_batch, nq),
        in_specs=[vec_spec] * 4 + [
            pl.BlockSpec((1, DA_VDIM), lambda b, i: (0, 0)),
            pl.BlockSpec((q_tile, D_MODEL), lambda b, i: (q0 + b * nq + i, 0)),
        ] + kv_specs,
        out_specs=pl.BlockSpec((q_tile, D_MODEL), lambda b, i: (b * nq + i, 0)),
        out_shape=jax.ShapeDtypeStruct((n_batch * t_q, D_MODEL), BF16),
        compiler_params=_cp(56),
        name=name,
    )(*lam_vecs, g_sub, q, *kv_args)


def _outproj_kernel(*refs, x_stacked):
    uc_ref, ul_ref = refs[:2]
    n_x = 1 if x_stacked else 2
    x_refs = refs[2:2 + n_x]
    w_ref, gate_ref, g_ref, sh_ref, sc_ref, wr_ref, x1_ref, hs_ref, lt_ref, wb_ref = refs[2 + n_x:]
    _cast_once(w_ref, wb_ref)

    def body(latent):
        u_ref = ul_ref if latent else uc_ref
        x_ref = x_refs[0] if x_stacked else x_refs[int(latent)]
        x1 = x_ref[...] + gate_ref[...] * _dot(u_ref[...], wb_ref[...])
        x1_ref[...] = x1
        h2 = _modulate(x1, g_ref[...], sh_ref[...], sc_ref[...])
        _to_slabs(hs_ref, h2, ROW_TILE)
        lt_ref[...] = _dot_nt(wr_ref[...], h2.astype(BF16))

    _per_set(body)


def _outproj(u_ctx, u_lat, xs, w_out, mods, norm_g, wr_t, name):
    x_stacked = len(xs) == 1
    x_specs = [_row_spec(D_MODEL)] if x_stacked else [_ctx_row_spec(D_MODEL), _lat_row_spec(D_MODEL)]
    return pl.pallas_call(
        functools.partial(_outproj_kernel, x_stacked=x_stacked),
        grid=(N_TILES,),
        in_specs=[_ctx_row_spec(D_MODEL), _lat_row_spec(D_MODEL)] + x_specs + [
            _weight_spec(D_MODEL, D_MODEL),
            _mod_spec(2),
            _full_spec((1, D_MODEL)),
            _mod_spec(3),
            _mod_spec(4),
            _full_spec((LANES, D_MODEL)),
        ],
        out_specs=[
            _row_spec(D_MODEL),
            pl.BlockSpec((ROW_TILE * TOK_SLABS, LANES), lambda i: (i, 0)),
            pl.BlockSpec((LANES, ROW_TILE), lambda i: (0, i)),
        ],
        out_shape=[
            jax.ShapeDtypeStruct((N_ALL, D_MODEL), F32),
            jax.ShapeDtypeStruct((N_ALL * TOK_SLABS, LANES), F32),
            jax.ShapeDtypeStruct((LANES, N_ALL), F32),
        ],
        scratch_shapes=[pltpu.VMEM((D_MODEL, D_MODEL), BF16)],
        compiler_params=_cp(48),
        name=name,
    )(u_ctx, u_lat, *xs, w_out, mods, norm_g, mods, mods, wr_t)


N_TOK_BLOCKS = N_TOK // LANES
STACK_ROWS = N_TOK_BLOCKS * N_EXPERTS


def _router_kernel(lt_ref, tri_ref, blk_ref, idx_ref, gate_ref, cwin_ref, aff_ref, cend_ref, ctot_ref):
    lt = lt_ref[...]
    e = jnp.exp(lt - jnp.max(lt, axis=0, keepdims=True))
    aff = e / jnp.sum(e, axis=0, keepdims=True)
    aff = jnp.where(aff >= F32_MIN_NORMAL, aff, 0.0)

    def count_ge(x):
        return jnp.sum((aff >= x).astype(F32), axis=1, keepdims=True)

    def step(mid_of):
        def body(_, c):
            lo, hi = c
            mid = jnp.minimum(jnp.maximum(mid_of(lo, hi), lo), hi)
            ok = count_ge(mid) >= float(CAPACITY)
            return jnp.where(ok, mid, lo), jnp.where(ok, hi, mid)
        return body

    lo = jnp.full((N_EXPERTS, 1), F32_MIN_NORMAL, F32)
    hi = jnp.full((N_EXPERTS, 1), 2.0, F32)
    lo, hi = lax.fori_loop(0, 8, step(lambda a, b: jnp.sqrt(a * b)), (lo, hi))
    lo, hi = lax.fori_loop(0, 28, step(lambda a, b: a + (b - a) * 0.5), (lo, hi))
    kth = jnp.max(jnp.where(aff < hi, aff, 0.0), axis=1, keepdims=True)
    thr = jnp.where(count_ge(lo) >= float(CAPACITY), kth, 0.0)

    tri = tri_ref[...]
    blk = blk_ref[...]

    def stack(x):
        return jnp.concatenate([x[:, LANES * b:LANES * (b + 1)] for b in range(N_TOK_BLOCKS)], axis=0)

    def unstack(x):
        return jnp.concatenate([x[N_EXPERTS * b:N_EXPERTS * (b + 1), :] for b in range(N_TOK_BLOCKS)], axis=1)

    def cumsum_stacked(mask_st):
        win = _dot(mask_st.astype(BF16), tri)
        tot = win[:, LANES - 1:LANES]
        off = _dot(blk, jnp.broadcast_to(tot, (STACK_ROWS, LANES)).astype(BF16))[:, :1]
        return win, off, tot

    gt = aff > thr
    eq = aff == thr
    need = float(CAPACITY) - jnp.sum(gt.astype(F32), axis=1, keepdims=True)
    eq_f = eq.astype(F32)
    ewin, eoff, _ = cumsum_stacked(stack(eq_f))
    eq_before = unstack(ewin + eoff) - eq_f
    sel = jnp.logical_or(gt, jnp.logical_and(eq, eq_before < need))
    cwin, coff, ctot = cumsum_stacked(stack(sel.astype(F32)))
    cwin_ref[...] = cwin
    aff_ref[...] = stack(aff)
    cend_ref[...] = jnp.broadcast_to(coff + ctot, (STACK_ROWS, LANES))
    ctot_ref[...] = jnp.broadcast_to(ctot, (STACK_ROWS, LANES))

    blk_id = lax.broadcasted_iota(jnp.int32, (N_TOK_BLOCKS, 1), 0).astype(F32)
    slot = lax.broadcasted_iota(jnp.int32, (1, CAPACITY), 1).astype(F32)
    lane_f = lax.broadcasted_iota(jnp.int32, (LANES, 1), 0).astype(F32)

    def per_expert(ex):
        rows = pl.ds(ex, N_TOK_BLOCKS, stride=N_EXPERTS)
        cend = cend_ref[rows, :][:, :1]
        ctot_e = ctot_ref[rows, :][:, :1]
        before = cend <= slot
        nblk = jnp.sum(before.astype(F32), axis=0, keepdims=True)
        base = jnp.sum(jnp.where(before, ctot_e, 0.0), axis=0, keepdims=True)
        pick = jnp.where(blk_id == nblk, 1.0, 0.0).astype(BF16)
        cnt_in = _dot_tn(cwin_ref[rows, :].astype(BF16), pick)
        lane_idx = jnp.sum((cnt_in <= slot - base).astype(F32), axis=0, keepdims=True)
        idx_ref[pl.ds(ex, 1), :] = ((nblk * float(LANES) + lane_idx) * float(TOK_SLABS)).astype(jnp.int32)
        a_hi, a_mid, a_lo = _split3(aff_ref[rows, :])
        aff_blk = _dot_tn(a_hi, pick) + _dot_tn(a_mid, pick) + _dot_tn(a_lo, pick)
        gate_ref[pl.ds(ex, 1), :] = jnp.sum(jnp.where(lane_f == lane_idx, aff_blk, 0.0), axis=0, keepdims=True)

    def expert_pair(i, carry):
        per_expert(2 * i)
        per_expert(2 * i + 1)
        return carry

    lax.fori_loop(0, N_EXPERTS // 2, expert_pair, 0)


def _cumsum_consts():
    l = np.arange(LANES)
    tri = (l[:, None] <= l[None, :]).astype(np.float32)
    r = np.arange(STACK_ROWS)
    same = (r[:, None] % N_EXPERTS) == (r[None, :] % N_EXPERTS)
    earlier = (r[None, :] // N_EXPERTS) < (r[:, None] // N_EXPERTS)
    blk = (same & earlier).astype(np.float32)
    return jnp.asarray(tri, BF16), jnp.asarray(blk, BF16)


def _router(logits_t, token_set, tri, blk, name):
    return pl.pallas_call(
        _router_kernel,
        grid=(1,),
        in_specs=[
            pl.BlockSpec((N_EXPERTS, N_TOK), lambda i: (0, token_set)),
            _full_spec((LANES, LANES)),
            _full_spec((STACK_ROWS, STACK_ROWS)),
        ],
        out_specs=[_full_spec((N_EXPERTS, CAPACITY))] * 2,
        out_shape=[jax.ShapeDtypeStruct((N_EXPERTS, CAPACITY), jnp.int32),
                   jax.ShapeDtypeStruct((N_EXPERTS, CAPACITY), F32)],
        scratch_shapes=[pltpu.VMEM((STACK_ROWS, LANES), F32)] * 4,
        compiler_params=_cp(48),
        name=name,
    )(logits_t, tri, blk)


GATHER_UNROLL = 16
ROWS_PER_EXPERT = 2 * CAPACITY
GATHER_PER_STEP = ROWS_PER_EXPERT // (D_EXPERT // FF_TILE)


def _ffn_kernel(off0_ref, offn_ref, hs_ref, gc_ref, gl_ref, w1_ref, w3_ref, w2_ref, yc_ref, yl_ref,
                tile_ref, xe_ref, acc_ref, sem):
    e = pl.program_id(0)
    f = pl.program_id(1)
    last_e = pl.num_programs(0) - 1
    last_f = pl.num_programs(1) - 1

    def row_copy(off_ref, row):
        off = pl.multiple_of(off_ref[0, row], TOK_SLABS)
        return pltpu.make_async_copy(hs_ref.at[pl.ds(off, TOK_SLABS)],
                                     tile_ref.at[pl.ds(row * TOK_SLABS, TOK_SLABS)], sem)

    def wait_tile():
        pltpu.make_async_copy(hs_ref.at[pl.ds(0, ROWS_PER_EXPERT * TOK_SLABS)], tile_ref, sem).wait()

    @pl.when(jnp.logical_and(e == 0, f == 0))
    def _():
        def issue(i, carry):
            for u in range(GATHER_UNROLL):
                row_copy(off0_ref, i * GATHER_UNROLL + u).start()
            return carry
        lax.fori_loop(0, ROWS_PER_EXPERT // GATHER_UNROLL, issue, 0)

    @pl.when(f == 0)
    def _():
        wait_tile()
        xe_ref[...] = _from_slabs(tile_ref, ROWS_PER_EXPERT).astype(BF16)
        acc_ref[...] = jnp.zeros_like(acc_ref)

    for u in range(GATHER_PER_STEP):
        row_copy(offn_ref, f * GATHER_PER_STEP + u).start()

    x = xe_ref[...]
    h1 = _dot(x, w1_ref[...].astype(BF16))
    h3 = _dot(x, w3_ref[...].astype(BF16))
    hid = (h1 * jax.nn.sigmoid(h1) * h3).astype(BF16)
    acc_ref[...] += _dot(hid, w2_ref[...].astype(BF16))

    @pl.when(f == last_f)
    def _():
        _to_slabs(yc_ref, acc_ref[:CAPACITY, :] * gc_ref[...], CAPACITY)
        _to_slabs(yl_ref, acc_ref[CAPACITY:, :] * gl_ref[...], CAPACITY)

    @pl.when(jnp.logical_and(e == last_e, f == last_f))
    def _():
        wait_tile()


def _expert_ffn(offsets, hs, gate_ctx, gate_lat, w1, w3, w2, layer, name):
    nf = D_EXPERT // FF_TILE
    off_block = (None, 1, ROWS_PER_EXPERT)
    g_spec = pl.BlockSpec((None, CAPACITY, 1), lambda e, f: (e, 0, 0))
    y_spec = pl.BlockSpec((None, CAPACITY * TOK_SLABS, LANES), lambda e, f: (e, 0, 0))
    y_shape = jax.ShapeDtypeStruct((N_EXPERTS, CAPACITY * TOK_SLABS, LANES), F32)
    return pl.pallas_call(
        _ffn_kernel,
        grid=(N_EXPERTS, nf),
        in_specs=[
            pl.BlockSpec(off_block, lambda e, f: (0, 0, 0), memory_space=pltpu.SMEM),
            pl.BlockSpec(off_block, lambda e, f: (jnp.minimum(e + 1, N_EXPERTS - 1), 0, 0), memory_space=pltpu.SMEM),
            pl.BlockSpec(memory_space=pl.ANY),
            g_spec, g_spec,
            pl.BlockSpec((None, None, D_MODEL, FF_TILE), lambda e, f: (layer, e, 0, f)),
            pl.BlockSpec((None, None, D_MODEL, FF_TILE), lambda e, f: (layer, e, 0, f)),
            pl.BlockSpec((None, None, FF_TILE, D_MODEL), lambda e, f: (layer, e, f, 0)),
        ],
        out_specs=[y_spec, y_spec],
        out_shape=[y_shape, y_shape],
        scratch_shapes=[
            pltpu.VMEM((ROWS_PER_EXPERT * TOK_SLABS, LANES), F32),
            pltpu.VMEM((ROWS_PER_EXPERT, D_MODEL), BF16),
            pltpu.VMEM((ROWS_PER_EXPERT, D_MODEL), F32),
            pltpu.SemaphoreType.DMA(()),
        ],
        compiler_params=_cp(56),
        name=name,
    )(offsets, offsets, hs, gate_ctx, gate_lat, w1, w3, w2)


SCATTER_UNROLL = 16


def _scatter_expert(off_ref, ye_ref, y_ref, ex):
    def body(i, carry):
        upd = []
        for u in range(SCATTER_UNROLL):
            s = i * SCATTER_UNROLL + u
            rows = pl.ds(pl.multiple_of(off_ref[ex * CAPACITY + s], TOK_SLABS), TOK_SLABS)
            src = ye_ref[pl.ds(pl.multiple_of(s * TOK_SLABS, TOK_SLABS), TOK_SLABS), :]
            upd.append((rows, y_ref[rows, :] + src))
        for rows, val in upd:
            y_ref[rows, :] = val
        return carry

    lax.fori_loop(0, CAPACITY // SCATTER_UNROLL, body, 0)


def _combine_kernel(idx_ref, ye_ref, y_ref):
    ex = pl.program_id(0)

    @pl.when(ex == 0)
    def _():
        y_ref[...] = jnp.zeros_like(y_ref)

    _scatter_expert(idx_ref, ye_ref, y_ref, ex)


def _combine_final_kernel(idx_ref, ye_ref, xp_ref, gp_ref, fg_ref, o_ref, y_ref):
    i = pl.program_id(0)

    @pl.when(i == 0)
    def _():
        y_ref[...] = jnp.zeros_like(y_ref)

    @pl.when(i < N_EXPERTS)
    def _():
        _scatter_expert(idx_ref, ye_ref, y_ref, i)

    @pl.when(i >= N_EXPERTS)
    def _():
        slabs_per_tile = ROW_TILE * TOK_SLABS
        base = pl.multiple_of((i - N_EXPERTS) * slabs_per_tile, slabs_per_tile)
        x = xp_ref[...] + gp_ref[...] * _from_slabs(y_ref, ROW_TILE, base)
        o_ref[...] = _rmsnorm(x, fg_ref[...])


def _combine_final(idx_flat, ye, x_prev, token_set, mods_prev, final_g, name):
    def tile(i):
        return jnp.maximum(i - N_EXPERTS, 0)

    def gate_row(i, idx):
        return (_tile_group(token_set * N_ROW_TILES + tile(i)) * 6 + 5, 0, 0)

    return pl.pallas_call(
        _combine_final_kernel,
        grid_spec=pltpu.PrefetchScalarGridSpec(
            num_scalar_prefetch=1,
            grid=(N_EXPERTS + N_ROW_TILES,),
            in_specs=[
                pl.BlockSpec((None, CAPACITY * TOK_SLABS, LANES), lambda i, idx: (jnp.minimum(i, N_EXPERTS - 1), 0, 0)),
                pl.BlockSpec((ROW_TILE, D_MODEL), lambda i, idx: (token_set * N_ROW_TILES + tile(i), 0)),
                pl.BlockSpec((None, 1, D_MODEL), gate_row),
                pl.BlockSpec((1, D_MODEL), lambda i, idx: (0, 0)),
            ],
            out_specs=pl.BlockSpec((ROW_TILE, D_MODEL), lambda i, idx: (tile(i), 0)),
            scratch_shapes=[pltpu.VMEM((N_TOK * TOK_SLABS, LANES), F32)],
        ),
        out_shape=jax.ShapeDtypeStruct((N_TOK, D_MODEL), F32),
        compiler_params=_cp(40),
        name=name,
    )(idx_flat, ye, x_prev, mods_prev, final_g)


def _combine(idx_flat, ye, name):
    return pl.pallas_call(
        _combine_kernel,
        grid_spec=pltpu.PrefetchScalarGridSpec(
            num_scalar_prefetch=1,
            grid=(N_EXPERTS,),
            in_specs=[pl.BlockSpec((None, CAPACITY * TOK_SLABS, LANES), lambda e, idx: (e, 0, 0))],
            out_specs=pl.BlockSpec((N_TOK * TOK_SLABS, LANES), lambda e, idx: (0, 0)),
        ),
        out_shape=jax.ShapeDtypeStruct((N_TOK * TOK_SLABS, LANES), F32),
        compiler_params=_cp(56),
        name=name,
    )(idx_flat, ye)


def _moe(hs, lt, tri, blk, w1, w3, w2, layer):
    tag = "l%d" % layer
    off_c, g_c = _router(lt, 0, tri, blk, "router_%s_ctx" % tag)
    off_l, g_l = _router(lt, 1, tri, blk, "router_%s_lat" % tag)
    offsets = jnp.concatenate([off_c, off_l + N_TOK * TOK_SLABS], axis=1).reshape(N_EXPERTS, 1, ROWS_PER_EXPERT)
    ye_c, ye_l = _expert_ffn(offsets, hs, g_c.reshape(N_EXPERTS, CAPACITY, 1), g_l.reshape(N_EXPERTS, CAPACITY, 1),
                             w1, w3, w2, layer, "ffn_" + tag)
    return (off_c.reshape(N_EXPERTS * CAPACITY), ye_c), (off_l.reshape(N_EXPERTS * CAPACITY), ye_l)


def _gla_proj_kernel(xp_ref, yc_ref, yl_ref, gp_ref, g_ref, sh_ref, sc_ref, w_ref, wt_ref, wg_ref, bg_ref,
                     x_ref, q_ref, k_ref, v_ref, r_ref, lgf_ref, lgb_ref, wb_ref):
    _cast_once(w_ref, wb_ref)

    def body(latent):
        yt_ref = yl_ref if latent else yc_ref
        x = xp_ref[...] + gp_ref[...] * _from_slabs(yt_ref, ROW_TILE)
        x_ref[...] = x
        h = _modulate(x, g_ref[...], sh_ref[...], sc_ref[...]).astype(BF16)
        z = _dot(h, wb_ref[...])
        nq = GLA_HEADS * GLA_DK
        nv = GLA_HEADS * GLA_DV
        q_ref[...] = z[:, :nq] * (GLA_DK ** -0.5)
        k_ref[...] = z[:, nq:2 * nq]
        v_ref[...] = z[:, 2 * nq:2 * nq + nv].astype(BF16)
        r_ref[...] = z[:, 2 * nq + nv:].astype(BF16)
        zg = _dot(h, wt_ref[...]).astype(BF16)
        a = _dot(zg, wg_ref[...]) + bg_ref[...]
        ls = (jnp.minimum(a, 0.0) - jnp.log1p(jnp.exp(-jnp.abs(a)))) * (1.0 / GLA_GATE_NORM)
        lgf_ref[...] = ls[:, :nq]
        lgb_ref[...] = ls[:, nq:]

    _per_set(body)


def _gla_proj(x_prev, y_ctx, y_lat, mods_prev, mods, norm_g, w_in, w_tail, wg, bg):
    nq = GLA_HEADS * GLA_DK
    nv = GLA_HEADS * GLA_DV
    n_main = 2 * nq + 2 * nv
    slabs = ROW_TILE * TOK_SLABS
    return pl.pallas_call(
        _gla_proj_kernel,
        grid=(N_TILES,),
        in_specs=[
            _row_spec(D_MODEL),
            pl.BlockSpec((slabs, LANES), lambda i: (jnp.minimum(i, N_ROW_TILES - 1), 0)),
            pl.BlockSpec((slabs, LANES), lambda i: (jnp.maximum(i - N_ROW_TILES, 0), 0)),
            _mod_spec(5),
            _full_spec((1, D_MODEL)),
            _mod_spec(0),
            _mod_spec(1),
            _weight_spec(D_MODEL, n_main),
            _full_spec((D_MODEL, LANES)),
            _full_spec((LANES, 2 * nq)),
            _full_spec((1, 2 * nq)),
        ],
        out_specs=[_row_spec(D_MODEL), _row_spec(nq), _row_spec(nq), _row_spec(nv), _row_spec(nv),
                   _row_spec(nq), _row_spec(nq)],
        out_shape=[
            jax.ShapeDtypeStruct((N_ALL, D_MODEL), F32),
            jax.ShapeDtypeStruct((N_ALL, nq), F32),
            jax.ShapeDtypeStruct((N_ALL, nq), F32),
            jax.ShapeDtypeStruct((N_ALL, nv), BF16),
            jax.ShapeDtypeStruct((N_ALL, nv), BF16),
            jax.ShapeDtypeStruct((N_ALL, nq), F32),
            jax.ShapeDtypeStruct((N_ALL, nq), F32),
        ],
        scratch_shapes=[pltpu.VMEM((D_MODEL, n_main), BF16)],
        compiler_params=_cp(56),
        name="gla_proj",
    )(x_prev, y_ctx, y_lat, mods_prev, norm_g, mods, mods, w_in, w_tail, wg, bg)


GLA_LEVELS = (32, 16, 8)
GLA_UNROLL = 8
GLA_GROUP = 4 * GLA_CHUNK
GLA_SAFE_DECAY = 60.0
NEG_BIG = -1e30


def _gla_intra_fast(qe, kc, b2, fwd):
    c = GLA_CHUNK
    row = lax.broadcasted_iota(jnp.int32, (c, 1), 0)
    col = lax.broadcasted_iota(jnp.int32, (1, c), 1)
    keep = (col <= row) if fwd else (col >= row)
    kq = (kc * jnp.exp2(-b2)).astype(BF16)
    return jnp.where(keep, _dot_nt(qe, kq), 0.0)


def _gla_intra_robust(qc, kc, b2, fwd):
    c = GLA_CHUNK
    row = lax.broadcasted_iota(jnp.int32, (c, 1), 0)
    col = lax.broadcasted_iota(jnp.int32, (1, c), 1)
    a = jnp.zeros((c, c), F32)
    for g in GLA_LEVELS:
        odd = ((row >> int(math.log2(g))) & 1) == 1
        later = odd if fwd else jnp.logical_not(odd)
        refs = []
        for p in range(c // (2 * g)):
            r0 = 2 * g * p + (g - 1 if fwd else g)
            refs.append(jnp.broadcast_to(b2[r0:r0 + 1], (2 * g, GLA_DK)))
        ref = jnp.concatenate(refs, axis=0) if len(refs) > 1 else refs[0]
        qt = qc * jnp.exp2(jnp.where(later, b2 - ref, NEG_BIG))
        kt = kc * jnp.exp2(jnp.where(later, NEG_BIG, ref - b2))
        same_parent = (row >> int(math.log2(2 * g))) == (col >> int(math.log2(2 * g)))
        a = a + jnp.where(same_parent, _dot_nt(qt.astype(BF16), kt.astype(BF16)), 0.0)

    sub = lax.broadcasted_iota(jnp.int32, (SUBLANES, 1), 0)
    strips = []
    for blk in range(c // SUBLANES):
        r0 = SUBLANES * blk
        qb = qc[r0:r0 + SUBLANES]
        bb = b2[r0:r0 + SUBLANES]
        strip = jnp.zeros((SUBLANES, c), F32)
        for jj in range(SUBLANES):
            j = r0 + jj
            cond = (sub >= jj) if fwd else (sub <= jj)
            t = qb * kc[j:j + 1] * jnp.exp2(jnp.where(cond, bb - b2[j:j + 1], NEG_BIG))
            strip = jnp.where(col == j, jnp.sum(t, axis=1, keepdims=True), strip)
        strips.append(strip)
    return a + jnp.concatenate(strips, axis=0)


def _gla_state_increment(kc, vc, b2f, b2b):
    c = GLA_CHUNK
    btf = b2f[c - 1:c]
    btb = b2b[0:1]
    kd = jnp.concatenate([(kc * jnp.exp2(btf - b2f)).astype(BF16), (kc * jnp.exp2(btb - b2b)).astype(BF16)], axis=1)
    return _dot_tn(vc, kd), jnp.concatenate([jnp.exp2(btf), jnp.exp2(btb)], axis=1)


def _gla_scores(qc, kc, b2f, b2b, fast):
    qe_f = (qc * jnp.exp2(b2f)).astype(BF16)
    qe_b = (qc * jnp.exp2(b2b)).astype(BF16)
    if fast:
        a = _gla_intra_fast(qe_f, kc, b2f, True) + _gla_intra_fast(qe_b, kc, b2b, False)
    else:
        a = _gla_intra_robust(qc, kc, b2f, True) + _gla_intra_robust(qc, kc, b2b, False)
    return jnp.concatenate([qe_f, qe_b], axis=1), a.astype(BF16)


def _gla_kernel(q_ref, k_ref, v_ref, r_ref, gf_ref, gb_ref, *refs, t_len, heads, zero_state):
    if zero_state:
        s0f_ref = s0b_ref = None
    else:
        s0f_ref, s0b_ref = refs[:2]
        refs = refs[2:]
    go_ref, trif_ref, trib_ref, u_ref, sf_ref, sb_ref = refs[:6]
    scratch = refs[6:]
    _gla_passes(q_ref, k_ref, v_ref, r_ref, gf_ref, gb_ref, s0f_ref, s0b_ref, go_ref, trif_ref, trib_ref,
                u_ref, sf_ref, sb_ref, *scratch, t_len=t_len, heads=heads)


def _gla_passes(q_ref, k_ref, v_ref, r_ref, gf_ref, gb_ref, s0f_ref, s0b_ref, go_ref, trif_ref, trib_ref,
                u_ref, sf_ref, sb_ref, *scratch, t_len, heads):
    c = GLA_CHUNK
    n = t_len // c
    per_head = len(scratch) // heads
    hd = []
    for hh in range(heads):
        dk = slice(hh * GLA_DK, (hh + 1) * GLA_DK)
        dv = slice(hh * GLA_DV, (hh + 1) * GLA_DV)
        b2f, b2b, qe, oi, kv, dec, sst, st = scratch[hh * per_head:(hh + 1) * per_head]
        hd.append(dict(q=q_ref.at[:, dk], k=k_ref.at[:, dk], v=v_ref.at[:, dv], r=r_ref.at[:, dv],
                       gf=gf_ref.at[:, dk], gb=gb_ref.at[:, dk],
                       s0f=None if s0f_ref is None else s0f_ref.at[hh],
                       s0b=None if s0b_ref is None else s0b_ref.at[hh],
                       u=u_ref.at[:, dv], sf=sf_ref.at[hh], sb=sb_ref.at[hh],
                       b2f=b2f, b2b=b2b, qe=qe, oi=oi, kv=kv, dec=dec, sst=sst, st=st))

    def rows_of(ci):
        return pl.ds(pl.multiple_of(ci * c, c), c)

    def state_rows_of(ci):
        return pl.ds(pl.multiple_of(ci * GLA_DV, GLA_DV), GLA_DV)

    n_groups = t_len // GLA_GROUP
    cum_unroll = 2 if n_groups % 2 == 0 else 1

    def cum_body(i, carry):
        sums = []
        for h in hd:
            for u in range(cum_unroll):
                rows = pl.ds(pl.multiple_of((i * cum_unroll + u) * GLA_GROUP, GLA_GROUP), GLA_GROUP)
                for g_ref, tri_ref, b_ref in ((h["gf"], trif_ref, h["b2f"]), (h["gb"], trib_ref, h["b2b"])):
                    s3 = _dot(tri_ref[...], jnp.concatenate(_split3(g_ref[rows, :]), axis=1))
                    sums.append((b_ref, rows, s3))
        for b_ref, rows, s3 in sums:
            b_ref[rows, :] = (s3[:, :GLA_DK] + s3[:, GLA_DK:2 * GLA_DK] + s3[:, 2 * GLA_DK:]) * LOG2E
        return carry

    lax.fori_loop(0, n_groups // cum_unroll, cum_body, 0)

    def chunk_total(g_ref):
        return jnp.min(jnp.sum(g_ref[...].reshape(n, c, GLA_DK), axis=1))

    totals = [chunk_total(h[g]) for h in hd for g in ("gf", "gb")]
    safe = functools.reduce(jnp.minimum, totals) >= -GLA_SAFE_DECAY

    def make_local(fast, unroll):
        def body(i, carry):
            work = [(h, i * unroll + u) for h in hd for u in range(unroll)]
            for h, ci in work:
                rows = rows_of(ci)
                kv, dec = _gla_state_increment(h["k"][rows, :], h["v"][rows, :], h["b2f"][rows, :], h["b2b"][rows, :])
                h["kv"][state_rows_of(ci), :] = kv
                h["dec"][pl.ds(ci, 1), :] = dec
            scores = []
            for h, ci in work:
                rows = rows_of(ci)
                qe, a = _gla_scores(h["q"][rows, :], h["k"][rows, :], h["b2f"][rows, :], h["b2b"][rows, :], fast)
                h["qe"][rows, :] = qe
                scores.append(a)
            for (h, ci), a in zip(work, scores):
                rows = rows_of(ci)
                h["oi"][rows, :] = _dot(a, h["v"][rows, :])
            return carry
        return body

    unroll = min(GLA_UNROLL, n)

    @pl.when(safe)
    def _():
        lax.fori_loop(0, n // unroll, make_local(True, unroll), 0)

    @pl.when(jnp.logical_not(safe))
    def _():
        lax.fori_loop(0, n, make_local(False, 1), 0)

    fl = slice(0, GLA_DK)
    bl = slice(GLA_DK, 2 * GLA_DK)
    for h in hd:
        if h["s0f"] is None:
            h["st"][...] = jnp.zeros_like(h["st"])
        else:
            h["st"][...] = jnp.concatenate([h["s0f"][...].T, h["s0b"][...].T], axis=1)

    def rec_body(i, carry):
        cf = i
        cb = n - 1 - i
        for h in hd:
            st = h["st"][...]
            h["sst"][state_rows_of(cf), fl] = st[:, fl].astype(BF16)
            h["sst"][state_rows_of(cb), bl] = st[:, bl].astype(BF16)
            dec = jnp.concatenate([h["dec"][pl.ds(cf, 1), :][:, fl], h["dec"][pl.ds(cb, 1), :][:, bl]], axis=1)
            kv = jnp.concatenate([h["kv"][state_rows_of(cf), fl], h["kv"][state_rows_of(cb), bl]], axis=1)
            h["st"][...] = st * dec + kv
        return carry

    lax.fori_loop(0, n, rec_body, 0)
    for h in hd:
        h["sf"][...] = h["st"][:, fl].T
        h["sb"][...] = h["st"][:, bl].T

    go = go_ref[...]

    def out_body(i, carry):
        work = [(h, i * unroll + u) for h in hd for u in range(unroll)]
        inter = [_dot_nt(h["qe"][rows_of(ci), :], h["sst"][state_rows_of(ci), :]) for h, ci in work]
        for (h, ci), o_inter in zip(work, inter):
            rows = rows_of(ci)
            r = h["r"][rows, :].astype(F32)
            h["u"][rows, :] = (_rmsnorm(h["oi"][rows, :] + o_inter, go) * (r * jax.nn.sigmoid(r))).astype(BF16)
        return carry

    lax.fori_loop(0, n // unroll, out_body, 0)


def _gla_tri():
    i = np.arange(GLA_GROUP)
    same = (i[:, None] // GLA_CHUNK) == (i[None, :] // GLA_CHUNK)
    fwd = (same & (i[None, :] <= i[:, None])).astype(np.float32)
    bwd = (same & (i[None, :] >= i[:, None])).astype(np.float32)
    return jnp.asarray(fwd, BF16), jnp.asarray(bwd, BF16)


def _gla(q, k, v, r, lgf, lgb, batch0, s0f, s0b, g_out, n_batch, t_len, heads, name):
    zero_state = s0f is None
    trif, trib = _gla_tri()
    n_chunks = t_len // GLA_CHUNK
    qk_spec = pl.BlockSpec((None, t_len, heads * GLA_DK), lambda b, h: (batch0 + b, 0, h))
    vin_spec = pl.BlockSpec((None, t_len, heads * GLA_DV), lambda b, h: (batch0 + b, 0, h))
    v_spec = pl.BlockSpec((None, t_len, heads * GLA_DV), lambda b, h: (b, 0, h))
    s_spec = pl.BlockSpec((None, heads, GLA_DK, GLA_DV), lambda b, h: (b, h, 0, 0))
    const = lambda shape: pl.BlockSpec(shape, lambda b, h: (0,) * len(shape))
    s_shape = jax.ShapeDtypeStruct((n_batch, GLA_HEADS, GLA_DK, GLA_DV), F32)
    states = [] if zero_state else [s0f, s0b]
    return pl.pallas_call(
        functools.partial(_gla_kernel, t_len=t_len, heads=heads, zero_state=zero_state),
        grid=(n_batch, GLA_HEADS // heads),
        in_specs=[qk_spec, qk_spec, vin_spec, vin_spec, qk_spec, qk_spec] + [s_spec] * len(states)
        + [const((1, GLA_DV)), const((GLA_GROUP, GLA_GROUP)), const((GLA_GROUP, GLA_GROUP))],
        out_specs=[v_spec, s_spec, s_spec],
        out_shape=[jax.ShapeDtypeStruct((n_batch, t_len, GLA_HEADS * GLA_DV), BF16), s_shape, s_shape],
        scratch_shapes=[
            pltpu.VMEM((t_len, GLA_DK), F32), pltpu.VMEM((t_len, GLA_DK), F32),
            pltpu.VMEM((t_len, 2 * GLA_DK), BF16),
            pltpu.VMEM((t_len, GLA_DV), F32),
            pltpu.VMEM((n_chunks * GLA_DV, 2 * GLA_DK), F32),
            pltpu.VMEM((max(n_chunks, SUBLANES), 2 * GLA_DK), F32),
            pltpu.VMEM((n_chunks * GLA_DV, 2 * GLA_DK), BF16),
            pltpu.VMEM((GLA_DV, 2 * GLA_DK), F32),
        ] * heads,
        compiler_params=_cp(48),
        name=name,
    )(q, k, v, r, lgf, lgb, *states, g_out, trif, trib)


def kernel(x_prompt, x_sample, cache_k, cache_v, state_fwd, state_bwd, c, c_ctx, w_mod, b_mod, norm_g,
           da_w_in, da_w_out, da_lam_q1, da_lam_k1, da_lam_q2, da_lam_k2, da_g_sub, gla_w_in, gla_w_gf2,
           gla_b_gf, gla_w_gb2, gla_b_gb, gla_g_out, gla_w_out, moe_w_router, moe_w1, moe_w3, moe_w2, final_g):
    n_ctx_b = x_prompt.shape[0]
    n_lat_b = x_sample.shape[0]
    x_ctx = x_prompt.reshape(N_TOK, D_MODEL)
    x_lat = x_sample.reshape(N_TOK, D_MODEL)

    cvec = jnp.zeros((SUBLANES, D_MODEL), F32).at[0].set(c_ctx).at[1:1 + n_lat_b].set(c)
    mods = _mod_params(cvec, w_mod, b_mod)
    tri, blk = _cumsum_consts()
    wr_t = [jnp.zeros((LANES, D_MODEL), BF16).at[:N_EXPERTS].set(moe_w_router[i].T.astype(BF16))
            for i in range(DEPTH)]
    ng = norm_g.reshape(DEPTH, 2, 1, D_MODEL)

    lam_init = 0.8 - 0.6 * math.exp(-0.3 * 0)
    w_in = da_w_in
    w_out = da_w_out
    lam_vecs = [v[0].reshape(1, DA_DIM) for v in (da_lam_q1, da_lam_k1, da_lam_q2, da_lam_k2)]
    g_sub = da_g_sub[0].reshape(1, DA_VDIM)

    q, k, v, kf, vf = _da_proj(x_ctx, x_lat, mods[0], ng[0, 0], w_in, _rope_tables())
    past = cache_k.shape[2]
    ck = cache_k[:, 0].reshape(n_lat_b * past, D_MODEL)
    cv = cache_v[:, 0].reshape(n_lat_b * past, D_MODEL)
    u_ctx = _attention(lam_vecs, g_sub, q, 0, [(k, v, 0, SEQ)], n_ctx_b, SEQ, ATTN_Q_TILE, lam_init, "attn_ctx")
    u_lat = _attention(lam_vecs, g_sub, q, N_TOK, [(k, v, N_TOK, DEC_SEQ), (ck, cv, 0, past)], n_lat_b, DEC_SEQ,
                       ATTN_Q_TILE, lam_init, "attn_lat")

    x1, hs, lt = _outproj(u_ctx, u_lat, (x_ctx, x_lat), w_out, mods[0], ng[0, 1], wr_t[0], "outproj0")
    (idx_c, ye_c), (idx_l, ye_l) = _moe(hs, lt, tri, blk, moe_w1, moe_w3, moe_w2, 0)
    y_ctx = _combine(idx_c, ye_c, "combine_l0_ctx")
    y_lat = _combine(idx_l, ye_l, "combine_l0_lat")

    nq = GLA_HEADS * GLA_DK
    n_main = 2 * nq + 2 * GLA_HEADS * GLA_DV
    w_tail = jnp.zeros((D_MODEL, LANES), F32).at[:, :2 * GLA_GATE_RANK].set(gla_w_in[0][:, n_main:]).astype(BF16)
    wg = jnp.zeros((LANES, 2 * nq), F32)
    wg = wg.at[:GLA_GATE_RANK, :nq].set(gla_w_gf2[0]).at[GLA_GATE_RANK:2 * GLA_GATE_RANK, nq:].set(gla_w_gb2[0])
    wg = wg.astype(BF16)
    bg = jnp.concatenate([gla_b_gf[0], gla_b_gb[0]]).reshape(1, 2 * nq)
    w_out1 = gla_w_out
    g_out = gla_g_out[0].reshape(1, GLA_DV)

    x2, *gla_in = _gla_proj(x1, y_ctx, y_lat, mods[0], mods[1], ng[1, 0], gla_w_in, w_tail, wg, bg)

    def gla_side(n_b, t_len, s0f, s0b, tag):
        seqs = [a.reshape(N_ALL // t_len, t_len, a.shape[-1]) for a in gla_in]
        batch0 = 0 if s0f is None else N_TOK // t_len
        heads = GLA_HEADS if t_len <= GLA_GROUP else 1
        u, sf, sb = _gla(*seqs, batch0, s0f, s0b, g_out, n_b, t_len, heads, "gla_" + tag)
        return u.reshape(N_TOK, D_MODEL), sf, sb

    ug_ctx, sf, sb = gla_side(n_ctx_b, SEQ, None, None, "ctx")
    ug_lat, _, _ = gla_side(n_lat_b, DEC_SEQ, state_fwd[:, 0], state_bwd[:, 0], "lat")

    x3, hs, lt = _outproj(ug_ctx, ug_lat, (x2,), w_out1, mods[1], ng[1, 1], wr_t[1], "outproj1")
    (idx_c, ye_c), (idx_l, ye_l) = _moe(hs, lt, tri, blk, moe_w1, moe_w3, moe_w2, 1)

    fg = final_g.reshape(1, D_MODEL)
    y_prompt = _combine_final(idx_c, ye_c, x3, 0, mods[1], fg, "final_ctx").reshape(x_prompt.shape)
    y_sample = _combine_final(idx_l, ye_l, x3, 1, mods[1], fg, "final_lat").reshape(x_sample.shape)
    new_k = kf.reshape(n_ctx_b, 1, SEQ, 2 * DA_HEADS, DA_DIM)
    new_v = vf.reshape(n_ctx_b, 1, SEQ, DA_HEADS, DA_VDIM)
    return (y_prompt, y_sample, new_k, new_v, sf[:, None], sb[:, None])
```

```python
import functools
import math

import numpy as np
import jax
import jax.numpy as jnp
from jax import lax
from jax.experimental import pallas as pl
from jax.experimental.pallas import tpu as pltpu

F32 = jnp.float32
BF16 = jnp.bfloat16

D_MODEL = 1024
DEPTH = 2
SEQ = 256
DEC_SEQ = 2048
GRID_W = 64
N_TOK = 4096
DA_HEADS = 8
DA_DIM = 64
DA_VDIM = 128
ROPE_BASE = 10000.0
GLA_HEADS = 4
GLA_DK = 128
GLA_DV = 256
GLA_GATE_RANK = 16
GLA_GATE_NORM = 16.0
GLA_CHUNK = 64
N_EXPERTS = 16
CAPACITY = 512
D_EXPERT = 2048
EPS = 1e-6
F32_MIN_NORMAL = 2.0 ** -126
LOG2E = 1.4426950408889634

LANES = 128
SUBLANES = 8
ROW_TILE = 512
FF_TILE = 512
TOK_SLABS = D_MODEL // LANES
MIB = 1024 * 1024


def _cp(vmem_mib, sem=None):
    return pltpu.CompilerParams(vmem_limit_bytes=vmem_mib * MIB, dimension_semantics=sem)


def _dot(a, b):
    return jnp.dot(a, b, preferred_element_type=F32)


def _dot_nt(a, b):
    return lax.dot_general(a, b, (((1,), (1,)), ((), ())), preferred_element_type=F32)


def _dot_tn(a, b):
    return lax.dot_general(a, b, (((0,), (0,)), ((), ())), preferred_element_type=F32)


def _rmsnorm(x, g):
    return x * lax.rsqrt(jnp.mean(x * x, axis=-1, keepdims=True) + EPS) * g


def _modulate(x, g, shift, scale):
    return _rmsnorm(x, g) * (1.0 + scale) + shift


def _split3(x):
    hi = x.astype(BF16)
    r = x - hi.astype(F32)
    mid = r.astype(BF16)
    lo = (r - mid.astype(F32)).astype(BF16)
    return hi, mid, lo


def _from_slabs(ref, rows, base=0, pitch=TOK_SLABS):
    return jnp.concatenate([ref[pl.ds(base + c, rows, stride=pitch), :] for c in range(TOK_SLABS)], axis=1)


def _to_slabs(ref, val, rows):
    for c in range(TOK_SLABS):
        ref[pl.ds(c, rows, stride=TOK_SLABS), :] = val[:, LANES * c:LANES * (c + 1)]


def _mod_kernel(c_ref, w_ref, b_ref, o_ref):
    c = c_ref[...]
    s = c * jax.nn.sigmoid(c)
    w = w_ref[...]
    s_hi = s.astype(BF16)
    s_lo = (s - s_hi.astype(F32)).astype(BF16)
    w_hi = w.astype(BF16)
    w_lo = (w - w_hi.astype(F32)).astype(BF16)
    o_ref[...] = _dot(s_hi, w_hi) + _dot(s_hi, w_lo) + _dot(s_lo, w_hi) + b_ref[...]


def _mod_params(cvec, w_mod, b_mod):
    n6 = 6 * D_MODEL
    out = pl.pallas_call(
        _mod_kernel,
        grid=(DEPTH, 6),
        in_specs=[
            pl.BlockSpec((SUBLANES, D_MODEL), lambda i, j: (0, 0)),
            pl.BlockSpec((None, D_MODEL, D_MODEL), lambda i, j: (i, 0, j)),
            pl.BlockSpec((None, 1, D_MODEL), lambda i, j: (i, 0, j)),
        ],
        out_specs=pl.BlockSpec((None, SUBLANES, D_MODEL), lambda i, j: (i, 0, j)),
        out_shape=jax.ShapeDtypeStruct((DEPTH, SUBLANES, n6), F32),
        compiler_params=_cp(32),
        name="mod_params",
    )(cvec, w_mod, b_mod.reshape(DEPTH, 1, n6))
    return out.reshape(DEPTH, SUBLANES * 6, 1, D_MODEL)


N_ROW_TILES = N_TOK // ROW_TILE
N_TILES = 2 * N_ROW_TILES
N_ALL = 2 * N_TOK


def _tile_group(i):
    return jnp.where(i < N_ROW_TILES, 0, 1 + (i - N_ROW_TILES) // (DEC_SEQ // ROW_TILE))


def _mod_spec(k):
    return pl.BlockSpec((None, 1, D_MODEL), lambda i: (_tile_group(i) * 6 + k, 0, 0))


def _row_spec(width):
    return pl.BlockSpec((ROW_TILE, width), lambda i: (i, 0))


def _ctx_row_spec(width):
    return pl.BlockSpec((ROW_TILE, width), lambda i: (jnp.minimum(i, N_ROW_TILES - 1), 0))


def _lat_row_spec(width):
    return pl.BlockSpec((ROW_TILE, width), lambda i: (jnp.maximum(i - N_ROW_TILES, 0), 0))


def _full_spec(shape):
    nd = len(shape)
    return pl.BlockSpec(shape, lambda i: (0,) * nd)


def _per_set(body):
    i = pl.program_id(0)

    @pl.when(i < N_ROW_TILES)
    def _():
        body(False)

    @pl.when(i >= N_ROW_TILES)
    def _():
        body(True)


def _weight_spec(rows, cols):
    return pl.BlockSpec((None, rows, cols), lambda i: (0, 0, 0), pipeline_mode=pl.Buffered(1))


def _cast_once(w_ref, wb_ref):
    @pl.when(pl.program_id(0) == 0)
    def _():
        wb_ref[...] = w_ref[...].astype(BF16)


def _da_proj_kernel(xc_ref, xl_ref, g_ref, sh_ref, sc_ref, w_ref, cos_ref, sin_ref,
                    q_ref, k_ref, v_ref, kf_ref, vf_ref, wb_ref):
    _cast_once(w_ref, wb_ref)

    def body(latent):
        x_ref = xl_ref if latent else xc_ref
        h = _modulate(x_ref[...], g_ref[...], sh_ref[...], sc_ref[...])
        z = _dot(h.astype(BF16), wb_ref[...])
        q = z[:, :D_MODEL]
        k = z[:, D_MODEL:2 * D_MODEL]
        v = z[:, 2 * D_MODEL:]
        if latent:
            reps = D_MODEL // LANES
            cos = jnp.concatenate([cos_ref[...]] * reps, axis=1)
            sin = jnp.concatenate([sin_ref[...]] * reps, axis=1)
            lane = lax.broadcasted_iota(jnp.int32, (1, D_MODEL), 1)
            first = (lane & 16) == 0

            def rot(t):
                partner = jnp.where(first, pltpu.roll(t, D_MODEL - 16, 1), pltpu.roll(t, 16, 1))
                return t * cos + partner * sin

            q = rot(q)
            k = rot(k)
        else:
            kf_ref[...] = k
            vf_ref[...] = v
        q_ref[...] = (q * (DA_DIM ** -0.5 * LOG2E)).astype(BF16)
        k_ref[...] = k.astype(BF16)
        v_ref[...] = v.astype(BF16)

    _per_set(body)


def _da_proj(x_ctx, x_lat, mods, norm_g, w_in, rope_tabs):
    tiles_per_batch = DEC_SEQ // ROW_TILE
    tab_spec = pl.BlockSpec((ROW_TILE, LANES), lambda i: (jnp.maximum(i - N_ROW_TILES, 0) % tiles_per_batch, 0))
    return pl.pallas_call(
        _da_proj_kernel,
        grid=(N_TILES,),
        in_specs=[
            _ctx_row_spec(D_MODEL),
            _lat_row_spec(D_MODEL),
            _full_spec((1, D_MODEL)),
            _mod_spec(0),
            _mod_spec(1),
            _weight_spec(D_MODEL, 3 * D_MODEL),
            tab_spec, tab_spec,
        ],
        out_specs=[_row_spec(D_MODEL)] * 3 + [_ctx_row_spec(D_MODEL)] * 2,
        out_shape=[jax.ShapeDtypeStruct((N_ALL, D_MODEL), BF16)] * 3 + [jax.ShapeDtypeStruct((N_TOK, D_MODEL), F32)] * 2,
        scratch_shapes=[pltpu.VMEM((D_MODEL, 3 * D_MODEL), BF16)],
        compiler_params=_cp(56),
        name="da_proj",
    )(x_ctx, x_lat, norm_g, mods, mods, w_in, *rope_tabs)


def _rope_tables():
    t = np.arange(DEC_SEQ)
    rows = (t // GRID_W).astype(np.float32)
    cols = (t % GRID_W).astype(np.float32)
    half = DA_DIM // 4
    freqs = (np.float32(ROPE_BASE) ** (-np.arange(half, dtype=np.float32) / np.float32(half))).astype(np.float32)
    ang_r = rows[:, None] * freqs
    ang_c = cols[:, None] * freqs
    cos64 = np.concatenate([np.cos(ang_r)] * 2 + [np.cos(ang_c)] * 2, axis=1)
    sin64 = np.concatenate([-np.sin(ang_r), np.sin(ang_r), -np.sin(ang_c), np.sin(ang_c)], axis=1)
    reps = LANES // DA_DIM
    return (jnp.asarray(np.concatenate([cos64] * reps, axis=1), F32),
            jnp.asarray(np.concatenate([sin64] * reps, axis=1), F32))


ATTN_Q_TILE = 256


def _attn_kernel(lq1_ref, lk1_ref, lq2_ref, lk2_ref, gs_ref, q_ref, *refs, lam_init, seg_lens):
    n_seg = len(seg_lens)
    kv_refs, o_ref = refs[:2 * n_seg], refs[2 * n_seg]
    lam =(jnp.exp(jnp.sum(lq1_ref[...] * lk1_ref[...], axis=-1, keepdims=True))
           - jnp.exp(jnp.sum(lq2_ref[...] * lk2_ref[...], axis=-1, keepdims=True)) + lam_init)
    lane = lax.broadcasted_iota(jnp.int32, (1, DA_VDIM), 1)
    first = lane < DA_DIM
    gs = gs_ref[...]
    tq = q_ref.shape[0]

    def softmax_pv(qq, ks, vs):
        ss = [_dot_nt(qq, kk) for kk in ks]
        m = functools.reduce(jnp.maximum, [jnp.max(s, axis=-1, keepdims=True) for s in ss])
        ps = [jnp.exp2(s - m) for s in ss]
        l = functools.reduce(jnp.add, [jnp.sum(p, axis=-1, keepdims=True) for p in ps])
        o = functools.reduce(jnp.add, [_dot(p.astype(BF16), vv) for p, vv in zip(ps, vs)])
        return o, l

    for h in range(DA_HEADS):
        sl = slice(DA_VDIM * h, DA_VDIM * (h + 1))
        qh = q_ref[:, sl]
        ks = [kv_refs[2 * s][:, sl].astype(BF16) for s in range(n_seg)]
        vs = [kv_refs[2 * s + 1][:, sl].astype(BF16) for s in range(n_seg)]
        zero = jnp.zeros_like(qh)
        qq = jnp.concatenate([jnp.where(first, qh, zero), jnp.where(first, zero, qh)], axis=0)
        oo, ll = softmax_pv(qq, ks, vs)
        o = oo[:tq] * (1.0 / ll[:tq]) - oo[tq:] * (lam / ll[tq:])
        o_ref[:, sl] = (_rmsnorm(o, gs) * (1.0 - lam_init)).astype(BF16)


def _attention(lam_vecs, g_sub, q, q_row0, kv_segs, n_batch, t_q, q_tile, lam_init, name):
    nq = t_q // q_tile
    q0 = q_row0 // q_tile
    vec_spec = pl.BlockSpec((1, DA_DIM), lambda b, i: (0, 0))
    kv_specs, kv_args = [], []
    for k, v, row0, t_k in kv_segs:
        kv_specs += [pl.BlockSpec((t_k, D_MODEL), lambda b, i, b0=row0 // t_k: (b0 + b, 0))] * 2
        kv_args += [k, v]
    seg_lens = tuple(t_k for _, _, _, t_k in kv_segs)
    return pl.pallas_call(
        functools.partial(_attn_kernel, lam_init=lam_init, seg_lens=seg_lens),
        grid=(n_batch, nq),
        in_specs=[vec_spec] * 4 + [
            pl.BlockSpec((1, DA_VDIM), lambda b, i: (0, 0)),
            pl.BlockSpec((q_tile, D_MODEL), lambda b, i: (q0 + b * nq + i, 0)),
        ] + kv_specs,
        out_specs=pl.BlockSpec((q_tile, D_MODEL), lambda b, i: (b * nq + i, 0)),
        out_shape=jax.ShapeDtypeStruct((n_batch * t_q, D_MODEL), BF16),
        compiler_params=_cp(56),
        name=name,
    )(*lam_vecs, g_sub, q, *kv_args)


def _outproj_kernel(*refs, x_stacked):
    uc_ref, ul_ref = refs[:2]
    n_x = 1 if x_stacked else 2
    x_refs = refs[2:2 + n_x]
    w_ref, gate_ref, g_ref, sh_ref, sc_ref, wr_ref, x1_ref, hs_ref, lt_ref, wb_ref = refs[2 + n_x:]
    _cast_once(w_ref, wb_ref)

    def body(latent):
        u_ref = ul_ref if latent else uc_ref
        x_ref = x_refs[0] if x_stacked else x_refs[int(latent)]
        x1 = x_ref[...] + gate_ref[...] * _dot(u_ref[...], wb_ref[...])
        x1_ref[...] = x1
        h2 = _modulate(x1, g_ref[...], sh_ref[...], sc_ref[...])
        _to_slabs(hs_ref, h2, ROW_TILE)
        lt_ref[...] = _dot_nt(wr_ref[...], h2.astype(BF16))

    _per_set(body)


def _outproj(u_ctx, u_lat, xs, w_out, mods, norm_g, wr_t, name):
    x_stacked = len(xs) == 1
    x_specs = [_row_spec(D_MODEL)] if x_stacked else [_ctx_row_spec(D_MODEL), _lat_row_spec(D_MODEL)]
    return pl.pallas_call(
        functools.partial(_outproj_kernel, x_stacked=x_stacked),
        grid=(N_TILES,),
        in_specs=[_ctx_row_spec(D_MODEL), _lat_row_spec(D_MODEL)] + x_specs + [
            _weight_spec(D_MODEL, D_MODEL),
            _mod_spec(2),
            _full_spec((1, D_MODEL)),
            _mod_spec(3),
            _mod_spec(4),
            _full_spec((LANES, D_MODEL)),
        ],
        out_specs=[
            _row_spec(D_MODEL),
            pl.BlockSpec((ROW_TILE * TOK_SLABS, LANES), lambda i: (i, 0)),
            pl.BlockSpec((LANES, ROW_TILE), lambda i: (0, i)),
        ],
        out_shape=[
            jax.ShapeDtypeStruct((N_ALL, D_MODEL), F32),
            jax.ShapeDtypeStruct((N_ALL * TOK_SLABS, LANES), F32),
            jax.ShapeDtypeStruct((LANES, N_ALL), F32),
        ],
        scratch_shapes=[pltpu.VMEM((D_MODEL, D_MODEL), BF16)],
        compiler_params=_cp(48),
        name=name,
    )(u_ctx, u_lat, *xs, w_out, mods, norm_g, mods, mods, wr_t)


N_TOK_BLOCKS = N_TOK // LANES
STACK_ROWS = N_TOK_BLOCKS * N_EXPERTS


def _router_kernel(lt_ref, tri_ref, blk_ref, idx_ref, gate_ref, cwin_ref, aff_ref, cend_ref, ctot_ref):
    lt = lt_ref[...]
    e = jnp.exp(lt - jnp.max(lt, axis=0, keepdims=True))
    aff = e / jnp.sum(e, axis=0, keepdims=True)
    aff = jnp.where(aff >= F32_MIN_NORMAL, aff, 0.0)

    def count_ge(x):
        return jnp.sum((aff >= x).astype(F32), axis=1, keepdims=True)

    def step(mid_of):
        def body(_, c):
            lo, hi = c
            mid = jnp.minimum(jnp.maximum(mid_of(lo, hi), lo), hi)
            ok = count_ge(mid) >= float(CAPACITY)
            return jnp.where(ok, mid, lo), jnp.where(ok, hi, mid)
        return body

    lo = jnp.full((N_EXPERTS, 1), F32_MIN_NORMAL, F32)
    hi = jnp.full((N_EXPERTS, 1), 2.0, F32)
    lo, hi = lax.fori_loop(0, 8, step(lambda a, b: jnp.sqrt(a * b)), (lo, hi))
    lo, hi = lax.fori_loop(0, 28, step(lambda a, b: a + (b - a) * 0.5), (lo, hi))
    kth = jnp.max(jnp.where(aff < hi, aff, 0.0), axis=1, keepdims=True)
    thr = jnp.where(count_ge(lo) >= float(CAPACITY), kth, 0.0)

    tri = tri_ref[...]
    blk = blk_ref[...]

    def stack(x):
        return jnp.concatenate([x[:, LANES * b:LANES * (b + 1)] for b in range(N_TOK_BLOCKS)], axis=0)

    def unstack(x):
        return jnp.concatenate([x[N_EXPERTS * b:N_EXPERTS * (b + 1), :] for b in range(N_TOK_BLOCKS)], axis=1)

    def cumsum_stacked(mask_st):
        win = _dot(mask_st.astype(BF16), tri)
        tot = win[:, LANES - 1:LANES]
        off = _dot(blk, jnp.broadcast_to(tot, (STACK_ROWS, LANES)).astype(BF16))[:, :1]
        return win, off, tot

    gt = aff > thr
    eq = aff == thr
    need = float(CAPACITY) - jnp.sum(gt.astype(F32), axis=1, keepdims=True)
    eq_f = eq.astype(F32)
    ewin, eoff, _ = cumsum_stacked(stack(eq_f))
    eq_before = unstack(ewin + eoff) - eq_f
    sel = jnp.logical_or(gt, jnp.logical_and(eq, eq_before < need))
    cwin, coff, ctot = cumsum_stacked(stack(sel.astype(F32)))
    cwin_ref[...] = cwin
    aff_ref[...] = stack(aff)
    cend_ref[...] = jnp.broadcast_to(coff + ctot, (STACK_ROWS, LANES))
    ctot_ref[...] = jnp.broadcast_to(ctot, (STACK_ROWS, LANES))

    blk_id = lax.broadcasted_iota(jnp.int32, (N_TOK_BLOCKS, 1), 0).astype(F32)
    slot = lax.broadcasted_iota(jnp.int32, (1, CAPACITY), 1).astype(F32)
    lane_f = lax.broadcasted_iota(jnp.int32, (LANES, 1), 0).astype(F32)

    def per_expert(ex):
        rows = pl.ds(ex, N_TOK_BLOCKS, stride=N_EXPERTS)
        cend = cend_ref[rows, :][:, :1]
        ctot_e = ctot_ref[rows, :][:, :1]
        before = cend <= slot
        nblk = jnp.sum(before.astype(F32), axis=0, keepdims=True)
        base = jnp.sum(jnp.where(before, ctot_e, 0.0), axis=0, keepdims=True)
        pick = jnp.where(blk_id == nblk, 1.0, 0.0).astype(BF16)
        cnt_in = _dot_tn(cwin_ref[rows, :].astype(BF16), pick)
        lane_idx = jnp.sum((cnt_in <= slot - base).astype(F32), axis=0, keepdims=True)
        idx_ref[pl.ds(ex, 1), :] = ((nblk * float(LANES) + lane_idx) * float(TOK_SLABS)).astype(jnp.int32)
        a_hi, a_mid, a_lo = _split3(aff_ref[rows, :])
        aff_blk = _dot_tn(a_hi, pick) + _dot_tn(a_mid, pick) + _dot_tn(a_lo, pick)
        gate_ref[pl.ds(ex, 1), :] = jnp.sum(jnp.where(lane_f == lane_idx, aff_blk, 0.0), axis=0, keepdims=True)

    def expert_pair(i, carry):
        per_expert(2 * i)
        per_expert(2 * i + 1)
        return carry

    lax.fori_loop(0, N_EXPERTS // 2, expert_pair, 0)


def _cumsum_consts():
    l = np.arange(LANES)
    tri = (l[:, None] <= l[None, :]).astype(np.float32)
    r = np.arange(STACK_ROWS)
    same = (r[:, None] % N_EXPERTS) == (r[None, :] % N_EXPERTS)
    earlier = (r[None, :] // N_EXPERTS) < (r[:, None] // N_EXPERTS)
    blk = (same & earlier).astype(np.float32)
    return jnp.asarray(tri, BF16), jnp.asarray(blk, BF16)


def _router(logits_t, token_set, tri, blk, name):
    return pl.pallas_call(
        _router_kernel,
        grid=(1,),
        in_specs=[
            pl.BlockSpec((N_EXPERTS, N_TOK), lambda i: (0, token_set)),
            _full_spec((LANES, LANES)),
            _full_spec((STACK_ROWS, STACK_ROWS)),
        ],
        out_specs=[_full_spec((N_EXPERTS, CAPACITY))] * 2,
        out_shape=[jax.ShapeDtypeStruct((N_EXPERTS, CAPACITY), jnp.int32),
                   jax.ShapeDtypeStruct((N_EXPERTS, CAPACITY), F32)],
        scratch_shapes=[pltpu.VMEM((STACK_ROWS, LANES), F32)] * 4,
        compiler_params=_cp(48),
        name=name,
    )(logits_t, tri, blk)


GATHER_UNROLL = 16
ROWS_PER_EXPERT = 2 * CAPACITY
GATHER_PER_STEP = ROWS_PER_EXPERT // (D_EXPERT // FF_TILE)
TILE_PITCH = TOK_SLABS + 1


def _ffn_kernel(off0_ref, offn_ref, hs_ref, gc_ref, gl_ref, w1_ref, w3_ref, w2_ref, yc_ref, yl_ref,
                tile_ref, xe_ref, acc_ref, sem):
    e = pl.program_id(0)
    f = pl.program_id(1)
    last_e = pl.num_programs(0) - 1
    last_f = pl.num_programs(1) - 1

    def row_copy(off_ref, row):
        off = pl.multiple_of(off_ref[0, row], TOK_SLABS)
        return pltpu.make_async_copy(hs_ref.at[pl.ds(off, TOK_SLABS)],
                                     tile_ref.at[pl.ds(row * TILE_PITCH, TOK_SLABS)], sem)

    def wait_tile():
        n = ROWS_PER_EXPERT * TOK_SLABS
        pltpu.make_async_copy(hs_ref.at[pl.ds(0, n)], tile_ref.at[pl.ds(0, n)], sem).wait()

    @pl.when(jnp.logical_and(e == 0, f == 0))
    def _():
        def issue(i, carry):
            for u in range(GATHER_UNROLL):
                row_copy(off0_ref, i * GATHER_UNROLL + u).start()
            return carry
        lax.fori_loop(0, ROWS_PER_EXPERT // GATHER_UNROLL, issue, 0)

    @pl.when(f == 0)
    def _():
        wait_tile()
        xe_ref[...] = _from_slabs(tile_ref, ROWS_PER_EXPERT, pitch=TILE_PITCH).astype(BF16)
        acc_ref[...] = jnp.zeros_like(acc_ref)

    for u in range(GATHER_PER_STEP):
        row_copy(offn_ref, f * GATHER_PER_STEP + u).start()

    x = xe_ref[...]
    h1 = _dot(x, w1_ref[...].astype(BF16))
    h3 = _dot(x, w3_ref[...].astype(BF16))
    hid = (h1 * jax.nn.sigmoid(h1) * h3).astype(BF16)
    acc_ref[...] += _dot(hid, w2_ref[...].astype(BF16))

    @pl.when(f == last_f)
    def _():
        _to_slabs(yc_ref, acc_ref[:CAPACITY, :] * gc_ref[...], CAPACITY)
        _to_slabs(yl_ref, acc_ref[CAPACITY:, :] * gl_ref[...], CAPACITY)

    @pl.when(jnp.logical_and(e == last_e, f == last_f))
    def _():
        wait_tile()


def _expert_ffn(offsets, hs, gate_ctx, gate_lat, w1, w3, w2, layer, name):
    nf = D_EXPERT // FF_TILE
    off_block = (None, 1, ROWS_PER_EXPERT)
    g_spec = pl.BlockSpec((None, CAPACITY, 1), lambda e, f: (e, 0, 0))
    y_spec = pl.BlockSpec((None, CAPACITY * TOK_SLABS, LANES), lambda e, f: (e, 0, 0))
    y_shape = jax.ShapeDtypeStruct((N_EXPERTS, CAPACITY * TOK_SLABS, LANES), F32)
    return pl.pallas_call(
        _ffn_kernel,
        grid=(N_EXPERTS, nf),
        in_specs=[
            pl.BlockSpec(off_block, lambda e, f: (0, 0, 0), memory_space=pltpu.SMEM),
            pl.BlockSpec(off_block, lambda e, f: (jnp.minimum(e + 1, N_EXPERTS - 1), 0, 0), memory_space=pltpu.SMEM),
            pl.BlockSpec(memory_space=pl.ANY),
            g_spec, g_spec,
            pl.BlockSpec((None, None, D_MODEL, FF_TILE), lambda e, f: (layer, e, 0, f)),
            pl.BlockSpec((None, None, D_MODEL, FF_TILE), lambda e, f: (layer, e, 0, f)),
            pl.BlockSpec((None, None, FF_TILE, D_MODEL), lambda e, f: (layer, e, f, 0)),
        ],
        out_specs=[y_spec, y_spec],
        out_shape=[y_shape, y_shape],
        scratch_shapes=[
            pltpu.VMEM((ROWS_PER_EXPERT * TILE_PITCH, LANES), F32),
            pltpu.VMEM((ROWS_PER_EXPERT, D_MODEL), BF16),
            pltpu.VMEM((ROWS_PER_EXPERT, D_MODEL), F32),
            pltpu.SemaphoreType.DMA(()),
        ],
        compiler_params=_cp(56),
        name=name,
    )(offsets, offsets, hs, gate_ctx, gate_lat, w1, w3, w2)


SCATTER_UNROLL = 16


def _scatter_expert(off_ref, ye_ref, y_ref, ex):
    def body(i, carry):
        upd = []
        for u in range(SCATTER_UNROLL):
            s = i * SCATTER_UNROLL + u
            rows = pl.ds(pl.multiple_of(off_ref[ex * CAPACITY + s], TOK_SLABS), TOK_SLABS)
            src = ye_ref[pl.ds(pl.multiple_of(s * TOK_SLABS, TOK_SLABS), TOK_SLABS), :]
            upd.append((rows, y_ref[rows, :] + src))
        for rows, val in upd:
            y_ref[rows, :] = val
        return carry

    lax.fori_loop(0, CAPACITY // SCATTER_UNROLL, body, 0)


def _combine_kernel(idx_ref, ye_ref, y_ref):
    ex = pl.program_id(0)

    @pl.when(ex == 0)
    def _():
        y_ref[...] = jnp.zeros_like(y_ref)

    _scatter_expert(idx_ref, ye_ref, y_ref, ex)


def _combine_final_kernel(idx_ref, ye_ref, xp_ref, gp_ref, fg_ref, o_ref, y_ref):
    i = pl.program_id(0)

    @pl.when(i == 0)
    def _():
        y_ref[...] = jnp.zeros_like(y_ref)

    @pl.when(i < N_EXPERTS)
    def _():
        _scatter_expert(idx_ref, ye_ref, y_ref, i)

    @pl.when(i >= N_EXPERTS)
    def _():
        slabs_per_tile = ROW_TILE * TOK_SLABS
        base = pl.multiple_of((i - N_EXPERTS) * slabs_per_tile, slabs_per_tile)
        x = xp_ref[...] + gp_ref[...] * _from_slabs(y_ref, ROW_TILE, base)
        o_ref[...] = _rmsnorm(x, fg_ref[...])


def _combine_final(idx_flat, ye, x_prev, token_set, mods_prev, final_g, name):
    def tile(i):
        return jnp.maximum(i - N_EXPERTS, 0)

    def gate_row(i, idx):
        return (_tile_group(token_set * N_ROW_TILES + tile(i)) * 6 + 5, 0, 0)

    return pl.pallas_call(
        _combine_final_kernel,
        grid_spec=pltpu.PrefetchScalarGridSpec(
            num_scalar_prefetch=1,
            grid=(N_EXPERTS + N_ROW_TILES,),
            in_specs=[
                pl.BlockSpec((None, CAPACITY * TOK_SLABS, LANES), lambda i, idx: (jnp.minimum(i, N_EXPERTS - 1), 0, 0)),
                pl.BlockSpec((ROW_TILE, D_MODEL), lambda i, idx: (token_set * N_ROW_TILES + tile(i), 0)),
                pl.BlockSpec((None, 1, D_MODEL), gate_row),
                pl.BlockSpec((1, D_MODEL), lambda i, idx: (0, 0)),
            ],
            out_specs=pl.BlockSpec((ROW_TILE, D_MODEL), lambda i, idx: (tile(i), 0)),
            scratch_shapes=[pltpu.VMEM((N_TOK * TOK_SLABS, LANES), F32)],
        ),
        out_shape=jax.ShapeDtypeStruct((N_TOK, D_MODEL), F32),
        compiler_params=_cp(40),
        name=name,
    )(idx_flat, ye, x_prev, mods_prev, final_g)


def _combine(idx_flat, ye, name):
    return pl.pallas_call(
        _combine_kernel,
        grid_spec=pltpu.PrefetchScalarGridSpec(
            num_scalar_prefetch=1,
            grid=(N_EXPERTS,),
            in_specs=[pl.BlockSpec((None, CAPACITY * TOK_SLABS, LANES), lambda e, idx: (e, 0, 0))],
            out_specs=pl.BlockSpec((N_TOK * TOK_SLABS, LANES), lambda e, idx: (0, 0)),
        ),
        out_shape=jax.ShapeDtypeStruct((N_TOK * TOK_SLABS, LANES), F32),
        compiler_params=_cp(56),
        name=name,
    )(idx_flat, ye)


def _moe(hs, lt, tri, blk, w1, w3, w2, layer):
    tag = "l%d" % layer
    off_c, g_c = _router(lt, 0, tri, blk, "router_%s_ctx" % tag)
    off_l, g_l = _router(lt, 1, tri, blk, "router_%s_lat" % tag)
    offsets = jnp.concatenate([off_c, off_l + N_TOK * TOK_SLABS], axis=1).reshape(N_EXPERTS, 1, ROWS_PER_EXPERT)
    ye_c, ye_l = _expert_ffn(offsets, hs, g_c.reshape(N_EXPERTS, CAPACITY, 1), g_l.reshape(N_EXPERTS, CAPACITY, 1),
                             w1, w3, w2, layer, "ffn_" + tag)
    return (off_c.reshape(N_EXPERTS * CAPACITY), ye_c), (off_l.reshape(N_EXPERTS * CAPACITY), ye_l)


def _gla_proj_kernel(xp_ref, yc_ref, yl_ref, gp_ref, g_ref, sh_ref, sc_ref, w_ref, wt_ref, wg_ref, bg_ref,
                     x_ref, q_ref, k_ref, v_ref, r_ref, lgf_ref, lgb_ref, wb_ref):
    _cast_once(w_ref, wb_ref)

    def body(latent):
        yt_ref = yl_ref if latent else yc_ref
        x = xp_ref[...] + gp_ref[...] * _from_slabs(yt_ref, ROW_TILE)
        x_ref[...] = x
        h = _modulate(x, g_ref[...], sh_ref[...], sc_ref[...]).astype(BF16)
        z = _dot(h, wb_ref[...])
        nq = GLA_HEADS * GLA_DK
        nv = GLA_HEADS * GLA_DV
        q_ref[...] = z[:, :nq] * (GLA_DK ** -0.5)
        k_ref[...] = z[:, nq:2 * nq]
        v_ref[...] = z[:, 2 * nq:2 * nq + nv].astype(BF16)
        r_ref[...] = z[:, 2 * nq + nv:].astype(BF16)
        zg = _dot(h, wt_ref[...]).astype(BF16)
        a = _dot(zg, wg_ref[...]) + bg_ref[...]
        ls = (jnp.minimum(a, 0.0) - jnp.log1p(jnp.exp(-jnp.abs(a)))) * (1.0 / GLA_GATE_NORM)
        lgf_ref[...] = ls[:, :nq]
        lgb_ref[...] = ls[:, nq:]

    _per_set(body)


def _gla_proj(x_prev, y_ctx, y_lat, mods_prev, mods, norm_g, w_in, w_tail, wg, bg):
    nq = GLA_HEADS * GLA_DK
    nv = GLA_HEADS * GLA_DV
    n_main = 2 * nq + 2 * nv
    slabs = ROW_TILE * TOK_SLABS
    return pl.pallas_call(
        _gla_proj_kernel,
        grid=(N_TILES,),
        in_specs=[
            _row_spec(D_MODEL),
            pl.BlockSpec((slabs, LANES), lambda i: (jnp.minimum(i, N_ROW_TILES - 1), 0)),
            pl.BlockSpec((slabs, LANES), lambda i: (jnp.maximum(i - N_ROW_TILES, 0), 0)),
            _mod_spec(5),
            _full_spec((1, D_MODEL)),
            _mod_spec(0),
            _mod_spec(1),
            _weight_spec(D_MODEL, n_main),
            _full_spec((D_MODEL, LANES)),
            _full_spec((LANES, 2 * nq)),
            _full_spec((1, 2 * nq)),
        ],
        out_specs=[_row_spec(D_MODEL), _row_spec(nq), _row_spec(nq), _row_spec(nv), _row_spec(nv),
                   _row_spec(nq), _row_spec(nq)],
        out_shape=[
            jax.ShapeDtypeStruct((N_ALL, D_MODEL), F32),
            jax.ShapeDtypeStruct((N_ALL, nq), F32),
            jax.ShapeDtypeStruct((N_ALL, nq), F32),
            jax.ShapeDtypeStruct((N_ALL, nv), BF16),
            jax.ShapeDtypeStruct((N_ALL, nv), BF16),
            jax.ShapeDtypeStruct((N_ALL, nq), F32),
            jax.ShapeDtypeStruct((N_ALL, nq), F32),
        ],
        scratch_shapes=[pltpu.VMEM((D_MODEL, n_main), BF16)],
        compiler_params=_cp(56),
        name="gla_proj",
    )(x_prev, y_ctx, y_lat, mods_prev, norm_g, mods, mods, w_in, w_tail, wg, bg)


GLA_LEVELS = (32, 16, 8)
GLA_UNROLL = 8
GLA_GROUP = 4 * GLA_CHUNK
GLA_SAFE_DECAY = 60.0
NEG_BIG = -1e30


def _gla_intra_fast(qe, kc, b2, fwd):
    c = GLA_CHUNK
    row = lax.broadcasted_iota(jnp.int32, (c, 1), 0)
    col = lax.broadcasted_iota(jnp.int32, (1, c), 1)
    keep = (col <= row) if fwd else (col >= row)
    kq = (kc * jnp.exp2(-b2)).astype(BF16)
    return jnp.where(keep, _dot_nt(qe, kq), 0.0)


def _gla_intra_robust(qc, kc, b2, fwd):
    c = GLA_CHUNK
    row = lax.broadcasted_iota(jnp.int32, (c, 1), 0)
    col = lax.broadcasted_iota(jnp.int32, (1, c), 1)
    a = jnp.zeros((c, c), F32)
    for g in GLA_LEVELS:
        odd = ((row >> int(math.log2(g))) & 1) == 1
        later = odd if fwd else jnp.logical_not(odd)
        refs = []
        for p in range(c // (2 * g)):
            r0 = 2 * g * p + (g - 1 if fwd else g)
            refs.append(jnp.broadcast_to(b2[r0:r0 + 1], (2 * g, GLA_DK)))
        ref = jnp.concatenate(refs, axis=0) if len(refs) > 1 else refs[0]
        qt = qc * jnp.exp2(jnp.where(later, b2 - ref, NEG_BIG))
        kt = kc * jnp.exp2(jnp.where(later, NEG_BIG, ref - b2))
        same_parent = (row >> int(math.log2(2 * g))) == (col >> int(math.log2(2 * g)))
        a = a + jnp.where(same_parent, _dot_nt(qt.astype(BF16), kt.astype(BF16)), 0.0)

    sub = lax.broadcasted_iota(jnp.int32, (SUBLANES, 1), 0)
    strips = []
    for blk in range(c // SUBLANES):
        r0 = SUBLANES * blk
        qb = qc[r0:r0 + SUBLANES]
        bb = b2[r0:r0 + SUBLANES]
        strip = jnp.zeros((SUBLANES, c), F32)
        for jj in range(SUBLANES):
            j = r0 + jj
            cond = (sub >= jj) if fwd else (sub <= jj)
            t = qb * kc[j:j + 1] * jnp.exp2(jnp.where(cond, bb - b2[j:j + 1], NEG_BIG))
            strip = jnp.where(col == j, jnp.sum(t, axis=1, keepdims=True), strip)
        strips.append(strip)
    return a + jnp.concatenate(strips, axis=0)


def _gla_state_increment(kc, vc, b2f, b2b):
    c = GLA_CHUNK
    btf = b2f[c - 1:c]
    btb = b2b[0:1]
    kd = jnp.concatenate([(kc * jnp.exp2(btf - b2f)).astype(BF16), (kc * jnp.exp2(btb - b2b)).astype(BF16)], axis=1)
    return _dot_tn(vc, kd), jnp.concatenate([jnp.exp2(btf), jnp.exp2(btb)], axis=1)


def _gla_scores(qc, kc, b2f, b2b, fast):
    qe_f = (qc * jnp.exp2(b2f)).astype(BF16)
    qe_b = (qc * jnp.exp2(b2b)).astype(BF16)
    if fast:
        a = _gla_intra_fast(qe_f, kc, b2f, True) + _gla_intra_fast(qe_b, kc, b2b, False)
    else:
        a = _gla_intra_robust(qc, kc, b2f, True) + _gla_intra_robust(qc, kc, b2b, False)
    return jnp.concatenate([qe_f, qe_b], axis=1), a.astype(BF16)


def _gla_kernel(q_ref, k_ref, v_ref, r_ref, gf_ref, gb_ref, *refs, t_len, heads, zero_state):
    if zero_state:
        s0f_ref = s0b_ref = None
    else:
        s0f_ref, s0b_ref = refs[:2]
        refs = refs[2:]
    go_ref, trif_ref, trib_ref, u_ref, sf_ref, sb_ref = refs[:6]
    scratch = refs[6:]
    _gla_passes(q_ref, k_ref, v_ref, r_ref, gf_ref, gb_ref, s0f_ref, s0b_ref, go_ref, trif_ref, trib_ref,
                u_ref, sf_ref, sb_ref, *scratch, t_len=t_len, heads=heads)


def _gla_passes(q_ref, k_ref, v_ref, r_ref, gf_ref, gb_ref, s0f_ref, s0b_ref, go_ref, trif_ref, trib_ref,
                u_ref, sf_ref, sb_ref, *scratch, t_len, heads):
    c = GLA_CHUNK
    n = t_len // c
    per_head = len(scratch) // heads
    hd = []
    for hh in range(heads):
        dk = slice(hh * GLA_DK, (hh + 1) * GLA_DK)
        dv = slice(hh * GLA_DV, (hh + 1) * GLA_DV)
        b2f, b2b, qe, oi, kv, dec, sst, st = scratch[hh * per_head:(hh + 1) * per_head]
        hd.append(dict(q=q_ref.at[:, dk], k=k_ref.at[:, dk], v=v_ref.at[:, dv], r=r_ref.at[:, dv],
                       gf=gf_ref.at[:, dk], gb=gb_ref.at[:, dk],
                       s0f=None if s0f_ref is None else s0f_ref.at[hh],
                       s0b=None if s0b_ref is None else s0b_ref.at[hh],
                       u=u_ref.at[:, dv], sf=sf_ref.at[hh], sb=sb_ref.at[hh],
                       b2f=b2f, b2b=b2b, qe=qe, oi=oi, kv=kv, dec=dec, sst=sst, st=st))

    def rows_of(ci):
        return pl.ds(pl.multiple_of(ci * c, c), c)

    def state_rows_of(ci):
        return pl.ds(pl.multiple_of(ci * GLA_DV, GLA_DV), GLA_DV)

    n_groups = t_len // GLA_GROUP
    cum_unroll = 2 if n_groups % 2 == 0 else 1

    def cum_body(i, carry):
        sums = []
        for h in hd:
            for u in range(cum_unroll):
                rows = pl.ds(pl.multiple_of((i * cum_unroll + u) * GLA_GROUP, GLA_GROUP), GLA_GROUP)
                for g_ref, tri_ref, b_ref in ((h["gf"], trif_ref, h["b2f"]), (h["gb"], trib_ref, h["b2b"])):
                    s3 = _dot(tri_ref[...], jnp.concatenate(_split3(g_ref[rows, :]), axis=1))
                    sums.append((b_ref, rows, s3))
        for b_ref, rows, s3 in sums:
            b_ref[rows, :] = (s3[:, :GLA_DK] + s3[:, GLA_DK:2 * GLA_DK] + s3[:, 2 * GLA_DK:]) * LOG2E
        return carry

    lax.fori_loop(0, n_groups // cum_unroll, cum_body, 0)

    def chunk_total(g_ref):
        return jnp.min(jnp.sum(g_ref[...].reshape(n, c, GLA_DK), axis=1))

    totals = [chunk_total(h[g]) for h in hd for g in ("gf", "gb")]
    safe = functools.reduce(jnp.minimum, totals) >= -GLA_SAFE_DECAY

    def make_local(fast, unroll):
        def body(i, carry):
            work = [(h, i * unroll + u) for h in hd for u in range(unroll)]
            for h, ci in work:
                rows = rows_of(ci)
                kv, dec = _gla_state_increment(h["k"][rows, :], h["v"][rows, :], h["b2f"][rows, :], h["b2b"][rows, :])
                h["kv"][state_rows_of(ci), :] = kv
                h["dec"][pl.ds(ci, 1), :] = dec
            scores = []
            for h, ci in work:
                rows = rows_of(ci)
                qe, a = _gla_scores(h["q"][rows, :], h["k"][rows, :], h["b2f"][rows, :], h["b2b"][rows, :], fast)
                h["qe"][rows, :] = qe
                scores.append(a)
            for (h, ci), a in zip(work, scores):
                rows = rows_of(ci)
                h["oi"][rows, :] = _dot(a, h["v"][rows, :])
            return carry
        return body

    unroll = min(GLA_UNROLL, n)

    @pl.when(safe)
    def _():
        lax.fori_loop(0, n // unroll, make_local(True, unroll), 0)

    @pl.when(jnp.logical_not(safe))
    def _():
        lax.fori_loop(0, n, make_local(False, 1), 0)

    fl = slice(0, GLA_DK)
    bl = slice(GLA_DK, 2 * GLA_DK)
    for h in hd:
        if h["s0f"] is None:
            h["st"][...] = jnp.zeros_like(h["st"])
        else:
            h["st"][...] = jnp.concatenate([h["s0f"][...].T, h["s0b"][...].T], axis=1)

    def rec_body(i, carry):
        cf = i
        cb = n - 1 - i
        for h in hd:
            st = h["st"][...]
            h["sst"][state_rows_of(cf), fl] = st[:, fl].astype(BF16)
            h["sst"][state_rows_of(cb), bl] = st[:, bl].astype(BF16)
            dec = jnp.concatenate([h["dec"][pl.ds(cf, 1), :][:, fl], h["dec"][pl.ds(cb, 1), :][:, bl]], axis=1)
            kv = jnp.concatenate([h["kv"][state_rows_of(cf), fl], h["kv"][state_rows_of(cb), bl]], axis=1)
            h["st"][...] = st * dec + kv
        return carry

    lax.fori_loop(0, n, rec_body, 0)
    for h in hd:
        h["sf"][...] = h["st"][:, fl].T
        h["sb"][...] = h["st"][:, bl].T

    go = go_ref[...]

    def out_body(i, carry):
        work = [(h, i * unroll + u) for h in hd for u in range(unroll)]
        inter = [_dot_nt(h["qe"][rows_of(ci), :], h["sst"][state_rows_of(ci), :]) for h, ci in work]
        for (h, ci), o_inter in zip(work, inter):
            rows = rows_of(ci)
            r = h["r"][rows, :].astype(F32)
            h["u"][rows, :] = (_rmsnorm(h["oi"][rows, :] + o_inter, go) * (r * jax.nn.sigmoid(r))).astype(BF16)
        return carry

    lax.fori_loop(0, n // unroll, out_body, 0)


def _gla_tri():
    i = np.arange(GLA_GROUP)
    same = (i[:, None] // GLA_CHUNK) == (i[None, :] // GLA_CHUNK)
    fwd = (same & (i[None, :] <= i[:, None])).astype(np.float32)
    bwd = (same & (i[None, :] >= i[:, None])).astype(np.float32)
    return jnp.asarray(fwd, BF16), jnp.asarray(bwd, BF16)


def _gla(q, k, v, r, lgf, lgb, batch0, s0f, s0b, g_out, n_batch, t_len, heads, name):
    zero_state = s0f is None
    trif, trib = _gla_tri()
    n_chunks = t_len // GLA_CHUNK
    qk_spec = pl.BlockSpec((None, t_len, heads * GLA_DK), lambda b, h: (batch0 + b, 0, h))
    vin_spec = pl.BlockSpec((None, t_len, heads * GLA_DV), lambda b, h: (batch0 + b, 0, h))
    v_spec = pl.BlockSpec((None, t_len, heads * GLA_DV), lambda b, h: (b, 0, h))
    s_spec = pl.BlockSpec((None, heads, GLA_DK, GLA_DV), lambda b, h: (b, h, 0, 0))
    const = lambda shape: pl.BlockSpec(shape, lambda b, h: (0,) * len(shape))
    s_shape = jax.ShapeDtypeStruct((n_batch, GLA_HEADS, GLA_DK, GLA_DV), F32)
    states = [] if zero_state else [s0f, s0b]
    return pl.pallas_call(
        functools.partial(_gla_kernel, t_len=t_len, heads=heads, zero_state=zero_state),
        grid=(n_batch, GLA_HEADS // heads),
        in_specs=[qk_spec, qk_spec, vin_spec, vin_spec, qk_spec, qk_spec] + [s_spec] * len(states)
        + [const((1, GLA_DV)), const((GLA_GROUP, GLA_GROUP)), const((GLA_GROUP, GLA_GROUP))],
        out_specs=[v_spec, s_spec, s_spec],
        out_shape=[jax.ShapeDtypeStruct((n_batch, t_len, GLA_HEADS * GLA_DV), BF16), s_shape, s_shape],
        scratch_shapes=[
            pltpu.VMEM((t_len, GLA_DK), F32), pltpu.VMEM((t_len, GLA_DK), F32),
            pltpu.VMEM((t_len, 2 * GLA_DK), BF16),
            pltpu.VMEM((t_len, GLA_DV), F32),
            pltpu.VMEM((n_chunks * GLA_DV, 2 * GLA_DK), F32),
            pltpu.VMEM((max(n_chunks, SUBLANES), 2 * GLA_DK), F32),
            pltpu.VMEM((n_chunks * GLA_DV, 2 * GLA_DK), BF16),
            pltpu.VMEM((GLA_DV, 2 * GLA_DK), F32),
        ] * heads,
        compiler_params=_cp(48),
        name=name,
    )(q, k, v, r, lgf, lgb, *states, g_out, trif, trib)


def kernel(x_prompt, x_sample, cache_k, cache_v, state_fwd, state_bwd, c, c_ctx, w_mod, b_mod, norm_g,
           da_w_in, da_w_out, da_lam_q1, da_lam_k1, da_lam_q2, da_lam_k2, da_g_sub, gla_w_in, gla_w_gf2,
           gla_b_gf, gla_w_gb2, gla_b_gb, gla_g_out, gla_w_out, moe_w_router, moe_w1, moe_w3, moe_w2, final_g):
    n_ctx_b = x_prompt.shape[0]
    n_lat_b = x_sample.shape[0]
    x_ctx = x_prompt.reshape(N_TOK, D_MODEL)
    x_lat = x_sample.reshape(N_TOK, D_MODEL)

    cvec = jnp.zeros((SUBLANES, D_MODEL), F32).at[0].set(c_ctx).at[1:1 + n_lat_b].set(c)
    mods = _mod_params(cvec, w_mod, b_mod)
    tri, blk = _cumsum_consts()
    wr_t = [jnp.zeros((LANES, D_MODEL), BF16).at[:N_EXPERTS].set(moe_w_router[i].T.astype(BF16))
            for i in range(DEPTH)]
    ng = norm_g.reshape(DEPTH, 2, 1, D_MODEL)

    lam_init = 0.8 - 0.6 * math.exp(-0.3 * 0)
    w_in = da_w_in
    w_out = da_w_out
    lam_vecs = [v[0].reshape(1, DA_DIM) for v in (da_lam_q1, da_lam_k1, da_lam_q2, da_lam_k2)]
    g_sub = da_g_sub[0].reshape(1, DA_VDIM)

    q, k, v, kf, vf = _da_proj(x_ctx, x_lat, mods[0], ng[0, 0], w_in, _rope_tables())
    past = cache_k.shape[2]
    ck = cache_k[:, 0].reshape(n_lat_b * past, D_MODEL)
    cv = cache_v[:, 0].reshape(n_lat_b * past, D_MODEL)
    u_ctx = _attention(lam_vecs, g_sub, q, 0, [(k, v, 0, SEQ)], n_ctx_b, SEQ, ATTN_Q_TILE, lam_init, "attn_ctx")
    u_lat = _attention(lam_vecs, g_sub, q, N_TOK, [(k, v, N_TOK, DEC_SEQ), (ck, cv, 0, past)], n_lat_b, DEC_SEQ,
                       ATTN_Q_TILE, lam_init, "attn_lat")

    x1, hs, lt = _outproj(u_ctx, u_lat, (x_ctx, x_lat), w_out, mods[0], ng[0, 1], wr_t[0], "outproj0")
    (idx_c, ye_c), (idx_l, ye_l) = _moe(hs, lt, tri, blk, moe_w1, moe_w3, moe_w2, 0)
    y_ctx = _combine(idx_c, ye_c, "combine_l0_ctx")
    y_lat = _combine(idx_l, ye_l, "combine_l0_lat")

    nq = GLA_HEADS * GLA_DK
    n_main = 2 * nq + 2 * GLA_HEADS * GLA_DV
    w_tail = jnp.zeros((D_MODEL, LANES), F32).at[:, :2 * GLA_GATE_RANK].set(gla_w_in[0][:, n_main:]).astype(BF16)
    wg = jnp.zeros((LANES, 2 * nq), F32)
    wg = wg.at[:GLA_GATE_RANK, :nq].set(gla_w_gf2[0]).at[GLA_GATE_RANK:2 * GLA_GATE_RANK, nq:].set(gla_w_gb2[0])
    wg = wg.astype(BF16)
    bg = jnp.concatenate([gla_b_gf[0], gla_b_gb[0]]).reshape(1, 2 * nq)
    w_out1 = gla_w_out
    g_out = gla_g_out[0].reshape(1, GLA_DV)

    x2, *gla_in = _gla_proj(x1, y_ctx, y_lat, mods[0], mods[1], ng[1, 0], gla_w_in, w_tail, wg, bg)

    def gla_side(n_b, t_len, s0f, s0b, tag):
        seqs = [a.reshape(N_ALL // t_len, t_len, a.shape[-1]) for a in gla_in]
        batch0 = 0 if s0f is None else N_TOK // t_len
        heads = GLA_HEADS if t_len <= GLA_GROUP else 1
        u, sf, sb = _gla(*seqs, batch0, s0f, s0b, g_out, n_b, t_len, heads, "gla_" + tag)
        return u.reshape(N_TOK, D_MODEL), sf, sb

    ug_ctx, sf, sb = gla_side(n_ctx_b, SEQ, None, None, "ctx")
    ug_lat, _, _ = gla_side(n_lat_b, DEC_SEQ, state_fwd[:, 0], state_bwd[:, 0], "lat")

    x3, hs, lt = _outproj(ug_ctx, ug_lat, (x2,), w_out1, mods[1], ng[1, 1], wr_t[1], "outproj1")
    (idx_c, ye_c), (idx_l, ye_l) = _moe(hs, lt, tri, blk, moe_w1, moe_w3, moe_w2, 1)

    fg = final_g.reshape(1, D_MODEL)
    y_prompt = _combine_final(idx_c, ye_c, x3, 0, mods[1], fg, "final_ctx").reshape(x_prompt.shape)
    y_sample = _combine_final(idx_l, ye_l, x3, 1, mods[1], fg, "final_lat").reshape(x_sample.shape)
    new_k = kf.reshape(n_ctx_b, 1, SEQ, 2 * DA_HEADS, DA_DIM)
    new_v = vf.reshape(n_ctx_b, 1, SEQ, DA_HEADS, DA_VDIM)
    return (y_prompt, y_sample, new_k, new_v, sf[:, None], sb[:, None])
```

```python
import functools
import math

import numpy as np
import jax
import jax.numpy as jnp
from jax import lax
from jax.experimental import pallas as pl
from jax.experimental.pallas import tpu as pltpu

F32 = jnp.float32
BF16 = jnp.bfloat16

D_MODEL = 1024
DEPTH = 2
SEQ = 256
DEC_SEQ = 2048
GRID_W = 64
N_TOK = 4096
DA_HEADS = 8
DA_DIM = 64
DA_VDIM = 128
ROPE_BASE = 10000.0
GLA_HEADS = 4
GLA_DK = 128
GLA_DV = 256
GLA_GATE_RANK = 16
GLA_GATE_NORM = 16.0
GLA_CHUNK = 64
N_EXPERTS = 16
CAPACITY = 512
D_EXPERT = 2048
EPS = 1e-6
F32_MIN_NORMAL = 2.0 ** -126
LOG2E = 1.4426950408889634

LANES = 128
SUBLANES = 8
ROW_TILE = 512
FF_TILE = 512
TOK_SLABS = D_MODEL // LANES
MIB = 1024 * 1024


def _cp(vmem_mib, sem=None):
    return pltpu.CompilerParams(vmem_limit_bytes=vmem_mib * MIB, dimension_semantics=sem)


def _dot(a, b):
    return jnp.dot(a, b, preferred_element_type=F32)


def _dot_nt(a, b):
    return lax.dot_general(a, b, (((1,), (1,)), ((), ())), preferred_element_type=F32)


def _dot_tn(a, b):
    return lax.dot_general(a, b, (((0,), (0,)), ((), ())), preferred_element_type=F32)


def _rmsnorm(x, g):
    return x * lax.rsqrt(jnp.mean(x * x, axis=-1, keepdims=True) + EPS) * g


def _modulate(x, g, shift, scale):
    return _rmsnorm(x, g) * (1.0 + scale) + shift


def _split3(x):
    hi = x.astype(BF16)
    r = x - hi.astype(F32)
    mid = r.astype(BF16)
    lo = (r - mid.astype(F32)).astype(BF16)
    return hi, mid, lo


def _from_slabs(ref, rows, base=0, pitch=TOK_SLABS):
    return jnp.concatenate([ref[pl.ds(base + c, rows, stride=pitch), :] for c in range(TOK_SLABS)], axis=1)


def _to_slabs(ref, val, rows):
    for c in range(TOK_SLABS):
        ref[pl.ds(c, rows, stride=TOK_SLABS), :] = val[:, LANES * c:LANES * (c + 1)]


def _mod_kernel(c_ref, w_ref, b_ref, o_ref):
    c = c_ref[...]
    s = c * jax.nn.sigmoid(c)
    w = w_ref[...]
    s_hi = s.astype(BF16)
    s_lo = (s - s_hi.astype(F32)).astype(BF16)
    w_hi = w.astype(BF16)
    w_lo = (w - w_hi.astype(F32)).astype(BF16)
    o_ref[...] = _dot(s_hi, w_hi) + _dot(s_hi, w_lo) + _dot(s_lo, w_hi) + b_ref[...]


def _mod_params(cvec, w_mod, b_mod):
    n6 = 6 * D_MODEL
    out = pl.pallas_call(
        _mod_kernel,
        grid=(DEPTH, 6),
        in_specs=[
            pl.BlockSpec((SUBLANES, D_MODEL), lambda i, j: (0, 0)),
            pl.BlockSpec((None, D_MODEL, D_MODEL), lambda i, j: (i, 0, j)),
            pl.BlockSpec((None, 1, D_MODEL), lambda i, j: (i, 0, j)),
        ],
        out_specs=pl.BlockSpec((None, SUBLANES, D_MODEL), lambda i, j: (i, 0, j)),
        out_shape=jax.ShapeDtypeStruct((DEPTH, SUBLANES, n6), F32),
        compiler_params=_cp(32),
        name="mod_params",
    )(cvec, w_mod, b_mod.reshape(DEPTH, 1, n6))
    return out.reshape(DEPTH, SUBLANES * 6, 1, D_MODEL)


N_ROW_TILES = N_TOK // ROW_TILE
N_TILES = 2 * N_ROW_TILES
N_ALL = 2 * N_TOK


def _tile_group(i):
    return jnp.where(i < N_ROW_TILES, 0, 1 + (i - N_ROW_TILES) // (DEC_SEQ // ROW_TILE))


def _mod_spec(k):
    return pl.BlockSpec((None, 1, D_MODEL), lambda i: (_tile_group(i) * 6 + k, 0, 0))


def _row_spec(width):
    return pl.BlockSpec((ROW_TILE, width), lambda i: (i, 0))


def _ctx_row_spec(width):
    return pl.BlockSpec((ROW_TILE, width), lambda i: (jnp.minimum(i, N_ROW_TILES - 1), 0))


def _lat_row_spec(width):
    return pl.BlockSpec((ROW_TILE, width), lambda i: (jnp.maximum(i - N_ROW_TILES, 0), 0))


def _full_spec(shape):
    nd = len(shape)
    return pl.BlockSpec(shape, lambda i: (0,) * nd)


def _per_set(body):
    i = pl.program_id(0)

    @pl.when(i < N_ROW_TILES)
    def _():
        body(False)

    @pl.when(i >= N_ROW_TILES)
    def _():
        body(True)


def _weight_spec(rows, cols):
    return pl.BlockSpec((None, rows, cols), lambda i: (0, 0, 0), pipeline_mode=pl.Buffered(1))


def _cast_once(w_ref, wb_ref):
    @pl.when(pl.program_id(0) == 0)
    def _():
        wb_ref[...] = w_ref[...].astype(BF16)


def _da_proj_kernel(xc_ref, xl_ref, g_ref, sh_ref, sc_ref, w_ref, cos_ref, sin_ref,
                    q_ref, k_ref, v_ref, kf_ref, vf_ref, wb_ref):
    _cast_once(w_ref, wb_ref)

    def body(latent):
        x_ref = xl_ref if latent else xc_ref
        h = _modulate(x_ref[...], g_ref[...], sh_ref[...], sc_ref[...])
        z = _dot(h.astype(BF16), wb_ref[...])
        q = z[:, :D_MODEL]
        k = z[:, D_MODEL:2 * D_MODEL]
        v = z[:, 2 * D_MODEL:]
        if latent:
            reps = D_MODEL // LANES
            cos = jnp.concatenate([cos_ref[...]] * reps, axis=1)
            sin = jnp.concatenate([sin_ref[...]] * reps, axis=1)
            lane = lax.broadcasted_iota(jnp.int32, (1, D_MODEL), 1)
            first = (lane & 16) == 0

            def rot(t):
                partner = jnp.where(first, pltpu.roll(t, D_MODEL - 16, 1), pltpu.roll(t, 16, 1))
                return t * cos + partner * sin

            q = rot(q)
            k = rot(k)
        else:
            kf_ref[...] = k
            vf_ref[...] = v
        q_ref[...] = (q * (DA_DIM ** -0.5 * LOG2E)).astype(BF16)
        k_ref[...] = k.astype(BF16)
        v_ref[...] = v.astype(BF16)

    _per_set(body)


def _da_proj(x_ctx, x_lat, mods, norm_g, w_in, rope_tabs):
    tiles_per_batch = DEC_SEQ // ROW_TILE
    tab_spec = pl.BlockSpec((ROW_TILE, LANES), lambda i: (jnp.maximum(i - N_ROW_TILES, 0) % tiles_per_batch, 0))
    return pl.pallas_call(
        _da_proj_kernel,
        grid=(N_TILES,),
        in_specs=[
            _ctx_row_spec(D_MODEL),
            _lat_row_spec(D_MODEL),
            _full_spec((1, D_MODEL)),
            _mod_spec(0),
            _mod_spec(1),
            _weight_spec(D_MODEL, 3 * D_MODEL),
            tab_spec, tab_spec,
        ],
        out_specs=[_row_spec(D_MODEL)] * 3 + [_ctx_row_spec(D_MODEL)] * 2,
        out_shape=[jax.ShapeDtypeStruct((N_ALL, D_MODEL), BF16)] * 3 + [jax.ShapeDtypeStruct((N_TOK, D_MODEL), F32)] * 2,
        scratch_shapes=[pltpu.VMEM((D_MODEL, 3 * D_MODEL), BF16)],
        compiler_params=_cp(56),
        name="da_proj",
    )(x_ctx, x_lat, norm_g, mods, mods, w_in, *rope_tabs)


def _rope_tables():
    t = np.arange(DEC_SEQ)
    rows = (t // GRID_W).astype(np.float32)
    cols = (t % GRID_W).astype(np.float32)
    half = DA_DIM // 4
    freqs = (np.float32(ROPE_BASE) ** (-np.arange(half, dtype=np.float32) / np.float32(half))).astype(np.float32)
    ang_r = rows[:, None] * freqs
    ang_c = cols[:, None] * freqs
    cos64 = np.concatenate([np.cos(ang_r)] * 2 + [np.cos(ang_c)] * 2, axis=1)
    sin64 = np.concatenate([-np.sin(ang_r), np.sin(ang_r), -np.sin(ang_c), np.sin(ang_c)], axis=1)
    reps = LANES // DA_DIM
    return (jnp.asarray(np.concatenate([cos64] * reps, axis=1), F32),
            jnp.asarray(np.concatenate([sin64] * reps, axis=1), F32))


ATTN_Q_TILE = 256


def _attn_kernel(lq1_ref, lk1_ref, lq2_ref, lk2_ref, gs_ref, q_ref, *refs, lam_init, seg_lens):
    n_seg = len(seg_lens)
    kv_refs, o_ref = refs[:2 * n_seg], refs[2 * n_seg]
    lam =(jnp.exp(jnp.sum(lq1_ref[...] * lk1_ref[...], axis=-1, keepdims=True))
           - jnp.exp(jnp.sum(lq2_ref[...] * lk2_ref[...], axis=-1, keepdims=True)) + lam_init)
    lane = lax.broadcasted_iota(jnp.int32, (1, DA_VDIM), 1)
    first = lane < DA_DIM
    gs = gs_ref[...]
    tq = q_ref.shape[0]

    def softmax_pv(qq, ks, vs):
        ss = [_dot_nt(qq, kk) for kk in ks]
        m = functools.reduce(jnp.maximum, [jnp.max(s, axis=-1, keepdims=True) for s in ss])
        ps = [jnp.exp2(s - m) for s in ss]
        l = functools.reduce(jnp.add, [jnp.sum(p, axis=-1, keepdims=True) for p in ps])
        o = functools.reduce(jnp.add, [_dot(p.astype(BF16), vv) for p, vv in zip(ps, vs)])
        return o, l

    for h in range(DA_HEADS):
        sl = slice(DA_VDIM * h, DA_VDIM * (h + 1))
        qh = q_ref[:, sl]
        ks = [kv_refs[2 * s][:, sl].astype(BF16) for s in range(n_seg)]
        vs = [kv_refs[2 * s + 1][:, sl].astype(BF16) for s in range(n_seg)]
        zero = jnp.zeros_like(qh)
        qq = jnp.concatenate([jnp.where(first, qh, zero), jnp.where(first, zero, qh)], axis=0)
        oo, ll = softmax_pv(qq, ks, vs)
        o = oo[:tq] * (1.0 / ll[:tq]) - oo[tq:] * (lam / ll[tq:])
        o_ref[:, sl] = (_rmsnorm(o, gs) * (1.0 - lam_init)).astype(BF16)


def _attention(lam_vecs, g_sub, q, q_row0, kv_segs, n_batch, t_q, q_tile, lam_init, name):
    nq = t_q // q_tile
    q0 = q_row0 // q_tile
    vec_spec = pl.BlockSpec((1, DA_DIM), lambda b, i: (0, 0))
    kv_specs, kv_args = [], []
    for k, v, row0, t_k in kv_segs:
        kv_specs += [pl.BlockSpec((t_k, D_MODEL), lambda b, i, b0=row0 // t_k: (b0 + b, 0))] * 2
        kv_args += [k, v]
    seg_lens = tuple(t_k for _, _, _, t_k in kv_segs)
    return pl.pallas_call(
        functools.partial(_attn_kernel, lam_init=lam_init, seg_lens=seg_lens),
        grid=(n_batch, nq),
        in_specs=[vec_spec] * 4 + [
            pl.BlockSpec((1, DA_VDIM), lambda b, i: (0, 0)),
            pl.BlockSpec((q_tile, D_MODEL), lambda b, i: (q0 + b * nq + i, 0)),
        ] + kv_specs,
        out_specs=pl.BlockSpec((q_tile, D_MODEL), lambda b, i: (b * nq + i, 0)),
        out_shape=jax.ShapeDtypeStruct((n_batch * t_q, D_MODEL), BF16),
        compiler_params=_cp(56),
        name=name,
    )(*lam_vecs, g_sub, q, *kv_args)


def _slab_spec():
    return pl.BlockSpec((ROW_TILE * TOK_SLABS, LANES), lambda i: (i, 0))


def _outproj_kernel(*refs, x_slabs):
    uc_ref, ul_ref = refs[:2]
    n_x = 1 if x_slabs else 2
    x_refs = refs[2:2 + n_x]
    w_ref, gate_ref, g_ref, sh_ref, sc_ref, wr_ref, x1_ref, hs_ref, lt_ref, wb_ref = refs[2 + n_x:]
    _cast_once(w_ref, wb_ref)

    def body(latent):
        u_ref = ul_ref if latent else uc_ref
        x = _from_slabs(x_refs[0], ROW_TILE) if x_slabs else x_refs[int(latent)][...]
        x1 = x + gate_ref[...] * _dot(u_ref[...], wb_ref[...])
        _to_slabs(x1_ref, x1, ROW_TILE)
        h2 = _modulate(x1, g_ref[...], sh_ref[...], sc_ref[...])
        _to_slabs(hs_ref, h2, ROW_TILE)
        lt_ref[...] = _dot_nt(wr_ref[...], h2.astype(BF16))

    _per_set(body)


def _outproj(u_ctx, u_lat, xs, w_out, mods, norm_g, wr_t, name):
    x_slabs = len(xs) == 1
    x_specs = [_slab_spec()] if x_slabs else [_ctx_row_spec(D_MODEL), _lat_row_spec(D_MODEL)]
    return pl.pallas_call(
        functools.partial(_outproj_kernel, x_slabs=x_slabs),
        grid=(N_TILES,),
        in_specs=[_ctx_row_spec(D_MODEL), _lat_row_spec(D_MODEL)] + x_specs + [
            _weight_spec(D_MODEL, D_MODEL),
            _mod_spec(2),
            _full_spec((1, D_MODEL)),
            _mod_spec(3),
            _mod_spec(4),
            _full_spec((LANES, D_MODEL)),
        ],
        out_specs=[_slab_spec(), _slab_spec(), pl.BlockSpec((LANES, ROW_TILE), lambda i: (0, i))],
        out_shape=[
            jax.ShapeDtypeStruct((N_ALL * TOK_SLABS, LANES), F32),
            jax.ShapeDtypeStruct((N_ALL * TOK_SLABS, LANES), F32),
            jax.ShapeDtypeStruct((LANES, N_ALL), F32),
        ],
        scratch_shapes=[pltpu.VMEM((D_MODEL, D_MODEL), BF16)],
        compiler_params=_cp(48),
        name=name,
    )(u_ctx, u_lat, *xs, w_out, mods, norm_g, mods, mods, wr_t)


N_TOK_BLOCKS = N_TOK // LANES
STACK_ROWS = N_TOK_BLOCKS * N_EXPERTS


def _router_kernel(lt_ref, tri_ref, blk_ref, idx_ref, gate_ref, cwin_ref, aff_ref, cend_ref, ctot_ref):
    lt = lt_ref[...]
    e = jnp.exp(lt - jnp.max(lt, axis=0, keepdims=True))
    aff = e / jnp.sum(e, axis=0, keepdims=True)
    aff = jnp.where(aff >= F32_MIN_NORMAL, aff, 0.0)

    def count_ge(x):
        return jnp.sum((aff >= x).astype(F32), axis=1, keepdims=True)

    def step(mid_of):
        def body(_, c):
            lo, hi = c
            mid = jnp.minimum(jnp.maximum(mid_of(lo, hi), lo), hi)
            ok = count_ge(mid) >= float(CAPACITY)
            return jnp.where(ok, mid, lo), jnp.where(ok, hi, mid)
        return body

    lo = jnp.full((N_EXPERTS, 1), F32_MIN_NORMAL, F32)
    hi = jnp.full((N_EXPERTS, 1), 2.0, F32)
    lo, hi = lax.fori_loop(0, 8, step(lambda a, b: jnp.sqrt(a * b)), (lo, hi))
    lo, hi = lax.fori_loop(0, 28, step(lambda a, b: a + (b - a) * 0.5), (lo, hi))
    kth = jnp.max(jnp.where(aff < hi, aff, 0.0), axis=1, keepdims=True)
    thr = jnp.where(count_ge(lo) >= float(CAPACITY), kth, 0.0)

    tri = tri_ref[...]
    blk = blk_ref[...]

    def stack(x):
        return jnp.concatenate([x[:, LANES * b:LANES * (b + 1)] for b in range(N_TOK_BLOCKS)], axis=0)

    def unstack(x):
        return jnp.concatenate([x[N_EXPERTS * b:N_EXPERTS * (b + 1), :] for b in range(N_TOK_BLOCKS)], axis=1)

    def cumsum_stacked(mask_st):
        win = _dot(mask_st.astype(BF16), tri)
        tot = win[:, LANES - 1:LANES]
        off = _dot(blk, jnp.broadcast_to(tot, (STACK_ROWS, LANES)).astype(BF16))[:, :1]
        return win, off, tot

    gt = aff > thr
    eq = aff == thr
    need = float(CAPACITY) - jnp.sum(gt.astype(F32), axis=1, keepdims=True)
    eq_f = eq.astype(F32)
    ewin, eoff, _ = cumsum_stacked(stack(eq_f))
    eq_before = unstack(ewin + eoff) - eq_f
    sel = jnp.logical_or(gt, jnp.logical_and(eq, eq_before < need))
    cwin, coff, ctot = cumsum_stacked(stack(sel.astype(F32)))
    cwin_ref[...] = cwin
    aff_ref[...] = stack(aff)
    cend_ref[...] = jnp.broadcast_to(coff + ctot, (STACK_ROWS, LANES))
    ctot_ref[...] = jnp.broadcast_to(ctot, (STACK_ROWS, LANES))

    blk_id = lax.broadcasted_iota(jnp.int32, (N_TOK_BLOCKS, 1), 0).astype(F32)
    slot = lax.broadcasted_iota(jnp.int32, (1, CAPACITY), 1).astype(F32)
    lane_f = lax.broadcasted_iota(jnp.int32, (LANES, 1), 0).astype(F32)

    def per_expert(ex):
        rows = pl.ds(ex, N_TOK_BLOCKS, stride=N_EXPERTS)
        cend = cend_ref[rows, :][:, :1]
        ctot_e = ctot_ref[rows, :][:, :1]
        before = cend <= slot
        nblk = jnp.sum(before.astype(F32), axis=0, keepdims=True)
        base = jnp.sum(jnp.where(before, ctot_e, 0.0), axis=0, keepdims=True)
        pick = jnp.where(blk_id == nblk, 1.0, 0.0).astype(BF16)
        cnt_in = _dot_tn(cwin_ref[rows, :].astype(BF16), pick)
        lane_idx = jnp.sum((cnt_in <= slot - base).astype(F32), axis=0, keepdims=True)
        idx_ref[pl.ds(ex, 1), :] = ((nblk * float(LANES) + lane_idx) * float(TOK_SLABS)).astype(jnp.int32)
        a_hi, a_mid, a_lo = _split3(aff_ref[rows, :])
        aff_blk = _dot_tn(a_hi, pick) + _dot_tn(a_mid, pick) + _dot_tn(a_lo, pick)
        gate_ref[pl.ds(ex, 1), :] = jnp.sum(jnp.where(lane_f == lane_idx, aff_blk, 0.0), axis=0, keepdims=True)

    def expert_pair(i, carry):
        per_expert(2 * i)
        per_expert(2 * i + 1)
        return carry

    lax.fori_loop(0, N_EXPERTS // 2, expert_pair, 0)


def _cumsum_consts():
    l = np.arange(LANES)
    tri = (l[:, None] <= l[None, :]).astype(np.float32)
    r = np.arange(STACK_ROWS)
    same = (r[:, None] % N_EXPERTS) == (r[None, :] % N_EXPERTS)
    earlier = (r[None, :] // N_EXPERTS) < (r[:, None] // N_EXPERTS)
    blk = (same & earlier).astype(np.float32)
    return jnp.asarray(tri, BF16), jnp.asarray(blk, BF16)


def _router(logits_t, token_set, tri, blk, name):
    return pl.pallas_call(
        _router_kernel,
        grid=(1,),
        in_specs=[
            pl.BlockSpec((N_EXPERTS, N_TOK), lambda i: (0, token_set)),
            _full_spec((LANES, LANES)),
            _full_spec((STACK_ROWS, STACK_ROWS)),
        ],
        out_specs=[_full_spec((N_EXPERTS, CAPACITY))] * 2,
        out_shape=[jax.ShapeDtypeStruct((N_EXPERTS, CAPACITY), jnp.int32),
                   jax.ShapeDtypeStruct((N_EXPERTS, CAPACITY), F32)],
        scratch_shapes=[pltpu.VMEM((STACK_ROWS, LANES), F32)] * 4,
        compiler_params=_cp(48),
        name=name,
    )(logits_t, tri, blk)


GATHER_UNROLL = 16
ROWS_PER_EXPERT = 2 * CAPACITY
GATHER_PER_STEP = ROWS_PER_EXPERT // (D_EXPERT // FF_TILE)
TILE_PITCH = TOK_SLABS + 1


def _ffn_kernel(off0_ref, offn_ref, offp_ref, hs_ref, xin_ref, gc_ref, gl_ref, lb_ref, mc_ref, ml0_ref, ml1_ref,
                w1_ref, w3_ref, w2_ref, x_ref, tile_ref, xe_ref, acc_ref, rmw_ref, ye_ref, sem_in, sem_g, sem_s):
    e = pl.program_id(0)
    f = pl.program_id(1)
    last_e = pl.num_programs(0) - 1
    last_f = pl.num_programs(1) - 1
    n_slab_rows = ROWS_PER_EXPERT * TOK_SLABS
    half_rows = ROWS_PER_EXPERT // 2

    def in_copy(off_ref, row):
        off = pl.multiple_of(off_ref[0, row], TOK_SLABS)
        return pltpu.make_async_copy(hs_ref.at[pl.ds(off, TOK_SLABS)],
                                     tile_ref.at[pl.ds(row * TILE_PITCH, TOK_SLABS)], sem_in)

    def wait_in():
        pltpu.make_async_copy(hs_ref.at[pl.ds(0, n_slab_rows)], tile_ref.at[pl.ds(0, n_slab_rows)], sem_in).wait()

    def x_rows(off_ref, row):
        return x_ref.at[pl.ds(pl.multiple_of(off_ref[0, row], TOK_SLABS), TOK_SLABS)]

    def rmw_rows(row):
        return rmw_ref.at[pl.ds(row * TOK_SLABS, TOK_SLABS)]

    def fetch(off_ref, row):
        return pltpu.make_async_copy(x_rows(off_ref, row), rmw_rows(row), sem_g)

    def write_back(off_ref, row):
        return pltpu.make_async_copy(rmw_rows(row), x_rows(off_ref, row), sem_s)

    def wait_fetch():
        pltpu.make_async_copy(x_ref.at[pl.ds(0, n_slab_rows)], rmw_ref, sem_g).wait()

    def wait_write_back():
        pltpu.make_async_copy(rmw_ref, x_ref.at[pl.ds(0, n_slab_rows)], sem_s).wait()

    def issue_all(make_copy, off_ref):
        def body(i, carry):
            for u in range(GATHER_UNROLL):
                make_copy(off_ref, i * GATHER_UNROLL + u).start()
            return carry
        lax.fori_loop(0, ROWS_PER_EXPERT // GATHER_UNROLL, body, 0)

    has_prev = e >= 1

    @pl.when(jnp.logical_and(e == 0, f == 0))
    def _():
        issue_all(in_copy, off0_ref)

    @pl.when(jnp.logical_and(e >= 2, f == 0))
    def _():
        wait_write_back()

    @pl.when(jnp.logical_and(has_prev, f == 2))
    def _():
        wait_fetch()
        rmw_ref[...] += ye_ref[...]

    @pl.when(f == 0)
    def _():
        wait_in()
        xe_ref[...] = _from_slabs(tile_ref, ROWS_PER_EXPERT, pitch=TILE_PITCH).astype(BF16)
        acc_ref[...] = jnp.zeros_like(acc_ref)

    for u in range(GATHER_PER_STEP):
        in_copy(offn_ref, f * GATHER_PER_STEP + u).start()

    fetching = jnp.logical_and(has_prev, f < 2)
    writing = jnp.logical_and(has_prev, f >= 2)
    half0 = (f % 2) * half_rows
    for u in range(half_rows):
        @pl.when(fetching)
        def _():
            fetch(offp_ref, half0 + u).start()

        @pl.when(writing)
        def _():
            write_back(offp_ref, half0 + u).start()

    x = xe_ref[...]
    h1 = _dot(x, w1_ref[...].astype(BF16))
    h3 = _dot(x, w3_ref[...].astype(BF16))
    hid = (h1 * jax.nn.sigmoid(h1) * h3).astype(BF16)
    acc_ref[...] += _dot(hid, w2_ref[...].astype(BF16))

    @pl.when(f == last_f)
    def _():
        m_lat = jnp.where(lb_ref[...] > 0, ml1_ref[...], ml0_ref[...])
        _to_slabs(ye_ref.at[pl.ds(0, CAPACITY * TOK_SLABS)], acc_ref[:CAPACITY, :] * gc_ref[...] * mc_ref[...], CAPACITY)
        _to_slabs(ye_ref.at[pl.ds(CAPACITY * TOK_SLABS, CAPACITY * TOK_SLABS)],
                  acc_ref[CAPACITY:, :] * gl_ref[...] * m_lat, CAPACITY)

    @pl.when(jnp.logical_and(e == last_e, f == last_f))
    def _():
        wait_in()
        wait_write_back()
        issue_all(fetch, offn_ref)
        wait_fetch()
        rmw_ref[...] += ye_ref[...]
        issue_all(write_back, offn_ref)
        wait_write_back()


def _expert_ffn(offsets, hs, xs, gate_ctx, gate_lat, lat_batch, mods, w1, w3, w2, layer, name):
    nf = D_EXPERT // FF_TILE
    assert nf == 4
    off_block = (None, 1, ROWS_PER_EXPERT)
    smem = lambda index_map: pl.BlockSpec(off_block, index_map, memory_space=pltpu.SMEM)
    g_spec = pl.BlockSpec((None, CAPACITY, 1), lambda e, f: (e, 0, 0))
    mod_row = lambda group: pl.BlockSpec((None, 1, D_MODEL), lambda e, f: (group * 6 + 5, 0, 0))
    any_spec = pl.BlockSpec(memory_space=pl.ANY)
    return pl.pallas_call(
        _ffn_kernel,
        grid=(N_EXPERTS, nf),
        in_specs=[
            smem(lambda e, f: (0, 0, 0)),
            smem(lambda e, f: (jnp.minimum(e + 1, N_EXPERTS - 1), 0, 0)),
            smem(lambda e, f: (jnp.maximum(e - 1, 0), 0, 0)),
            any_spec, any_spec,
            g_spec, g_spec, g_spec,
            mod_row(0), mod_row(1), mod_row(2),
            pl.BlockSpec((None, None, D_MODEL, FF_TILE), lambda e, f: (layer, e, 0, f)),
            pl.BlockSpec((None, None, D_MODEL, FF_TILE), lambda e, f: (layer, e, 0, f)),
            pl.BlockSpec((None, None, FF_TILE, D_MODEL), lambda e, f: (layer, e, f, 0)),
        ],
        out_specs=any_spec,
        out_shape=jax.ShapeDtypeStruct(xs.shape, xs.dtype),
        input_output_aliases={4: 0},
        scratch_shapes=[
            pltpu.VMEM((ROWS_PER_EXPERT * TILE_PITCH, LANES), F32),
            pltpu.VMEM((ROWS_PER_EXPERT, D_MODEL), BF16),
            pltpu.VMEM((ROWS_PER_EXPERT, D_MODEL), F32),
            pltpu.VMEM((ROWS_PER_EXPERT * TOK_SLABS, LANES), F32),
            pltpu.VMEM((ROWS_PER_EXPERT * TOK_SLABS, LANES), F32),
            pltpu.SemaphoreType.DMA(()), pltpu.SemaphoreType.DMA(()), pltpu.SemaphoreType.DMA(()),
        ],
        compiler_params=_cp(56),
        name=name,
    )(offsets, offsets, offsets, hs, xs, gate_ctx, gate_lat, lat_batch, mods, mods, mods, w1, w3, w2)


def _final_kernel(xs_ref, fg_ref, oc_ref, ol_ref):
    def body(latent):
        o_ref = ol_ref if latent else oc_ref
        o_ref[...] = _rmsnorm(_from_slabs(xs_ref, ROW_TILE), fg_ref[...])

    _per_set(body)


def _final(xs, final_g):
    out = jax.ShapeDtypeStruct((N_TOK, D_MODEL), F32)
    return pl.pallas_call(
        _final_kernel,
        grid=(N_TILES,),
        in_specs=[_slab_spec(), _full_spec((1, D_MODEL))],
        out_specs=[_ctx_row_spec(D_MODEL), _lat_row_spec(D_MODEL)],
        out_shape=[out, out],
        compiler_params=_cp(32),
        name="final_norm",
    )(xs, final_g)


def _moe(xs, hs, lt, mods, tri, blk, w1, w3, w2, layer):
    tag = "l%d" % layer
    off_c, g_c = _router(lt, 0, tri, blk, "router_%s_ctx" % tag)
    off_l, g_l = _router(lt, 1, tri, blk, "router_%s_lat" % tag)
    offsets = jnp.concatenate([off_c, off_l + N_TOK * TOK_SLABS], axis=1).reshape(N_EXPERTS, 1, ROWS_PER_EXPERT)
    col = lambda a: a.reshape(N_EXPERTS, CAPACITY, 1)
    lat_batch = col((off_l >= DEC_SEQ * TOK_SLABS).astype(jnp.int32))
    return _expert_ffn(offsets, hs, xs, col(g_c), col(g_l), lat_batch, mods, w1, w3, w2, layer, "ffn_" + tag)


def _gla_proj_kernel(xs_ref, g_ref, sh_ref, sc_ref, w_ref, wt_ref, wg_ref, bg_ref,
                     q_ref, k_ref, v_ref, r_ref, lgf_ref, lgb_ref, wb_ref):
    _cast_once(w_ref, wb_ref)
    x = _from_slabs(xs_ref, ROW_TILE)
    h = _modulate(x, g_ref[...], sh_ref[...], sc_ref[...]).astype(BF16)
    z = _dot(h, wb_ref[...])
    nq = GLA_HEADS * GLA_DK
    nv = GLA_HEADS * GLA_DV
    q_ref[...] = z[:, :nq] * (GLA_DK ** -0.5)
    k_ref[...] = z[:, nq:2 * nq]
    v_ref[...] = z[:, 2 * nq:2 * nq + nv].astype(BF16)
    r_ref[...] = z[:, 2 * nq + nv:].astype(BF16)
    zg = _dot(h, wt_ref[...]).astype(BF16)
    a = _dot(zg, wg_ref[...]) + bg_ref[...]
    ls = (jnp.minimum(a, 0.0) - jnp.log1p(jnp.exp(-jnp.abs(a)))) * (1.0 / GLA_GATE_NORM)
    lgf_ref[...] = ls[:, :nq]
    lgb_ref[...] = ls[:, nq:]


def _gla_proj(xs, mods, norm_g, w_in, w_tail, wg, bg):
    nq = GLA_HEADS * GLA_DK
    nv = GLA_HEADS * GLA_DV
    n_main = 2 * nq + 2 * nv
    return pl.pallas_call(
        _gla_proj_kernel,
        grid=(N_TILES,),
        in_specs=[
            _slab_spec(),
            _full_spec((1, D_MODEL)),
            _mod_spec(0),
            _mod_spec(1),
            _weight_spec(D_MODEL, n_main),
            _full_spec((D_MODEL, LANES)),
            _full_spec((LANES, 2 * nq)),
            _full_spec((1, 2 * nq)),
        ],
        out_specs=[_row_spec(nq), _row_spec(nq), _row_spec(nv), _row_spec(nv), _row_spec(nq), _row_spec(nq)],
        out_shape=[
            jax.ShapeDtypeStruct((N_ALL, nq), F32),
            jax.ShapeDtypeStruct((N_ALL, nq), F32),
            jax.ShapeDtypeStruct((N_ALL, nv), BF16),
            jax.ShapeDtypeStruct((N_ALL, nv), BF16),
            jax.ShapeDtypeStruct((N_ALL, nq), F32),
            jax.ShapeDtypeStruct((N_ALL, nq), F32),
        ],
        scratch_shapes=[pltpu.VMEM((D_MODEL, n_main), BF16)],
        compiler_params=_cp(56),
        name="gla_proj",
    )(xs, norm_g, mods, mods, w_in, w_tail, wg, bg)


GLA_LEVELS = (32, 16, 8)
GLA_UNROLL = 8
GLA_GROUP = 4 * GLA_CHUNK
GLA_SAFE_DECAY = 60.0
NEG_BIG = -1e30


def _gla_intra_fast(qe, kc, b2, fwd):
    c = GLA_CHUNK
    row = lax.broadcasted_iota(jnp.int32, (c, 1), 0)
    col = lax.broadcasted_iota(jnp.int32, (1, c), 1)
    keep = (col <= row) if fwd else (col >= row)
    kq = (kc * jnp.exp2(-b2)).astype(BF16)
    return jnp.where(keep, _dot_nt(qe, kq), 0.0)


def _gla_intra_robust(qc, kc, b2, fwd):
    c = GLA_CHUNK
    row = lax.broadcasted_iota(jnp.int32, (c, 1), 0)
    col = lax.broadcasted_iota(jnp.int32, (1, c), 1)
    a = jnp.zeros((c, c), F32)
    for g in GLA_LEVELS:
        odd = ((row >> int(math.log2(g))) & 1) == 1
        later = odd if fwd else jnp.logical_not(odd)
        refs = []
        for p in range(c // (2 * g)):
            r0 = 2 * g * p + (g - 1 if fwd else g)
            refs.append(jnp.broadcast_to(b2[r0:r0 + 1], (2 * g, GLA_DK)))
        ref = jnp.concatenate(refs, axis=0) if len(refs) > 1 else refs[0]
        qt = qc * jnp.exp2(jnp.where(later, b2 - ref, NEG_BIG))
        kt = kc * jnp.exp2(jnp.where(later, NEG_BIG, ref - b2))
        same_parent = (row >> int(math.log2(2 * g))) == (col >> int(math.log2(2 * g)))
        a = a + jnp.where(same_parent, _dot_nt(qt.astype(BF16), kt.astype(BF16)), 0.0)

    sub = lax.broadcasted_iota(jnp.int32, (SUBLANES, 1), 0)
    strips = []
    for blk in range(c // SUBLANES):
        r0 = SUBLANES * blk
        qb = qc[r0:r0 + SUBLANES]
        bb = b2[r0:r0 + SUBLANES]
        strip = jnp.zeros((SUBLANES, c), F32)
        for jj in range(SUBLANES):
            j = r0 + jj
            cond = (sub >= jj) if fwd else (sub <= jj)
            t = qb * kc[j:j + 1] * jnp.exp2(jnp.where(cond, bb - b2[j:j + 1], NEG_BIG))
            strip = jnp.where(col == j, jnp.sum(t, axis=1, keepdims=True), strip)
        strips.append(strip)
    return a + jnp.concatenate(strips, axis=0)


def _gla_state_increment(kc, vc, b2f, b2b):
    c = GLA_CHUNK
    btf = b2f[c - 1:c]
    btb = b2b[0:1]
    kd = jnp.concatenate([(kc * jnp.exp2(btf - b2f)).astype(BF16), (kc * jnp.exp2(btb - b2b)).astype(BF16)], axis=1)
    return _dot_tn(vc, kd), jnp.concatenate([jnp.exp2(btf), jnp.exp2(btb)], axis=1)


def _gla_scores(qc, kc, b2f, b2b, fast):
    qe_f = (qc * jnp.exp2(b2f)).astype(BF16)
    qe_b = (qc * jnp.exp2(b2b)).astype(BF16)
    if fast:
        a = _gla_intra_fast(qe_f, kc, b2f, True) + _gla_intra_fast(qe_b, kc, b2b, False)
    else:
        a = _gla_intra_robust(qc, kc, b2f, True) + _gla_intra_robust(qc, kc, b2b, False)
    return jnp.concatenate([qe_f, qe_b], axis=1), a.astype(BF16)


def _gla_kernel(q_ref, k_ref, v_ref, r_ref, gf_ref, gb_ref, *refs, t_len, heads, zero_state):
    if zero_state:
        s0f_ref = s0b_ref = None
    else:
        s0f_ref, s0b_ref = refs[:2]
        refs = refs[2:]
    go_ref, trif_ref, trib_ref, u_ref, sf_ref, sb_ref = refs[:6]
    scratch = refs[6:]
    _gla_passes(q_ref, k_ref, v_ref, r_ref, gf_ref, gb_ref, s0f_ref, s0b_ref, go_ref, trif_ref, trib_ref,
                u_ref, sf_ref, sb_ref, *scratch, t_len=t_len, heads=heads)


def _gla_passes(q_ref, k_ref, v_ref, r_ref, gf_ref, gb_ref, s0f_ref, s0b_ref, go_ref, trif_ref, trib_ref,
                u_ref, sf_ref, sb_ref, *scratch, t_len, heads):
    c = GLA_CHUNK
    n = t_len // c
    per_head = len(scratch) // heads
    hd = []
    for hh in range(heads):
        dk = slice(hh * GLA_DK, (hh + 1) * GLA_DK)
        dv = slice(hh * GLA_DV, (hh + 1) * GLA_DV)
        b2f, b2b, qe, oi, kv, dec, sst, st = scratch[hh * per_head:(hh + 1) * per_head]
        hd.append(dict(q=q_ref.at[:, dk], k=k_ref.at[:, dk], v=v_ref.at[:, dv], r=r_ref.at[:, dv],
                       gf=gf_ref.at[:, dk], gb=gb_ref.at[:, dk],
                       s0f=None if s0f_ref is None else s0f_ref.at[hh],
                       s0b=None if s0b_ref is None else s0b_ref.at[hh],
                       u=u_ref.at[:, dv], sf=sf_ref.at[hh], sb=sb_ref.at[hh],
                       b2f=b2f, b2b=b2b, qe=qe, oi=oi, kv=kv, dec=dec, sst=sst, st=st))

    def rows_of(ci):
        return pl.ds(pl.multiple_of(ci * c, c), c)

    def state_rows_of(ci):
        return pl.ds(pl.multiple_of(ci * GLA_DV, GLA_DV), GLA_DV)

    n_groups = t_len // GLA_GROUP
    cum_unroll = 2 if n_groups % 2 == 0 else 1

    def cum_body(i, carry):
        sums = []
        for h in hd:
            for u in range(cum_unroll):
                rows = pl.ds(pl.multiple_of((i * cum_unroll + u) * GLA_GROUP, GLA_GROUP), GLA_GROUP)
                for g_ref, tri_ref, b_ref in ((h["gf"], trif_ref, h["b2f"]), (h["gb"], trib_ref, h["b2b"])):
                    s3 = _dot(tri_ref[...], jnp.concatenate(_split3(g_ref[rows, :]), axis=1))
                    sums.append((b_ref, rows, s3))
        for b_ref, rows, s3 in sums:
            b_ref[rows, :] = (s3[:, :GLA_DK] + s3[:, GLA_DK:2 * GLA_DK] + s3[:, 2 * GLA_DK:]) * LOG2E
        return carry

    lax.fori_loop(0, n_groups // cum_unroll, cum_body, 0)

    def chunk_total(g_ref):
        return jnp.min(jnp.sum(g_ref[...].reshape(n, c, GLA_DK), axis=1))

    totals = [chunk_total(h[g]) for h in hd for g in ("gf", "gb")]
    safe = functools.reduce(jnp.minimum, totals) >= -GLA_SAFE_DECAY

    def make_local(fast, unroll):
        def body(i, carry):
            work = [(h, i * unroll + u) for h in hd for u in range(unroll)]
            for h, ci in work:
                rows = rows_of(ci)
                kv, dec = _gla_state_increment(h["k"][rows, :], h["v"][rows, :], h["b2f"][rows, :], h["b2b"][rows, :])
                h["kv"][state_rows_of(ci), :] = kv
                h["dec"][pl.ds(ci, 1), :] = dec
            scores = []
            for h, ci in work:
                rows = rows_of(ci)
                qe, a = _gla_scores(h["q"][rows, :], h["k"][rows, :], h["b2f"][rows, :], h["b2b"][rows, :], fast)
                h["qe"][rows, :] = qe
                scores.append(a)
            for (h, ci), a in zip(work, scores):
                rows = rows_of(ci)
                h["oi"][rows, :] = _dot(a, h["v"][rows, :])
            return carry
        return body

    unroll = min(GLA_UNROLL, n)

    @pl.when(safe)
    def _():
        lax.fori_loop(0, n // unroll, make_local(True, unroll), 0)

    @pl.when(jnp.logical_not(safe))
    def _():
        lax.fori_loop(0, n, make_local(False, 1), 0)

    fl = slice(0, GLA_DK)
    bl = slice(GLA_DK, 2 * GLA_DK)
    for h in hd:
        if h["s0f"] is None:
            h["st"][...] = jnp.zeros_like(h["st"])
        else:
            h["st"][...] = jnp.concatenate([h["s0f"][...].T, h["s0b"][...].T], axis=1)

    def rec_body(i, carry):
        cf = i
        cb = n - 1 - i
        for h in hd:
            st = h["st"][...]
            h["sst"][state_rows_of(cf), fl] = st[:, fl].astype(BF16)
            h["sst"][state_rows_of(cb), bl] = st[:, bl].astype(BF16)
            dec = jnp.concatenate([h["dec"][pl.ds(cf, 1), :][:, fl], h["dec"][pl.ds(cb, 1), :][:, bl]], axis=1)
            kv = jnp.concatenate([h["kv"][state_rows_of(cf), fl], h["kv"][state_rows_of(cb), bl]], axis=1)
            h["st"][...] = st * dec + kv
        return carry

    lax.fori_loop(0, n, rec_body, 0)
    for h in hd:
        h["sf"][...] = h["st"][:, fl].T
        h["sb"][...] = h["st"][:, bl].T

    go = go_ref[...]

    def out_body(i, carry):
        work = [(h, i * unroll + u) for h in hd for u in range(unroll)]
        inter = [_dot_nt(h["qe"][rows_of(ci), :], h["sst"][state_rows_of(ci), :]) for h, ci in work]
        for (h, ci), o_inter in zip(work, inter):
            rows = rows_of(ci)
            r = h["r"][rows, :].astype(F32)
            h["u"][rows, :] = (_rmsnorm(h["oi"][rows, :] + o_inter, go) * (r * jax.nn.sigmoid(r))).astype(BF16)
        return carry

    lax.fori_loop(0, n // unroll, out_body, 0)


def _gla_tri():
    i = np.arange(GLA_GROUP)
    same = (i[:, None] // GLA_CHUNK) == (i[None, :] // GLA_CHUNK)
    fwd = (same & (i[None, :] <= i[:, None])).astype(np.float32)
    bwd = (same & (i[None, :] >= i[:, None])).astype(np.float32)
    return jnp.asarray(fwd, BF16), jnp.asarray(bwd, BF16)


def _gla(q, k, v, r, lgf, lgb, batch0, s0f, s0b, g_out, n_batch, t_len, heads, name):
    zero_state = s0f is None
    trif, trib = _gla_tri()
    n_chunks = t_len // GLA_CHUNK
    qk_spec = pl.BlockSpec((None, t_len, heads * GLA_DK), lambda b, h: (batch0 + b, 0, h))
    vin_spec = pl.BlockSpec((None, t_len, heads * GLA_DV), lambda b, h: (batch0 + b, 0, h))
    v_spec = pl.BlockSpec((None, t_len, heads * GLA_DV), lambda b, h: (b, 0, h))
    s_spec = pl.BlockSpec((None, heads, GLA_DK, GLA_DV), lambda b, h: (b, h, 0, 0))
    const = lambda shape: pl.BlockSpec(shape, lambda b, h: (0,) * len(shape))
    s_shape = jax.ShapeDtypeStruct((n_batch, GLA_HEADS, GLA_DK, GLA_DV), F32)
    states = [] if zero_state else [s0f, s0b]
    return pl.pallas_call(
        functools.partial(_gla_kernel, t_len=t_len, heads=heads, zero_state=zero_state),
        grid=(n_batch, GLA_HEADS // heads),
        in_specs=[qk_spec, qk_spec, vin_spec, vin_spec, qk_spec, qk_spec] + [s_spec] * len(states)
        + [const((1, GLA_DV)), const((GLA_GROUP, GLA_GROUP)), const((GLA_GROUP, GLA_GROUP))],
        out_specs=[v_spec, s_spec, s_spec],
        out_shape=[jax.ShapeDtypeStruct((n_batch, t_len, GLA_HEADS * GLA_DV), BF16), s_shape, s_shape],
        scratch_shapes=[
            pltpu.VMEM((t_len, GLA_DK), F32), pltpu.VMEM((t_len, GLA_DK), F32),
            pltpu.VMEM((t_len, 2 * GLA_DK), BF16),
            pltpu.VMEM((t_len, GLA_DV), F32),
            pltpu.VMEM((n_chunks * GLA_DV, 2 * GLA_DK), F32),
            pltpu.VMEM((max(n_chunks, SUBLANES), 2 * GLA_DK), F32),
            pltpu.VMEM((n_chunks * GLA_DV, 2 * GLA_DK), BF16),
            pltpu.VMEM((GLA_DV, 2 * GLA_DK), F32),
        ] * heads,
        compiler_params=_cp(48),
        name=name,
    )(q, k, v, r, lgf, lgb, *states, g_out, trif, trib)


def kernel(x_prompt, x_sample, cache_k, cache_v, state_fwd, state_bwd, c, c_ctx, w_mod, b_mod, norm_g,
           da_w_in, da_w_out, da_lam_q1, da_lam_k1, da_lam_q2, da_lam_k2, da_g_sub, gla_w_in, gla_w_gf2,
           gla_b_gf, gla_w_gb2, gla_b_gb, gla_g_out, gla_w_out, moe_w_router, moe_w1, moe_w3, moe_w2, final_g):
    n_ctx_b = x_prompt.shape[0]
    n_lat_b = x_sample.shape[0]
    x_ctx = x_prompt.reshape(N_TOK, D_MODEL)
    x_lat = x_sample.reshape(N_TOK, D_MODEL)

    cvec = jnp.zeros((SUBLANES, D_MODEL), F32).at[0].set(c_ctx).at[1:1 + n_lat_b].set(c)
    mods = _mod_params(cvec, w_mod, b_mod)
    tri, blk = _cumsum_consts()
    wr_t = [jnp.zeros((LANES, D_MODEL), BF16).at[:N_EXPERTS].set(moe_w_router[i].T.astype(BF16))
            for i in range(DEPTH)]
    ng = norm_g.reshape(DEPTH, 2, 1, D_MODEL)

    lam_init = 0.8 - 0.6 * math.exp(-0.3 * 0)
    w_in = da_w_in
    w_out = da_w_out
    lam_vecs = [v[0].reshape(1, DA_DIM) for v in (da_lam_q1, da_lam_k1, da_lam_q2, da_lam_k2)]
    g_sub = da_g_sub[0].reshape(1, DA_VDIM)

    q, k, v, kf, vf = _da_proj(x_ctx, x_lat, mods[0], ng[0, 0], w_in, _rope_tables())
    past = cache_k.shape[2]
    ck = cache_k[:, 0].reshape(n_lat_b * past, D_MODEL)
    cv = cache_v[:, 0].reshape(n_lat_b * past, D_MODEL)
    u_ctx = _attention(lam_vecs, g_sub, q, 0, [(k, v, 0, SEQ)], n_ctx_b, SEQ, ATTN_Q_TILE, lam_init, "attn_ctx")
    u_lat = _attention(lam_vecs, g_sub, q, N_TOK, [(k, v, N_TOK, DEC_SEQ), (ck, cv, 0, past)], n_lat_b, DEC_SEQ,
                       ATTN_Q_TILE, lam_init, "attn_lat")

    xs, hs, lt = _outproj(u_ctx, u_lat, (x_ctx, x_lat), w_out, mods[0], ng[0, 1], wr_t[0], "outproj0")
    xs = _moe(xs, hs, lt, mods[0], tri, blk, moe_w1, moe_w3, moe_w2, 0)

    nq = GLA_HEADS * GLA_DK
    n_main = 2 * nq + 2 * GLA_HEADS * GLA_DV
    w_tail = jnp.zeros((D_MODEL, LANES), F32).at[:, :2 * GLA_GATE_RANK].set(gla_w_in[0][:, n_main:]).astype(BF16)
    wg = jnp.zeros((LANES, 2 * nq), F32)
    wg = wg.at[:GLA_GATE_RANK, :nq].set(gla_w_gf2[0]).at[GLA_GATE_RANK:2 * GLA_GATE_RANK, nq:].set(gla_w_gb2[0])
    wg = wg.astype(BF16)
    bg = jnp.concatenate([gla_b_gf[0], gla_b_gb[0]]).reshape(1, 2 * nq)
    w_out1 = gla_w_out
    g_out = gla_g_out[0].reshape(1, GLA_DV)

    gla_in = _gla_proj(xs, mods[1], ng[1, 0], gla_w_in, w_tail, wg, bg)

    def gla_side(n_b, t_len, s0f, s0b, tag):
        seqs = [a.reshape(N_ALL // t_len, t_len, a.shape[-1]) for a in gla_in]
        batch0 = 0 if s0f is None else N_TOK // t_len
        heads = GLA_HEADS if t_len <= GLA_GROUP else 1
        u, sf, sb = _gla(*seqs, batch0, s0f, s0b, g_out, n_b, t_len, heads, "gla_" + tag)
        return u.reshape(N_TOK, D_MODEL), sf, sb

    ug_ctx, sf, sb = gla_side(n_ctx_b, SEQ, None, None, "ctx")
    ug_lat, _, _ = gla_side(n_lat_b, DEC_SEQ, state_fwd[:, 0], state_bwd[:, 0], "lat")

    xs, hs, lt = _outproj(ug_ctx, ug_lat, (xs,), w_out1, mods[1], ng[1, 1], wr_t[1], "outproj1")
    xs = _moe(xs, hs, lt, mods[1], tri, blk, moe_w1, moe_w3, moe_w2, 1)

    y_ctx, y_lat = _final(xs, final_g.reshape(1, D_MODEL))
    y_prompt = y_ctx.reshape(x_prompt.shape)
    y_sample = y_lat.reshape(x_sample.shape)
    new_k = kf.reshape(n_ctx_b, 1, SEQ, 2 * DA_HEADS, DA_DIM)
    new_v = vf.reshape(n_ctx_b, 1, SEQ, DA_HEADS, DA_VDIM)
    return (y_prompt, y_sample, new_k, new_v, sf[:, None], sb[:, None])
```

```python
import functools
import math

import numpy as np
import jax
import jax.numpy as jnp
from jax import lax
from jax.experimental import pallas as pl
from jax.experimental.pallas import tpu as pltpu

F32 = jnp.float32
BF16 = jnp.bfloat16

D_MODEL = 1024
DEPTH = 2
SEQ = 256
DEC_SEQ = 2048
GRID_W = 64
N_TOK = 4096
DA_HEADS = 8
DA_DIM = 64
DA_VDIM = 128
ROPE_BASE = 10000.0
GLA_HEADS = 4
GLA_DK = 128
GLA_DV = 256
GLA_GATE_RANK = 16
GLA_GATE_NORM = 16.0
GLA_CHUNK = 64
N_EXPERTS = 16
CAPACITY = 512
D_EXPERT = 2048
EPS = 1e-6
F32_MIN_NORMAL = 2.0 ** -126
LOG2E = 1.4426950408889634

LANES = 128
SUBLANES = 8
ROW_TILE = 512
FF_TILE = 512
TOK_SLABS = D_MODEL // LANES
MIB = 1024 * 1024


def _cp(vmem_mib, sem=None):
    return pltpu.CompilerParams(vmem_limit_bytes=vmem_mib * MIB, dimension_semantics=sem)


def _dot(a, b):
    return jnp.dot(a, b, preferred_element_type=F32)


def _dot_nt(a, b):
    return lax.dot_general(a, b, (((1,), (1,)), ((), ())), preferred_element_type=F32)


def _dot_tn(a, b):
    return lax.dot_general(a, b, (((0,), (0,)), ((), ())), preferred_element_type=F32)


def _rmsnorm(x, g):
    return x * lax.rsqrt(jnp.mean(x * x, axis=-1, keepdims=True) + EPS) * g


def _modulate(x, g, shift, scale):
    return _rmsnorm(x, g) * (1.0 + scale) + shift


def _split3(x):
    hi = x.astype(BF16)
    r = x - hi.astype(F32)
    mid = r.astype(BF16)
    lo = (r - mid.astype(F32)).astype(BF16)
    return hi, mid, lo


def _from_slabs(ref, rows, base=0, pitch=TOK_SLABS):
    return jnp.concatenate([ref[pl.ds(base + c, rows, stride=pitch), :] for c in range(TOK_SLABS)], axis=1)


def _to_slabs(ref, val, rows):
    for c in range(TOK_SLABS):
        ref[pl.ds(c, rows, stride=TOK_SLABS), :] = val[:, LANES * c:LANES * (c + 1)]


def _mod_kernel(c_ref, w_ref, b_ref, o_ref):
    c = c_ref[...]
    s = c * jax.nn.sigmoid(c)
    w = w_ref[...]
    s_hi = s.astype(BF16)
    s_lo = (s - s_hi.astype(F32)).astype(BF16)
    w_hi = w.astype(BF16)
    w_lo = (w - w_hi.astype(F32)).astype(BF16)
    o_ref[...] = _dot(s_hi, w_hi) + _dot(s_hi, w_lo) + _dot(s_lo, w_hi) + b_ref[...]


def _mod_params(cvec, w_mod, b_mod):
    n6 = 6 * D_MODEL
    out = pl.pallas_call(
        _mod_kernel,
        grid=(DEPTH, 6),
        in_specs=[
            pl.BlockSpec((SUBLANES, D_MODEL), lambda i, j: (0, 0)),
            pl.BlockSpec((None, D_MODEL, D_MODEL), lambda i, j: (i, 0, j)),
            pl.BlockSpec((None, 1, D_MODEL), lambda i, j: (i, 0, j)),
        ],
        out_specs=pl.BlockSpec((None, SUBLANES, D_MODEL), lambda i, j: (i, 0, j)),
        out_shape=jax.ShapeDtypeStruct((DEPTH, SUBLANES, n6), F32),
        compiler_params=_cp(32),
        name="mod_params",
    )(cvec, w_mod, b_mod.reshape(DEPTH, 1, n6))
    return out.reshape(DEPTH, SUBLANES * 6, 1, D_MODEL)


N_ROW_TILES = N_TOK // ROW_TILE
N_TILES = 2 * N_ROW_TILES
N_ALL = 2 * N_TOK


def _tile_group(i):
    return jnp.where(i < N_ROW_TILES, 0, 1 + (i - N_ROW_TILES) // (DEC_SEQ // ROW_TILE))


def _mod_spec(k):
    return pl.BlockSpec((None, 1, D_MODEL), lambda i: (_tile_group(i) * 6 + k, 0, 0))


def _row_spec(width):
    return pl.BlockSpec((ROW_TILE, width), lambda i: (i, 0))


def _ctx_row_spec(width):
    return pl.BlockSpec((ROW_TILE, width), lambda i: (jnp.minimum(i, N_ROW_TILES - 1), 0))


def _lat_row_spec(width):
    return pl.BlockSpec((ROW_TILE, width), lambda i: (jnp.maximum(i - N_ROW_TILES, 0), 0))


def _full_spec(shape):
    nd = len(shape)
    return pl.BlockSpec(shape, lambda i: (0,) * nd)


def _per_set(body):
    i = pl.program_id(0)

    @pl.when(i < N_ROW_TILES)
    def _():
        body(False)

    @pl.when(i >= N_ROW_TILES)
    def _():
        body(True)


def _weight_spec(rows, cols):
    return pl.BlockSpec((None, rows, cols), lambda i: (0, 0, 0), pipeline_mode=pl.Buffered(1))


def _cast_once(w_ref, wb_ref):
    @pl.when(pl.program_id(0) == 0)
    def _():
        wb_ref[...] = w_ref[...].astype(BF16)


def _da_proj_kernel(xc_ref, xl_ref, g_ref, sh_ref, sc_ref, w_ref, cos_ref, sin_ref,
                    q_ref, k_ref, v_ref, kf_ref, vf_ref, wb_ref):
    _cast_once(w_ref, wb_ref)

    def body(latent):
        x_ref = xl_ref if latent else xc_ref
        h = _modulate(x_ref[...], g_ref[...], sh_ref[...], sc_ref[...])
        z = _dot(h.astype(BF16), wb_ref[...])
        q = z[:, :D_MODEL]
        k = z[:, D_MODEL:2 * D_MODEL]
        v = z[:, 2 * D_MODEL:]
        if latent:
            reps = D_MODEL // LANES
            cos = jnp.concatenate([cos_ref[...]] * reps, axis=1)
            sin = jnp.concatenate([sin_ref[...]] * reps, axis=1)
            lane = lax.broadcasted_iota(jnp.int32, (1, D_MODEL), 1)
            first = (lane & 16) == 0

            def rot(t):
                partner = jnp.where(first, pltpu.roll(t, D_MODEL - 16, 1), pltpu.roll(t, 16, 1))
                return t * cos + partner * sin

            q = rot(q)
            k = rot(k)
        else:
            for b in range(ROW_TILE // SEQ):
                for s in range(2 * DA_HEADS):
                    kf_ref[b, :, s, :] = k[SEQ * b:SEQ * (b + 1), DA_DIM * s:DA_DIM * (s + 1)]
            vf_ref[...] = v
        q_ref[...] = (q * (DA_DIM ** -0.5 * LOG2E)).astype(BF16)
        k_ref[...] = k.astype(BF16)
        v_ref[...] = v.astype(BF16)

    _per_set(body)


def _da_proj(x_ctx, x_lat, mods, norm_g, w_in, rope_tabs):
    tiles_per_batch = DEC_SEQ // ROW_TILE
    tab_spec = pl.BlockSpec((ROW_TILE, LANES), lambda i: (jnp.maximum(i - N_ROW_TILES, 0) % tiles_per_batch, 0))
    seqs_per_tile = ROW_TILE // SEQ
    kf_spec = pl.BlockSpec((seqs_per_tile, None, SEQ, 2 * DA_HEADS, DA_DIM),
                           lambda i: (jnp.minimum(i, N_ROW_TILES - 1), 0, 0, 0, 0))
    return pl.pallas_call(
        _da_proj_kernel,
        grid=(N_TILES,),
        in_specs=[
            _ctx_row_spec(D_MODEL),
            _lat_row_spec(D_MODEL),
            _full_spec((1, D_MODEL)),
            _mod_spec(0),
            _mod_spec(1),
            _weight_spec(D_MODEL, 3 * D_MODEL),
            tab_spec, tab_spec,
        ],
        out_specs=[_row_spec(D_MODEL)] * 3 + [kf_spec, _ctx_row_spec(D_MODEL)],
        out_shape=[jax.ShapeDtypeStruct((N_ALL, D_MODEL), BF16)] * 3 + [
            jax.ShapeDtypeStruct((N_TOK // SEQ, 1, SEQ, 2 * DA_HEADS, DA_DIM), F32),
            jax.ShapeDtypeStruct((N_TOK, D_MODEL), F32)],
        scratch_shapes=[pltpu.VMEM((D_MODEL, 3 * D_MODEL), BF16)],
        compiler_params=_cp(56),
        name="da_proj",
    )(x_ctx, x_lat, norm_g, mods, mods, w_in, *rope_tabs)


def _rope_tables():
    t = np.arange(DEC_SEQ)
    rows = (t // GRID_W).astype(np.float32)
    cols = (t % GRID_W).astype(np.float32)
    half = DA_DIM // 4
    freqs = (np.float32(ROPE_BASE) ** (-np.arange(half, dtype=np.float32) / np.float32(half))).astype(np.float32)
    ang_r = rows[:, None] * freqs
    ang_c = cols[:, None] * freqs
    cos64 = np.concatenate([np.cos(ang_r)] * 2 + [np.cos(ang_c)] * 2, axis=1)
    sin64 = np.concatenate([-np.sin(ang_r), np.sin(ang_r), -np.sin(ang_c), np.sin(ang_c)], axis=1)
    reps = LANES // DA_DIM
    return (jnp.asarray(np.concatenate([cos64] * reps, axis=1), F32),
            jnp.asarray(np.concatenate([sin64] * reps, axis=1), F32))


ATTN_Q_TILE = 256


def _attn_kernel(lq1_ref, lk1_ref, lq2_ref, lk2_ref, gs_ref, q_ref, *refs, lam_init, seg_lens):
    n_seg = len(seg_lens)
    kv_refs, o_ref = refs[:2 * n_seg], refs[2 * n_seg]
    lam =(jnp.exp(jnp.sum(lq1_ref[...] * lk1_ref[...], axis=-1, keepdims=True))
           - jnp.exp(jnp.sum(lq2_ref[...] * lk2_ref[...], axis=-1, keepdims=True)) + lam_init)
    lane = lax.broadcasted_iota(jnp.int32, (1, DA_VDIM), 1)
    first = lane < DA_DIM
    gs = gs_ref[...]
    tq = q_ref.shape[0]

    def softmax_pv(qq, ks, vs):
        ss = [_dot_nt(qq, kk) for kk in ks]
        m = functools.reduce(jnp.maximum, [jnp.max(s, axis=-1, keepdims=True) for s in ss])
        ps = [jnp.exp2(s - m) for s in ss]
        l = functools.reduce(jnp.add, [jnp.sum(p, axis=-1, keepdims=True) for p in ps])
        o = functools.reduce(jnp.add, [_dot(p.astype(BF16), vv) for p, vv in zip(ps, vs)])
        return o, l

    for h in range(DA_HEADS):
        sl = slice(DA_VDIM * h, DA_VDIM * (h + 1))
        qh = q_ref[:, sl]
        ks = [kv_refs[2 * s][:, sl].astype(BF16) for s in range(n_seg)]
        vs = [kv_refs[2 * s + 1][:, sl].astype(BF16) for s in range(n_seg)]
        zero = jnp.zeros_like(qh)
        qq = jnp.concatenate([jnp.where(first, qh, zero), jnp.where(first, zero, qh)], axis=0)
        oo, ll = softmax_pv(qq, ks, vs)
        o = oo[:tq] * (1.0 / ll[:tq]) - oo[tq:] * (lam / ll[tq:])
        o_ref[:, sl] = (_rmsnorm(o, gs) * (1.0 - lam_init)).astype(BF16)


def _attention(lam_vecs, g_sub, q, q_row0, kv_segs, n_batch, t_q, q_tile, lam_init, name):
    nq = t_q // q_tile
    q0 = q_row0 // q_tile
    vec_spec = pl.BlockSpec((1, DA_DIM), lambda b, i: (0, 0))
    kv_specs, kv_args = [], []
    for k, v, row0, t_k in kv_segs:
        kv_specs += [pl.BlockSpec((t_k, D_MODEL), lambda b, i, b0=row0 // t_k: (b0 + b, 0))] * 2
        kv_args += [k, v]
    seg_lens = tuple(t_k for _, _, _, t_k in kv_segs)
    return pl.pallas_call(
        functools.partial(_attn_kernel, lam_init=lam_init, seg_lens=seg_lens),
        grid=(n_batch, nq),
        in_specs=[vec_spec] * 4 + [
            pl.BlockSpec((1, DA_VDIM), lambda b, i: (0, 0)),
            pl.BlockSpec((q_tile, D_MODEL), lambda b, i: (q0 + b * nq + i, 0)),
        ] + kv_specs,
        out_specs=pl.BlockSpec((q_tile, D_MODEL), lambda b, i: (b * nq + i, 0)),
        out_shape=jax.ShapeDtypeStruct((n_batch * t_q, D_MODEL), BF16),
        compiler_params=_cp(56),
        name=name,
    )(*lam_vecs, g_sub, q, *kv_args)


def _outproj_kernel(*refs, x_stacked):
    uc_ref, ul_ref = refs[:2]
    n_x = 1 if x_stacked else 2
    x_refs = refs[2:2 + n_x]
    w_ref, gate_ref, g_ref, sh_ref, sc_ref, wr_ref, x1_ref, hs_ref, lt_ref, wb_ref = refs[2 + n_x:]
    _cast_once(w_ref, wb_ref)

    def body(latent):
        u_ref = ul_ref if latent else uc_ref
        x_ref = x_refs[0] if x_stacked else x_refs[int(latent)]
        x1 = x_ref[...] + gate_ref[...] * _dot(u_ref[...], wb_ref[...])
        x1_ref[...] = x1
        h2 = _modulate(x1, g_ref[...], sh_ref[...], sc_ref[...])
        _to_slabs(hs_ref, h2, ROW_TILE)
        lt_ref[...] = _dot_nt(wr_ref[...], h2.astype(BF16))

    _per_set(body)


def _outproj(u_ctx, u_lat, xs, w_out, mods, norm_g, wr_t, name):
    x_stacked = len(xs) == 1
    x_specs = [_row_spec(D_MODEL)] if x_stacked else [_ctx_row_spec(D_MODEL), _lat_row_spec(D_MODEL)]
    return pl.pallas_call(
        functools.partial(_outproj_kernel, x_stacked=x_stacked),
        grid=(N_TILES,),
        in_specs=[_ctx_row_spec(D_MODEL), _lat_row_spec(D_MODEL)] + x_specs + [
            _weight_spec(D_MODEL, D_MODEL),
            _mod_spec(2),
            _full_spec((1, D_MODEL)),
            _mod_spec(3),
            _mod_spec(4),
            _full_spec((LANES, D_MODEL)),
        ],
        out_specs=[
            _row_spec(D_MODEL),
            pl.BlockSpec((ROW_TILE * TOK_SLABS, LANES), lambda i: (i, 0)),
            pl.BlockSpec((LANES, ROW_TILE), lambda i: (0, i)),
        ],
        out_shape=[
            jax.ShapeDtypeStruct((N_ALL, D_MODEL), F32),
            jax.ShapeDtypeStruct((N_ALL * TOK_SLABS, LANES), F32),
            jax.ShapeDtypeStruct((LANES, N_ALL), F32),
        ],
        scratch_shapes=[pltpu.VMEM((D_MODEL, D_MODEL), BF16)],
        compiler_params=_cp(48),
        name=name,
    )(u_ctx, u_lat, *xs, w_out, mods, norm_g, mods, mods, wr_t)


N_TOK_BLOCKS = N_TOK // LANES
STACK_ROWS = N_TOK_BLOCKS * N_EXPERTS


def _router_kernel(lt_ref, tri_ref, blk_ref, idx_ref, gate_ref, cwin_ref, aff_ref, cend_ref, ctot_ref):
    lt = lt_ref[...]
    e = jnp.exp(lt - jnp.max(lt, axis=0, keepdims=True))
    aff = e / jnp.sum(e, axis=0, keepdims=True)
    aff = jnp.where(aff >= F32_MIN_NORMAL, aff, 0.0)

    def count_ge(x):
        return jnp.sum((aff >= x).astype(F32), axis=1, keepdims=True)

    def step(mid_of):
        def body(_, c):
            lo, hi = c
            mid = jnp.minimum(jnp.maximum(mid_of(lo, hi), lo), hi)
            ok = count_ge(mid) >= float(CAPACITY)
            return jnp.where(ok, mid, lo), jnp.where(ok, hi, mid)
        return body

    lo = jnp.full((N_EXPERTS, 1), F32_MIN_NORMAL, F32)
    hi = jnp.full((N_EXPERTS, 1), 2.0, F32)
    lo, hi = lax.fori_loop(0, 8, step(lambda a, b: jnp.sqrt(a * b)), (lo, hi))
    lo, hi = lax.fori_loop(0, 28, step(lambda a, b: a + (b - a) * 0.5), (lo, hi))
    kth = jnp.max(jnp.where(aff < hi, aff, 0.0), axis=1, keepdims=True)
    thr = jnp.where(count_ge(lo) >= float(CAPACITY), kth, 0.0)

    tri = tri_ref[...]
    blk = blk_ref[...]

    def stack(x):
        return jnp.concatenate([x[:, LANES * b:LANES * (b + 1)] for b in range(N_TOK_BLOCKS)], axis=0)

    def unstack(x):
        return jnp.concatenate([x[N_EXPERTS * b:N_EXPERTS * (b + 1), :] for b in range(N_TOK_BLOCKS)], axis=1)

    def cumsum_stacked(mask_st):
        win = _dot(mask_st.astype(BF16), tri)
        tot = win[:, LANES - 1:LANES]
        off = _dot(blk, jnp.broadcast_to(tot, (STACK_ROWS, LANES)).astype(BF16))[:, :1]
        return win, off, tot

    gt = aff > thr
    eq = aff == thr
    need = float(CAPACITY) - jnp.sum(gt.astype(F32), axis=1, keepdims=True)
    eq_f = eq.astype(F32)
    ewin, eoff, _ = cumsum_stacked(stack(eq_f))
    eq_before = unstack(ewin + eoff) - eq_f
    sel = jnp.logical_or(gt, jnp.logical_and(eq, eq_before < need))
    cwin, coff, ctot = cumsum_stacked(stack(sel.astype(F32)))
    cwin_ref[...] = cwin
    aff_ref[...] = stack(aff)
    cend_ref[...] = jnp.broadcast_to(coff + ctot, (STACK_ROWS, LANES))
    ctot_ref[...] = jnp.broadcast_to(ctot, (STACK_ROWS, LANES))

    blk_id = lax.broadcasted_iota(jnp.int32, (N_TOK_BLOCKS, 1), 0).astype(F32)
    slot = lax.broadcasted_iota(jnp.int32, (1, CAPACITY), 1).astype(F32)
    lane_f = lax.broadcasted_iota(jnp.int32, (LANES, 1), 0).astype(F32)

    def per_expert(ex):
        rows = pl.ds(ex, N_TOK_BLOCKS, stride=N_EXPERTS)
        cend = cend_ref[rows, :][:, :1]
        ctot_e = ctot_ref[rows, :][:, :1]
        before = cend <= slot
        nblk = jnp.sum(before.astype(F32), axis=0, keepdims=True)
        base = jnp.sum(jnp.where(before, ctot_e, 0.0), axis=0, keepdims=True)
        pick = jnp.where(blk_id == nblk, 1.0, 0.0).astype(BF16)
        cnt_in = _dot_tn(cwin_ref[rows, :].astype(BF16), pick)
        lane_idx = jnp.sum((cnt_in <= slot - base).astype(F32), axis=0, keepdims=True)
        idx_ref[pl.ds(ex, 1), :] = ((nblk * float(LANES) + lane_idx) * float(TOK_SLABS)).astype(jnp.int32)
        a_hi, a_mid, a_lo = _split3(aff_ref[rows, :])
        aff_blk = _dot_tn(a_hi, pick) + _dot_tn(a_mid, pick) + _dot_tn(a_lo, pick)
        gate_ref[pl.ds(ex, 1), :] = jnp.sum(jnp.where(lane_f == lane_idx, aff_blk, 0.0), axis=0, keepdims=True)

    def expert_pair(i, carry):
        per_expert(2 * i)
        per_expert(2 * i + 1)
        return carry

    lax.fori_loop(0, N_EXPERTS // 2, expert_pair, 0)


def _cumsum_consts():
    l = np.arange(LANES)
    tri = (l[:, None] <= l[None, :]).astype(np.float32)
    r = np.arange(STACK_ROWS)
    same = (r[:, None] % N_EXPERTS) == (r[None, :] % N_EXPERTS)
    earlier = (r[None, :] // N_EXPERTS) < (r[:, None] // N_EXPERTS)
    blk = (same & earlier).astype(np.float32)
    return jnp.asarray(tri, BF16), jnp.asarray(blk, BF16)


def _router(logits_t, token_set, tri, blk, name):
    return pl.pallas_call(
        _router_kernel,
        grid=(1,),
        in_specs=[
            pl.BlockSpec((N_EXPERTS, N_TOK), lambda i: (0, token_set)),
            _full_spec((LANES, LANES)),
            _full_spec((STACK_ROWS, STACK_ROWS)),
        ],
        out_specs=[_full_spec((N_EXPERTS, CAPACITY))] * 2,
        out_shape=[jax.ShapeDtypeStruct((N_EXPERTS, CAPACITY), jnp.int32),
                   jax.ShapeDtypeStruct((N_EXPERTS, CAPACITY), F32)],
        scratch_shapes=[pltpu.VMEM((STACK_ROWS, LANES), F32)] * 4,
        compiler_params=_cp(48),
        name=name,
    )(logits_t, tri, blk)


GATHER_UNROLL = 16
ROWS_PER_EXPERT = 2 * CAPACITY
GATHER_PER_STEP = ROWS_PER_EXPERT // (D_EXPERT // FF_TILE)
TILE_PITCH = TOK_SLABS + 1


def _ffn_kernel(off0_ref, offn_ref, hs_ref, gc_ref, gl_ref, w1_ref, w3_ref, w2_ref, yc_ref, yl_ref,
                tile_ref, xe_ref, acc_ref, sem):
    e = pl.program_id(0)
    f = pl.program_id(1)
    last_e = pl.num_programs(0) - 1
    last_f = pl.num_programs(1) - 1

    def row_copy(off_ref, row):
        off = pl.multiple_of(off_ref[0, row], TOK_SLABS)
        return pltpu.make_async_copy(hs_ref.at[pl.ds(off, TOK_SLABS)],
                                     tile_ref.at[pl.ds(row * TILE_PITCH, TOK_SLABS)], sem)

    def wait_tile():
        n = ROWS_PER_EXPERT * TOK_SLABS
        pltpu.make_async_copy(hs_ref.at[pl.ds(0, n)], tile_ref.at[pl.ds(0, n)], sem).wait()

    @pl.when(jnp.logical_and(e == 0, f == 0))
    def _():
        def issue(i, carry):
            for u in range(GATHER_UNROLL):
                row_copy(off0_ref, i * GATHER_UNROLL + u).start()
            return carry
        lax.fori_loop(0, ROWS_PER_EXPERT // GATHER_UNROLL, issue, 0)

    @pl.when(f == 0)
    def _():
        wait_tile()
        xe_ref[...] = _from_slabs(tile_ref, ROWS_PER_EXPERT, pitch=TILE_PITCH).astype(BF16)
        acc_ref[...] = jnp.zeros_like(acc_ref)

    for u in range(GATHER_PER_STEP):
        row_copy(offn_ref, f * GATHER_PER_STEP + u).start()

    x = xe_ref[...]
    h1 = _dot(x, w1_ref[...].astype(BF16))
    h3 = _dot(x, w3_ref[...].astype(BF16))
    hid = (h1 * jax.nn.sigmoid(h1) * h3).astype(BF16)
    acc_ref[...] += _dot(hid, w2_ref[...].astype(BF16))

    @pl.when(f == last_f)
    def _():
        _to_slabs(yc_ref, acc_ref[:CAPACITY, :] * gc_ref[...], CAPACITY)
        _to_slabs(yl_ref, acc_ref[CAPACITY:, :] * gl_ref[...], CAPACITY)

    @pl.when(jnp.logical_and(e == last_e, f == last_f))
    def _():
        wait_tile()


def _expert_ffn(offsets, hs, gate_ctx, gate_lat, w1, w3, w2, layer, name):
    nf = D_EXPERT // FF_TILE
    off_block = (None, 1, ROWS_PER_EXPERT)
    g_spec = pl.BlockSpec((None, CAPACITY, 1), lambda e, f: (e, 0, 0))
    y_spec = pl.BlockSpec((None, CAPACITY * TOK_SLABS, LANES), lambda e, f: (e, 0, 0))
    y_shape = jax.ShapeDtypeStruct((N_EXPERTS, CAPACITY * TOK_SLABS, LANES), F32)
    return pl.pallas_call(
        _ffn_kernel,
        grid=(N_EXPERTS, nf),
        in_specs=[
            pl.BlockSpec(off_block, lambda e, f: (0, 0, 0), memory_space=pltpu.SMEM),
            pl.BlockSpec(off_block, lambda e, f: (jnp.minimum(e + 1, N_EXPERTS - 1), 0, 0), memory_space=pltpu.SMEM),
            pl.BlockSpec(memory_space=pl.ANY),
            g_spec, g_spec,
            pl.BlockSpec((None, None, D_MODEL, FF_TILE), lambda e, f: (layer, e, 0, f)),
            pl.BlockSpec((None, None, D_MODEL, FF_TILE), lambda e, f: (layer, e, 0, f)),
            pl.BlockSpec((None, None, FF_TILE, D_MODEL), lambda e, f: (layer, e, f, 0)),
        ],
        out_specs=[y_spec, y_spec],
        out_shape=[y_shape, y_shape],
        scratch_shapes=[
            pltpu.VMEM((ROWS_PER_EXPERT * TILE_PITCH, LANES), F32),
            pltpu.VMEM((ROWS_PER_EXPERT, D_MODEL), BF16),
            pltpu.VMEM((ROWS_PER_EXPERT, D_MODEL), F32),
            pltpu.SemaphoreType.DMA(()),
        ],
        compiler_params=_cp(56),
        name=name,
    )(offsets, offsets, hs, gate_ctx, gate_lat, w1, w3, w2)


SCATTER_UNROLL = 16


def _scatter_expert(off_ref, ye_ref, y_ref, ex):
    def body(i, carry):
        upd = []
        for u in range(SCATTER_UNROLL):
            s = i * SCATTER_UNROLL + u
            rows = pl.ds(pl.multiple_of(off_ref[ex * CAPACITY + s], TOK_SLABS), TOK_SLABS)
            src = ye_ref[pl.ds(pl.multiple_of(s * TOK_SLABS, TOK_SLABS), TOK_SLABS), :]
            upd.append((rows, y_ref[rows, :] + src))
        for rows, val in upd:
            y_ref[rows, :] = val
        return carry

    lax.fori_loop(0, CAPACITY // SCATTER_UNROLL, body, 0)


def _combine_kernel(idx_ref, ye_ref, y_ref):
    ex = pl.program_id(0)

    @pl.when(ex == 0)
    def _():
        y_ref[...] = jnp.zeros_like(y_ref)

    _scatter_expert(idx_ref, ye_ref, y_ref, ex)


def _combine_final_kernel(idx_ref, ye_ref, xp_ref, gp_ref, fg_ref, o_ref, y_ref):
    i = pl.program_id(0)

    @pl.when(i == 0)
    def _():
        y_ref[...] = jnp.zeros_like(y_ref)

    @pl.when(i < N_EXPERTS)
    def _():
        _scatter_expert(idx_ref, ye_ref, y_ref, i)

    @pl.when(i >= N_EXPERTS)
    def _():
        slabs_per_tile = ROW_TILE * TOK_SLABS
        base = pl.multiple_of((i - N_EXPERTS) * slabs_per_tile, slabs_per_tile)
        x = xp_ref[...] + gp_ref[...] * _from_slabs(y_ref, ROW_TILE, base)
        o_ref[...] = _rmsnorm(x, fg_ref[...])


def _combine_final(idx_flat, ye, x_prev, token_set, mods_prev, final_g, name):
    def tile(i):
        return jnp.maximum(i - N_EXPERTS, 0)

    def gate_row(i, idx):
        return (_tile_group(token_set * N_ROW_TILES + tile(i)) * 6 + 5, 0, 0)

    return pl.pallas_call(
        _combine_final_kernel,
        grid_spec=pltpu.PrefetchScalarGridSpec(
            num_scalar_prefetch=1,
            grid=(N_EXPERTS + N_ROW_TILES,),
            in_specs=[
                pl.BlockSpec((None, CAPACITY * TOK_SLABS, LANES), lambda i, idx: (jnp.minimum(i, N_EXPERTS - 1), 0, 0)),
                pl.BlockSpec((ROW_TILE, D_MODEL), lambda i, idx: (token_set * N_ROW_TILES + tile(i), 0)),
                pl.BlockSpec((None, 1, D_MODEL), gate_row),
                pl.BlockSpec((1, D_MODEL), lambda i, idx: (0, 0)),
            ],
            out_specs=pl.BlockSpec((ROW_TILE, D_MODEL), lambda i, idx: (tile(i), 0)),
            scratch_shapes=[pltpu.VMEM((N_TOK * TOK_SLABS, LANES), F32)],
        ),
        out_shape=jax.ShapeDtypeStruct((N_TOK, D_MODEL), F32),
        compiler_params=_cp(40),
        name=name,
    )(idx_flat, ye, x_prev, mods_prev, final_g)


def _combine(idx_flat, ye, name):
    return pl.pallas_call(
        _combine_kernel,
        grid_spec=pltpu.PrefetchScalarGridSpec(
            num_scalar_prefetch=1,
            grid=(N_EXPERTS,),
            in_specs=[pl.BlockSpec((None, CAPACITY * TOK_SLABS, LANES), lambda e, idx: (e, 0, 0))],
            out_specs=pl.BlockSpec((N_TOK * TOK_SLABS, LANES), lambda e, idx: (0, 0)),
        ),
        out_shape=jax.ShapeDtypeStruct((N_TOK * TOK_SLABS, LANES), F32),
        compiler_params=_cp(56),
        name=name,
    )(idx_flat, ye)


def _moe(hs, lt, tri, blk, w1, w3, w2, layer):
    tag = "l%d" % layer
    off_c, g_c = _router(lt, 0, tri, blk, "router_%s_ctx" % tag)
    off_l, g_l = _router(lt, 1, tri, blk, "router_%s_lat" % tag)
    offsets = jnp.concatenate([off_c, off_l + N_TOK * TOK_SLABS], axis=1).reshape(N_EXPERTS, 1, ROWS_PER_EXPERT)
    ye_c, ye_l = _expert_ffn(offsets, hs, g_c.reshape(N_EXPERTS, CAPACITY, 1), g_l.reshape(N_EXPERTS, CAPACITY, 1),
                             w1, w3, w2, layer, "ffn_" + tag)
    return (off_c.reshape(N_EXPERTS * CAPACITY), ye_c), (off_l.reshape(N_EXPERTS * CAPACITY), ye_l)


def _gla_proj_kernel(xp_ref, yc_ref, yl_ref, gp_ref, g_ref, sh_ref, sc_ref, w_ref, wt_ref, wg_ref, bg_ref,
                     x_ref, q_ref, k_ref, v_ref, r_ref, lgf_ref, lgb_ref, wb_ref):
    _cast_once(w_ref, wb_ref)

    def body(latent):
        yt_ref = yl_ref if latent else yc_ref
        x = xp_ref[...] + gp_ref[...] * _from_slabs(yt_ref, ROW_TILE)
        x_ref[...] = x
        h = _modulate(x, g_ref[...], sh_ref[...], sc_ref[...]).astype(BF16)
        z = _dot(h, wb_ref[...])
        nq = GLA_HEADS * GLA_DK
        nv = GLA_HEADS * GLA_DV
        q_ref[...] = z[:, :nq] * (GLA_DK ** -0.5)
        k_ref[...] = z[:, nq:2 * nq]
        v_ref[...] = z[:, 2 * nq:2 * nq + nv].astype(BF16)
        r_ref[...] = z[:, 2 * nq + nv:].astype(BF16)
        zg = _dot(h, wt_ref[...]).astype(BF16)
        a = _dot(zg, wg_ref[...]) + bg_ref[...]
        ls = (jnp.minimum(a, 0.0) - jnp.log1p(jnp.exp(-jnp.abs(a)))) * (1.0 / GLA_GATE_NORM)
        lgf_ref[...] = ls[:, :nq]
        lgb_ref[...] = ls[:, nq:]

    _per_set(body)


def _gla_proj(x_prev, y_ctx, y_lat, mods_prev, mods, norm_g, w_in, w_tail, wg, bg):
    nq = GLA_HEADS * GLA_DK
    nv = GLA_HEADS * GLA_DV
    n_main = 2 * nq + 2 * nv
    slabs = ROW_TILE * TOK_SLABS
    return pl.pallas_call(
        _gla_proj_kernel,
        grid=(N_TILES,),
        in_specs=[
            _row_spec(D_MODEL),
            pl.BlockSpec((slabs, LANES), lambda i: (jnp.minimum(i, N_ROW_TILES - 1), 0)),
            pl.BlockSpec((slabs, LANES), lambda i: (jnp.maximum(i - N_ROW_TILES, 0), 0)),
            _mod_spec(5),
            _full_spec((1, D_MODEL)),
            _mod_spec(0),
            _mod_spec(1),
            _weight_spec(D_MODEL, n_main),
            _full_spec((D_MODEL, LANES)),
            _full_spec((LANES, 2 * nq)),
            _full_spec((1, 2 * nq)),
        ],
        out_specs=[_row_spec(D_MODEL), _row_spec(nq), _row_spec(nq), _row_spec(nv), _row_spec(nv),
                   _row_spec(nq), _row_spec(nq)],
        out_shape=[
            jax.ShapeDtypeStruct((N_ALL, D_MODEL), F32),
            jax.ShapeDtypeStruct((N_ALL, nq), F32),
            jax.ShapeDtypeStruct((N_ALL, nq), F32),
            jax.ShapeDtypeStruct((N_ALL, nv), BF16),
            jax.ShapeDtypeStruct((N_ALL, nv), BF16),
            jax.ShapeDtypeStruct((N_ALL, nq), F32),
            jax.ShapeDtypeStruct((N_ALL, nq), F32),
        ],
        scratch_shapes=[pltpu.VMEM((D_MODEL, n_main), BF16)],
        compiler_params=_cp(56),
        name="gla_proj",
    )(x_prev, y_ctx, y_lat, mods_prev, norm_g, mods, mods, w_in, w_tail, wg, bg)


GLA_LEVELS = (32, 16, 8)
GLA_UNROLL = 8
GLA_GROUP = 4 * GLA_CHUNK
GLA_SAFE_DECAY = 60.0
NEG_BIG = -1e30


def _gla_intra_fast(qe, kc, b2, fwd):
    c = GLA_CHUNK
    row = lax.broadcasted_iota(jnp.int32, (c, 1), 0)
    col = lax.broadcasted_iota(jnp.int32, (1, c), 1)
    keep = (col <= row) if fwd else (col >= row)
    kq = (kc * jnp.exp2(-b2)).astype(BF16)
    return jnp.where(keep, _dot_nt(qe, kq), 0.0)


def _gla_intra_robust(qc, kc, b2, fwd):
    c = GLA_CHUNK
    row = lax.broadcasted_iota(jnp.int32, (c, 1), 0)
    col = lax.broadcasted_iota(jnp.int32, (1, c), 1)
    a = jnp.zeros((c, c), F32)
    for g in GLA_LEVELS:
        odd = ((row >> int(math.log2(g))) & 1) == 1
        later = odd if fwd else jnp.logical_not(odd)
        refs = []
        for p in range(c // (2 * g)):
            r0 = 2 * g * p + (g - 1 if fwd else g)
            refs.append(jnp.broadcast_to(b2[r0:r0 + 1], (2 * g, GLA_DK)))
        ref = jnp.concatenate(refs, axis=0) if len(refs) > 1 else refs[0]
        qt = qc * jnp.exp2(jnp.where(later, b2 - ref, NEG_BIG))
        kt = kc * jnp.exp2(jnp.where(later, NEG_BIG, ref - b2))
        same_parent = (row >> int(math.log2(2 * g))) == (col >> int(math.log2(2 * g)))
        a = a + jnp.where(same_parent, _dot_nt(qt.astype(BF16), kt.astype(BF16)), 0.0)

    sub = lax.broadcasted_iota(jnp.int32, (SUBLANES, 1), 0)
    strips = []
    for blk in range(c // SUBLANES):
        r0 = SUBLANES * blk
        qb = qc[r0:r0 + SUBLANES]
        bb = b2[r0:r0 + SUBLANES]
        strip = jnp.zeros((SUBLANES, c), F32)
        for jj in range(SUBLANES):
            j = r0 + jj
            cond = (sub >= jj) if fwd else (sub <= jj)
            t = qb * kc[j:j + 1] * jnp.exp2(jnp.where(cond, bb - b2[j:j + 1], NEG_BIG))
            strip = jnp.where(col == j, jnp.sum(t, axis=1, keepdims=True), strip)
        strips.append(strip)
    return a + jnp.concatenate(strips, axis=0)


def _gla_state_increment(kc, vc, b2f, b2b):
    c = GLA_CHUNK
    btf = b2f[c - 1:c]
    btb = b2b[0:1]
    kd = jnp.concatenate([(kc * jnp.exp2(btf - b2f)).astype(BF16), (kc * jnp.exp2(btb - b2b)).astype(BF16)], axis=1)
    return _dot_tn(vc, kd), jnp.concatenate([jnp.exp2(btf), jnp.exp2(btb)], axis=1)


def _gla_scores(qc, kc, b2f, b2b, fast):
    qe_f = (qc * jnp.exp2(b2f)).astype(BF16)
    qe_b = (qc * jnp.exp2(b2b)).astype(BF16)
    if fast:
        a = _gla_intra_fast(qe_f, kc, b2f, True) + _gla_intra_fast(qe_b, kc, b2b, False)
    else:
        a = _gla_intra_robust(qc, kc, b2f, True) + _gla_intra_robust(qc, kc, b2b, False)
    return jnp.concatenate([qe_f, qe_b], axis=1), a.astype(BF16)


def _gla_kernel(q_ref, k_ref, v_ref, r_ref, gf_ref, gb_ref, *refs, t_len, heads, zero_state):
    if zero_state:
        s0f_ref = s0b_ref = None
    else:
        s0f_ref, s0b_ref = refs[:2]
        refs = refs[2:]
    go_ref, trif_ref, trib_ref, u_ref, sf_ref, sb_ref = refs[:6]
    scratch = refs[6:]
    _gla_passes(q_ref, k_ref, v_ref, r_ref, gf_ref, gb_ref, s0f_ref, s0b_ref, go_ref, trif_ref, trib_ref,
                u_ref, sf_ref, sb_ref, *scratch, t_len=t_len, heads=heads)


def _gla_passes(q_ref, k_ref, v_ref, r_ref, gf_ref, gb_ref, s0f_ref, s0b_ref, go_ref, trif_ref, trib_ref,
                u_ref, sf_ref, sb_ref, *scratch, t_len, heads):
    c = GLA_CHUNK
    n = t_len // c
    per_head = len(scratch) // heads
    hd = []
    for hh in range(heads):
        dk = slice(hh * GLA_DK, (hh + 1) * GLA_DK)
        dv = slice(hh * GLA_DV, (hh + 1) * GLA_DV)
        b2f, b2b, qe, oi, kv, dec, sst, st = scratch[hh * per_head:(hh + 1) * per_head]
        hd.append(dict(q=q_ref.at[:, dk], k=k_ref.at[:, dk], v=v_ref.at[:, dv], r=r_ref.at[:, dv],
                       gf=gf_ref.at[:, dk], gb=gb_ref.at[:, dk],
                       s0f=None if s0f_ref is None else s0f_ref.at[hh],
                       s0b=None if s0b_ref is None else s0b_ref.at[hh],
                       u=u_ref.at[:, dv], sf=sf_ref.at[hh], sb=sb_ref.at[hh],
                       b2f=b2f, b2b=b2b, qe=qe, oi=oi, kv=kv, dec=dec, sst=sst, st=st))

    def rows_of(ci):
        return pl.ds(pl.multiple_of(ci * c, c), c)

    def state_rows_of(ci):
        return pl.ds(pl.multiple_of(ci * GLA_DV, GLA_DV), GLA_DV)

    n_groups = t_len // GLA_GROUP
    cum_unroll = 2 if n_groups % 2 == 0 else 1

    def cum_body(i, carry):
        sums = []
        for h in hd:
            for u in range(cum_unroll):
                rows = pl.ds(pl.multiple_of((i * cum_unroll + u) * GLA_GROUP, GLA_GROUP), GLA_GROUP)
                for g_ref, tri_ref, b_ref in ((h["gf"], trif_ref, h["b2f"]), (h["gb"], trib_ref, h["b2b"])):
                    s3 = _dot(tri_ref[...], jnp.concatenate(_split3(g_ref[rows, :]), axis=1))
                    sums.append((b_ref, rows, s3))
        for b_ref, rows, s3 in sums:
            b_ref[rows, :] = (s3[:, :GLA_DK] + s3[:, GLA_DK:2 * GLA_DK] + s3[:, 2 * GLA_DK:]) * LOG2E
        return carry

    lax.fori_loop(0, n_groups // cum_unroll, cum_body, 0)

    def chunk_total(g_ref):
        return jnp.min(jnp.sum(g_ref[...].reshape(n, c, GLA_DK), axis=1))

    totals = [chunk_total(h[g]) for h in hd for g in ("gf", "gb")]
    safe = functools.reduce(jnp.minimum, totals) >= -GLA_SAFE_DECAY

    def make_local(fast, unroll):
        def body(i, carry):
            work = [(h, i * unroll + u) for h in hd for u in range(unroll)]
            for h, ci in work:
                rows = rows_of(ci)
                kv, dec = _gla_state_increment(h["k"][rows, :], h["v"][rows, :], h["b2f"][rows, :], h["b2b"][rows, :])
                h["kv"][state_rows_of(ci), :] = kv
                h["dec"][pl.ds(ci, 1), :] = dec
            scores = []
            for h, ci in work:
                rows = rows_of(ci)
                qe, a = _gla_scores(h["q"][rows, :], h["k"][rows, :], h["b2f"][rows, :], h["b2b"][rows, :], fast)
                h["qe"][rows, :] = qe
                scores.append(a)
            for (h, ci), a in zip(work, scores):
                rows = rows_of(ci)
                h["oi"][rows, :] = _dot(a, h["v"][rows, :])
            return carry
        return body

    unroll = min(GLA_UNROLL, n)

    @pl.when(safe)
    def _():
        lax.fori_loop(0, n // unroll, make_local(True, unroll), 0)

    @pl.when(jnp.logical_not(safe))
    def _():
        lax.fori_loop(0, n, make_local(False, 1), 0)

    fl = slice(0, GLA_DK)
    bl = slice(GLA_DK, 2 * GLA_DK)
    for h in hd:
        if h["s0f"] is None:
            h["st"][...] = jnp.zeros_like(h["st"])
        else:
            h["st"][...] = jnp.concatenate([h["s0f"][...].T, h["s0b"][...].T], axis=1)

    def rec_body(i, carry):
        cf = i
        cb = n - 1 - i
        for h in hd:
            st = h["st"][...]
            h["sst"][state_rows_of(cf), fl] = st[:, fl].astype(BF16)
            h["sst"][state_rows_of(cb), bl] = st[:, bl].astype(BF16)
            dec = jnp.concatenate([h["dec"][pl.ds(cf, 1), :][:, fl], h["dec"][pl.ds(cb, 1), :][:, bl]], axis=1)
            kv = jnp.concatenate([h["kv"][state_rows_of(cf), fl], h["kv"][state_rows_of(cb), bl]], axis=1)
            h["st"][...] = st * dec + kv
        return carry

    lax.fori_loop(0, n, rec_body, 0)
    for h in hd:
        h["sf"][...] = h["st"][:, fl].T
        h["sb"][...] = h["st"][:, bl].T

    go = go_ref[...]

    def out_body(i, carry):
        work = [(h, i * unroll + u) for h in hd for u in range(unroll)]
        inter = [_dot_nt(h["qe"][rows_of(ci), :], h["sst"][state_rows_of(ci), :]) for h, ci in work]
        for (h, ci), o_inter in zip(work, inter):
            rows = rows_of(ci)
            r = h["r"][rows, :].astype(F32)
            h["u"][rows, :] = (_rmsnorm(h["oi"][rows, :] + o_inter, go) * (r * jax.nn.sigmoid(r))).astype(BF16)
        return carry

    lax.fori_loop(0, n // unroll, out_body, 0)


def _gla_tri():
    i = np.arange(GLA_GROUP)
    same = (i[:, None] // GLA_CHUNK) == (i[None, :] // GLA_CHUNK)
    fwd = (same & (i[None, :] <= i[:, None])).astype(np.float32)
    bwd = (same & (i[None, :] >= i[:, None])).astype(np.float32)
    return jnp.asarray(fwd, BF16), jnp.asarray(bwd, BF16)


def _gla(q, k, v, r, lgf, lgb, batch0, s0f, s0b, g_out, n_batch, t_len, heads, name):
    zero_state = s0f is None
    trif, trib = _gla_tri()
    n_chunks = t_len // GLA_CHUNK
    qk_spec = pl.BlockSpec((None, t_len, heads * GLA_DK), lambda b, h: (batch0 + b, 0, h))
    vin_spec = pl.BlockSpec((None, t_len, heads * GLA_DV), lambda b, h: (batch0 + b, 0, h))
    v_spec = pl.BlockSpec((None, t_len, heads * GLA_DV), lambda b, h: (b, 0, h))
    s_spec = pl.BlockSpec((None, heads, GLA_DK, GLA_DV), lambda b, h: (b, h, 0, 0))
    const = lambda shape: pl.BlockSpec(shape, lambda b, h: (0,) * len(shape))
    s_shape = jax.ShapeDtypeStruct((n_batch, GLA_HEADS, GLA_DK, GLA_DV), F32)
    states = [] if zero_state else [s0f, s0b]
    return pl.pallas_call(
        functools.partial(_gla_kernel, t_len=t_len, heads=heads, zero_state=zero_state),
        grid=(n_batch, GLA_HEADS // heads),
        in_specs=[qk_spec, qk_spec, vin_spec, vin_spec, qk_spec, qk_spec] + [s_spec] * len(states)
        + [const((1, GLA_DV)), const((GLA_GROUP, GLA_GROUP)), const((GLA_GROUP, GLA_GROUP))],
        out_specs=[v_spec, s_spec, s_spec],
        out_shape=[jax.ShapeDtypeStruct((n_batch, t_len, GLA_HEADS * GLA_DV), BF16), s_shape, s_shape],
        scratch_shapes=[
            pltpu.VMEM((t_len, GLA_DK), F32), pltpu.VMEM((t_len, GLA_DK), F32),
            pltpu.VMEM((t_len, 2 * GLA_DK), BF16),
            pltpu.VMEM((t_len, GLA_DV), F32),
            pltpu.VMEM((n_chunks * GLA_DV, 2 * GLA_DK), F32),
            pltpu.VMEM((max(n_chunks, SUBLANES), 2 * GLA_DK), F32),
            pltpu.VMEM((n_chunks * GLA_DV, 2 * GLA_DK), BF16),
            pltpu.VMEM((GLA_DV, 2 * GLA_DK), F32),
        ] * heads,
        compiler_params=_cp(48),
        name=name,
    )(q, k, v, r, lgf, lgb, *states, g_out, trif, trib)


def kernel(x_prompt, x_sample, cache_k, cache_v, state_fwd, state_bwd, c, c_ctx, w_mod, b_mod, norm_g,
           da_w_in, da_w_out, da_lam_q1, da_lam_k1, da_lam_q2, da_lam_k2, da_g_sub, gla_w_in, gla_w_gf2,
           gla_b_gf, gla_w_gb2, gla_b_gb, gla_g_out, gla_w_out, moe_w_router, moe_w1, moe_w3, moe_w2, final_g):
    n_ctx_b = x_prompt.shape[0]
    n_lat_b = x_sample.shape[0]
    x_ctx = x_prompt.reshape(N_TOK, D_MODEL)
    x_lat = x_sample.reshape(N_TOK, D_MODEL)

    cvec = jnp.zeros((SUBLANES, D_MODEL), F32).at[0].set(c_ctx).at[1:1 + n_lat_b].set(c)
    mods = _mod_params(cvec, w_mod, b_mod)
    tri, blk = _cumsum_consts()
    wr_t = [jnp.zeros((LANES, D_MODEL), BF16).at[:N_EXPERTS].set(moe_w_router[i].T.astype(BF16))
            for i in range(DEPTH)]
    ng = norm_g.reshape(DEPTH, 2, 1, D_MODEL)

    lam_init = 0.8 - 0.6 * math.exp(-0.3 * 0)
    w_in = da_w_in
    w_out = da_w_out
    lam_vecs = [v[0].reshape(1, DA_DIM) for v in (da_lam_q1, da_lam_k1, da_lam_q2, da_lam_k2)]
    g_sub = da_g_sub[0].reshape(1, DA_VDIM)

    q, k, v, kf, vf = _da_proj(x_ctx, x_lat, mods[0], ng[0, 0], w_in, _rope_tables())
    past = cache_k.shape[2]
    ck = cache_k[:, 0].reshape(n_lat_b * past, D_MODEL)
    cv = cache_v[:, 0].reshape(n_lat_b * past, D_MODEL)
    u_ctx = _attention(lam_vecs, g_sub, q, 0, [(k, v, 0, SEQ)], n_ctx_b, SEQ, ATTN_Q_TILE, lam_init, "attn_ctx")
    u_lat = _attention(lam_vecs, g_sub, q, N_TOK, [(k, v, N_TOK, DEC_SEQ), (ck, cv, 0, past)], n_lat_b, DEC_SEQ,
                       ATTN_Q_TILE, lam_init, "attn_lat")

    x1, hs, lt = _outproj(u_ctx, u_lat, (x_ctx, x_lat), w_out, mods[0], ng[0, 1], wr_t[0], "outproj0")
    (idx_c, ye_c), (idx_l, ye_l) = _moe(hs, lt, tri, blk, moe_w1, moe_w3, moe_w2, 0)
    y_ctx = _combine(idx_c, ye_c, "combine_l0_ctx")
    y_lat = _combine(idx_l, ye_l, "combine_l0_lat")

    nq = GLA_HEADS * GLA_DK
    n_main = 2 * nq + 2 * GLA_HEADS * GLA_DV
    w_tail = jnp.zeros((D_MODEL, LANES), F32).at[:, :2 * GLA_GATE_RANK].set(gla_w_in[0][:, n_main:]).astype(BF16)
    wg = jnp.zeros((LANES, 2 * nq), F32)
    wg = wg.at[:GLA_GATE_RANK, :nq].set(gla_w_gf2[0]).at[GLA_GATE_RANK:2 * GLA_GATE_RANK, nq:].set(gla_w_gb2[0])
    wg = wg.astype(BF16)
    bg = jnp.concatenate([gla_b_gf[0], gla_b_gb[0]]).reshape(1, 2 * nq)
    w_out1 = gla_w_out
    g_out = gla_g_out[0].reshape(1, GLA_DV)

    x2, *gla_in = _gla_proj(x1, y_ctx, y_lat, mods[0], mods[1], ng[1, 0], gla_w_in, w_tail, wg, bg)

    def gla_side(n_b, t_len, s0f, s0b, tag):
        seqs = [a.reshape(N_ALL // t_len, t_len, a.shape[-1]) for a in gla_in]
        batch0 = 0 if s0f is None else N_TOK // t_len
        heads = GLA_HEADS if t_len <= GLA_GROUP else 1
        u, sf, sb = _gla(*seqs, batch0, s0f, s0b, g_out, n_b, t_len, heads, "gla_" + tag)
        return u.reshape(N_TOK, D_MODEL), sf, sb

    ug_ctx, sf, sb = gla_side(n_ctx_b, SEQ, None, None, "ctx")
    ug_lat, _, _ = gla_side(n_lat_b, DEC_SEQ, state_fwd[:, 0], state_bwd[:, 0], "lat")

    x3, hs, lt = _outproj(ug_ctx, ug_lat, (x2,), w_out1, mods[1], ng[1, 1], wr_t[1], "outproj1")
    (idx_c, ye_c), (idx_l, ye_l) = _moe(hs, lt, tri, blk, moe_w1, moe_w3, moe_w2, 1)

    fg = final_g.reshape(1, D_MODEL)
    y_prompt = _combine_final(idx_c, ye_c, x3, 0, mods[1], fg, "final_ctx").reshape(x_prompt.shape)
    y_sample = _combine_final(idx_l, ye_l, x3, 1, mods[1], fg, "final_lat").reshape(x_sample.shape)
    new_v = vf.reshape(n_ctx_b, 1, SEQ, DA_HEADS, DA_VDIM)
    return (y_prompt, y_sample, kf, new_v, sf[:, None], sb[:, None])
```

```python
import functools
import math

import numpy as np
import jax
import jax.numpy as jnp
from jax import lax
from jax.experimental import pallas as pl
from jax.experimental.pallas import tpu as pltpu

F32 = jnp.float32
BF16 = jnp.bfloat16

D_MODEL = 1024
DEPTH = 2
SEQ = 256
DEC_SEQ = 2048
GRID_W = 64
N_TOK = 4096
DA_HEADS = 8
DA_DIM = 64
DA_VDIM = 128
ROPE_BASE = 10000.0
GLA_HEADS = 4
GLA_DK = 128
GLA_DV = 256
GLA_GATE_RANK = 16
GLA_GATE_NORM = 16.0
GLA_CHUNK = 64
N_EXPERTS = 16
CAPACITY = 512
D_EXPERT = 2048
EPS = 1e-6
F32_MIN_NORMAL = 2.0 ** -126
LOG2E = 1.4426950408889634

LANES = 128
SUBLANES = 8
ROW_TILE = 512
FF_TILE = 512
TOK_SLABS = D_MODEL // LANES
MIB = 1024 * 1024


def _cp(vmem_mib, sem=None):
    return pltpu.CompilerParams(vmem_limit_bytes=vmem_mib * MIB, dimension_semantics=sem)


def _dot(a, b):
    return jnp.dot(a, b, preferred_element_type=F32)


def _dot_nt(a, b):
    return lax.dot_general(a, b, (((1,), (1,)), ((), ())), preferred_element_type=F32)


def _dot_tn(a, b):
    return lax.dot_general(a, b, (((0,), (0,)), ((), ())), preferred_element_type=F32)


def _rmsnorm(x, g):
    return x * lax.rsqrt(jnp.mean(x * x, axis=-1, keepdims=True) + EPS) * g


def _modulate(x, g, shift, scale):
    return _rmsnorm(x, g) * (1.0 + scale) + shift


def _split3(x):
    hi = x.astype(BF16)
    r = x - hi.astype(F32)
    mid = r.astype(BF16)
    lo = (r - mid.astype(F32)).astype(BF16)
    return hi, mid, lo


def _from_slabs(ref, rows, base=0, pitch=TOK_SLABS):
    return jnp.concatenate([ref[pl.ds(base + c, rows, stride=pitch), :] for c in range(TOK_SLABS)], axis=1)


def _to_slabs(ref, val, rows):
    for c in range(TOK_SLABS):
        ref[pl.ds(c, rows, stride=TOK_SLABS), :] = val[:, LANES * c:LANES * (c + 1)]


def _mod_kernel(c_ref, w_ref, b_ref, o_ref):
    c = c_ref[...]
    s = c * jax.nn.sigmoid(c)
    w = w_ref[...]
    s_hi = s.astype(BF16)
    s_lo = (s - s_hi.astype(F32)).astype(BF16)
    w_hi = w.astype(BF16)
    w_lo = (w - w_hi.astype(F32)).astype(BF16)
    o_ref[...] = _dot(s_hi, w_hi) + _dot(s_hi, w_lo) + _dot(s_lo, w_hi) + b_ref[...]


def _mod_params(cvec, w_mod, b_mod):
    n6 = 6 * D_MODEL
    out = pl.pallas_call(
        _mod_kernel,
        grid=(DEPTH, 6),
        in_specs=[
            pl.BlockSpec((SUBLANES, D_MODEL), lambda i, j: (0, 0)),
            pl.BlockSpec((None, D_MODEL, D_MODEL), lambda i, j: (i, 0, j)),
            pl.BlockSpec((None, 1, D_MODEL), lambda i, j: (i, 0, j)),
        ],
        out_specs=pl.BlockSpec((None, SUBLANES, D_MODEL), lambda i, j: (i, 0, j)),
        out_shape=jax.ShapeDtypeStruct((DEPTH, SUBLANES, n6), F32),
        compiler_params=_cp(32),
        name="mod_params",
    )(cvec, w_mod, b_mod.reshape(DEPTH, 1, n6))
    return out.reshape(DEPTH, SUBLANES * 6, 1, D_MODEL)


N_ROW_TILES = N_TOK // ROW_TILE
N_TILES = 2 * N_ROW_TILES
N_ALL = 2 * N_TOK


def _tile_group(i):
    return jnp.where(i < N_ROW_TILES, 0, 1 + (i - N_ROW_TILES) // (DEC_SEQ // ROW_TILE))


def _mod_spec(k):
    return pl.BlockSpec((None, 1, D_MODEL), lambda i: (_tile_group(i) * 6 + k, 0, 0))


def _row_spec(width):
    return pl.BlockSpec((ROW_TILE, width), lambda i: (i, 0))


def _ctx_row_spec(width):
    return pl.BlockSpec((ROW_TILE, width), lambda i: (jnp.minimum(i, N_ROW_TILES - 1), 0))


def _lat_row_spec(width):
    return pl.BlockSpec((ROW_TILE, width), lambda i: (jnp.maximum(i - N_ROW_TILES, 0), 0))


def _full_spec(shape):
    nd = len(shape)
    return pl.BlockSpec(shape, lambda i: (0,) * nd)


def _per_set(body):
    i = pl.program_id(0)

    @pl.when(i < N_ROW_TILES)
    def _():
        body(False)

    @pl.when(i >= N_ROW_TILES)
    def _():
        body(True)


def _weight_spec(rows, cols):
    return pl.BlockSpec((None, rows, cols), lambda i: (0, 0, 0), pipeline_mode=pl.Buffered(1))


def _cast_once(w_ref, wb_ref):
    @pl.when(pl.program_id(0) == 0)
    def _():
        wb_ref[...] = w_ref[...].astype(BF16)


def _da_proj_kernel(xc_ref, xl_ref, g_ref, sh_ref, sc_ref, w_ref, cos_ref, sin_ref,
                    q_ref, k_ref, v_ref, kf_ref, vf_ref, wb_ref):
    _cast_once(w_ref, wb_ref)

    def body(latent):
        x_ref = xl_ref if latent else xc_ref
        h = _modulate(x_ref[...], g_ref[...], sh_ref[...], sc_ref[...])
        z = _dot(h.astype(BF16), wb_ref[...])
        q = z[:, :D_MODEL]
        k = z[:, D_MODEL:2 * D_MODEL]
        v = z[:, 2 * D_MODEL:]
        if latent:
            reps = D_MODEL // LANES
            cos = jnp.concatenate([cos_ref[...]] * reps, axis=1)
            sin = jnp.concatenate([sin_ref[...]] * reps, axis=1)
            lane = lax.broadcasted_iota(jnp.int32, (1, D_MODEL), 1)
            first = (lane & 16) == 0

            def rot(t):
                partner = jnp.where(first, pltpu.roll(t, D_MODEL - 16, 1), pltpu.roll(t, 16, 1))
                return t * cos + partner * sin

            q = rot(q)
            k = rot(k)
        else:
            kf_ref[...] = k
            vf_ref[...] = v
        q_ref[...] = (q * (DA_DIM ** -0.5 * LOG2E)).astype(BF16)
        k_ref[...] = k.astype(BF16)
        v_ref[...] = v.astype(BF16)

    _per_set(body)


def _da_proj(x_ctx, x_lat, mods, norm_g, w_in, rope_tabs):
    tiles_per_batch = DEC_SEQ // ROW_TILE
    tab_spec = pl.BlockSpec((ROW_TILE, LANES), lambda i: (jnp.maximum(i - N_ROW_TILES, 0) % tiles_per_batch, 0))
    return pl.pallas_call(
        _da_proj_kernel,
        grid=(N_TILES,),
        in_specs=[
            _ctx_row_spec(D_MODEL),
            _lat_row_spec(D_MODEL),
            _full_spec((1, D_MODEL)),
            _mod_spec(0),
            _mod_spec(1),
            _weight_spec(D_MODEL, 3 * D_MODEL),
            tab_spec, tab_spec,
        ],
        out_specs=[_row_spec(D_MODEL)] * 3 + [_ctx_row_spec(D_MODEL)] * 2,
        out_shape=[jax.ShapeDtypeStruct((N_ALL, D_MODEL), BF16)] * 3 + [jax.ShapeDtypeStruct((N_TOK, D_MODEL), F32)] * 2,
        scratch_shapes=[pltpu.VMEM((D_MODEL, 3 * D_MODEL), BF16)],
        compiler_params=_cp(56),
        name="da_proj",
    )(x_ctx, x_lat, norm_g, mods, mods, w_in, *rope_tabs)


def _rope_tables():
    t = np.arange(DEC_SEQ)
    rows = (t // GRID_W).astype(np.float32)
    cols = (t % GRID_W).astype(np.float32)
    half = DA_DIM // 4
    freqs = (np.float32(ROPE_BASE) ** (-np.arange(half, dtype=np.float32) / np.float32(half))).astype(np.float32)
    ang_r = rows[:, None] * freqs
    ang_c = cols[:, None] * freqs
    cos64 = np.concatenate([np.cos(ang_r)] * 2 + [np.cos(ang_c)] * 2, axis=1)
    sin64 = np.concatenate([-np.sin(ang_r), np.sin(ang_r), -np.sin(ang_c), np.sin(ang_c)], axis=1)
    reps = LANES // DA_DIM
    return (jnp.asarray(np.concatenate([cos64] * reps, axis=1), F32),
            jnp.asarray(np.concatenate([sin64] * reps, axis=1), F32))


ATTN_Q_TILE = 256


def _attn_kernel(lq1_ref, lk1_ref, lq2_ref, lk2_ref, gs_ref, q_ref, *refs, lam_init, seg_lens):
    n_seg = len(seg_lens)
    kv_refs, o_ref = refs[:2 * n_seg], refs[2 * n_seg]
    lam =(jnp.exp(jnp.sum(lq1_ref[...] * lk1_ref[...], axis=-1, keepdims=True))
           - jnp.exp(jnp.sum(lq2_ref[...] * lk2_ref[...], axis=-1, keepdims=True)) + lam_init)
    lane = lax.broadcasted_iota(jnp.int32, (1, DA_VDIM), 1)
    first = lane < DA_DIM
    gs = gs_ref[...]
    tq = q_ref.shape[0]

    def softmax_pv(qq, ks, vs):
        ss = [_dot_nt(qq, kk) for kk in ks]
        m = functools.reduce(jnp.maximum, [jnp.max(s, axis=-1, keepdims=True) for s in ss])
        ps = [jnp.exp2(s - m) for s in ss]
        l = functools.reduce(jnp.add, [jnp.sum(p, axis=-1, keepdims=True) for p in ps])
        o = functools.reduce(jnp.add, [_dot(p.astype(BF16), vv) for p, vv in zip(ps, vs)])
        return o, l

    for h in range(DA_HEADS):
        sl = slice(DA_VDIM * h, DA_VDIM * (h + 1))
        qh = q_ref[:, sl]
        ks = [kv_refs[2 * s][:, sl].astype(BF16) for s in range(n_seg)]
        vs = [kv_refs[2 * s + 1][:, sl].astype(BF16) for s in range(n_seg)]
        zero = jnp.zeros_like(qh)
        qq = jnp.concatenate([jnp.where(first, qh, zero), jnp.where(first, zero, qh)], axis=0)
        oo, ll = softmax_pv(qq, ks, vs)
        o = oo[:tq] * (1.0 / ll[:tq]) - oo[tq:] * (lam / ll[tq:])
        o_ref[:, sl] = (_rmsnorm(o, gs) * (1.0 - lam_init)).astype(BF16)


def _attention(lam_vecs, g_sub, q, q_row0, kv_segs, n_batch, t_q, q_tile, lam_init, name):
    nq = t_q // q_tile
    q0 = q_row0 // q_tile
    vec_spec = pl.BlockSpec((1, DA_DIM), lambda b, i: (0, 0))
    kv_specs, kv_args = [], []
    for k, v, row0, t_k in kv_segs:
        kv_specs += [pl.BlockSpec((t_k, D_MODEL), lambda b, i, b0=row0 // t_k: (b0 + b, 0))] * 2
        kv_args += [k, v]
    seg_lens = tuple(t_k for _, _, _, t_k in kv_segs)
    return pl.pallas_call(
        functools.partial(_attn_kernel, lam_init=lam_init, seg_lens=seg_lens),
        grid=(n_batch, nq),
        in_specs=[vec_spec] * 4 + [
            pl.BlockSpec((1, DA_VDIM), lambda b, i: (0, 0)),
            pl.BlockSpec((q_tile, D_MODEL), lambda b, i: (q0 + b * nq + i, 0)),
        ] + kv_specs,
        out_specs=pl.BlockSpec((q_tile, D_MODEL), lambda b, i: (b * nq + i, 0)),
        out_shape=jax.ShapeDtypeStruct((n_batch * t_q, D_MODEL), BF16),
        compiler_params=_cp(56),
        name=name,
    )(*lam_vecs, g_sub, q, *kv_args)


def _outproj_kernel(*refs, x_stacked):
    uc_ref, ul_ref = refs[:2]
    n_x = 1 if x_stacked else 2
    x_refs = refs[2:2 + n_x]
    w_ref, gate_ref, g_ref, sh_ref, sc_ref, wr_ref, x1_ref, hs_ref, lt_ref, wb_ref = refs[2 + n_x:]
    _cast_once(w_ref, wb_ref)

    def body(latent):
        u_ref = ul_ref if latent else uc_ref
        x_ref = x_refs[0] if x_stacked else x_refs[int(latent)]
        x1 = x_ref[...] + gate_ref[...] * _dot(u_ref[...], wb_ref[...])
        x1_ref[...] = x1
        h2 = _modulate(x1, g_ref[...], sh_ref[...], sc_ref[...])
        _to_slabs(hs_ref, h2, ROW_TILE)
        lt_ref[...] = _dot_nt(wr_ref[...], h2.astype(BF16))

    _per_set(body)


def _outproj(u_ctx, u_lat, xs, w_out, mods, norm_g, wr_t, name):
    x_stacked = len(xs) == 1
    x_specs = [_row_spec(D_MODEL)] if x_stacked else [_ctx_row_spec(D_MODEL), _lat_row_spec(D_MODEL)]
    return pl.pallas_call(
        functools.partial(_outproj_kernel, x_stacked=x_stacked),
        grid=(N_TILES,),
        in_specs=[_ctx_row_spec(D_MODEL), _lat_row_spec(D_MODEL)] + x_specs + [
            _weight_spec(D_MODEL, D_MODEL),
            _mod_spec(2),
            _full_spec((1, D_MODEL)),
            _mod_spec(3),
            _mod_spec(4),
            _full_spec((LANES, D_MODEL)),
        ],
        out_specs=[
            _row_spec(D_MODEL),
            pl.BlockSpec((ROW_TILE * TOK_SLABS, LANES), lambda i: (i, 0)),
            pl.BlockSpec((LANES, ROW_TILE), lambda i: (0, i)),
        ],
        out_shape=[
            jax.ShapeDtypeStruct((N_ALL, D_MODEL), F32),
            jax.ShapeDtypeStruct((N_ALL * TOK_SLABS, LANES), F32),
            jax.ShapeDtypeStruct((LANES, N_ALL), F32),
        ],
        scratch_shapes=[pltpu.VMEM((D_MODEL, D_MODEL), BF16)],
        compiler_params=_cp(48),
        name=name,
    )(u_ctx, u_lat, *xs, w_out, mods, norm_g, mods, mods, wr_t)


N_TOK_BLOCKS = N_TOK // LANES
STACK_ROWS = N_TOK_BLOCKS * N_EXPERTS


def _router_kernel(lt_ref, tri_ref, blk_ref, idx_ref, gate_ref, cwin_ref, aff_ref, cend_ref, ctot_ref):
    lt = lt_ref[...]
    e = jnp.exp(lt - jnp.max(lt, axis=0, keepdims=True))
    aff = e / jnp.sum(e, axis=0, keepdims=True)
    aff = jnp.where(aff >= F32_MIN_NORMAL, aff, 0.0)

    def count_ge(x):
        return jnp.sum((aff >= x).astype(F32), axis=1, keepdims=True)

    def step(mid_of):
        def body(_, c):
            lo, hi = c
            mid = jnp.minimum(jnp.maximum(mid_of(lo, hi), lo), hi)
            ok = count_ge(mid) >= float(CAPACITY)
            return jnp.where(ok, mid, lo), jnp.where(ok, hi, mid)
        return body

    lo = jnp.full((N_EXPERTS, 1), F32_MIN_NORMAL, F32)
    hi = jnp.full((N_EXPERTS, 1), 2.0, F32)
    lo, hi = lax.fori_loop(0, 8, step(lambda a, b: jnp.sqrt(a * b)), (lo, hi))
    lo, hi = lax.fori_loop(0, 28, step(lambda a, b: a + (b - a) * 0.5), (lo, hi))
    kth = jnp.max(jnp.where(aff < hi, aff, 0.0), axis=1, keepdims=True)
    thr = jnp.where(count_ge(lo) >= float(CAPACITY), kth, 0.0)

    tri = tri_ref[...]
    blk = blk_ref[...]

    def stack(x):
        return jnp.concatenate([x[:, LANES * b:LANES * (b + 1)] for b in range(N_TOK_BLOCKS)], axis=0)

    def unstack(x):
        return jnp.concatenate([x[N_EXPERTS * b:N_EXPERTS * (b + 1), :] for b in range(N_TOK_BLOCKS)], axis=1)

    def cumsum_stacked(mask_st):
        win = _dot(mask_st.astype(BF16), tri)
        tot = win[:, LANES - 1:LANES]
        off = _dot(blk, jnp.broadcast_to(tot, (STACK_ROWS, LANES)).astype(BF16))[:, :1]
        return win, off, tot

    gt = aff > thr
    eq = aff == thr
    need = float(CAPACITY) - jnp.sum(gt.astype(F32), axis=1, keepdims=True)
    eq_f = eq.astype(F32)
    ewin, eoff, _ = cumsum_stacked(stack(eq_f))
    eq_before = unstack(ewin + eoff) - eq_f
    sel = jnp.logical_or(gt, jnp.logical_and(eq, eq_before < need))
    cwin, coff, ctot = cumsum_stacked(stack(sel.astype(F32)))
    cwin_ref[...] = cwin
    aff_ref[...] = stack(aff)
    cend_ref[...] = jnp.broadcast_to(coff + ctot, (STACK_ROWS, LANES))
    ctot_ref[...] = jnp.broadcast_to(ctot, (STACK_ROWS, LANES))

    blk_id = lax.broadcasted_iota(jnp.int32, (N_TOK_BLOCKS, 1), 0).astype(F32)
    slot = lax.broadcasted_iota(jnp.int32, (1, CAPACITY), 1).astype(F32)
    lane_f = lax.broadcasted_iota(jnp.int32, (LANES, 1), 0).astype(F32)

    def per_expert(ex):
        rows = pl.ds(ex, N_TOK_BLOCKS, stride=N_EXPERTS)
        cend = cend_ref[rows, :][:, :1]
        ctot_e = ctot_ref[rows, :][:, :1]
        before = cend <= slot
        nblk = jnp.sum(before.astype(F32), axis=0, keepdims=True)
        base = jnp.sum(jnp.where(before, ctot_e, 0.0), axis=0, keepdims=True)
        pick = jnp.where(blk_id == nblk, 1.0, 0.0).astype(BF16)
        cnt_in = _dot_tn(cwin_ref[rows, :].astype(BF16), pick)
        lane_idx = jnp.sum((cnt_in <= slot - base).astype(F32), axis=0, keepdims=True)
        idx_ref[pl.ds(ex, 1), :] = ((nblk * float(LANES) + lane_idx) * float(TOK_SLABS)).astype(jnp.int32)
        a_hi, a_mid, a_lo = _split3(aff_ref[rows, :])
        aff_blk = _dot_tn(a_hi, pick) + _dot_tn(a_mid, pick) + _dot_tn(a_lo, pick)
        gate_ref[pl.ds(ex, 1), :] = jnp.sum(jnp.where(lane_f == lane_idx, aff_blk, 0.0), axis=0, keepdims=True)

    def expert_pair(i, carry):
        per_expert(2 * i)
        per_expert(2 * i + 1)
        return carry

    lax.fori_loop(0, N_EXPERTS // 2, expert_pair, 0)


def _cumsum_consts():
    l = np.arange(LANES)
    tri = (l[:, None] <= l[None, :]).astype(np.float32)
    r = np.arange(STACK_ROWS)
    same = (r[:, None] % N_EXPERTS) == (r[None, :] % N_EXPERTS)
    earlier = (r[None, :] // N_EXPERTS) < (r[:, None] // N_EXPERTS)
    blk = (same & earlier).astype(np.float32)
    return jnp.asarray(tri, BF16), jnp.asarray(blk, BF16)


def _router(logits_t, token_set, tri, blk, name):
    return pl.pallas_call(
        _router_kernel,
        grid=(1,),
        in_specs=[
            pl.BlockSpec((N_EXPERTS, N_TOK), lambda i: (0, token_set)),
            _full_spec((LANES, LANES)),
            _full_spec((STACK_ROWS, STACK_ROWS)),
        ],
        out_specs=[_full_spec((N_EXPERTS, CAPACITY))] * 2,
        out_shape=[jax.ShapeDtypeStruct((N_EXPERTS, CAPACITY), jnp.int32),
                   jax.ShapeDtypeStruct((N_EXPERTS, CAPACITY), F32)],
        scratch_shapes=[pltpu.VMEM((STACK_ROWS, LANES), F32)] * 4,
        compiler_params=_cp(48),
        name=name,
    )(logits_t, tri, blk)


GATHER_UNROLL = 16
ROWS_PER_EXPERT = 2 * CAPACITY
GATHER_PER_STEP = ROWS_PER_EXPERT // (D_EXPERT // FF_TILE)
TILE_PITCH = TOK_SLABS + 1


def _ffn_kernel(off0_ref, offn_ref, hs_ref, g_ref, w1_ref, w3_ref, w2_ref, yc_ref, yl_ref,
                tile_ref, xe_ref, acc_ref, sem):
    e = pl.program_id(0)
    f = pl.program_id(1)
    last_e = pl.num_programs(0) - 1
    last_f = pl.num_programs(1) - 1

    def row_copy(off_ref, row):
        off = pl.multiple_of(off_ref[0, row], TOK_SLABS)
        return pltpu.make_async_copy(hs_ref.at[pl.ds(off, TOK_SLABS)],
                                     tile_ref.at[pl.ds(row * TILE_PITCH, TOK_SLABS)], sem)

    def wait_tile():
        n = ROWS_PER_EXPERT * TOK_SLABS
        pltpu.make_async_copy(hs_ref.at[pl.ds(0, n)], tile_ref.at[pl.ds(0, n)], sem).wait()

    @pl.when(jnp.logical_and(e == 0, f == 0))
    def _():
        def issue(i, carry):
            for u in range(GATHER_UNROLL):
                row_copy(off0_ref, i * GATHER_UNROLL + u).start()
            return carry
        lax.fori_loop(0, ROWS_PER_EXPERT // GATHER_UNROLL, issue, 0)

    @pl.when(f == 0)
    def _():
        wait_tile()
        xe_ref[...] = _from_slabs(tile_ref, ROWS_PER_EXPERT, pitch=TILE_PITCH).astype(BF16)
        acc_ref[...] = jnp.zeros_like(acc_ref)

    for u in range(GATHER_PER_STEP):
        row_copy(offn_ref, f * GATHER_PER_STEP + u).start()

    x = xe_ref[...]
    h1 = _dot(x, w1_ref[...].astype(BF16))
    h3 = _dot(x, w3_ref[...].astype(BF16))
    hid = (h1 * jax.nn.sigmoid(h1) * h3).astype(BF16)
    acc_ref[...] += _dot(hid, w2_ref[...].astype(BF16))

    @pl.when(f == last_f)
    def _():
        gate = jnp.broadcast_to(g_ref[...], (SUBLANES, ROWS_PER_EXPERT)).T[:, :1]
        _to_slabs(yc_ref, acc_ref[:CAPACITY, :] * gate[:CAPACITY], CAPACITY)
        _to_slabs(yl_ref, acc_ref[CAPACITY:, :] * gate[CAPACITY:], CAPACITY)

    @pl.when(jnp.logical_and(e == last_e, f == last_f))
    def _():
        wait_tile()


def _expert_ffn(offsets, hs, gates, w1, w3, w2, layer, name):
    nf = D_EXPERT // FF_TILE
    off_block = (None, 1, ROWS_PER_EXPERT)
    y_spec = pl.BlockSpec((None, CAPACITY * TOK_SLABS, LANES), lambda e, f: (e, 0, 0))
    y_shape = jax.ShapeDtypeStruct((N_EXPERTS, CAPACITY * TOK_SLABS, LANES), F32)
    return pl.pallas_call(
        _ffn_kernel,
        grid=(N_EXPERTS, nf),
        in_specs=[
            pl.BlockSpec(off_block, lambda e, f: (0, 0, 0), memory_space=pltpu.SMEM),
            pl.BlockSpec(off_block, lambda e, f: (jnp.minimum(e + 1, N_EXPERTS - 1), 0, 0), memory_space=pltpu.SMEM),
            pl.BlockSpec(memory_space=pl.ANY),
            pl.BlockSpec(off_block, lambda e, f: (e, 0, 0)),
            pl.BlockSpec((None, None, D_MODEL, FF_TILE), lambda e, f: (layer, e, 0, f)),
            pl.BlockSpec((None, None, D_MODEL, FF_TILE), lambda e, f: (layer, e, 0, f)),
            pl.BlockSpec((None, None, FF_TILE, D_MODEL), lambda e, f: (layer, e, f, 0)),
        ],
        out_specs=[y_spec, y_spec],
        out_shape=[y_shape, y_shape],
        scratch_shapes=[
            pltpu.VMEM((ROWS_PER_EXPERT * TILE_PITCH, LANES), F32),
            pltpu.VMEM((ROWS_PER_EXPERT, D_MODEL), BF16),
            pltpu.VMEM((ROWS_PER_EXPERT, D_MODEL), F32),
            pltpu.SemaphoreType.DMA(()),
        ],
        compiler_params=_cp(56),
        name=name,
    )(offsets, offsets, hs, gates, w1, w3, w2)


SCATTER_UNROLL = 16


def _scatter_expert(off_ref, ye_ref, y_ref, ex):
    def body(i, carry):
        upd = []
        for u in range(SCATTER_UNROLL):
            s = i * SCATTER_UNROLL + u
            rows = pl.ds(pl.multiple_of(off_ref[ex * CAPACITY + s], TOK_SLABS), TOK_SLABS)
            src = ye_ref[pl.ds(pl.multiple_of(s * TOK_SLABS, TOK_SLABS), TOK_SLABS), :]
            upd.append((rows, y_ref[rows, :] + src))
        for rows, val in upd:
            y_ref[rows, :] = val
        return carry

    lax.fori_loop(0, CAPACITY // SCATTER_UNROLL, body, 0)


def _combine_kernel(idx_ref, ye_ref, y_ref):
    ex = pl.program_id(0)

    @pl.when(ex == 0)
    def _():
        y_ref[...] = jnp.zeros_like(y_ref)

    _scatter_expert(idx_ref, ye_ref, y_ref, ex)


def _combine_final_kernel(idx_ref, ye_ref, xp_ref, gp_ref, fg_ref, o_ref, y_ref):
    i = pl.program_id(0)

    @pl.when(i == 0)
    def _():
        y_ref[...] = jnp.zeros_like(y_ref)

    @pl.when(i < N_EXPERTS)
    def _():
        _scatter_expert(idx_ref, ye_ref, y_ref, i)

    @pl.when(i >= N_EXPERTS)
    def _():
        slabs_per_tile = ROW_TILE * TOK_SLABS
        base = pl.multiple_of((i - N_EXPERTS) * slabs_per_tile, slabs_per_tile)
        x = xp_ref[...] + gp_ref[...] * _from_slabs(y_ref, ROW_TILE, base)
        o_ref[...] = _rmsnorm(x, fg_ref[...])


def _combine_final(idx_flat, ye, x_prev, token_set, mods_prev, final_g, name):
    def tile(i):
        return jnp.maximum(i - N_EXPERTS, 0)

    def gate_row(i, idx):
        return (_tile_group(token_set * N_ROW_TILES + tile(i)) * 6 + 5, 0, 0)

    return pl.pallas_call(
        _combine_final_kernel,
        grid_spec=pltpu.PrefetchScalarGridSpec(
            num_scalar_prefetch=1,
            grid=(N_EXPERTS + N_ROW_TILES,),
            in_specs=[
                pl.BlockSpec((None, CAPACITY * TOK_SLABS, LANES), lambda i, idx: (jnp.minimum(i, N_EXPERTS - 1), 0, 0)),
                pl.BlockSpec((ROW_TILE, D_MODEL), lambda i, idx: (token_set * N_ROW_TILES + tile(i), 0)),
                pl.BlockSpec((None, 1, D_MODEL), gate_row),
                pl.BlockSpec((1, D_MODEL), lambda i, idx: (0, 0)),
            ],
            out_specs=pl.BlockSpec((ROW_TILE, D_MODEL), lambda i, idx: (tile(i), 0)),
            scratch_shapes=[pltpu.VMEM((N_TOK * TOK_SLABS, LANES), F32)],
        ),
        out_shape=jax.ShapeDtypeStruct((N_TOK, D_MODEL), F32),
        compiler_params=_cp(40),
        name=name,
    )(idx_flat, ye, x_prev, mods_prev, final_g)


def _combine(idx_flat, ye, name):
    return pl.pallas_call(
        _combine_kernel,
        grid_spec=pltpu.PrefetchScalarGridSpec(
            num_scalar_prefetch=1,
            grid=(N_EXPERTS,),
            in_specs=[pl.BlockSpec((None, CAPACITY * TOK_SLABS, LANES), lambda e, idx: (e, 0, 0))],
            out_specs=pl.BlockSpec((N_TOK * TOK_SLABS, LANES), lambda e, idx: (0, 0)),
        ),
        out_shape=jax.ShapeDtypeStruct((N_TOK * TOK_SLABS, LANES), F32),
        compiler_params=_cp(56),
        name=name,
    )(idx_flat, ye)


def _moe(hs, lt, tri, blk, w1, w3, w2, layer):
    tag = "l%d" % layer
    off_c, g_c = _router(lt, 0, tri, blk, "router_%s_ctx" % tag)
    off_l, g_l = _router(lt, 1, tri, blk, "router_%s_lat" % tag)
    offsets = jnp.concatenate([off_c, off_l + N_TOK * TOK_SLABS], axis=1).reshape(N_EXPERTS, 1, ROWS_PER_EXPERT)
    gates = jnp.concatenate([g_c, g_l], axis=1).reshape(N_EXPERTS, 1, ROWS_PER_EXPERT)
    ye_c, ye_l = _expert_ffn(offsets, hs, gates, w1, w3, w2, layer, "ffn_" + tag)
    return (off_c.reshape(N_EXPERTS * CAPACITY), ye_c), (off_l.reshape(N_EXPERTS * CAPACITY), ye_l)


def _gla_proj_kernel(xp_ref, yc_ref, yl_ref, gp_ref, g_ref, sh_ref, sc_ref, w_ref, wt_ref, wg_ref, bg_ref,
                     x_ref, q_ref, k_ref, v_ref, r_ref, lgf_ref, lgb_ref, wb_ref):
    _cast_once(w_ref, wb_ref)

    def body(latent):
        yt_ref = yl_ref if latent else yc_ref
        x = xp_ref[...] + gp_ref[...] * _from_slabs(yt_ref, ROW_TILE)
        x_ref[...] = x
        h = _modulate(x, g_ref[...], sh_ref[...], sc_ref[...]).astype(BF16)
        z = _dot(h, wb_ref[...])
        nq = GLA_HEADS * GLA_DK
        nv = GLA_HEADS * GLA_DV
        q_ref[...] = z[:, :nq] * (GLA_DK ** -0.5)
        k_ref[...] = z[:, nq:2 * nq]
        v_ref[...] = z[:, 2 * nq:2 * nq + nv].astype(BF16)
        r_ref[...] = z[:, 2 * nq + nv:].astype(BF16)
        zg = _dot(h, wt_ref[...]).astype(BF16)
        a = _dot(zg, wg_ref[...]) + bg_ref[...]
        ls = (jnp.minimum(a, 0.0) - jnp.log1p(jnp.exp(-jnp.abs(a)))) * (1.0 / GLA_GATE_NORM)
        lgf_ref[...] = ls[:, :nq]
        lgb_ref[...] = ls[:, nq:]

    _per_set(body)


def _gla_proj(x_prev, y_ctx, y_lat, mods_prev, mods, norm_g, w_in, w_tail, wg, bg):
    nq = GLA_HEADS * GLA_DK
    nv = GLA_HEADS * GLA_DV
    n_main = 2 * nq + 2 * nv
    slabs = ROW_TILE * TOK_SLABS
    return pl.pallas_call(
        _gla_proj_kernel,
        grid=(N_TILES,),
        in_specs=[
            _row_spec(D_MODEL),
            pl.BlockSpec((slabs, LANES), lambda i: (jnp.minimum(i, N_ROW_TILES - 1), 0)),
            pl.BlockSpec((slabs, LANES), lambda i: (jnp.maximum(i - N_ROW_TILES, 0), 0)),
            _mod_spec(5),
            _full_spec((1, D_MODEL)),
            _mod_spec(0),
            _mod_spec(1),
            _weight_spec(D_MODEL, n_main),
            _full_spec((D_MODEL, LANES)),
            _full_spec((LANES, 2 * nq)),
            _full_spec((1, 2 * nq)),
        ],
        out_specs=[_row_spec(D_MODEL), _row_spec(nq), _row_spec(nq), _row_spec(nv), _row_spec(nv),
                   _row_spec(nq), _row_spec(nq)],
        out_shape=[
            jax.ShapeDtypeStruct((N_ALL, D_MODEL), F32),
            jax.ShapeDtypeStruct((N_ALL, nq), F32),
            jax.ShapeDtypeStruct((N_ALL, nq), F32),
            jax.ShapeDtypeStruct((N_ALL, nv), BF16),
            jax.ShapeDtypeStruct((N_ALL, nv), BF16),
            jax.ShapeDtypeStruct((N_ALL, nq), F32),
            jax.ShapeDtypeStruct((N_ALL, nq), F32),
        ],
        scratch_shapes=[pltpu.VMEM((D_MODEL, n_main), BF16)],
        compiler_params=_cp(56),
        name="gla_proj",
    )(x_prev, y_ctx, y_lat, mods_prev, norm_g, mods, mods, w_in, w_tail, wg, bg)


GLA_LEVELS = (32, 16, 8)
GLA_UNROLL = 8
GLA_GROUP = 4 * GLA_CHUNK
GLA_SAFE_DECAY = 60.0
NEG_BIG = -1e30


def _gla_intra_fast(qe, kc, b2, fwd):
    c = GLA_CHUNK
    row = lax.broadcasted_iota(jnp.int32, (c, 1), 0)
    col = lax.broadcasted_iota(jnp.int32, (1, c), 1)
    keep = (col <= row) if fwd else (col >= row)
    kq = (kc * jnp.exp2(-b2)).astype(BF16)
    return jnp.where(keep, _dot_nt(qe, kq), 0.0)


def _gla_intra_robust(qc, kc, b2, fwd):
    c = GLA_CHUNK
    row = lax.broadcasted_iota(jnp.int32, (c, 1), 0)
    col = lax.broadcasted_iota(jnp.int32, (1, c), 1)
    a = jnp.zeros((c, c), F32)
    for g in GLA_LEVELS:
        odd = ((row >> int(math.log2(g))) & 1) == 1
        later = odd if fwd else jnp.logical_not(odd)
        refs = []
        for p in range(c // (2 * g)):
            r0 = 2 * g * p + (g - 1 if fwd else g)
            refs.append(jnp.broadcast_to(b2[r0:r0 + 1], (2 * g, GLA_DK)))
        ref = jnp.concatenate(refs, axis=0) if len(refs) > 1 else refs[0]
        qt = qc * jnp.exp2(jnp.where(later, b2 - ref, NEG_BIG))
        kt = kc * jnp.exp2(jnp.where(later, NEG_BIG, ref - b2))
        same_parent = (row >> int(math.log2(2 * g))) == (col >> int(math.log2(2 * g)))
        a = a + jnp.where(same_parent, _dot_nt(qt.astype(BF16), kt.astype(BF16)), 0.0)

    sub = lax.broadcasted_iota(jnp.int32, (SUBLANES, 1), 0)
    strips = []
    for blk in range(c // SUBLANES):
        r0 = SUBLANES * blk
        qb = qc[r0:r0 + SUBLANES]
        bb = b2[r0:r0 + SUBLANES]
        strip = jnp.zeros((SUBLANES, c), F32)
        for jj in range(SUBLANES):
            j = r0 + jj
            cond = (sub >= jj) if fwd else (sub <= jj)
            t = qb * kc[j:j + 1] * jnp.exp2(jnp.where(cond, bb - b2[j:j + 1], NEG_BIG))
            strip = jnp.where(col == j, jnp.sum(t, axis=1, keepdims=True), strip)
        strips.append(strip)
    return a + jnp.concatenate(strips, axis=0)


def _gla_state_increment(kc, vc, b2f, b2b):
    c = GLA_CHUNK
    btf = b2f[c - 1:c]
    btb = b2b[0:1]
    kd = jnp.concatenate([(kc * jnp.exp2(btf - b2f)).astype(BF16), (kc * jnp.exp2(btb - b2b)).astype(BF16)], axis=1)
    return _dot_tn(vc, kd), jnp.concatenate([jnp.exp2(btf), jnp.exp2(btb)], axis=1)


def _gla_scores(qc, kc, b2f, b2b, fast):
    qe_f = (qc * jnp.exp2(b2f)).astype(BF16)
    qe_b = (qc * jnp.exp2(b2b)).astype(BF16)
    if fast:
        a = _gla_intra_fast(qe_f, kc, b2f, True) + _gla_intra_fast(qe_b, kc, b2b, False)
    else:
        a = _gla_intra_robust(qc, kc, b2f, True) + _gla_intra_robust(qc, kc, b2b, False)
    return jnp.concatenate([qe_f, qe_b], axis=1), a.astype(BF16)


def _gla_kernel(q_ref, k_ref, v_ref, r_ref, gf_ref, gb_ref, *refs, t_len, heads, zero_state):
    if zero_state:
        s0f_ref = s0b_ref = None
    else:
        s0f_ref, s0b_ref = refs[:2]
        refs = refs[2:]
    go_ref, trif_ref, trib_ref, u_ref, sf_ref, sb_ref = refs[:6]
    scratch = refs[6:]
    _gla_passes(q_ref, k_ref, v_ref, r_ref, gf_ref, gb_ref, s0f_ref, s0b_ref, go_ref, trif_ref, trib_ref,
                u_ref, sf_ref, sb_ref, *scratch, t_len=t_len, heads=heads)


def _gla_passes(q_ref, k_ref, v_ref, r_ref, gf_ref, gb_ref, s0f_ref, s0b_ref, go_ref, trif_ref, trib_ref,
                u_ref, sf_ref, sb_ref, *scratch, t_len, heads):
    c = GLA_CHUNK
    n = t_len // c
    per_head = len(scratch) // heads
    hd = []
    for hh in range(heads):
        dk = slice(hh * GLA_DK, (hh + 1) * GLA_DK)
        dv = slice(hh * GLA_DV, (hh + 1) * GLA_DV)
        b2f, b2b, qe, oi, kv, dec, sst, st = scratch[hh * per_head:(hh + 1) * per_head]
        hd.append(dict(q=q_ref.at[:, dk], k=k_ref.at[:, dk], v=v_ref.at[:, dv], r=r_ref.at[:, dv],
                       gf=gf_ref.at[:, dk], gb=gb_ref.at[:, dk],
                       s0f=None if s0f_ref is None else s0f_ref.at[hh],
                       s0b=None if s0b_ref is None else s0b_ref.at[hh],
                       u=u_ref.at[:, dv], sf=sf_ref.at[hh], sb=sb_ref.at[hh],
                       b2f=b2f, b2b=b2b, qe=qe, oi=oi, kv=kv, dec=dec, sst=sst, st=st))

    def rows_of(ci):
        return pl.ds(pl.multiple_of(ci * c, c), c)

    def state_rows_of(ci):
        return pl.ds(pl.multiple_of(ci * GLA_DV, GLA_DV), GLA_DV)

    n_groups = t_len // GLA_GROUP
    cum_unroll = 2 if n_groups % 2 == 0 else 1

    def cum_body(i, carry):
        sums = []
        for h in hd:
            for u in range(cum_unroll):
                rows = pl.ds(pl.multiple_of((i * cum_unroll + u) * GLA_GROUP, GLA_GROUP), GLA_GROUP)
                for g_ref, tri_ref, b_ref in ((h["gf"], trif_ref, h["b2f"]), (h["gb"], trib_ref, h["b2b"])):
                    s3 = _dot(tri_ref[...], jnp.concatenate(_split3(g_ref[rows, :]), axis=1))
                    sums.append((b_ref, rows, s3))
        for b_ref, rows, s3 in sums:
            b_ref[rows, :] = (s3[:, :GLA_DK] + s3[:, GLA_DK:2 * GLA_DK] + s3[:, 2 * GLA_DK:]) * LOG2E
        return carry

    lax.fori_loop(0, n_groups // cum_unroll, cum_body, 0)

    def chunk_total(g_ref):
        return jnp.min(jnp.sum(g_ref[...].reshape(n, c, GLA_DK), axis=1))

    totals = [chunk_total(h[g]) for h in hd for g in ("gf", "gb")]
    safe = functools.reduce(jnp.minimum, totals) >= -GLA_SAFE_DECAY

    def make_local(fast, unroll):
        def body(i, carry):
            work = [(h, i * unroll + u) for h in hd for u in range(unroll)]
            for h, ci in work:
                rows = rows_of(ci)
                kv, dec = _gla_state_increment(h["k"][rows, :], h["v"][rows, :], h["b2f"][rows, :], h["b2b"][rows, :])
                h["kv"][state_rows_of(ci), :] = kv
                h["dec"][pl.ds(ci, 1), :] = dec
            scores = []
            for h, ci in work:
                rows = rows_of(ci)
                qe, a = _gla_scores(h["q"][rows, :], h["k"][rows, :], h["b2f"][rows, :], h["b2b"][rows, :], fast)
                h["qe"][rows, :] = qe
                scores.append(a)
            for (h, ci), a in zip(work, scores):
                rows = rows_of(ci)
                h["oi"][rows, :] = _dot(a, h["v"][rows, :])
            return carry
        return body

    unroll = min(GLA_UNROLL, n)

    @pl.when(safe)
    def _():
        lax.fori_loop(0, n // unroll, make_local(True, unroll), 0)

    @pl.when(jnp.logical_not(safe))
    def _():
        lax.fori_loop(0, n, make_local(False, 1), 0)

    fl = slice(0, GLA_DK)
    bl = slice(GLA_DK, 2 * GLA_DK)
    for h in hd:
        if h["s0f"] is None:
            h["st"][...] = jnp.zeros_like(h["st"])
        else:
            h["st"][...] = jnp.concatenate([h["s0f"][...].T, h["s0b"][...].T], axis=1)

    def rec_body(i, carry):
        cf = i
        cb = n - 1 - i
        for h in hd:
            st = h["st"][...]
            h["sst"][state_rows_of(cf), fl] = st[:, fl].astype(BF16)
            h["sst"][state_rows_of(cb), bl] = st[:, bl].astype(BF16)
            dec = jnp.concatenate([h["dec"][pl.ds(cf, 1), :][:, fl], h["dec"][pl.ds(cb, 1), :][:, bl]], axis=1)
            kv = jnp.concatenate([h["kv"][state_rows_of(cf), fl], h["kv"][state_rows_of(cb), bl]], axis=1)
            h["st"][...] = st * dec + kv
        return carry

    lax.fori_loop(0, n, rec_body, 0)
    for h in hd:
        h["sf"][...] = h["st"][:, fl].T
        h["sb"][...] = h["st"][:, bl].T

    go = go_ref[...]

    def out_body(i, carry):
        work = [(h, i * unroll + u) for h in hd for u in range(unroll)]
        inter = [_dot_nt(h["qe"][rows_of(ci), :], h["sst"][state_rows_of(ci), :]) for h, ci in work]
        for (h, ci), o_inter in zip(work, inter):
            rows = rows_of(ci)
            r = h["r"][rows, :].astype(F32)
            h["u"][rows, :] = (_rmsnorm(h["oi"][rows, :] + o_inter, go) * (r * jax.nn.sigmoid(r))).astype(BF16)
        return carry

    lax.fori_loop(0, n // unroll, out_body, 0)


def _gla_tri():
    i = np.arange(GLA_GROUP)
    same = (i[:, None] // GLA_CHUNK) == (i[None, :] // GLA_CHUNK)
    fwd = (same & (i[None, :] <= i[:, None])).astype(np.float32)
    bwd = (same & (i[None, :] >= i[:, None])).astype(np.float32)
    return jnp.asarray(fwd, BF16), jnp.asarray(bwd, BF16)


def _gla(q, k, v, r, lgf, lgb, batch0, s0f, s0b, g_out, n_batch, t_len, heads, name):
    zero_state = s0f is None
    trif, trib = _gla_tri()
    n_chunks = t_len // GLA_CHUNK
    qk_spec = pl.BlockSpec((None, t_len, heads * GLA_DK), lambda b, h: (batch0 + b, 0, h))
    vin_spec = pl.BlockSpec((None, t_len, heads * GLA_DV), lambda b, h: (batch0 + b, 0, h))
    v_spec = pl.BlockSpec((None, t_len, heads * GLA_DV), lambda b, h: (b, 0, h))
    s_spec = pl.BlockSpec((None, heads, GLA_DK, GLA_DV), lambda b, h: (b, h, 0, 0))
    const = lambda shape: pl.BlockSpec(shape, lambda b, h: (0,) * len(shape))
    s_shape = jax.ShapeDtypeStruct((n_batch, GLA_HEADS, GLA_DK, GLA_DV), F32)
    states = [] if zero_state else [s0f, s0b]
    return pl.pallas_call(
        functools.partial(_gla_kernel, t_len=t_len, heads=heads, zero_state=zero_state),
        grid=(n_batch, GLA_HEADS // heads),
        in_specs=[qk_spec, qk_spec, vin_spec, vin_spec, qk_spec, qk_spec] + [s_spec] * len(states)
        + [const((1, GLA_DV)), const((GLA_GROUP, GLA_GROUP)), const((GLA_GROUP, GLA_GROUP))],
        out_specs=[v_spec, s_spec, s_spec],
        out_shape=[jax.ShapeDtypeStruct((n_batch, t_len, GLA_HEADS * GLA_DV), BF16), s_shape, s_shape],
        scratch_shapes=[
            pltpu.VMEM((t_len, GLA_DK), F32), pltpu.VMEM((t_len, GLA_DK), F32),
            pltpu.VMEM((t_len, 2 * GLA_DK), BF16),
            pltpu.VMEM((t_len, GLA_DV), F32),
            pltpu.VMEM((n_chunks * GLA_DV, 2 * GLA_DK), F32),
            pltpu.VMEM((max(n_chunks, SUBLANES), 2 * GLA_DK), F32),
            pltpu.VMEM((n_chunks * GLA_DV, 2 * GLA_DK), BF16),
            pltpu.VMEM((GLA_DV, 2 * GLA_DK), F32),
        ] * heads,
        compiler_params=_cp(48),
        name=name,
    )(q, k, v, r, lgf, lgb, *states, g_out, trif, trib)


def kernel(x_prompt, x_sample, cache_k, cache_v, state_fwd, state_bwd, c, c_ctx, w_mod, b_mod, norm_g,
           da_w_in, da_w_out, da_lam_q1, da_lam_k1, da_lam_q2, da_lam_k2, da_g_sub, gla_w_in, gla_w_gf2,
           gla_b_gf, gla_w_gb2, gla_b_gb, gla_g_out, gla_w_out, moe_w_router, moe_w1, moe_w3, moe_w2, final_g):
    n_ctx_b = x_prompt.shape[0]
    n_lat_b = x_sample.shape[0]
    x_ctx = x_prompt.reshape(N_TOK, D_MODEL)
    x_lat = x_sample.reshape(N_TOK, D_MODEL)

    cvec = jnp.zeros((SUBLANES, D_MODEL), F32).at[0].set(c_ctx).at[1:1 + n_lat_b].set(c)
    mods = _mod_params(cvec, w_mod, b_mod)
    tri, blk = _cumsum_consts()
    wr_t = [jnp.zeros((LANES, D_MODEL), BF16).at[:N_EXPERTS].set(moe_w_router[i].T.astype(BF16))
            for i in range(DEPTH)]
    ng = norm_g.reshape(DEPTH, 2, 1, D_MODEL)

    lam_init = 0.8 - 0.6 * math.exp(-0.3 * 0)
    w_in = da_w_in
    w_out = da_w_out
    lam_vecs = [v[0].reshape(1, DA_DIM) for v in (da_lam_q1, da_lam_k1, da_lam_q2, da_lam_k2)]
    g_sub = da_g_sub[0].reshape(1, DA_VDIM)

    q, k, v, kf, vf = _da_proj(x_ctx, x_lat, mods[0], ng[0, 0], w_in, _rope_tables())
    past = cache_k.shape[2]
    ck = cache_k[:, 0].reshape(n_lat_b * past, D_MODEL)
    cv = cache_v[:, 0].reshape(n_lat_b * past, D_MODEL)
    u_ctx = _attention(lam_vecs, g_sub, q, 0, [(k, v, 0, SEQ)], n_ctx_b, SEQ, ATTN_Q_TILE, lam_init, "attn_ctx")
    u_lat = _attention(lam_vecs, g_sub, q, N_TOK, [(k, v, N_TOK, DEC_SEQ), (ck, cv, 0, past)], n_lat_b, DEC_SEQ,
                       ATTN_Q_TILE, lam_init, "attn_lat")

    x1, hs, lt = _outproj(u_ctx, u_lat, (x_ctx, x_lat), w_out, mods[0], ng[0, 1], wr_t[0], "outproj0")
    (idx_c, ye_c), (idx_l, ye_l) = _moe(hs, lt, tri, blk, moe_w1, moe_w3, moe_w2, 0)
    y_ctx = _combine(idx_c, ye_c, "combine_l0_ctx")
    y_lat = _combine(idx_l, ye_l, "combine_l0_lat")

    nq = GLA_HEADS * GLA_DK
    n_main = 2 * nq + 2 * GLA_HEADS * GLA_DV
    w_tail = jnp.zeros((D_MODEL, LANES), F32).at[:, :2 * GLA_GATE_RANK].set(gla_w_in[0][:, n_main:]).astype(BF16)
    wg = jnp.zeros((LANES, 2 * nq), F32)
    wg = wg.at[:GLA_GATE_RANK, :nq].set(gla_w_gf2[0]).at[GLA_GATE_RANK:2 * GLA_GATE_RANK, nq:].set(gla_w_gb2[0])
    wg = wg.astype(BF16)
    bg = jnp.concatenate([gla_b_gf[0], gla_b_gb[0]]).reshape(1, 2 * nq)
    w_out1 = gla_w_out
    g_out = gla_g_out[0].reshape(1, GLA_DV)

    x2, *gla_in = _gla_proj(x1, y_ctx, y_lat, mods[0], mods[1], ng[1, 0], gla_w_in, w_tail, wg, bg)

    def gla_side(n_b, t_len, s0f, s0b, tag):
        seqs = [a.reshape(N_ALL // t_len, t_len, a.shape[-1]) for a in gla_in]
        batch0 = 0 if s0f is None else N_TOK // t_len
        heads = GLA_HEADS if t_len <= GLA_GROUP else 1
        u, sf, sb = _gla(*seqs, batch0, s0f, s0b, g_out, n_b, t_len, heads, "gla_" + tag)
        return u.reshape(N_TOK, D_MODEL), sf, sb

    ug_ctx, sf, sb = gla_side(n_ctx_b, SEQ, None, None, "ctx")
    ug_lat, _, _ = gla_side(n_lat_b, DEC_SEQ, state_fwd[:, 0], state_bwd[:, 0], "lat")

    x3, hs, lt = _outproj(ug_ctx, ug_lat, (x2,), w_out1, mods[1], ng[1, 1], wr_t[1], "outproj1")
    (idx_c, ye_c), (idx_l, ye_l) = _moe(hs, lt, tri, blk, moe_w1, moe_w3, moe_w2, 1)

    fg = final_g.reshape(1, D_MODEL)
    y_prompt = _combine_final(idx_c, ye_c, x3, 0, mods[1], fg, "final_ctx").reshape(x_prompt.shape)
    y_sample = _combine_final(idx_l, ye_l, x3, 1, mods[1], fg, "final_lat").reshape(x_sample.shape)
    new_k = kf.reshape(n_ctx_b, 1, SEQ, 2 * DA_HEADS, DA_DIM)
    new_v = vf.reshape(n_ctx_b, 1, SEQ, DA_HEADS, DA_VDIM)
    return (y_prompt, y_sample, new_k, new_v, sf[:, None], sb[:, None])
```

```python
import functools
import math

import numpy as np
import jax
import jax.numpy as jnp
from jax import lax
from jax.experimental import pallas as pl
from jax.experimental.pallas import tpu as pltpu

F32 = jnp.float32
BF16 = jnp.bfloat16

D_MODEL = 1024
DEPTH = 2
SEQ = 256
DEC_SEQ = 2048
GRID_W = 64
N_TOK = 4096
DA_HEADS = 8
DA_DIM = 64
DA_VDIM = 128
ROPE_BASE = 10000.0
GLA_HEADS = 4
GLA_DK = 128
GLA_DV = 256
GLA_GATE_RANK = 16
GLA_GATE_NORM = 16.0
GLA_CHUNK = 64
N_EXPERTS = 16
CAPACITY = 512
D_EXPERT = 2048
EPS = 1e-6
F32_MIN_NORMAL = 2.0 ** -126
LOG2E = 1.4426950408889634

LANES = 128
SUBLANES = 8
ROW_TILE = 512
FF_TILE = 512
TOK_SLABS = D_MODEL // LANES
MIB = 1024 * 1024


def _cp(vmem_mib, sem=None):
    return pltpu.CompilerParams(vmem_limit_bytes=vmem_mib * MIB, dimension_semantics=sem)


def _dot(a, b):
    return jnp.dot(a, b, preferred_element_type=F32)


def _dot_nt(a, b):
    return lax.dot_general(a, b, (((1,), (1,)), ((), ())), preferred_element_type=F32)


def _dot_tn(a, b):
    return lax.dot_general(a, b, (((0,), (0,)), ((), ())), preferred_element_type=F32)


def _rmsnorm(x, g):
    return x * lax.rsqrt(jnp.mean(x * x, axis=-1, keepdims=True) + EPS) * g


def _modulate(x, g, shift, scale):
    return _rmsnorm(x, g) * (1.0 + scale) + shift


def _split3(x):
    hi = x.astype(BF16)
    r = x - hi.astype(F32)
    mid = r.astype(BF16)
    lo = (r - mid.astype(F32)).astype(BF16)
    return hi, mid, lo


def _from_slabs(ref, rows, base=0, pitch=TOK_SLABS):
    return jnp.concatenate([ref[pl.ds(base + c, rows, stride=pitch), :] for c in range(TOK_SLABS)], axis=1)


def _to_slabs(ref, val, rows):
    for c in range(TOK_SLABS):
        ref[pl.ds(c, rows, stride=TOK_SLABS), :] = val[:, LANES * c:LANES * (c + 1)]


def _mod_kernel(c_ref, w_ref, b_ref, o_ref):
    c = c_ref[...]
    s = c * jax.nn.sigmoid(c)
    w = w_ref[...]
    s_hi = s.astype(BF16)
    s_lo = (s - s_hi.astype(F32)).astype(BF16)
    w_hi = w.astype(BF16)
    w_lo = (w - w_hi.astype(F32)).astype(BF16)
    o_ref[...] = _dot(s_hi, w_hi) + _dot(s_hi, w_lo) + _dot(s_lo, w_hi) + b_ref[...]


def _mod_params(cvec, w_mod, b_mod):
    n6 = 6 * D_MODEL
    out = pl.pallas_call(
        _mod_kernel,
        grid=(DEPTH, 6),
        in_specs=[
            pl.BlockSpec((SUBLANES, D_MODEL), lambda i, j: (0, 0)),
            pl.BlockSpec((None, D_MODEL, D_MODEL), lambda i, j: (i, 0, j)),
            pl.BlockSpec((None, 1, D_MODEL), lambda i, j: (i, 0, j)),
        ],
        out_specs=pl.BlockSpec((None, SUBLANES, D_MODEL), lambda i, j: (i, 0, j)),
        out_shape=jax.ShapeDtypeStruct((DEPTH, SUBLANES, n6), F32),
        compiler_params=_cp(32),
        name="mod_params",
    )(cvec, w_mod, b_mod.reshape(DEPTH, 1, n6))
    return out.reshape(DEPTH, SUBLANES * 6, 1, D_MODEL)


N_ROW_TILES = N_TOK // ROW_TILE
N_TILES = 2 * N_ROW_TILES
N_ALL = 2 * N_TOK


def _tile_group(i):
    return jnp.where(i < N_ROW_TILES, 0, 1 + (i - N_ROW_TILES) // (DEC_SEQ // ROW_TILE))


def _mod_spec(k):
    return pl.BlockSpec((None, 1, D_MODEL), lambda i: (_tile_group(i) * 6 + k, 0, 0))


def _row_spec(width):
    return pl.BlockSpec((ROW_TILE, width), lambda i: (i, 0))


def _ctx_row_spec(width):
    return pl.BlockSpec((ROW_TILE, width), lambda i: (jnp.minimum(i, N_ROW_TILES - 1), 0))


def _lat_row_spec(width):
    return pl.BlockSpec((ROW_TILE, width), lambda i: (jnp.maximum(i - N_ROW_TILES, 0), 0))


def _full_spec(shape):
    nd = len(shape)
    return pl.BlockSpec(shape, lambda i: (0,) * nd)


def _per_set(body):
    i = pl.program_id(0)

    @pl.when(i < N_ROW_TILES)
    def _():
        body(False)

    @pl.when(i >= N_ROW_TILES)
    def _():
        body(True)


def _weight_spec(rows, cols):
    return pl.BlockSpec((None, rows, cols), lambda i: (0, 0, 0), pipeline_mode=pl.Buffered(1))


def _cast_once(w_ref, wb_ref):
    @pl.when(pl.program_id(0) == 0)
    def _():
        wb_ref[...] = w_ref[...].astype(BF16)


def _da_proj_kernel(xc_ref, xl_ref, g_ref, sh_ref, sc_ref, w_ref, cos_ref, sin_ref,
                    q_ref, k_ref, v_ref, kf_ref, vf_ref, wb_ref):
    _cast_once(w_ref, wb_ref)

    def body(latent):
        x_ref = xl_ref if latent else xc_ref
        h = _modulate(x_ref[...], g_ref[...], sh_ref[...], sc_ref[...])
        z = _dot(h.astype(BF16), wb_ref[...])
        q = z[:, :D_MODEL]
        k = z[:, D_MODEL:2 * D_MODEL]
        v = z[:, 2 * D_MODEL:]
        if latent:
            reps = D_MODEL // LANES
            cos = jnp.concatenate([cos_ref[...]] * reps, axis=1)
            sin = jnp.concatenate([sin_ref[...]] * reps, axis=1)
            lane = lax.broadcasted_iota(jnp.int32, (1, D_MODEL), 1)
            first = (lane & 16) == 0

            def rot(t):
                partner = jnp.where(first, pltpu.roll(t, D_MODEL - 16, 1), pltpu.roll(t, 16, 1))
                return t * cos + partner * sin

            q = rot(q)
            k = rot(k)
        else:
            kf_ref[...] = k
            vf_ref[...] = v
        q_ref[...] = (q * (DA_DIM ** -0.5 * LOG2E)).astype(BF16)
        k_ref[...] = k.astype(BF16)
        v_ref[...] = v.astype(BF16)

    _per_set(body)


def _da_proj(x_ctx, x_lat, mods, norm_g, w_in, rope_tabs):
    tiles_per_batch = DEC_SEQ // ROW_TILE
    tab_spec = pl.BlockSpec((ROW_TILE, LANES), lambda i: (jnp.maximum(i - N_ROW_TILES, 0) % tiles_per_batch, 0))
    return pl.pallas_call(
        _da_proj_kernel,
        grid=(N_TILES,),
        in_specs=[
            _ctx_row_spec(D_MODEL),
            _lat_row_spec(D_MODEL),
            _full_spec((1, D_MODEL)),
            _mod_spec(0),
            _mod_spec(1),
            _weight_spec(D_MODEL, 3 * D_MODEL),
            tab_spec, tab_spec,
        ],
        out_specs=[_row_spec(D_MODEL)] * 3 + [_ctx_row_spec(D_MODEL)] * 2,
        out_shape=[jax.ShapeDtypeStruct((N_ALL, D_MODEL), BF16)] * 3 + [jax.ShapeDtypeStruct((N_TOK, D_MODEL), F32)] * 2,
        scratch_shapes=[pltpu.VMEM((D_MODEL, 3 * D_MODEL), BF16)],
        compiler_params=_cp(56),
        name="da_proj",
    )(x_ctx, x_lat, norm_g, mods, mods, w_in, *rope_tabs)


def _rope_tables():
    t = np.arange(DEC_SEQ)
    rows = (t // GRID_W).astype(np.float32)
    cols = (t % GRID_W).astype(np.float32)
    half = DA_DIM // 4
    freqs = (np.float32(ROPE_BASE) ** (-np.arange(half, dtype=np.float32) / np.float32(half))).astype(np.float32)
    ang_r = rows[:, None] * freqs
    ang_c = cols[:, None] * freqs
    cos64 = np.concatenate([np.cos(ang_r)] * 2 + [np.cos(ang_c)] * 2, axis=1)
    sin64 = np.concatenate([-np.sin(ang_r), np.sin(ang_r), -np.sin(ang_c), np.sin(ang_c)], axis=1)
    reps = LANES // DA_DIM
    return (jnp.asarray(np.concatenate([cos64] * reps, axis=1), F32),
            jnp.asarray(np.concatenate([sin64] * reps, axis=1), F32))


ATTN_Q_TILE = 256


def _attn_kernel(lq1_ref, lk1_ref, lq2_ref, lk2_ref, gs_ref, q_ref, *refs, lam_init, seg_lens, cache_keys):
    n_seg = len(seg_lens)
    kv_refs, o_ref = refs[:2 * n_seg], refs[2 * n_seg]
    k_refs = [kv_refs[2 * s] for s in range(n_seg)]
    if cache_keys:
        kc_ref = refs[2 * n_seg + 1]

        @pl.when(pl.program_id(1) == 0)
        def _():
            for s in range(2 * DA_HEADS):
                kc_ref[:, DA_DIM * s:DA_DIM * (s + 1)] = k_refs[-1][:, s, :].astype(BF16)

        k_refs[-1] = kc_ref
    lam =(jnp.exp(jnp.sum(lq1_ref[...] * lk1_ref[...], axis=-1, keepdims=True))
           - jnp.exp(jnp.sum(lq2_ref[...] * lk2_ref[...], axis=-1, keepdims=True)) + lam_init)
    lane = lax.broadcasted_iota(jnp.int32, (1, DA_VDIM), 1)
    first = lane < DA_DIM
    gs = gs_ref[...]
    tq = q_ref.shape[0]

    def softmax_pv(qq, ks, vs):
        ss = [_dot_nt(qq, kk) for kk in ks]
        m = functools.reduce(jnp.maximum, [jnp.max(s, axis=-1, keepdims=True) for s in ss])
        ps = [jnp.exp2(s - m) for s in ss]
        l = functools.reduce(jnp.add, [jnp.sum(p, axis=-1, keepdims=True) for p in ps])
        o = functools.reduce(jnp.add, [_dot(p.astype(BF16), vv) for p, vv in zip(ps, vs)])
        return o, l

    for h in range(DA_HEADS):
        sl = slice(DA_VDIM * h, DA_VDIM * (h + 1))
        qh = q_ref[:, sl]
        ks = [k_ref[:, sl].astype(BF16) for k_ref in k_refs]
        vs = [kv_refs[2 * s + 1][:, sl].astype(BF16) for s in range(n_seg)]
        zero = jnp.zeros_like(qh)
        qq = jnp.concatenate([jnp.where(first, qh, zero), jnp.where(first, zero, qh)], axis=0)
        oo, ll = softmax_pv(qq, ks, vs)
        o = oo[:tq] * (1.0 / ll[:tq]) - oo[tq:] * (lam / ll[tq:])
        o_ref[:, sl] = (_rmsnorm(o, gs) * (1.0 - lam_init)).astype(BF16)


def _attention(lam_vecs, g_sub, q, q_row0, kv_segs, n_batch, t_q, q_tile, lam_init, name, cache_keys=False):
    nq = t_q // q_tile
    q0 = q_row0 // q_tile
    vec_spec = pl.BlockSpec((1, DA_DIM), lambda b, i: (0, 0))
    kv_specs, kv_args = [], []
    for k, v, row0, t_k in kv_segs:
        kv_specs += [pl.BlockSpec((t_k, D_MODEL), lambda b, i, b0=row0 // t_k: (b0 + b, 0))] * 2
        kv_args += [k, v]
    seg_lens = tuple(t_k for _, _, _, t_k in kv_segs)
    scratch = []
    if cache_keys:
        kv_specs[-2] = pl.BlockSpec((None, None, seg_lens[-1], 2 * DA_HEADS, DA_DIM), lambda b, i: (b, 0, 0, 0, 0))
        scratch = [pltpu.VMEM((seg_lens[-1], D_MODEL), BF16)]
    return pl.pallas_call(
        functools.partial(_attn_kernel, lam_init=lam_init, seg_lens=seg_lens, cache_keys=cache_keys),
        grid=(n_batch, nq),
        in_specs=[vec_spec] * 4 + [
            pl.BlockSpec((1, DA_VDIM), lambda b, i: (0, 0)),
            pl.BlockSpec((q_tile, D_MODEL), lambda b, i: (q0 + b * nq + i, 0)),
        ] + kv_specs,
        out_specs=pl.BlockSpec((q_tile, D_MODEL), lambda b, i: (b * nq + i, 0)),
        out_shape=jax.ShapeDtypeStruct((n_batch * t_q, D_MODEL), BF16),
        scratch_shapes=scratch,
        compiler_params=_cp(56),
        name=name,
    )(*lam_vecs, g_sub, q, *kv_args)


def _outproj_kernel(*refs, x_stacked):
    uc_ref, ul_ref = refs[:2]
    n_x = 1 if x_stacked else 2
    x_refs = refs[2:2 + n_x]
    w_ref, gate_ref, g_ref, sh_ref, sc_ref, wr_ref, x1_ref, hs_ref, lt_ref, wb_ref = refs[2 + n_x:]
    _cast_once(w_ref, wb_ref)

    def body(latent):
        u_ref = ul_ref if latent else uc_ref
        x_ref = x_refs[0] if x_stacked else x_refs[int(latent)]
        x1 = x_ref[...] + gate_ref[...] * _dot(u_ref[...], wb_ref[...])
        x1_ref[...] = x1
        h2 = _modulate(x1, g_ref[...], sh_ref[...], sc_ref[...])
        _to_slabs(hs_ref, h2, ROW_TILE)
        lt_ref[...] = _dot_nt(wr_ref[...], h2.astype(BF16))

    _per_set(body)


def _outproj(u_ctx, u_lat, xs, w_out, mods, norm_g, wr_t, name):
    x_stacked = len(xs) == 1
    x_specs = [_row_spec(D_MODEL)] if x_stacked else [_ctx_row_spec(D_MODEL), _lat_row_spec(D_MODEL)]
    return pl.pallas_call(
        functools.partial(_outproj_kernel, x_stacked=x_stacked),
        grid=(N_TILES,),
        in_specs=[_ctx_row_spec(D_MODEL), _lat_row_spec(D_MODEL)] + x_specs + [
            _weight_spec(D_MODEL, D_MODEL),
            _mod_spec(2),
            _full_spec((1, D_MODEL)),
            _mod_spec(3),
            _mod_spec(4),
            _full_spec((LANES, D_MODEL)),
        ],
        out_specs=[
            _row_spec(D_MODEL),
            pl.BlockSpec((ROW_TILE * TOK_SLABS, LANES), lambda i: (i, 0)),
            pl.BlockSpec((LANES, ROW_TILE), lambda i: (0, i)),
        ],
        out_shape=[
            jax.ShapeDtypeStruct((N_ALL, D_MODEL), F32),
            jax.ShapeDtypeStruct((N_ALL * TOK_SLABS, LANES), F32),
            jax.ShapeDtypeStruct((LANES, N_ALL), F32),
        ],
        scratch_shapes=[pltpu.VMEM((D_MODEL, D_MODEL), BF16)],
        compiler_params=_cp(48),
        name=name,
    )(u_ctx, u_lat, *xs, w_out, mods, norm_g, mods, mods, wr_t)


N_TOK_BLOCKS = N_TOK // LANES
STACK_ROWS = N_TOK_BLOCKS * N_EXPERTS


def _router_kernel(lt_ref, tri_ref, blk_ref, idx_ref, gate_ref, cwin_ref, aff_ref, cend_ref, ctot_ref):
    lt = lt_ref[...]
    e = jnp.exp(lt - jnp.max(lt, axis=0, keepdims=True))
    aff = e / jnp.sum(e, axis=0, keepdims=True)
    aff = jnp.where(aff >= F32_MIN_NORMAL, aff, 0.0)

    def count_ge(x):
        return jnp.sum((aff >= x).astype(F32), axis=1, keepdims=True)

    def step(mid_of):
        def body(_, c):
            lo, hi = c
            mid = jnp.minimum(jnp.maximum(mid_of(lo, hi), lo), hi)
            ok = count_ge(mid) >= float(CAPACITY)
            return jnp.where(ok, mid, lo), jnp.where(ok, hi, mid)
        return body

    lo = jnp.full((N_EXPERTS, 1), F32_MIN_NORMAL, F32)
    hi = jnp.full((N_EXPERTS, 1), 2.0, F32)
    lo, hi = lax.fori_loop(0, 8, step(lambda a, b: jnp.sqrt(a * b)), (lo, hi))
    lo, hi = lax.fori_loop(0, 28, step(lambda a, b: a + (b - a) * 0.5), (lo, hi))
    kth = jnp.max(jnp.where(aff < hi, aff, 0.0), axis=1, keepdims=True)
    thr = jnp.where(count_ge(lo) >= float(CAPACITY), kth, 0.0)

    tri = tri_ref[...]
    blk = blk_ref[...]

    def stack(x):
        return jnp.concatenate([x[:, LANES * b:LANES * (b + 1)] for b in range(N_TOK_BLOCKS)], axis=0)

    def unstack(x):
        return jnp.concatenate([x[N_EXPERTS * b:N_EXPERTS * (b + 1), :] for b in range(N_TOK_BLOCKS)], axis=1)

    def cumsum_stacked(mask_st):
        win = _dot(mask_st.astype(BF16), tri)
        tot = win[:, LANES - 1:LANES]
        off = _dot(blk, jnp.broadcast_to(tot, (STACK_ROWS, LANES)).astype(BF16))[:, :1]
        return win, off, tot

    gt = aff > thr
    eq = aff == thr
    need = float(CAPACITY) - jnp.sum(gt.astype(F32), axis=1, keepdims=True)
    eq_f = eq.astype(F32)
    ewin, eoff, _ = cumsum_stacked(stack(eq_f))
    eq_before = unstack(ewin + eoff) - eq_f
    sel = jnp.logical_or(gt, jnp.logical_and(eq, eq_before < need))
    cwin, coff, ctot = cumsum_stacked(stack(sel.astype(F32)))
    cwin_ref[...] = cwin
    aff_ref[...] = stack(aff)
    cend_ref[...] = jnp.broadcast_to(coff + ctot, (STACK_ROWS, LANES))
    ctot_ref[...] = jnp.broadcast_to(ctot, (STACK_ROWS, LANES))

    blk_id = lax.broadcasted_iota(jnp.int32, (N_TOK_BLOCKS, 1), 0).astype(F32)
    slot = lax.broadcasted_iota(jnp.int32, (1, CAPACITY), 1).astype(F32)
    lane_f = lax.broadcasted_iota(jnp.int32, (LANES, 1), 0).astype(F32)

    def per_expert(ex):
        rows = pl.ds(ex, N_TOK_BLOCKS, stride=N_EXPERTS)
        cend = cend_ref[rows, :][:, :1]
        ctot_e = ctot_ref[rows, :][:, :1]
        before = cend <= slot
        nblk = jnp.sum(before.astype(F32), axis=0, keepdims=True)
        base = jnp.sum(jnp.where(before, ctot_e, 0.0), axis=0, keepdims=True)
        pick = jnp.where(blk_id == nblk, 1.0, 0.0).astype(BF16)
        cnt_in = _dot_tn(cwin_ref[rows, :].astype(BF16), pick)
        lane_idx = jnp.sum((cnt_in <= slot - base).astype(F32), axis=0, keepdims=True)
        idx_ref[pl.ds(ex, 1), :] = ((nblk * float(LANES) + lane_idx) * float(TOK_SLABS)).astype(jnp.int32)
        a_hi, a_mid, a_lo = _split3(aff_ref[rows, :])
        aff_blk = _dot_tn(a_hi, pick) + _dot_tn(a_mid, pick) + _dot_tn(a_lo, pick)
        gate_ref[pl.ds(ex, 1), :] = jnp.sum(jnp.where(lane_f == lane_idx, aff_blk, 0.0), axis=0, keepdims=True)

    def expert_pair(i, carry):
        per_expert(2 * i)
        per_expert(2 * i + 1)
        return carry

    lax.fori_loop(0, N_EXPERTS // 2, expert_pair, 0)


def _cumsum_consts():
    l = np.arange(LANES)
    tri = (l[:, None] <= l[None, :]).astype(np.float32)
    r = np.arange(STACK_ROWS)
    same = (r[:, None] % N_EXPERTS) == (r[None, :] % N_EXPERTS)
    earlier = (r[None, :] // N_EXPERTS) < (r[:, None] // N_EXPERTS)
    blk = (same & earlier).astype(np.float32)
    return jnp.asarray(tri, BF16), jnp.asarray(blk, BF16)


def _router(logits_t, token_set, tri, blk, name):
    return pl.pallas_call(
        _router_kernel,
        grid=(1,),
        in_specs=[
            pl.BlockSpec((N_EXPERTS, N_TOK), lambda i: (0, token_set)),
            _full_spec((LANES, LANES)),
            _full_spec((STACK_ROWS, STACK_ROWS)),
        ],
        out_specs=[_full_spec((N_EXPERTS, CAPACITY))] * 2,
        out_shape=[jax.ShapeDtypeStruct((N_EXPERTS, CAPACITY), jnp.int32),
                   jax.ShapeDtypeStruct((N_EXPERTS, CAPACITY), F32)],
        scratch_shapes=[pltpu.VMEM((STACK_ROWS, LANES), F32)] * 4,
        compiler_params=_cp(48),
        name=name,
    )(logits_t, tri, blk)


GATHER_UNROLL = 16
ROWS_PER_EXPERT = 2 * CAPACITY
GATHER_PER_STEP = ROWS_PER_EXPERT // (D_EXPERT // FF_TILE)
TILE_PITCH = TOK_SLABS + 1


def _ffn_kernel(off0_ref, offn_ref, hs_ref, g_ref, w1_ref, w3_ref, w2_ref, yc_ref, yl_ref,
                tile_ref, xe_ref, acc_ref, sem):
    e = pl.program_id(0)
    f = pl.program_id(1)
    last_e = pl.num_programs(0) - 1
    last_f = pl.num_programs(1) - 1

    def row_copy(off_ref, row):
        off = pl.multiple_of(off_ref[0, row], TOK_SLABS)
        return pltpu.make_async_copy(hs_ref.at[pl.ds(off, TOK_SLABS)],
                                     tile_ref.at[pl.ds(row * TILE_PITCH, TOK_SLABS)], sem)

    def wait_tile():
        n = ROWS_PER_EXPERT * TOK_SLABS
        pltpu.make_async_copy(hs_ref.at[pl.ds(0, n)], tile_ref.at[pl.ds(0, n)], sem).wait()

    @pl.when(jnp.logical_and(e == 0, f == 0))
    def _():
        def issue(i, carry):
            for u in range(GATHER_UNROLL):
                row_copy(off0_ref, i * GATHER_UNROLL + u).start()
            return carry
        lax.fori_loop(0, ROWS_PER_EXPERT // GATHER_UNROLL, issue, 0)

    @pl.when(f == 0)
    def _():
        wait_tile()
        xe_ref[...] = _from_slabs(tile_ref, ROWS_PER_EXPERT, pitch=TILE_PITCH).astype(BF16)
        acc_ref[...] = jnp.zeros_like(acc_ref)

    for u in range(GATHER_PER_STEP):
        row_copy(offn_ref, f * GATHER_PER_STEP + u).start()

    x = xe_ref[...]
    h1 = _dot(x, w1_ref[...].astype(BF16))
    h3 = _dot(x, w3_ref[...].astype(BF16))
    hid = (h1 * jax.nn.sigmoid(h1) * h3).astype(BF16)
    acc_ref[...] += _dot(hid, w2_ref[...].astype(BF16))

    @pl.when(f == last_f)
    def _():
        gate = jnp.broadcast_to(g_ref[...], (SUBLANES, ROWS_PER_EXPERT)).T[:, :1]
        _to_slabs(yc_ref, acc_ref[:CAPACITY, :] * gate[:CAPACITY], CAPACITY)
        _to_slabs(yl_ref, acc_ref[CAPACITY:, :] * gate[CAPACITY:], CAPACITY)

    @pl.when(jnp.logical_and(e == last_e, f == last_f))
    def _():
        wait_tile()


def _expert_ffn(offsets, hs, gates, w1, w3, w2, layer, name):
    nf = D_EXPERT // FF_TILE
    off_block = (None, 1, ROWS_PER_EXPERT)
    y_spec = pl.BlockSpec((None, CAPACITY * TOK_SLABS, LANES), lambda e, f: (e, 0, 0))
    y_shape = jax.ShapeDtypeStruct((N_EXPERTS, CAPACITY * TOK_SLABS, LANES), F32)
    return pl.pallas_call(
        _ffn_kernel,
        grid=(N_EXPERTS, nf),
        in_specs=[
            pl.BlockSpec(off_block, lambda e, f: (0, 0, 0), memory_space=pltpu.SMEM),
            pl.BlockSpec(off_block, lambda e, f: (jnp.minimum(e + 1, N_EXPERTS - 1), 0, 0), memory_space=pltpu.SMEM),
            pl.BlockSpec(memory_space=pl.ANY),
            pl.BlockSpec(off_block, lambda e, f: (e, 0, 0)),
            pl.BlockSpec((None, None, D_MODEL, FF_TILE), lambda e, f: (layer, e, 0, f)),
            pl.BlockSpec((None, None, D_MODEL, FF_TILE), lambda e, f: (layer, e, 0, f)),
            pl.BlockSpec((None, None, FF_TILE, D_MODEL), lambda e, f: (layer, e, f, 0)),
        ],
        out_specs=[y_spec, y_spec],
        out_shape=[y_shape, y_shape],
        scratch_shapes=[
            pltpu.VMEM((ROWS_PER_EXPERT * TILE_PITCH, LANES), F32),
            pltpu.VMEM((ROWS_PER_EXPERT, D_MODEL), BF16),
            pltpu.VMEM((ROWS_PER_EXPERT, D_MODEL), F32),
            pltpu.SemaphoreType.DMA(()),
        ],
        compiler_params=_cp(56),
        name=name,
    )(offsets, offsets, hs, gates, w1, w3, w2)


SCATTER_UNROLL = 16


def _scatter_expert(off_ref, ye_ref, y_ref, ex):
    def body(i, carry):
        upd = []
        for u in range(SCATTER_UNROLL):
            s = i * SCATTER_UNROLL + u
            rows = pl.ds(pl.multiple_of(off_ref[ex * CAPACITY + s], TOK_SLABS), TOK_SLABS)
            src = ye_ref[pl.ds(pl.multiple_of(s * TOK_SLABS, TOK_SLABS), TOK_SLABS), :]
            upd.append((rows, y_ref[rows, :] + src))
        for rows, val in upd:
            y_ref[rows, :] = val
        return carry

    lax.fori_loop(0, CAPACITY // SCATTER_UNROLL, body, 0)


def _combine_kernel(idx_ref, ye_ref, y_ref):
    ex = pl.program_id(0)

    @pl.when(ex == 0)
    def _():
        y_ref[...] = jnp.zeros_like(y_ref)

    _scatter_expert(idx_ref, ye_ref, y_ref, ex)


def _combine_final_kernel(idx_ref, ye_ref, xp_ref, gp_ref, fg_ref, o_ref, y_ref):
    i = pl.program_id(0)

    @pl.when(i == 0)
    def _():
        y_ref[...] = jnp.zeros_like(y_ref)

    @pl.when(i < N_EXPERTS)
    def _():
        _scatter_expert(idx_ref, ye_ref, y_ref, i)

    @pl.when(i >= N_EXPERTS)
    def _():
        slabs_per_tile = ROW_TILE * TOK_SLABS
        base = pl.multiple_of((i - N_EXPERTS) * slabs_per_tile, slabs_per_tile)
        x = xp_ref[...] + gp_ref[...] * _from_slabs(y_ref, ROW_TILE, base)
        o_ref[...] = _rmsnorm(x, fg_ref[...])


def _combine_final(idx_flat, ye, x_prev, token_set, mods_prev, final_g, name):
    def tile(i):
        return jnp.maximum(i - N_EXPERTS, 0)

    def gate_row(i, idx):
        return (_tile_group(token_set * N_ROW_TILES + tile(i)) * 6 + 5, 0, 0)

    return pl.pallas_call(
        _combine_final_kernel,
        grid_spec=pltpu.PrefetchScalarGridSpec(
            num_scalar_prefetch=1,
            grid=(N_EXPERTS + N_ROW_TILES,),
            in_specs=[
                pl.BlockSpec((None, CAPACITY * TOK_SLABS, LANES), lambda i, idx: (jnp.minimum(i, N_EXPERTS - 1), 0, 0)),
                pl.BlockSpec((ROW_TILE, D_MODEL), lambda i, idx: (token_set * N_ROW_TILES + tile(i), 0)),
                pl.BlockSpec((None, 1, D_MODEL), gate_row),
                pl.BlockSpec((1, D_MODEL), lambda i, idx: (0, 0)),
            ],
            out_specs=pl.BlockSpec((ROW_TILE, D_MODEL), lambda i, idx: (tile(i), 0)),
            scratch_shapes=[pltpu.VMEM((N_TOK * TOK_SLABS, LANES), F32)],
        ),
        out_shape=jax.ShapeDtypeStruct((N_TOK, D_MODEL), F32),
        compiler_params=_cp(40),
        name=name,
    )(idx_flat, ye, x_prev, mods_prev, final_g)


def _combine(idx_flat, ye, name):
    return pl.pallas_call(
        _combine_kernel,
        grid_spec=pltpu.PrefetchScalarGridSpec(
            num_scalar_prefetch=1,
            grid=(N_EXPERTS,),
            in_specs=[pl.BlockSpec((None, CAPACITY * TOK_SLABS, LANES), lambda e, idx: (e, 0, 0))],
            out_specs=pl.BlockSpec((N_TOK * TOK_SLABS, LANES), lambda e, idx: (0, 0)),
        ),
        out_shape=jax.ShapeDtypeStruct((N_TOK * TOK_SLABS, LANES), F32),
        compiler_params=_cp(56),
        name=name,
    )(idx_flat, ye)


def _moe(hs, lt, tri, blk, w1, w3, w2, layer):
    tag = "l%d" % layer
    off_c, g_c = _router(lt, 0, tri, blk, "router_%s_ctx" % tag)
    off_l, g_l = _router(lt, 1, tri, blk, "router_%s_lat" % tag)
    offsets = jnp.concatenate([off_c, off_l + N_TOK * TOK_SLABS], axis=1).reshape(N_EXPERTS, 1, ROWS_PER_EXPERT)
    gates = jnp.concatenate([g_c, g_l], axis=1).reshape(N_EXPERTS, 1, ROWS_PER_EXPERT)
    ye_c, ye_l = _expert_ffn(offsets, hs, gates, w1, w3, w2, layer, "ffn_" + tag)
    return (off_c.reshape(N_EXPERTS * CAPACITY), ye_c), (off_l.reshape(N_EXPERTS * CAPACITY), ye_l)


def _gla_proj_kernel(xp_ref, yc_ref, yl_ref, gp_ref, g_ref, sh_ref, sc_ref, w_ref, wt_ref, wg_ref, bg_ref,
                     x_ref, q_ref, k_ref, v_ref, r_ref, lgf_ref, lgb_ref, wb_ref):
    _cast_once(w_ref, wb_ref)

    def body(latent):
        yt_ref = yl_ref if latent else yc_ref
        x = xp_ref[...] + gp_ref[...] * _from_slabs(yt_ref, ROW_TILE)
        x_ref[...] = x
        h = _modulate(x, g_ref[...], sh_ref[...], sc_ref[...]).astype(BF16)
        z = _dot(h, wb_ref[...])
        nq = GLA_HEADS * GLA_DK
        nv = GLA_HEADS * GLA_DV
        q_ref[...] = z[:, :nq] * (GLA_DK ** -0.5)
        k_ref[...] = z[:, nq:2 * nq]
        v_ref[...] = z[:, 2 * nq:2 * nq + nv].astype(BF16)
        r_ref[...] = z[:, 2 * nq + nv:].astype(BF16)
        zg = _dot(h, wt_ref[...]).astype(BF16)
        a = _dot(zg, wg_ref[...]) + bg_ref[...]
        ls = (jnp.minimum(a, 0.0) - jnp.log1p(jnp.exp(-jnp.abs(a)))) * (1.0 / GLA_GATE_NORM)
        lgf_ref[...] = ls[:, :nq]
        lgb_ref[...] = ls[:, nq:]

    _per_set(body)


def _gla_proj(x_prev, y_ctx, y_lat, mods_prev, mods, norm_g, w_in, w_tail, wg, bg):
    nq = GLA_HEADS * GLA_DK
    nv = GLA_HEADS * GLA_DV
    n_main = 2 * nq + 2 * nv
    slabs = ROW_TILE * TOK_SLABS
    return pl.pallas_call(
        _gla_proj_kernel,
        grid=(N_TILES,),
        in_specs=[
            _row_spec(D_MODEL),
            pl.BlockSpec((slabs, LANES), lambda i: (jnp.minimum(i, N_ROW_TILES - 1), 0)),
            pl.BlockSpec((slabs, LANES), lambda i: (jnp.maximum(i - N_ROW_TILES, 0), 0)),
            _mod_spec(5),
            _full_spec((1, D_MODEL)),
            _mod_spec(0),
            _mod_spec(1),
            _weight_spec(D_MODEL, n_main),
            _full_spec((D_MODEL, LANES)),
            _full_spec((LANES, 2 * nq)),
            _full_spec((1, 2 * nq)),
        ],
        out_specs=[_row_spec(D_MODEL), _row_spec(nq), _row_spec(nq), _row_spec(nv), _row_spec(nv),
                   _row_spec(nq), _row_spec(nq)],
        out_shape=[
            jax.ShapeDtypeStruct((N_ALL, D_MODEL), F32),
            jax.ShapeDtypeStruct((N_ALL, nq), F32),
            jax.ShapeDtypeStruct((N_ALL, nq), F32),
            jax.ShapeDtypeStruct((N_ALL, nv), BF16),
            jax.ShapeDtypeStruct((N_ALL, nv), BF16),
            jax.ShapeDtypeStruct((N_ALL, nq), F32),
            jax.ShapeDtypeStruct((N_ALL, nq), F32),
        ],
        scratch_shapes=[pltpu.VMEM((D_MODEL, n_main), BF16)],
        compiler_params=_cp(56),
        name="gla_proj",
    )(x_prev, y_ctx, y_lat, mods_prev, norm_g, mods, mods, w_in, w_tail, wg, bg)


GLA_LEVELS = (32, 16, 8)
GLA_UNROLL = 8
GLA_GROUP = 4 * GLA_CHUNK
GLA_SAFE_DECAY = 60.0
NEG_BIG = -1e30


def _gla_intra_fast(qe, kc, b2, fwd):
    c = GLA_CHUNK
    row = lax.broadcasted_iota(jnp.int32, (c, 1), 0)
    col = lax.broadcasted_iota(jnp.int32, (1, c), 1)
    keep = (col <= row) if fwd else (col >= row)
    kq = (kc * jnp.exp2(-b2)).astype(BF16)
    return jnp.where(keep, _dot_nt(qe, kq), 0.0)


def _gla_intra_robust(qc, kc, b2, fwd):
    c = GLA_CHUNK
    row = lax.broadcasted_iota(jnp.int32, (c, 1), 0)
    col = lax.broadcasted_iota(jnp.int32, (1, c), 1)
    a = jnp.zeros((c, c), F32)
    for g in GLA_LEVELS:
        odd = ((row >> int(math.log2(g))) & 1) == 1
        later = odd if fwd else jnp.logical_not(odd)
        refs = []
        for p in range(c // (2 * g)):
            r0 = 2 * g * p + (g - 1 if fwd else g)
            refs.append(jnp.broadcast_to(b2[r0:r0 + 1], (2 * g, GLA_DK)))
        ref = jnp.concatenate(refs, axis=0) if len(refs) > 1 else refs[0]
        qt = qc * jnp.exp2(jnp.where(later, b2 - ref, NEG_BIG))
        kt = kc * jnp.exp2(jnp.where(later, NEG_BIG, ref - b2))
        same_parent = (row >> int(math.log2(2 * g))) == (col >> int(math.log2(2 * g)))
        a = a + jnp.where(same_parent, _dot_nt(qt.astype(BF16), kt.astype(BF16)), 0.0)

    sub = lax.broadcasted_iota(jnp.int32, (SUBLANES, 1), 0)
    strips = []
    for blk in range(c // SUBLANES):
        r0 = SUBLANES * blk
        qb = qc[r0:r0 + SUBLANES]
        bb = b2[r0:r0 + SUBLANES]
        strip = jnp.zeros((SUBLANES, c), F32)
        for jj in range(SUBLANES):
            j = r0 + jj
            cond = (sub >= jj) if fwd else (sub <= jj)
            t = qb * kc[j:j + 1] * jnp.exp2(jnp.where(cond, bb - b2[j:j + 1], NEG_BIG))
            strip = jnp.where(col == j, jnp.sum(t, axis=1, keepdims=True), strip)
        strips.append(strip)
    return a + jnp.concatenate(strips, axis=0)


def _gla_state_increment(kc, vc, b2f, b2b):
    c = GLA_CHUNK
    btf = b2f[c - 1:c]
    btb = b2b[0:1]
    kd = jnp.concatenate([(kc * jnp.exp2(btf - b2f)).astype(BF16), (kc * jnp.exp2(btb - b2b)).astype(BF16)], axis=1)
    return _dot_tn(vc, kd), jnp.concatenate([jnp.exp2(btf), jnp.exp2(btb)], axis=1)


def _gla_scores(qc, kc, b2f, b2b, fast):
    qe_f = (qc * jnp.exp2(b2f)).astype(BF16)
    qe_b = (qc * jnp.exp2(b2b)).astype(BF16)
    if fast:
        a = _gla_intra_fast(qe_f, kc, b2f, True) + _gla_intra_fast(qe_b, kc, b2b, False)
    else:
        a = _gla_intra_robust(qc, kc, b2f, True) + _gla_intra_robust(qc, kc, b2b, False)
    return jnp.concatenate([qe_f, qe_b], axis=1), a.astype(BF16)


def _gla_kernel(q_ref, k_ref, v_ref, r_ref, gf_ref, gb_ref, *refs, t_len, heads, zero_state):
    if zero_state:
        s0f_ref = s0b_ref = None
    else:
        s0f_ref, s0b_ref = refs[:2]
        refs = refs[2:]
    go_ref, trif_ref, trib_ref, u_ref, sf_ref, sb_ref = refs[:6]
    scratch = refs[6:]
    _gla_passes(q_ref, k_ref, v_ref, r_ref, gf_ref, gb_ref, s0f_ref, s0b_ref, go_ref, trif_ref, trib_ref,
                u_ref, sf_ref, sb_ref, *scratch, t_len=t_len, heads=heads)


def _gla_passes(q_ref, k_ref, v_ref, r_ref, gf_ref, gb_ref, s0f_ref, s0b_ref, go_ref, trif_ref, trib_ref,
                u_ref, sf_ref, sb_ref, *scratch, t_len, heads):
    c = GLA_CHUNK
    n = t_len // c
    per_head = len(scratch) // heads
    hd = []
    for hh in range(heads):
        dk = slice(hh * GLA_DK, (hh + 1) * GLA_DK)
        dv = slice(hh * GLA_DV, (hh + 1) * GLA_DV)
        b2f, b2b, qe, oi, kv, dec, sst, st = scratch[hh * per_head:(hh + 1) * per_head]
        hd.append(dict(q=q_ref.at[:, dk], k=k_ref.at[:, dk], v=v_ref.at[:, dv], r=r_ref.at[:, dv],
                       gf=gf_ref.at[:, dk], gb=gb_ref.at[:, dk],
                       s0f=None if s0f_ref is None else s0f_ref.at[hh],
                       s0b=None if s0b_ref is None else s0b_ref.at[hh],
                       u=u_ref.at[:, dv], sf=sf_ref.at[hh], sb=sb_ref.at[hh],
                       b2f=b2f, b2b=b2b, qe=qe, oi=oi, kv=kv, dec=dec, sst=sst, st=st))

    def rows_of(ci):
        return pl.ds(pl.multiple_of(ci * c, c), c)

    def state_rows_of(ci):
        return pl.ds(pl.multiple_of(ci * GLA_DV, GLA_DV), GLA_DV)

    n_groups = t_len // GLA_GROUP
    cum_unroll = 2 if n_groups % 2 == 0 else 1

    def cum_body(i, carry):
        sums = []
        for h in hd:
            for u in range(cum_unroll):
                rows = pl.ds(pl.multiple_of((i * cum_unroll + u) * GLA_GROUP, GLA_GROUP), GLA_GROUP)
                for g_ref, tri_ref, b_ref in ((h["gf"], trif_ref, h["b2f"]), (h["gb"], trib_ref, h["b2b"])):
                    s3 = _dot(tri_ref[...], jnp.concatenate(_split3(g_ref[rows, :]), axis=1))
                    sums.append((b_ref, rows, s3))
        for b_ref, rows, s3 in sums:
            b_ref[rows, :] = (s3[:, :GLA_DK] + s3[:, GLA_DK:2 * GLA_DK] + s3[:, 2 * GLA_DK:]) * LOG2E
        return carry

    lax.fori_loop(0, n_groups // cum_unroll, cum_body, 0)

    def chunk_total(g_ref):
        return jnp.min(jnp.sum(g_ref[...].reshape(n, c, GLA_DK), axis=1))

    totals = [chunk_total(h[g]) for h in hd for g in ("gf", "gb")]
    safe = functools.reduce(jnp.minimum, totals) >= -GLA_SAFE_DECAY

    def make_local(fast, unroll):
        def body(i, carry):
            work = [(h, i * unroll + u) for h in hd for u in range(unroll)]
            for h, ci in work:
                rows = rows_of(ci)
                kv, dec = _gla_state_increment(h["k"][rows, :], h["v"][rows, :], h["b2f"][rows, :], h["b2b"][rows, :])
                h["kv"][state_rows_of(ci), :] = kv
                h["dec"][pl.ds(ci, 1), :] = dec
            scores = []
            for h, ci in work:
                rows = rows_of(ci)
                qe, a = _gla_scores(h["q"][rows, :], h["k"][rows, :], h["b2f"][rows, :], h["b2b"][rows, :], fast)
                h["qe"][rows, :] = qe
                scores.append(a)
            for (h, ci), a in zip(work, scores):
                rows = rows_of(ci)
                h["oi"][rows, :] = _dot(a, h["v"][rows, :])
            return carry
        return body

    unroll = min(GLA_UNROLL, n)

    @pl.when(safe)
    def _():
        lax.fori_loop(0, n // unroll, make_local(True, unroll), 0)

    @pl.when(jnp.logical_not(safe))
    def _():
        lax.fori_loop(0, n, make_local(False, 1), 0)

    fl = slice(0, GLA_DK)
    bl = slice(GLA_DK, 2 * GLA_DK)
    for h in hd:
        if h["s0f"] is None:
            h["st"][...] = jnp.zeros_like(h["st"])
        else:
            h["st"][...] = jnp.concatenate([h["s0f"][...].T, h["s0b"][...].T], axis=1)

    def rec_body(i, carry):
        cf = i
        cb = n - 1 - i
        for h in hd:
            st = h["st"][...]
            h["sst"][state_rows_of(cf), fl] = st[:, fl].astype(BF16)
            h["sst"][state_rows_of(cb), bl] = st[:, bl].astype(BF16)
            dec = jnp.concatenate([h["dec"][pl.ds(cf, 1), :][:, fl], h["dec"][pl.ds(cb, 1), :][:, bl]], axis=1)
            kv = jnp.concatenate([h["kv"][state_rows_of(cf), fl], h["kv"][state_rows_of(cb), bl]], axis=1)
            h["st"][...] = st * dec + kv
        return carry

    lax.fori_loop(0, n, rec_body, 0)
    for h in hd:
        h["sf"][...] = h["st"][:, fl].T
        h["sb"][...] = h["st"][:, bl].T

    go = go_ref[...]

    def out_body(i, carry):
        work = [(h, i * unroll + u) for h in hd for u in range(unroll)]
        inter = [_dot_nt(h["qe"][rows_of(ci), :], h["sst"][state_rows_of(ci), :]) for h, ci in work]
        for (h, ci), o_inter in zip(work, inter):
            rows = rows_of(ci)
            r = h["r"][rows, :].astype(F32)
            h["u"][rows, :] = (_rmsnorm(h["oi"][rows, :] + o_inter, go) * (r * jax.nn.sigmoid(r))).astype(BF16)
        return carry

    lax.fori_loop(0, n // unroll, out_body, 0)


def _gla_tri():
    i = np.arange(GLA_GROUP)
    same = (i[:, None] // GLA_CHUNK) == (i[None, :] // GLA_CHUNK)
    fwd = (same & (i[None, :] <= i[:, None])).astype(np.float32)
    bwd = (same & (i[None, :] >= i[:, None])).astype(np.float32)
    return jnp.asarray(fwd, BF16), jnp.asarray(bwd, BF16)


def _gla(q, k, v, r, lgf, lgb, batch0, s0f, s0b, g_out, n_batch, t_len, heads, name):
    zero_state = s0f is None
    trif, trib = _gla_tri()
    n_chunks = t_len // GLA_CHUNK
    qk_spec = pl.BlockSpec((None, t_len, heads * GLA_DK), lambda b, h: (batch0 + b, 0, h))
    vin_spec = pl.BlockSpec((None, t_len, heads * GLA_DV), lambda b, h: (batch0 + b, 0, h))
    v_spec = pl.BlockSpec((None, t_len, heads * GLA_DV), lambda b, h: (b, 0, h))
    s_spec = pl.BlockSpec((None, heads, GLA_DK, GLA_DV), lambda b, h: (b, h, 0, 0))
    const = lambda shape: pl.BlockSpec(shape, lambda b, h: (0,) * len(shape))
    s_shape = jax.ShapeDtypeStruct((n_batch, GLA_HEADS, GLA_DK, GLA_DV), F32)
    states = [] if zero_state else [s0f, s0b]
    return pl.pallas_call(
        functools.partial(_gla_kernel, t_len=t_len, heads=heads, zero_state=zero_state),
        grid=(n_batch, GLA_HEADS // heads),
        in_specs=[qk_spec, qk_spec, vin_spec, vin_spec, qk_spec, qk_spec] + [s_spec] * len(states)
        + [const((1, GLA_DV)), const((GLA_GROUP, GLA_GROUP)), const((GLA_GROUP, GLA_GROUP))],
        out_specs=[v_spec, s_spec, s_spec],
        out_shape=[jax.ShapeDtypeStruct((n_batch, t_len, GLA_HEADS * GLA_DV), BF16), s_shape, s_shape],
        scratch_shapes=[
            pltpu.VMEM((t_len, GLA_DK), F32), pltpu.VMEM((t_len, GLA_DK), F32),
            pltpu.VMEM((t_len, 2 * GLA_DK), BF16),
            pltpu.VMEM((t_len, GLA_DV), F32),
            pltpu.VMEM((n_chunks * GLA_DV, 2 * GLA_DK), F32),
            pltpu.VMEM((max(n_chunks, SUBLANES), 2 * GLA_DK), F32),
            pltpu.VMEM((n_chunks * GLA_DV, 2 * GLA_DK), BF16),
            pltpu.VMEM((GLA_DV, 2 * GLA_DK), F32),
        ] * heads,
        compiler_params=_cp(48),
        name=name,
    )(q, k, v, r, lgf, lgb, *states, g_out, trif, trib)


def kernel(x_prompt, x_sample, cache_k, cache_v, state_fwd, state_bwd, c, c_ctx, w_mod, b_mod, norm_g,
           da_w_in, da_w_out, da_lam_q1, da_lam_k1, da_lam_q2, da_lam_k2, da_g_sub, gla_w_in, gla_w_gf2,
           gla_b_gf, gla_w_gb2, gla_b_gb, gla_g_out, gla_w_out, moe_w_router, moe_w1, moe_w3, moe_w2, final_g):
    n_ctx_b = x_prompt.shape[0]
    n_lat_b = x_sample.shape[0]
    x_ctx = x_prompt.reshape(N_TOK, D_MODEL)
    x_lat = x_sample.reshape(N_TOK, D_MODEL)

    cvec = jnp.zeros((SUBLANES, D_MODEL), F32).at[0].set(c_ctx).at[1:1 + n_lat_b].set(c)
    mods = _mod_params(cvec, w_mod, b_mod)
    tri, blk = _cumsum_consts()
    wr_t = [jnp.zeros((LANES, D_MODEL), BF16).at[:N_EXPERTS].set(moe_w_router[i].T.astype(BF16))
            for i in range(DEPTH)]
    ng = norm_g.reshape(DEPTH, 2, 1, D_MODEL)

    lam_init = 0.8 - 0.6 * math.exp(-0.3 * 0)
    w_in = da_w_in
    w_out = da_w_out
    lam_vecs = [v[0].reshape(1, DA_DIM) for v in (da_lam_q1, da_lam_k1, da_lam_q2, da_lam_k2)]
    g_sub = da_g_sub[0].reshape(1, DA_VDIM)

    q, k, v, kf, vf = _da_proj(x_ctx, x_lat, mods[0], ng[0, 0], w_in, _rope_tables())
    past = cache_k.shape[2]
    cv = cache_v[:, 0].reshape(n_lat_b * past, D_MODEL)
    u_ctx = _attention(lam_vecs, g_sub, q, 0, [(k, v, 0, SEQ)], n_ctx_b, SEQ, ATTN_Q_TILE, lam_init, "attn_ctx")
    u_lat = _attention(lam_vecs, g_sub, q, N_TOK, [(k, v, N_TOK, DEC_SEQ), (cache_k, cv, 0, past)], n_lat_b, DEC_SEQ,
                       ATTN_Q_TILE, lam_init, "attn_lat", cache_keys=True)

    x1, hs, lt = _outproj(u_ctx, u_lat, (x_ctx, x_lat), w_out, mods[0], ng[0, 1], wr_t[0], "outproj0")
    (idx_c, ye_c), (idx_l, ye_l) = _moe(hs, lt, tri, blk, moe_w1, moe_w3, moe_w2, 0)
    y_ctx = _combine(idx_c, ye_c, "combine_l0_ctx")
    y_lat = _combine(idx_l, ye_l, "combine_l0_lat")

    nq = GLA_HEADS * GLA_DK
    n_main = 2 * nq + 2 * GLA_HEADS * GLA_DV
    w_tail = jnp.zeros((D_MODEL, LANES), F32).at[:, :2 * GLA_GATE_RANK].set(gla_w_in[0][:, n_main:]).astype(BF16)
    wg = jnp.zeros((LANES, 2 * nq), F32)
    wg = wg.at[:GLA_GATE_RANK, :nq].set(gla_w_gf2[0]).at[GLA_GATE_RANK:2 * GLA_GATE_RANK, nq:].set(gla_w_gb2[0])
    wg = wg.astype(BF16)
    bg = jnp.concatenate([gla_b_gf[0], gla_b_gb[0]]).reshape(1, 2 * nq)
    w_out1 = gla_w_out
    g_out = gla_g_out[0].reshape(1, GLA_DV)

    x2, *gla_in = _gla_proj(x1, y_ctx, y_lat, mods[0], mods[1], ng[1, 0], gla_w_in, w_tail, wg, bg)

    def gla_side(n_b, t_len, s0f, s0b, tag):
        seqs = [a.reshape(N_ALL // t_len, t_len, a.shape[-1]) for a in gla_in]
        batch0 = 0 if s0f is None else N_TOK // t_len
        heads = GLA_HEADS if t_len <= GLA_GROUP else 1
        u, sf, sb = _gla(*seqs, batch0, s0f, s0b, g_out, n_b, t_len, heads, "gla_" + tag)
        return u.reshape(N_TOK, D_MODEL), sf, sb

    ug_ctx, sf, sb = gla_side(n_ctx_b, SEQ, None, None, "ctx")
    ug_lat, _, _ = gla_side(n_lat_b, DEC_SEQ, state_fwd[:, 0], state_bwd[:, 0], "lat")

    x3, hs, lt = _outproj(ug_ctx, ug_lat, (x2,), w_out1, mods[1], ng[1, 1], wr_t[1], "outproj1")
    (idx_c, ye_c), (idx_l, ye_l) = _moe(hs, lt, tri, blk, moe_w1, moe_w3, moe_w2, 1)

    fg = final_g.reshape(1, D_MODEL)
    y_prompt = _combine_final(idx_c, ye_c, x3, 0, mods[1], fg, "final_ctx").reshape(x_prompt.shape)
    y_sample = _combine_final(idx_l, ye_l, x3, 1, mods[1], fg, "final_lat").reshape(x_sample.shape)
    new_k = kf.reshape(n_ctx_b, 1, SEQ, 2 * DA_HEADS, DA_DIM)
    new_v = vf.reshape(n_ctx_b, 1, SEQ, DA_HEADS, DA_VDIM)
    return (y_prompt, y_sample, new_k, new_v, sf[:, None], sb[:, None])
```

```python
import functools
import math

import numpy as np
import jax
import jax.numpy as jnp
from jax import lax
from jax.experimental import pallas as pl
from jax.experimental.pallas import tpu as pltpu

F32 = jnp.float32
BF16 = jnp.bfloat16

D_MODEL = 1024
DEPTH = 2
SEQ = 256
DEC_SEQ = 2048
GRID_W = 64
N_TOK = 4096
DA_HEADS = 8
DA_DIM = 64
DA_VDIM = 128
ROPE_BASE = 10000.0
GLA_HEADS = 4
GLA_DK = 128
GLA_DV = 256
GLA_GATE_RANK = 16
GLA_GATE_NORM = 16.0
GLA_CHUNK = 64
N_EXPERTS = 16
CAPACITY = 512
D_EXPERT = 2048
EPS = 1e-6
F32_MIN_NORMAL = 2.0 ** -126
LOG2E = 1.4426950408889634

LANES = 128
SUBLANES = 8
ROW_TILE = 512
FF_TILE = 512
TOK_SLABS = D_MODEL // LANES
MIB = 1024 * 1024


def _cp(vmem_mib, sem=None):
    return pltpu.CompilerParams(vmem_limit_bytes=vmem_mib * MIB, dimension_semantics=sem)


def _dot(a, b):
    return jnp.dot(a, b, preferred_element_type=F32)


def _dot_nt(a, b):
    return lax.dot_general(a, b, (((1,), (1,)), ((), ())), preferred_element_type=F32)


def _dot_tn(a, b):
    return lax.dot_general(a, b, (((0,), (0,)), ((), ())), preferred_element_type=F32)


def _rmsnorm(x, g):
    return x * lax.rsqrt(jnp.mean(x * x, axis=-1, keepdims=True) + EPS) * g


def _modulate(x, g, shift, scale):
    return _rmsnorm(x, g) * (1.0 + scale) + shift


def _split3(x):
    hi = x.astype(BF16)
    r = x - hi.astype(F32)
    mid = r.astype(BF16)
    lo = (r - mid.astype(F32)).astype(BF16)
    return hi, mid, lo


def _from_slabs(ref, rows, base=0, pitch=TOK_SLABS):
    return jnp.concatenate([ref[pl.ds(base + c, rows, stride=pitch), :] for c in range(TOK_SLABS)], axis=1)


def _to_slabs(ref, val, rows):
    for c in range(TOK_SLABS):
        ref[pl.ds(c, rows, stride=TOK_SLABS), :] = val[:, LANES * c:LANES * (c + 1)]


def _mod_kernel(c_ref, w_ref, b_ref, o_ref):
    c = c_ref[...]
    s = c * jax.nn.sigmoid(c)
    w = w_ref[...]
    s_hi = s.astype(BF16)
    s_lo = (s - s_hi.astype(F32)).astype(BF16)
    w_hi = w.astype(BF16)
    w_lo = (w - w_hi.astype(F32)).astype(BF16)
    o_ref[...] = _dot(s_hi, w_hi) + _dot(s_hi, w_lo) + _dot(s_lo, w_hi) + b_ref[...]


def _mod_params(cvec, w_mod, b_mod):
    n6 = 6 * D_MODEL
    out = pl.pallas_call(
        _mod_kernel,
        grid=(DEPTH, 6),
        in_specs=[
            pl.BlockSpec((SUBLANES, D_MODEL), lambda i, j: (0, 0)),
            pl.BlockSpec((None, D_MODEL, D_MODEL), lambda i, j: (i, 0, j)),
            pl.BlockSpec((None, 1, D_MODEL), lambda i, j: (i, 0, j)),
        ],
        out_specs=pl.BlockSpec((None, SUBLANES, D_MODEL), lambda i, j: (i, 0, j)),
        out_shape=jax.ShapeDtypeStruct((DEPTH, SUBLANES, n6), F32),
        compiler_params=_cp(32),
        name="mod_params",
    )(cvec, w_mod, b_mod.reshape(DEPTH, 1, n6))
    return out.reshape(DEPTH, SUBLANES * 6, 1, D_MODEL)


N_ROW_TILES = N_TOK // ROW_TILE
N_TILES = 2 * N_ROW_TILES
N_ALL = 2 * N_TOK


def _tile_group(i):
    return jnp.where(i < N_ROW_TILES, 0, 1 + (i - N_ROW_TILES) // (DEC_SEQ // ROW_TILE))


def _mod_spec(k):
    return pl.BlockSpec((None, 1, D_MODEL), lambda i: (_tile_group(i) * 6 + k, 0, 0))


def _row_spec(width):
    return pl.BlockSpec((ROW_TILE, width), lambda i: (i, 0))


def _ctx_row_spec(width):
    return pl.BlockSpec((ROW_TILE, width), lambda i: (jnp.minimum(i, N_ROW_TILES - 1), 0))


def _lat_row_spec(width):
    return pl.BlockSpec((ROW_TILE, width), lambda i: (jnp.maximum(i - N_ROW_TILES, 0), 0))


def _full_spec(shape):
    nd = len(shape)
    return pl.BlockSpec(shape, lambda i: (0,) * nd)


def _per_set(body):
    i = pl.program_id(0)

    @pl.when(i < N_ROW_TILES)
    def _():
        body(False)

    @pl.when(i >= N_ROW_TILES)
    def _():
        body(True)


def _weight_spec(rows, cols):
    return pl.BlockSpec((None, rows, cols), lambda i: (0, 0, 0), pipeline_mode=pl.Buffered(1))


def _cast_once(w_ref, wb_ref):
    @pl.when(pl.program_id(0) == 0)
    def _():
        wb_ref[...] = w_ref[...].astype(BF16)


def _da_proj_kernel(xc_ref, xl_ref, g_ref, sh_ref, sc_ref, w_ref, cos_ref, sin_ref,
                    q_ref, k_ref, v_ref, kf_ref, vf_ref, wb_ref):
    _cast_once(w_ref, wb_ref)

    def body(latent):
        x_ref = xl_ref if latent else xc_ref
        h = _modulate(x_ref[...], g_ref[...], sh_ref[...], sc_ref[...])
        z = _dot(h.astype(BF16), wb_ref[...])
        q = z[:, :D_MODEL]
        k = z[:, D_MODEL:2 * D_MODEL]
        v = z[:, 2 * D_MODEL:]
        if latent:
            reps = D_MODEL // LANES
            cos = jnp.concatenate([cos_ref[...]] * reps, axis=1)
            sin = jnp.concatenate([sin_ref[...]] * reps, axis=1)
            lane = lax.broadcasted_iota(jnp.int32, (1, D_MODEL), 1)
            first = (lane & 16) == 0

            def rot(t):
                partner = jnp.where(first, pltpu.roll(t, D_MODEL - 16, 1), pltpu.roll(t, 16, 1))
                return t * cos + partner * sin

            q = rot(q)
            k = rot(k)
        else:
            kf_ref[...] = k
            vf_ref[...] = v
        q_ref[...] = (q * (DA_DIM ** -0.5 * LOG2E)).astype(BF16)
        k_ref[...] = k.astype(BF16)
        v_ref[...] = v.astype(BF16)

    _per_set(body)


def _da_proj(x_ctx, x_lat, mods, norm_g, w_in, rope_tabs):
    tiles_per_batch = DEC_SEQ // ROW_TILE
    tab_spec = pl.BlockSpec((ROW_TILE, LANES), lambda i: (jnp.maximum(i - N_ROW_TILES, 0) % tiles_per_batch, 0))
    return pl.pallas_call(
        _da_proj_kernel,
        grid=(N_TILES,),
        in_specs=[
            _ctx_row_spec(D_MODEL),
            _lat_row_spec(D_MODEL),
            _full_spec((1, D_MODEL)),
            _mod_spec(0),
            _mod_spec(1),
            _weight_spec(D_MODEL, 3 * D_MODEL),
            tab_spec, tab_spec,
        ],
        out_specs=[_row_spec(D_MODEL)] * 3 + [_ctx_row_spec(D_MODEL)] * 2,
        out_shape=[jax.ShapeDtypeStruct((N_ALL, D_MODEL), BF16)] * 3 + [jax.ShapeDtypeStruct((N_TOK, D_MODEL), F32)] * 2,
        scratch_shapes=[pltpu.VMEM((D_MODEL, 3 * D_MODEL), BF16)],
        compiler_params=_cp(56),
        name="da_proj",
    )(x_ctx, x_lat, norm_g, mods, mods, w_in, *rope_tabs)


def _rope_tables():
    t = np.arange(DEC_SEQ)
    rows = (t // GRID_W).astype(np.float32)
    cols = (t % GRID_W).astype(np.float32)
    half = DA_DIM // 4
    freqs = (np.float32(ROPE_BASE) ** (-np.arange(half, dtype=np.float32) / np.float32(half))).astype(np.float32)
    ang_r = rows[:, None] * freqs
    ang_c = cols[:, None] * freqs
    cos64 = np.concatenate([np.cos(ang_r)] * 2 + [np.cos(ang_c)] * 2, axis=1)
    sin64 = np.concatenate([-np.sin(ang_r), np.sin(ang_r), -np.sin(ang_c), np.sin(ang_c)], axis=1)
    reps = LANES // DA_DIM
    return (jnp.asarray(np.concatenate([cos64] * reps, axis=1), F32),
            jnp.asarray(np.concatenate([sin64] * reps, axis=1), F32))


ATTN_Q_TILE = 256


def _attn_kernel(lq1_ref, lk1_ref, lq2_ref, lk2_ref, gs_ref, q_ref, *refs, lam_init, seg_lens, cache_keys):
    n_seg = len(seg_lens)
    kv_refs, o_ref = refs[:2 * n_seg], refs[2 * n_seg]
    k_refs = [kv_refs[2 * s] for s in range(n_seg)]
    v_refs = [kv_refs[2 * s + 1] for s in range(n_seg)]
    if cache_keys:
        kc_ref, vc_ref = refs[2 * n_seg + 1], refs[2 * n_seg + 2]

        @pl.when(pl.program_id(1) == 0)
        def _():
            for s in range(2 * DA_HEADS):
                kc_ref[:, DA_DIM * s:DA_DIM * (s + 1)] = k_refs[-1][:, s, :].astype(BF16)
            vc_ref[...] = _from_slabs(v_refs[-1], seg_lens[-1]).astype(BF16)

        k_refs[-1] = kc_ref
        v_refs[-1] = vc_ref
    lam =(jnp.exp(jnp.sum(lq1_ref[...] * lk1_ref[...], axis=-1, keepdims=True))
           - jnp.exp(jnp.sum(lq2_ref[...] * lk2_ref[...], axis=-1, keepdims=True)) + lam_init)
    lane = lax.broadcasted_iota(jnp.int32, (1, DA_VDIM), 1)
    first = lane < DA_DIM
    gs = gs_ref[...]
    tq = q_ref.shape[0]

    def softmax_pv(qq, ks, vs):
        ss = [_dot_nt(qq, kk) for kk in ks]
        m = functools.reduce(jnp.maximum, [jnp.max(s, axis=-1, keepdims=True) for s in ss])
        ps = [jnp.exp2(s - m) for s in ss]
        l = functools.reduce(jnp.add, [jnp.sum(p, axis=-1, keepdims=True) for p in ps])
        o = functools.reduce(jnp.add, [_dot(p.astype(BF16), vv) for p, vv in zip(ps, vs)])
        return o, l

    for h in range(DA_HEADS):
        sl = slice(DA_VDIM * h, DA_VDIM * (h + 1))
        qh = q_ref[:, sl]
        ks = [k_ref[:, sl].astype(BF16) for k_ref in k_refs]
        vs = [v_ref[:, sl].astype(BF16) for v_ref in v_refs]
        zero = jnp.zeros_like(qh)
        qq = jnp.concatenate([jnp.where(first, qh, zero), jnp.where(first, zero, qh)], axis=0)
        oo, ll = softmax_pv(qq, ks, vs)
        o = oo[:tq] * (1.0 / ll[:tq]) - oo[tq:] * (lam / ll[tq:])
        o_ref[:, sl] = (_rmsnorm(o, gs) * (1.0 - lam_init)).astype(BF16)


def _attention(lam_vecs, g_sub, q, q_row0, kv_segs, n_batch, t_q, q_tile, lam_init, name, cache_keys=False):
    nq = t_q // q_tile
    q0 = q_row0 // q_tile
    vec_spec = pl.BlockSpec((1, DA_DIM), lambda b, i: (0, 0))
    kv_specs, kv_args = [], []
    for k, v, row0, t_k in kv_segs:
        kv_specs += [pl.BlockSpec((t_k, D_MODEL), lambda b, i, b0=row0 // t_k: (b0 + b, 0))] * 2
        kv_args += [k, v]
    seg_lens = tuple(t_k for _, _, _, t_k in kv_segs)
    scratch = []
    if cache_keys:
        kv_specs[-2] = pl.BlockSpec((None, None, seg_lens[-1], 2 * DA_HEADS, DA_DIM), lambda b, i: (b, 0, 0, 0, 0))
        kv_specs[-1] = pl.BlockSpec((seg_lens[-1] * TOK_SLABS, LANES), lambda b, i: (b, 0))
        scratch = [pltpu.VMEM((seg_lens[-1], D_MODEL), BF16)] * 2
    return pl.pallas_call(
        functools.partial(_attn_kernel, lam_init=lam_init, seg_lens=seg_lens, cache_keys=cache_keys),
        grid=(n_batch, nq),
        in_specs=[vec_spec] * 4 + [
            pl.BlockSpec((1, DA_VDIM), lambda b, i: (0, 0)),
            pl.BlockSpec((q_tile, D_MODEL), lambda b, i: (q0 + b * nq + i, 0)),
        ] + kv_specs,
        out_specs=pl.BlockSpec((q_tile, D_MODEL), lambda b, i: (b * nq + i, 0)),
        out_shape=jax.ShapeDtypeStruct((n_batch * t_q, D_MODEL), BF16),
        scratch_shapes=scratch,
        compiler_params=_cp(56),
        name=name,
    )(*lam_vecs, g_sub, q, *kv_args)


def _outproj_kernel(*refs, x_stacked):
    uc_ref, ul_ref = refs[:2]
    n_x = 1 if x_stacked else 2
    x_refs = refs[2:2 + n_x]
    w_ref, gate_ref, g_ref, sh_ref, sc_ref, wr_ref, x1_ref, hs_ref, lt_ref, wb_ref = refs[2 + n_x:]
    _cast_once(w_ref, wb_ref)

    def body(latent):
        u_ref = ul_ref if latent else uc_ref
        x_ref = x_refs[0] if x_stacked else x_refs[int(latent)]
        x1 = x_ref[...] + gate_ref[...] * _dot(u_ref[...], wb_ref[...])
        x1_ref[...] = x1
        h2 = _modulate(x1, g_ref[...], sh_ref[...], sc_ref[...])
        _to_slabs(hs_ref, h2, ROW_TILE)
        lt_ref[...] = _dot_nt(wr_ref[...], h2.astype(BF16))

    _per_set(body)


def _outproj(u_ctx, u_lat, xs, w_out, mods, norm_g, wr_t, name):
    x_stacked = len(xs) == 1
    x_specs = [_row_spec(D_MODEL)] if x_stacked else [_ctx_row_spec(D_MODEL), _lat_row_spec(D_MODEL)]
    return pl.pallas_call(
        functools.partial(_outproj_kernel, x_stacked=x_stacked),
        grid=(N_TILES,),
        in_specs=[_ctx_row_spec(D_MODEL), _lat_row_spec(D_MODEL)] + x_specs + [
            _weight_spec(D_MODEL, D_MODEL),
            _mod_spec(2),
            _full_spec((1, D_MODEL)),
            _mod_spec(3),
            _mod_spec(4),
            _full_spec((LANES, D_MODEL)),
        ],
        out_specs=[
            _row_spec(D_MODEL),
            pl.BlockSpec((ROW_TILE * TOK_SLABS, LANES), lambda i: (i, 0)),
            pl.BlockSpec((LANES, ROW_TILE), lambda i: (0, i)),
        ],
        out_shape=[
            jax.ShapeDtypeStruct((N_ALL, D_MODEL), F32),
            jax.ShapeDtypeStruct((N_ALL * TOK_SLABS, LANES), F32),
            jax.ShapeDtypeStruct((LANES, N_ALL), F32),
        ],
        scratch_shapes=[pltpu.VMEM((D_MODEL, D_MODEL), BF16)],
        compiler_params=_cp(48),
        name=name,
    )(u_ctx, u_lat, *xs, w_out, mods, norm_g, mods, mods, wr_t)


N_TOK_BLOCKS = N_TOK // LANES
STACK_ROWS = N_TOK_BLOCKS * N_EXPERTS


def _router_kernel(lt_ref, tri_ref, blk_ref, idx_ref, gate_ref, cwin_ref, aff_ref, cend_ref, ctot_ref):
    lt = lt_ref[...]
    e = jnp.exp(lt - jnp.max(lt, axis=0, keepdims=True))
    aff = e / jnp.sum(e, axis=0, keepdims=True)
    aff = jnp.where(aff >= F32_MIN_NORMAL, aff, 0.0)

    def count_ge(x):
        return jnp.sum((aff >= x).astype(F32), axis=1, keepdims=True)

    def step(mid_of):
        def body(_, c):
            lo, hi = c
            mid = jnp.minimum(jnp.maximum(mid_of(lo, hi), lo), hi)
            ok = count_ge(mid) >= float(CAPACITY)
            return jnp.where(ok, mid, lo), jnp.where(ok, hi, mid)
        return body

    lo = jnp.full((N_EXPERTS, 1), F32_MIN_NORMAL, F32)
    hi = jnp.full((N_EXPERTS, 1), 2.0, F32)
    lo, hi = lax.fori_loop(0, 8, step(lambda a, b: jnp.sqrt(a * b)), (lo, hi))
    lo, hi = lax.fori_loop(0, 28, step(lambda a, b: a + (b - a) * 0.5), (lo, hi))
    kth = jnp.max(jnp.where(aff < hi, aff, 0.0), axis=1, keepdims=True)
    thr = jnp.where(count_ge(lo) >= float(CAPACITY), kth, 0.0)

    tri = tri_ref[...]
    blk = blk_ref[...]

    def stack(x):
        return jnp.concatenate([x[:, LANES * b:LANES * (b + 1)] for b in range(N_TOK_BLOCKS)], axis=0)

    def unstack(x):
        return jnp.concatenate([x[N_EXPERTS * b:N_EXPERTS * (b + 1), :] for b in range(N_TOK_BLOCKS)], axis=1)

    def cumsum_stacked(mask_st):
        win = _dot(mask_st.astype(BF16), tri)
        tot = win[:, LANES - 1:LANES]
        off = _dot(blk, jnp.broadcast_to(tot, (STACK_ROWS, LANES)).astype(BF16))[:, :1]
        return win, off, tot

    gt = aff > thr
    eq = aff == thr
    need = float(CAPACITY) - jnp.sum(gt.astype(F32), axis=1, keepdims=True)
    eq_f = eq.astype(F32)
    ewin, eoff, _ = cumsum_stacked(stack(eq_f))
    eq_before = unstack(ewin + eoff) - eq_f
    sel = jnp.logical_or(gt, jnp.logical_and(eq, eq_before < need))
    cwin, coff, ctot = cumsum_stacked(stack(sel.astype(F32)))
    cwin_ref[...] = cwin
    aff_ref[...] = stack(aff)
    cend_ref[...] = jnp.broadcast_to(coff + ctot, (STACK_ROWS, LANES))
    ctot_ref[...] = jnp.broadcast_to(ctot, (STACK_ROWS, LANES))

    blk_id = lax.broadcasted_iota(jnp.int32, (N_TOK_BLOCKS, 1), 0).astype(F32)
    slot = lax.broadcasted_iota(jnp.int32, (1, CAPACITY), 1).astype(F32)
    lane_f = lax.broadcasted_iota(jnp.int32, (LANES, 1), 0).astype(F32)

    def per_expert(ex):
        rows = pl.ds(ex, N_TOK_BLOCKS, stride=N_EXPERTS)
        cend = cend_ref[rows, :][:, :1]
        ctot_e = ctot_ref[rows, :][:, :1]
        before = cend <= slot
        nblk = jnp.sum(before.astype(F32), axis=0, keepdims=True)
        base = jnp.sum(jnp.where(before, ctot_e, 0.0), axis=0, keepdims=True)
        pick = jnp.where(blk_id == nblk, 1.0, 0.0).astype(BF16)
        cnt_in = _dot_tn(cwin_ref[rows, :].astype(BF16), pick)
        lane_idx = jnp.sum((cnt_in <= slot - base).astype(F32), axis=0, keepdims=True)
        idx_ref[pl.ds(ex, 1), :] = ((nblk * float(LANES) + lane_idx) * float(TOK_SLABS)).astype(jnp.int32)
        a_hi, a_mid, a_lo = _split3(aff_ref[rows, :])
        aff_blk = _dot_tn(a_hi, pick) + _dot_tn(a_mid, pick) + _dot_tn(a_lo, pick)
        gate_ref[pl.ds(ex, 1), :] = jnp.sum(jnp.where(lane_f == lane_idx, aff_blk, 0.0), axis=0, keepdims=True)

    def expert_pair(i, carry):
        per_expert(2 * i)
        per_expert(2 * i + 1)
        return carry

    lax.fori_loop(0, N_EXPERTS // 2, expert_pair, 0)


def _cumsum_consts():
    l = np.arange(LANES)
    tri = (l[:, None] <= l[None, :]).astype(np.float32)
    r = np.arange(STACK_ROWS)
    same = (r[:, None] % N_EXPERTS) == (r[None, :] % N_EXPERTS)
    earlier = (r[None, :] // N_EXPERTS) < (r[:, None] // N_EXPERTS)
    blk = (same & earlier).astype(np.float32)
    return jnp.asarray(tri, BF16), jnp.asarray(blk, BF16)


def _router(logits_t, token_set, tri, blk, name):
    return pl.pallas_call(
        _router_kernel,
        grid=(1,),
        in_specs=[
            pl.BlockSpec((N_EXPERTS, N_TOK), lambda i: (0, token_set)),
            _full_spec((LANES, LANES)),
            _full_spec((STACK_ROWS, STACK_ROWS)),
        ],
        out_specs=[_full_spec((N_EXPERTS, CAPACITY))] * 2,
        out_shape=[jax.ShapeDtypeStruct((N_EXPERTS, CAPACITY), jnp.int32),
                   jax.ShapeDtypeStruct((N_EXPERTS, CAPACITY), F32)],
        scratch_shapes=[pltpu.VMEM((STACK_ROWS, LANES), F32)] * 4,
        compiler_params=_cp(48),
        name=name,
    )(logits_t, tri, blk)


GATHER_UNROLL = 16
ROWS_PER_EXPERT = 2 * CAPACITY
GATHER_PER_STEP = ROWS_PER_EXPERT // (D_EXPERT // FF_TILE)
TILE_PITCH = TOK_SLABS + 1


def _ffn_kernel(off0_ref, offn_ref, hs_ref, g_ref, w1_ref, w3_ref, w2_ref, yc_ref, yl_ref,
                tile_ref, xe_ref, acc_ref, sem):
    e = pl.program_id(0)
    f = pl.program_id(1)
    last_e = pl.num_programs(0) - 1
    last_f = pl.num_programs(1) - 1

    def row_copy(off_ref, row):
        off = pl.multiple_of(off_ref[0, row], TOK_SLABS)
        return pltpu.make_async_copy(hs_ref.at[pl.ds(off, TOK_SLABS)],
                                     tile_ref.at[pl.ds(row * TILE_PITCH, TOK_SLABS)], sem)

    def wait_tile():
        n = ROWS_PER_EXPERT * TOK_SLABS
        pltpu.make_async_copy(hs_ref.at[pl.ds(0, n)], tile_ref.at[pl.ds(0, n)], sem).wait()

    @pl.when(jnp.logical_and(e == 0, f == 0))
    def _():
        def issue(i, carry):
            for u in range(GATHER_UNROLL):
                row_copy(off0_ref, i * GATHER_UNROLL + u).start()
            return carry
        lax.fori_loop(0, ROWS_PER_EXPERT // GATHER_UNROLL, issue, 0)

    @pl.when(f == 0)
    def _():
        wait_tile()
        xe_ref[...] = _from_slabs(tile_ref, ROWS_PER_EXPERT, pitch=TILE_PITCH).astype(BF16)
        acc_ref[...] = jnp.zeros_like(acc_ref)

    for u in range(GATHER_PER_STEP):
        row_copy(offn_ref, f * GATHER_PER_STEP + u).start()

    x = xe_ref[...]
    h1 = _dot(x, w1_ref[...].astype(BF16))
    h3 = _dot(x, w3_ref[...].astype(BF16))
    hid = (h1 * jax.nn.sigmoid(h1) * h3).astype(BF16)
    acc_ref[...] += _dot(hid, w2_ref[...].astype(BF16))

    @pl.when(f == last_f)
    def _():
        gate = jnp.broadcast_to(g_ref[...], (SUBLANES, ROWS_PER_EXPERT)).T[:, :1]
        _to_slabs(yc_ref, acc_ref[:CAPACITY, :] * gate[:CAPACITY], CAPACITY)
        _to_slabs(yl_ref, acc_ref[CAPACITY:, :] * gate[CAPACITY:], CAPACITY)

    @pl.when(jnp.logical_and(e == last_e, f == last_f))
    def _():
        wait_tile()


def _expert_ffn(offsets, hs, gates, w1, w3, w2, layer, name):
    nf = D_EXPERT // FF_TILE
    off_block = (None, 1, ROWS_PER_EXPERT)
    y_spec = pl.BlockSpec((None, CAPACITY * TOK_SLABS, LANES), lambda e, f: (e, 0, 0))
    y_shape = jax.ShapeDtypeStruct((N_EXPERTS, CAPACITY * TOK_SLABS, LANES), F32)
    return pl.pallas_call(
        _ffn_kernel,
        grid=(N_EXPERTS, nf),
        in_specs=[
            pl.BlockSpec(off_block, lambda e, f: (0, 0, 0), memory_space=pltpu.SMEM),
            pl.BlockSpec(off_block, lambda e, f: (jnp.minimum(e + 1, N_EXPERTS - 1), 0, 0), memory_space=pltpu.SMEM),
            pl.BlockSpec(memory_space=pl.ANY),
            pl.BlockSpec(off_block, lambda e, f: (e, 0, 0)),
            pl.BlockSpec((None, None, D_MODEL, FF_TILE), lambda e, f: (layer, e, 0, f)),
            pl.BlockSpec((None, None, D_MODEL, FF_TILE), lambda e, f: (layer, e, 0, f)),
            pl.BlockSpec((None, None, FF_TILE, D_MODEL), lambda e, f: (layer, e, f, 0)),
        ],
        out_specs=[y_spec, y_spec],
        out_shape=[y_shape, y_shape],
        scratch_shapes=[
            pltpu.VMEM((ROWS_PER_EXPERT * TILE_PITCH, LANES), F32),
            pltpu.VMEM((ROWS_PER_EXPERT, D_MODEL), BF16),
            pltpu.VMEM((ROWS_PER_EXPERT, D_MODEL), F32),
            pltpu.SemaphoreType.DMA(()),
        ],
        compiler_params=_cp(56),
        name=name,
    )(offsets, offsets, hs, gates, w1, w3, w2)


SCATTER_UNROLL = 16


def _scatter_expert(off_ref, ye_ref, y_ref, ex):
    def body(i, carry):
        upd = []
        for u in range(SCATTER_UNROLL):
            s = i * SCATTER_UNROLL + u
            rows = pl.ds(pl.multiple_of(off_ref[ex * CAPACITY + s], TOK_SLABS), TOK_SLABS)
            src = ye_ref[pl.ds(pl.multiple_of(s * TOK_SLABS, TOK_SLABS), TOK_SLABS), :]
            upd.append((rows, y_ref[rows, :] + src))
        for rows, val in upd:
            y_ref[rows, :] = val
        return carry

    lax.fori_loop(0, CAPACITY // SCATTER_UNROLL, body, 0)


def _combine_kernel(idx_ref, ye_ref, y_ref):
    ex = pl.program_id(0)

    @pl.when(ex == 0)
    def _():
        y_ref[...] = jnp.zeros_like(y_ref)

    _scatter_expert(idx_ref, ye_ref, y_ref, ex)


def _combine_final_kernel(idx_ref, ye_ref, xp_ref, gp_ref, fg_ref, o_ref, y_ref):
    i = pl.program_id(0)

    @pl.when(i == 0)
    def _():
        y_ref[...] = jnp.zeros_like(y_ref)

    @pl.when(i < N_EXPERTS)
    def _():
        _scatter_expert(idx_ref, ye_ref, y_ref, i)

    @pl.when(i >= N_EXPERTS)
    def _():
        slabs_per_tile = ROW_TILE * TOK_SLABS
        base = pl.multiple_of((i - N_EXPERTS) * slabs_per_tile, slabs_per_tile)
        x = xp_ref[...] + gp_ref[...] * _from_slabs(y_ref, ROW_TILE, base)
        o_ref[...] = _rmsnorm(x, fg_ref[...])


def _combine_final(idx_flat, ye, x_prev, token_set, mods_prev, final_g, name):
    def tile(i):
        return jnp.maximum(i - N_EXPERTS, 0)

    def gate_row(i, idx):
        return (_tile_group(token_set * N_ROW_TILES + tile(i)) * 6 + 5, 0, 0)

    return pl.pallas_call(
        _combine_final_kernel,
        grid_spec=pltpu.PrefetchScalarGridSpec(
            num_scalar_prefetch=1,
            grid=(N_EXPERTS + N_ROW_TILES,),
            in_specs=[
                pl.BlockSpec((None, CAPACITY * TOK_SLABS, LANES), lambda i, idx: (jnp.minimum(i, N_EXPERTS - 1), 0, 0)),
                pl.BlockSpec((ROW_TILE, D_MODEL), lambda i, idx: (token_set * N_ROW_TILES + tile(i), 0)),
                pl.BlockSpec((None, 1, D_MODEL), gate_row),
                pl.BlockSpec((1, D_MODEL), lambda i, idx: (0, 0)),
            ],
            out_specs=pl.BlockSpec((ROW_TILE, D_MODEL), lambda i, idx: (tile(i), 0)),
            scratch_shapes=[pltpu.VMEM((N_TOK * TOK_SLABS, LANES), F32)],
        ),
        out_shape=jax.ShapeDtypeStruct((N_TOK, D_MODEL), F32),
        compiler_params=_cp(40),
        name=name,
    )(idx_flat, ye, x_prev, mods_prev, final_g)


def _combine(idx_flat, ye, name):
    return pl.pallas_call(
        _combine_kernel,
        grid_spec=pltpu.PrefetchScalarGridSpec(
            num_scalar_prefetch=1,
            grid=(N_EXPERTS,),
            in_specs=[pl.BlockSpec((None, CAPACITY * TOK_SLABS, LANES), lambda e, idx: (e, 0, 0))],
            out_specs=pl.BlockSpec((N_TOK * TOK_SLABS, LANES), lambda e, idx: (0, 0)),
        ),
        out_shape=jax.ShapeDtypeStruct((N_TOK * TOK_SLABS, LANES), F32),
        compiler_params=_cp(56),
        name=name,
    )(idx_flat, ye)


def _moe(hs, lt, tri, blk, w1, w3, w2, layer):
    tag = "l%d" % layer
    off_c, g_c = _router(lt, 0, tri, blk, "router_%s_ctx" % tag)
    off_l, g_l = _router(lt, 1, tri, blk, "router_%s_lat" % tag)
    offsets = jnp.concatenate([off_c, off_l + N_TOK * TOK_SLABS], axis=1).reshape(N_EXPERTS, 1, ROWS_PER_EXPERT)
    gates = jnp.concatenate([g_c, g_l], axis=1).reshape(N_EXPERTS, 1, ROWS_PER_EXPERT)
    ye_c, ye_l = _expert_ffn(offsets, hs, gates, w1, w3, w2, layer, "ffn_" + tag)
    return (off_c.reshape(N_EXPERTS * CAPACITY), ye_c), (off_l.reshape(N_EXPERTS * CAPACITY), ye_l)


def _gla_proj_kernel(xp_ref, yc_ref, yl_ref, gp_ref, g_ref, sh_ref, sc_ref, w_ref, wt_ref, wg_ref, bg_ref,
                     x_ref, q_ref, k_ref, v_ref, r_ref, lgf_ref, lgb_ref, wb_ref):
    _cast_once(w_ref, wb_ref)

    def body(latent):
        yt_ref = yl_ref if latent else yc_ref
        x = xp_ref[...] + gp_ref[...] * _from_slabs(yt_ref, ROW_TILE)
        x_ref[...] = x
        h = _modulate(x, g_ref[...], sh_ref[...], sc_ref[...]).astype(BF16)
        z = _dot(h, wb_ref[...])
        nq = GLA_HEADS * GLA_DK
        nv = GLA_HEADS * GLA_DV
        q_ref[...] = z[:, :nq] * (GLA_DK ** -0.5)
        k_ref[...] = z[:, nq:2 * nq]
        v_ref[...] = z[:, 2 * nq:2 * nq + nv].astype(BF16)
        r_ref[...] = z[:, 2 * nq + nv:].astype(BF16)
        zg = _dot(h, wt_ref[...]).astype(BF16)
        a = _dot(zg, wg_ref[...]) + bg_ref[...]
        ls = (jnp.minimum(a, 0.0) - jnp.log1p(jnp.exp(-jnp.abs(a)))) * (1.0 / GLA_GATE_NORM)
        lgf_ref[...] = ls[:, :nq]
        lgb_ref[...] = ls[:, nq:]

    _per_set(body)


def _gla_proj(x_prev, y_ctx, y_lat, mods_prev, mods, norm_g, w_in, w_tail, wg, bg):
    nq = GLA_HEADS * GLA_DK
    nv = GLA_HEADS * GLA_DV
    n_main = 2 * nq + 2 * nv
    slabs = ROW_TILE * TOK_SLABS
    return pl.pallas_call(
        _gla_proj_kernel,
        grid=(N_TILES,),
        in_specs=[
            _row_spec(D_MODEL),
            pl.BlockSpec((slabs, LANES), lambda i: (jnp.minimum(i, N_ROW_TILES - 1), 0)),
            pl.BlockSpec((slabs, LANES), lambda i: (jnp.maximum(i - N_ROW_TILES, 0), 0)),
            _mod_spec(5),
            _full_spec((1, D_MODEL)),
            _mod_spec(0),
            _mod_spec(1),
            _weight_spec(D_MODEL, n_main),
            _full_spec((D_MODEL, LANES)),
            _full_spec((LANES, 2 * nq)),
            _full_spec((1, 2 * nq)),
        ],
        out_specs=[_row_spec(D_MODEL), _row_spec(nq), _row_spec(nq), _row_spec(nv), _row_spec(nv),
                   _row_spec(nq), _row_spec(nq)],
        out_shape=[
            jax.ShapeDtypeStruct((N_ALL, D_MODEL), F32),
            jax.ShapeDtypeStruct((N_ALL, nq), F32),
            jax.ShapeDtypeStruct((N_ALL, nq), F32),
            jax.ShapeDtypeStruct((N_ALL, nv), BF16),
            jax.ShapeDtypeStruct((N_ALL, nv), BF16),
            jax.ShapeDtypeStruct((N_ALL, nq), F32),
            jax.ShapeDtypeStruct((N_ALL, nq), F32),
        ],
        scratch_shapes=[pltpu.VMEM((D_MODEL, n_main), BF16)],
        compiler_params=_cp(56),
        name="gla_proj",
    )(x_prev, y_ctx, y_lat, mods_prev, norm_g, mods, mods, w_in, w_tail, wg, bg)


GLA_LEVELS = (32, 16, 8)
GLA_UNROLL = 8
GLA_GROUP = 4 * GLA_CHUNK
GLA_SAFE_DECAY = 60.0
NEG_BIG = -1e30


def _gla_intra_fast(qe, kc, b2, fwd):
    c = GLA_CHUNK
    row = lax.broadcasted_iota(jnp.int32, (c, 1), 0)
    col = lax.broadcasted_iota(jnp.int32, (1, c), 1)
    keep = (col <= row) if fwd else (col >= row)
    kq = (kc * jnp.exp2(-b2)).astype(BF16)
    return jnp.where(keep, _dot_nt(qe, kq), 0.0)


def _gla_intra_robust(qc, kc, b2, fwd):
    c = GLA_CHUNK
    row = lax.broadcasted_iota(jnp.int32, (c, 1), 0)
    col = lax.broadcasted_iota(jnp.int32, (1, c), 1)
    a = jnp.zeros((c, c), F32)
    for g in GLA_LEVELS:
        odd = ((row >> int(math.log2(g))) & 1) == 1
        later = odd if fwd else jnp.logical_not(odd)
        refs = []
        for p in range(c // (2 * g)):
            r0 = 2 * g * p + (g - 1 if fwd else g)
            refs.append(jnp.broadcast_to(b2[r0:r0 + 1], (2 * g, GLA_DK)))
        ref = jnp.concatenate(refs, axis=0) if len(refs) > 1 else refs[0]
        qt = qc * jnp.exp2(jnp.where(later, b2 - ref, NEG_BIG))
        kt = kc * jnp.exp2(jnp.where(later, NEG_BIG, ref - b2))
        same_parent = (row >> int(math.log2(2 * g))) == (col >> int(math.log2(2 * g)))
        a = a + jnp.where(same_parent, _dot_nt(qt.astype(BF16), kt.astype(BF16)), 0.0)

    sub = lax.broadcasted_iota(jnp.int32, (SUBLANES, 1), 0)
    strips = []
    for blk in range(c // SUBLANES):
        r0 = SUBLANES * blk
        qb = qc[r0:r0 + SUBLANES]
        bb = b2[r0:r0 + SUBLANES]
        strip = jnp.zeros((SUBLANES, c), F32)
        for jj in range(SUBLANES):
            j = r0 + jj
            cond = (sub >= jj) if fwd else (sub <= jj)
            t = qb * kc[j:j + 1] * jnp.exp2(jnp.where(cond, bb - b2[j:j + 1], NEG_BIG))
            strip = jnp.where(col == j, jnp.sum(t, axis=1, keepdims=True), strip)
        strips.append(strip)
    return a + jnp.concatenate(strips, axis=0)


def _gla_state_increment(kc, vc, b2f, b2b):
    c = GLA_CHUNK
    btf = b2f[c - 1:c]
    btb = b2b[0:1]
    kd = jnp.concatenate([(kc * jnp.exp2(btf - b2f)).astype(BF16), (kc * jnp.exp2(btb - b2b)).astype(BF16)], axis=1)
    return _dot_tn(vc, kd), jnp.concatenate([jnp.exp2(btf), jnp.exp2(btb)], axis=1)


def _gla_scores(qc, kc, b2f, b2b, fast):
    qe_f = (qc * jnp.exp2(b2f)).astype(BF16)
    qe_b = (qc * jnp.exp2(b2b)).astype(BF16)
    if fast:
        a = _gla_intra_fast(qe_f, kc, b2f, True) + _gla_intra_fast(qe_b, kc, b2b, False)
    else:
        a = _gla_intra_robust(qc, kc, b2f, True) + _gla_intra_robust(qc, kc, b2b, False)
    return jnp.concatenate([qe_f, qe_b], axis=1), a.astype(BF16)


def _gla_kernel(q_ref, k_ref, v_ref, r_ref, gf_ref, gb_ref, *refs, t_len, heads, zero_state):
    if zero_state:
        s0f_ref = s0b_ref = None
    else:
        s0f_ref, s0b_ref = refs[:2]
        refs = refs[2:]
    go_ref, trif_ref, trib_ref, u_ref, sf_ref, sb_ref = refs[:6]
    scratch = refs[6:]
    _gla_passes(q_ref, k_ref, v_ref, r_ref, gf_ref, gb_ref, s0f_ref, s0b_ref, go_ref, trif_ref, trib_ref,
                u_ref, sf_ref, sb_ref, *scratch, t_len=t_len, heads=heads)


def _gla_passes(q_ref, k_ref, v_ref, r_ref, gf_ref, gb_ref, s0f_ref, s0b_ref, go_ref, trif_ref, trib_ref,
                u_ref, sf_ref, sb_ref, *scratch, t_len, heads):
    c = GLA_CHUNK
    n = t_len // c
    per_head = len(scratch) // heads
    hd = []
    for hh in range(heads):
        dk = slice(hh * GLA_DK, (hh + 1) * GLA_DK)
        dv = slice(hh * GLA_DV, (hh + 1) * GLA_DV)
        b2f, b2b, qe, oi, kv, dec, sst, st = scratch[hh * per_head:(hh + 1) * per_head]
        hd.append(dict(q=q_ref.at[:, dk], k=k_ref.at[:, dk], v=v_ref.at[:, dv], r=r_ref.at[:, dv],
                       gf=gf_ref.at[:, dk], gb=gb_ref.at[:, dk],
                       s0f=None if s0f_ref is None else s0f_ref.at[hh],
                       s0b=None if s0b_ref is None else s0b_ref.at[hh],
                       u=u_ref.at[:, dv], sf=sf_ref.at[hh], sb=sb_ref.at[hh],
                       b2f=b2f, b2b=b2b, qe=qe, oi=oi, kv=kv, dec=dec, sst=sst, st=st))

    def rows_of(ci):
        return pl.ds(pl.multiple_of(ci * c, c), c)

    def state_rows_of(ci):
        return pl.ds(pl.multiple_of(ci * GLA_DV, GLA_DV), GLA_DV)

    n_groups = t_len // GLA_GROUP
    cum_unroll = 2 if n_groups % 2 == 0 else 1

    def cum_body(i, carry):
        sums = []
        for h in hd:
            for u in range(cum_unroll):
                rows = pl.ds(pl.multiple_of((i * cum_unroll + u) * GLA_GROUP, GLA_GROUP), GLA_GROUP)
                for g_ref, tri_ref, b_ref in ((h["gf"], trif_ref, h["b2f"]), (h["gb"], trib_ref, h["b2b"])):
                    s3 = _dot(tri_ref[...], jnp.concatenate(_split3(g_ref[rows, :]), axis=1))
                    sums.append((b_ref, rows, s3))
        for b_ref, rows, s3 in sums:
            b_ref[rows, :] = (s3[:, :GLA_DK] + s3[:, GLA_DK:2 * GLA_DK] + s3[:, 2 * GLA_DK:]) * LOG2E
        return carry

    lax.fori_loop(0, n_groups // cum_unroll, cum_body, 0)

    def chunk_total(g_ref):
        return jnp.min(jnp.sum(g_ref[...].reshape(n, c, GLA_DK), axis=1))

    totals = [chunk_total(h[g]) for h in hd for g in ("gf", "gb")]
    safe = functools.reduce(jnp.minimum, totals) >= -GLA_SAFE_DECAY

    def make_local(fast, unroll):
        def body(i, carry):
            work = [(h, i * unroll + u) for h in hd for u in range(unroll)]
            for h, ci in work:
                rows = rows_of(ci)
                kv, dec = _gla_state_increment(h["k"][rows, :], h["v"][rows, :], h["b2f"][rows, :], h["b2b"][rows, :])
                h["kv"][state_rows_of(ci), :] = kv
                h["dec"][pl.ds(ci, 1), :] = dec
            scores = []
            for h, ci in work:
                rows = rows_of(ci)
                qe, a = _gla_scores(h["q"][rows, :], h["k"][rows, :], h["b2f"][rows, :], h["b2b"][rows, :], fast)
                h["qe"][rows, :] = qe
                scores.append(a)
            for (h, ci), a in zip(work, scores):
                rows = rows_of(ci)
                h["oi"][rows, :] = _dot(a, h["v"][rows, :])
            return carry
        return body

    unroll = min(GLA_UNROLL, n)

    @pl.when(safe)
    def _():
        lax.fori_loop(0, n // unroll, make_local(True, unroll), 0)

    @pl.when(jnp.logical_not(safe))
    def _():
        lax.fori_loop(0, n, make_local(False, 1), 0)

    fl = slice(0, GLA_DK)
    bl = slice(GLA_DK, 2 * GLA_DK)
    for h in hd:
        if h["s0f"] is None:
            h["st"][...] = jnp.zeros_like(h["st"])
        else:
            h["st"][...] = jnp.concatenate([h["s0f"][...].T, h["s0b"][...].T], axis=1)

    def rec_body(i, carry):
        cf = i
        cb = n - 1 - i
        for h in hd:
            st = h["st"][...]
            h["sst"][state_rows_of(cf), fl] = st[:, fl].astype(BF16)
            h["sst"][state_rows_of(cb), bl] = st[:, bl].astype(BF16)
            dec = jnp.concatenate([h["dec"][pl.ds(cf, 1), :][:, fl], h["dec"][pl.ds(cb, 1), :][:, bl]], axis=1)
            kv = jnp.concatenate([h["kv"][state_rows_of(cf), fl], h["kv"][state_rows_of(cb), bl]], axis=1)
            h["st"][...] = st * dec + kv
        return carry

    lax.fori_loop(0, n, rec_body, 0)
    for h in hd:
        h["sf"][...] = h["st"][:, fl].T
        h["sb"][...] = h["st"][:, bl].T

    go = go_ref[...]

    def out_body(i, carry):
        work = [(h, i * unroll + u) for h in hd for u in range(unroll)]
        inter = [_dot_nt(h["qe"][rows_of(ci), :], h["sst"][state_rows_of(ci), :]) for h, ci in work]
        for (h, ci), o_inter in zip(work, inter):
            rows = rows_of(ci)
            r = h["r"][rows, :].astype(F32)
            h["u"][rows, :] = (_rmsnorm(h["oi"][rows, :] + o_inter, go) * (r * jax.nn.sigmoid(r))).astype(BF16)
        return carry

    lax.fori_loop(0, n // unroll, out_body, 0)


def _gla_tri():
    i = np.arange(GLA_GROUP)
    same = (i[:, None] // GLA_CHUNK) == (i[None, :] // GLA_CHUNK)
    fwd = (same & (i[None, :] <= i[:, None])).astype(np.float32)
    bwd = (same & (i[None, :] >= i[:, None])).astype(np.float32)
    return jnp.asarray(fwd, BF16), jnp.asarray(bwd, BF16)


def _gla(q, k, v, r, lgf, lgb, batch0, s0f, s0b, g_out, n_batch, t_len, heads, name):
    zero_state = s0f is None
    trif, trib = _gla_tri()
    n_chunks = t_len // GLA_CHUNK
    qk_spec = pl.BlockSpec((None, t_len, heads * GLA_DK), lambda b, h: (batch0 + b, 0, h))
    vin_spec = pl.BlockSpec((None, t_len, heads * GLA_DV), lambda b, h: (batch0 + b, 0, h))
    v_spec = pl.BlockSpec((None, t_len, heads * GLA_DV), lambda b, h: (b, 0, h))
    s_spec = pl.BlockSpec((None, heads, GLA_DK, GLA_DV), lambda b, h: (b, h, 0, 0))
    const = lambda shape: pl.BlockSpec(shape, lambda b, h: (0,) * len(shape))
    s_shape = jax.ShapeDtypeStruct((n_batch, GLA_HEADS, GLA_DK, GLA_DV), F32)
    states = [] if zero_state else [s0f, s0b]
    return pl.pallas_call(
        functools.partial(_gla_kernel, t_len=t_len, heads=heads, zero_state=zero_state),
        grid=(n_batch, GLA_HEADS // heads),
        in_specs=[qk_spec, qk_spec, vin_spec, vin_spec, qk_spec, qk_spec] + [s_spec] * len(states)
        + [const((1, GLA_DV)), const((GLA_GROUP, GLA_GROUP)), const((GLA_GROUP, GLA_GROUP))],
        out_specs=[v_spec, s_spec, s_spec],
        out_shape=[jax.ShapeDtypeStruct((n_batch, t_len, GLA_HEADS * GLA_DV), BF16), s_shape, s_shape],
        scratch_shapes=[
            pltpu.VMEM((t_len, GLA_DK), F32), pltpu.VMEM((t_len, GLA_DK), F32),
            pltpu.VMEM((t_len, 2 * GLA_DK), BF16),
            pltpu.VMEM((t_len, GLA_DV), F32),
            pltpu.VMEM((n_chunks * GLA_DV, 2 * GLA_DK), F32),
            pltpu.VMEM((max(n_chunks, SUBLANES), 2 * GLA_DK), F32),
            pltpu.VMEM((n_chunks * GLA_DV, 2 * GLA_DK), BF16),
            pltpu.VMEM((GLA_DV, 2 * GLA_DK), F32),
        ] * heads,
        compiler_params=_cp(48),
        name=name,
    )(q, k, v, r, lgf, lgb, *states, g_out, trif, trib)


def kernel(x_prompt, x_sample, cache_k, cache_v, state_fwd, state_bwd, c, c_ctx, w_mod, b_mod, norm_g,
           da_w_in, da_w_out, da_lam_q1, da_lam_k1, da_lam_q2, da_lam_k2, da_g_sub, gla_w_in, gla_w_gf2,
           gla_b_gf, gla_w_gb2, gla_b_gb, gla_g_out, gla_w_out, moe_w_router, moe_w1, moe_w3, moe_w2, final_g):
    n_ctx_b = x_prompt.shape[0]
    n_lat_b = x_sample.shape[0]
    x_ctx = x_prompt.reshape(N_TOK, D_MODEL)
    x_lat = x_sample.reshape(N_TOK, D_MODEL)

    cvec = jnp.zeros((SUBLANES, D_MODEL), F32).at[0].set(c_ctx).at[1:1 + n_lat_b].set(c)
    mods = _mod_params(cvec, w_mod, b_mod)
    tri, blk = _cumsum_consts()
    wr_t = [jnp.zeros((LANES, D_MODEL), BF16).at[:N_EXPERTS].set(moe_w_router[i].T.astype(BF16))
            for i in range(DEPTH)]
    ng = norm_g.reshape(DEPTH, 2, 1, D_MODEL)

    lam_init = 0.8 - 0.6 * math.exp(-0.3 * 0)
    w_in = da_w_in
    w_out = da_w_out
    lam_vecs = [v[0].reshape(1, DA_DIM) for v in (da_lam_q1, da_lam_k1, da_lam_q2, da_lam_k2)]
    g_sub = da_g_sub[0].reshape(1, DA_VDIM)

    q, k, v, kf, vf = _da_proj(x_ctx, x_lat, mods[0], ng[0, 0], w_in, _rope_tables())
    past = cache_k.shape[2]
    cv = cache_v[:, 0].reshape(n_lat_b * past * DA_HEADS, DA_VDIM)
    u_ctx = _attention(lam_vecs, g_sub, q, 0, [(k, v, 0, SEQ)], n_ctx_b, SEQ, ATTN_Q_TILE, lam_init, "attn_ctx")
    u_lat = _attention(lam_vecs, g_sub, q, N_TOK, [(k, v, N_TOK, DEC_SEQ), (cache_k, cv, 0, past)], n_lat_b, DEC_SEQ,
                       ATTN_Q_TILE, lam_init, "attn_lat", cache_keys=True)

    x1, hs, lt = _outproj(u_ctx, u_lat, (x_ctx, x_lat), w_out, mods[0], ng[0, 1], wr_t[0], "outproj0")
    (idx_c, ye_c), (idx_l, ye_l) = _moe(hs, lt, tri, blk, moe_w1, moe_w3, moe_w2, 0)
    y_ctx = _combine(idx_c, ye_c, "combine_l0_ctx")
    y_lat = _combine(idx_l, ye_l, "combine_l0_lat")

    nq = GLA_HEADS * GLA_DK
    n_main = 2 * nq + 2 * GLA_HEADS * GLA_DV
    w_tail = jnp.zeros((D_MODEL, LANES), F32).at[:, :2 * GLA_GATE_RANK].set(gla_w_in[0][:, n_main:]).astype(BF16)
    wg = jnp.zeros((LANES, 2 * nq), F32)
    wg = wg.at[:GLA_GATE_RANK, :nq].set(gla_w_gf2[0]).at[GLA_GATE_RANK:2 * GLA_GATE_RANK, nq:].set(gla_w_gb2[0])
    wg = wg.astype(BF16)
    bg = jnp.concatenate([gla_b_gf[0], gla_b_gb[0]]).reshape(1, 2 * nq)
    w_out1 = gla_w_out
    g_out = gla_g_out[0].reshape(1, GLA_DV)

    x2, *gla_in = _gla_proj(x1, y_ctx, y_lat, mods[0], mods[1], ng[1, 0], gla_w_in, w_tail, wg, bg)

    def gla_side(n_b, t_len, s0f, s0b, tag):
        seqs = [a.reshape(N_ALL // t_len, t_len, a.shape[-1]) for a in gla_in]
        batch0 = 0 if s0f is None else N_TOK // t_len
        heads = GLA_HEADS if t_len <= GLA_GROUP else 1
        u, sf, sb = _gla(*seqs, batch0, s0f, s0b, g_out, n_b, t_len, heads, "gla_" + tag)
        return u.reshape(N_TOK, D_MODEL), sf, sb

    ug_ctx, sf, sb = gla_side(n_ctx_b, SEQ, None, None, "ctx")
    ug_lat, _, _ = gla_side(n_lat_b, DEC_SEQ, state_fwd[:, 0], state_bwd[:, 0], "lat")

    x3, hs, lt = _outproj(ug_ctx, ug_lat, (x2,), w_out1, mods[1], ng[1, 1], wr_t[1], "outproj1")
    (idx_c, ye_c), (idx_l, ye_l) = _moe(hs, lt, tri, blk, moe_w1, moe_w3, moe_w2, 1)

    fg = final_g.reshape(1, D_MODEL)
    y_prompt = _combine_final(idx_c, ye_c, x3, 0, mods[1], fg, "final_ctx").reshape(x_prompt.shape)
    y_sample = _combine_final(idx_l, ye_l, x3, 1, mods[1], fg, "final_lat").reshape(x_sample.shape)
    new_k = kf.reshape(n_ctx_b, 1, SEQ, 2 * DA_HEADS, DA_DIM)
    new_v = vf.reshape(n_ctx_b, 1, SEQ, DA_HEADS, DA_VDIM)
    return (y_prompt, y_sample, new_k, new_v, sf[:, None], sb[:, None])
```
